```python
import jax, jax.numpy as jnp
from jax import lax
import numpy as np

D_MODEL = 1024
BATCH = 8
SEQ = 4096
DEPTH = 1

HEAD_DIM = 64
N_Q_HEADS = 8
N_KV_HEADS = 2
Q_GROUP = N_Q_HEADS // N_KV_HEADS
Q_WIDTH = N_Q_HEADS * HEAD_DIM
KV_WIDTH = N_KV_HEADS * HEAD_DIM
AXIS_DIM = HEAD_DIM // 2
ROPE_THETA = 10000.0
Q_BLOCK = 128
GRID_W = 64
LRU_WIDTH = D_MODEL
LRU_BLOCKS = 16
LRU_BLOCK_W = LRU_WIDTH // LRU_BLOCKS
CONV_WIDTH = 4
CONV_PAD = (1, 2)
RG_C = 8.0
N_BRANCHES = 2
IN_WIDTH = Q_WIDTH + 2 * KV_WIDTH + 2 * LRU_WIDTH + N_BRANCHES * D_MODEL
SPLITS = (Q_WIDTH, Q_WIDTH + KV_WIDTH, Q_WIDTH + 2 * KV_WIDTH,
          Q_WIDTH + 2 * KV_WIDTH + LRU_WIDTH, Q_WIDTH + 2 * KV_WIDTH + 2 * LRU_WIDTH)
N_EXPERTS = 32
TOP_K = 4
D_FF = D_MODEL
SWIGLU_LIMIT = 7.0
SWIGLU_ALPHA = 1.702
DISPATCH_BLOCK = 256
NORM_EPS = 1e-6

kernel_name = 'hybrid_gqa_rglru_moe_encoder'


def rmsnorm(x, g):
    xf = x.astype(jnp.float32)
    y = xf * lax.rsqrt(jnp.mean(xf * xf, axis=-1, keepdims=True) + NORM_EPS)
    return (y * g.astype(jnp.float32)).astype(x.dtype)


def axial_rope_tables(S, dtype):
    rows = S // GRID_W
    row = jnp.repeat(jnp.arange(rows, dtype=jnp.int32), GRID_W).astype(jnp.float32)
    col = jnp.tile(jnp.arange(GRID_W, dtype=jnp.int32), rows).astype(jnp.float32)
    inv = ROPE_THETA ** (-jnp.arange(0, AXIS_DIM, 2, dtype=jnp.float32) / AXIS_DIM)
    ang_r = row[:, None] * inv[None, :]
    ang_c = col[:, None] * inv[None, :]
    f = lambda a: a[None, :, None, :].astype(dtype)
    return f(jnp.cos(ang_r)), f(jnp.sin(ang_r)), f(jnp.cos(ang_c)), f(jnp.sin(ang_c))


def rotate(x, cos, sin):
    x1, x2 = jnp.split(x, 2, axis=-1)
    return jnp.concatenate([x1 * cos - x2 * sin, x2 * cos + x1 * sin], axis=-1)


def apply_axial_rope(x, tables):
    cos_r, sin_r, cos_c, sin_c = tables
    return jnp.concatenate([rotate(x[..., :AXIS_DIM], cos_r, sin_r),
                            rotate(x[..., AXIS_DIM:], cos_c, sin_c)], axis=-1)


def block_attention(q, k, v):
    B, S = q.shape[0], q.shape[1]
    scale = HEAD_DIM ** -0.5
    qb = q.reshape(B, S // Q_BLOCK, Q_BLOCK, N_KV_HEADS, Q_GROUP, HEAD_DIM)
    qb = jnp.moveaxis(qb, 1, 0)

    def one_block(qblk):
        s = jnp.einsum('bqkgd,bskd->bkgqs', qblk, k).astype(jnp.float32) * scale
        p = jax.nn.softmax(s, axis=-1).astype(v.dtype)
        return jnp.einsum('bkgqs,bskd->bqkgd', p, v)

    o = lax.map(one_block, qb)
    return jnp.moveaxis(o, 0, 1).reshape(B, S, Q_WIDTH)


def depthwise_conv(x, w, b):
    C = x.shape[-1]
    y = lax.conv_general_dilated(x, w[:, None, :].astype(x.dtype), window_strides=(1,),
                                 padding=(CONV_PAD,), dimension_numbers=('NWC', 'WIO', 'NWC'),
                                 feature_group_count=C)
    return y + b


def block_diag(x, w, b):
    B, S, _ = x.shape
    xb = x.reshape(B, S, LRU_BLOCKS, LRU_BLOCK_W)
    y = jnp.einsum('bsnc,ncd->bsnd', xb, w) + b
    return y.reshape(B, S, LRU_WIDTH)


def linear_combine(earlier, later):
    a1, b1 = earlier
    a2, b2 = later
    return a1 * a2, a2 * b1 + b2


def rglru(x, wa, ba, wi, bi, lam, reverse):
    r = jax.nn.sigmoid(block_diag(x, wa, ba)).astype(jnp.float32)
    i = jax.nn.sigmoid(block_diag(x, wi, bi))
    log_a = RG_C * r * jax.nn.log_sigmoid(lam.astype(jnp.float32))
    a = jnp.exp(log_a)
    u = jnp.sqrt(-jnp.expm1(2.0 * log_a)) * (i * x).astype(jnp.float32)
    _, h = lax.associative_scan(linear_combine, (a, u), reverse=reverse, axis=1)
    return h.astype(x.dtype)


def token_mixer(h, w_in, b_in, q_norm_g, k_norm_g, conv_w, conv_b, lru_wa, lru_ba,
                lru_wi, lru_bi, lru_lambda, w_attn_o, w_lru_o, w_out):
    B, S, _ = h.shape
    proj = h @ w_in + b_in
    q, k, v, xr, yr, gate_logits = jnp.split(proj, SPLITS, axis=-1)
    tables = axial_rope_tables(S, h.dtype)
    q = apply_axial_rope(rmsnorm(q.reshape(B, S, N_Q_HEADS, HEAD_DIM), q_norm_g), tables)
    k = apply_axial_rope(rmsnorm(k.reshape(B, S, N_KV_HEADS, HEAD_DIM), k_norm_g), tables)
    v = v.reshape(B, S, N_KV_HEADS, HEAD_DIM)
    attn = block_attention(q, k, v)
    xc = depthwise_conv(xr, conv_w, conv_b)
    lru = (rglru(xc, lru_wa[0], lru_ba[0], lru_wi[0], lru_bi[0], lru_lambda[0], False)
           + rglru(xc, lru_wa[1], lru_ba[1], lru_wi[1], lru_bi[1], lru_lambda[1], True))
    rec = lru * jax.nn.gelu(yr, approximate=True)
    g_attn, g_rec = jnp.split(jax.nn.sigmoid(gate_logits), N_BRANCHES, axis=-1)
    merged = g_attn * (attn @ w_attn_o) + g_rec * (rec @ w_lru_o)
    return merged @ w_out


def moe_ffn(h, w_router, b_router, w_gu, b_gu, w_down, b_down):
    B, S, D = h.shape
    t = h.reshape(B * S, D)
    T = B * S
    A = T * TOP_K
    logits = (t @ w_router + b_router).astype(jnp.float32)
    top_val, top_idx = lax.top_k(logits, TOP_K)
    gates = jax.nn.softmax(top_val, axis=-1).astype(t.dtype)
    e_flat = top_idx.reshape(A).astype(jnp.int32)
    tok_flat = jnp.arange(A, dtype=jnp.int32) // TOP_K
    e_sorted, order = lax.sort((e_flat, jnp.arange(A, dtype=jnp.int32)), num_keys=2)
    tok_sorted = tok_flat[order]
    gate_sorted = gates.reshape(A)[order]
    counts = jnp.bincount(e_flat, length=N_EXPERTS)
    padded = (counts + DISPATCH_BLOCK - 1) // DISPATCH_BLOCK * DISPATCH_BLOCK
    start = jnp.cumsum(counts) - counts
    pend = jnp.cumsum(padded)
    pstart = pend - padded
    dest = pstart[e_sorted] + jnp.arange(A, dtype=jnp.int32) - start[e_sorted]
    n_blocks = (A + N_EXPERTS * (DISPATCH_BLOCK - 1) + DISPATCH_BLOCK - 1) // DISPATCH_BLOCK
    P = n_blocks * DISPATCH_BLOCK
    buf = jnp.zeros((P, D), t.dtype).at[dest].set(t[tok_sorted])
    blk_start = jnp.arange(n_blocks, dtype=jnp.int32) * DISPATCH_BLOCK
    blk_exp = jnp.minimum(jnp.searchsorted(pend, blk_start, side='right'), N_EXPERTS - 1)

    def expert_block(args):
        xb, e = args
        gu = xb @ w_gu[e] + b_gu[e]
        g, u = jnp.split(gu, 2, axis=-1)
        g = jnp.minimum(g, SWIGLU_LIMIT)
        u = jnp.clip(u, -SWIGLU_LIMIT, SWIGLU_LIMIT)
        act = g * jax.nn.sigmoid(SWIGLU_ALPHA * g) * (u + 1.0)
        return act @ w_down[e] + b_down[e]

    out_buf = lax.map(expert_block, (buf.reshape(n_blocks, DISPATCH_BLOCK, D), blk_exp))
    y_sorted = out_buf.reshape(P, D)[dest] * gate_sorted[:, None]
    y = jax.ops.segment_sum(y_sorted, tok_sorted, num_segments=T)
    return y.reshape(B, S, D)


def setup_inputs(seed: int = 0) -> dict:
    key = jax.random.key(seed)
    ks = jax.random.split(key, 24)
    f32 = jnp.float32
    nrm = lambda k, shape, fan_in: jax.random.normal(k, shape, f32) * (fan_in ** -0.5)
    small = lambda k, shape: 0.01 * jax.random.normal(k, shape, f32)
    gain = lambda k, shape: 1.0 + 0.05 * jax.random.normal(k, shape, f32)
    u = jax.random.uniform(ks[13], (DEPTH, 2, LRU_WIDTH), f32, 0.9, 0.999)
    s = u ** (1.0 / RG_C)
    lru_lambda = jnp.log(s) - jnp.log1p(-s)
    return {
        'x': jax.random.normal(ks[0], (BATCH, SEQ, D_MODEL), f32),
        'norm1_g': gain(ks[1], (DEPTH, D_MODEL)),
        'w_in': nrm(ks[2], (DEPTH, D_MODEL, IN_WIDTH), D_MODEL),
        'b_in': small(ks[3], (DEPTH, IN_WIDTH)),
        'q_norm_g': gain(ks[4], (DEPTH, HEAD_DIM)),
        'k_norm_g': gain(ks[5], (DEPTH, HEAD_DIM)),
        'conv_w': nrm(ks[6], (DEPTH, CONV_WIDTH, LRU_WIDTH), CONV_WIDTH),
        'conv_b': small(ks[7], (DEPTH, LRU_WIDTH)),
        'lru_wa': nrm(ks[8], (DEPTH, 2, LRU_BLOCKS, LRU_BLOCK_W, LRU_BLOCK_W), LRU_BLOCK_W),
        'lru_ba': small(ks[9], (DEPTH, 2, LRU_BLOCKS, LRU_BLOCK_W)),
        'lru_wi': nrm(ks[10], (DEPTH, 2, LRU_BLOCKS, LRU_BLOCK_W, LRU_BLOCK_W), LRU_BLOCK_W),
        'lru_bi': small(ks[11], (DEPTH, 2, LRU_BLOCKS, LRU_BLOCK_W)),
        'lru_lambda': lru_lambda,
        'w_attn_o': nrm(ks[14], (DEPTH, Q_WIDTH, D_MODEL), Q_WIDTH),
        'w_lru_o': nrm(ks[15], (DEPTH, LRU_WIDTH, D_MODEL), LRU_WIDTH),
        'w_out': nrm(ks[16], (DEPTH, D_MODEL, D_MODEL), D_MODEL),
        'norm2_g': gain(ks[17], (DEPTH, D_MODEL)),
        'w_router': nrm(ks[18], (DEPTH, D_MODEL, N_EXPERTS), D_MODEL),
        'b_router': small(ks[19], (DEPTH, N_EXPERTS)),
        'w_gu': nrm(ks[20], (DEPTH, N_EXPERTS, D_MODEL, 2 * D_FF), D_MODEL),
        'b_gu': small(ks[21], (DEPTH, N_EXPERTS, 2 * D_FF)),
        'w_down': nrm(ks[22], (DEPTH, N_EXPERTS, D_FF, D_MODEL), D_FF),
        'b_down': small(ks[23], (DEPTH, N_EXPERTS, D_MODEL)),
        'final_g': gain(ks[12], (D_MODEL,)),
    }


def reference(x, norm1_g, w_in, b_in, q_norm_g, k_norm_g, conv_w, conv_b, lru_wa, lru_ba,
              lru_wi, lru_bi, lru_lambda, w_attn_o, w_lru_o, w_out, norm2_g, w_router,
              b_router, w_gu, b_gu, w_down, b_down, final_g):
    for l in range(DEPTH):
        h = rmsnorm(x, norm1_g[l])
        x = x + token_mixer(h, w_in[l], b_in[l], q_norm_g[l], k_norm_g[l], conv_w[l], conv_b[l],
                            lru_wa[l], lru_ba[l], lru_wi[l], lru_bi[l], lru_lambda[l],
                            w_attn_o[l], w_lru_o[l], w_out[l])
        h = rmsnorm(x, norm2_g[l])
        x = x + moe_ffn(h, w_router[l], b_router[l], w_gu[l], b_gu[l], w_down[l], b_down[l])
    return rmsnorm(x, final_g)
```

```python
import functools
import math

import jax
import jax.numpy as jnp
from jax import lax
from jax.experimental import pallas as pl
from jax.experimental.pallas import tpu as pltpu

F32 = jnp.float32
BF16 = jnp.bfloat16
I32 = jnp.int32

LANES = 128
SUBLANES = 8
VMEM_LIMIT_BYTES = 56 * 1024 * 1024

HEAD_DIM = 64
N_Q_HEADS = 8
N_KV_HEADS = 2
Q_GROUP = N_Q_HEADS // N_KV_HEADS
Q_WIDTH = N_Q_HEADS * HEAD_DIM
KV_WIDTH = N_KV_HEADS * HEAD_DIM
AXIS_DIM = HEAD_DIM // 2
ROT_HALF = AXIS_DIM // 2
ROPE_THETA = 10000.0
GRID_W = 64
LRU_BLOCKS = 16
RG_C = 8.0
N_EXPERTS = 32
TOP_K = 4
SWIGLU_LIMIT = 7.0
SWIGLU_ALPHA = 1.702
NORM_EPS = 1e-6
LOG2E = 1.4426950408889634
NEG_BIG = -1e30

ROW_TILE = 512
ATT_TQ = 128
ATT_TK = 512
LRU_CT = 256
LRU_ROWS = 256
TIME_CHUNKS = SUBLANES
FFN_TM = 512
DMA_TILE = 256


def _cparams(*sem):
    return pltpu.CompilerParams(dimension_semantics=sem, vmem_limit_bytes=VMEM_LIMIT_BYTES)


def _rms(x, g):
    return x * lax.rsqrt(jnp.mean(x * x, axis=-1, keepdims=True) + NORM_EPS) * g


def _qkv_kernel(x_ref, g1_ref, w_ref, b_ref, cos_ref, sin_ref, qg_ref, kg_ref, hsum_ref,
                q_ref, k_ref, v_ref):
    h = _rms(x_ref[...], g1_ref[...]).astype(BF16)
    p = jnp.dot(h, w_ref[...], preferred_element_type=F32) + b_ref[...]
    cos = cos_ref[...]
    sin = sin_ref[...]
    hsum = hsum_ref[...]
    lane = lax.broadcasted_iota(I32, cos.shape, 1)
    first_half = (lane % AXIS_DIM) < ROT_HALF

    def norm_rope(c, gain):
        sq = c * c
        hi = sq.astype(BF16)
        lo = (sq - hi.astype(F32)).astype(BF16)
        ms = (jnp.dot(hi, hsum, preferred_element_type=F32)
              + jnp.dot(lo, hsum, preferred_element_type=F32)) * (1.0 / HEAD_DIM)
        y = c * lax.rsqrt(ms + NORM_EPS) * gain
        partner = jnp.where(first_half, pltpu.roll(y, LANES - ROT_HALF, 1),
                            pltpu.roll(y, ROT_HALF, 1))
        return y * cos + partner * sin

    qg = qg_ref[...]
    for c in range(Q_WIDTH // LANES):
        y = norm_rope(p[:, c * LANES:(c + 1) * LANES], qg) * (HEAD_DIM ** -0.5 * LOG2E)
        q_ref[0, 2 * c] = y[:, :HEAD_DIM].astype(BF16)
        q_ref[0, 2 * c + 1] = y[:, HEAD_DIM:].astype(BF16)
    yk = norm_rope(p[:, Q_WIDTH:Q_WIDTH + KV_WIDTH], kg_ref[...])
    k_ref[0, 0] = yk[:, :HEAD_DIM].astype(BF16)
    k_ref[0, 1] = yk[:, HEAD_DIM:].astype(BF16)
    pv = p[:, Q_WIDTH + KV_WIDTH:]
    ones_col = (lax.broadcasted_iota(I32, (pv.shape[0], HEAD_DIM), 1) == 0).astype(BF16)
    v_ref[0, 0] = jnp.concatenate([pv[:, :HEAD_DIM].astype(BF16), ones_col], axis=1)
    v_ref[0, 1] = jnp.concatenate([pv[:, HEAD_DIM:].astype(BF16), ones_col], axis=1)


def _qkv(x2d, g1, w_qkv, b_qkv, cos_t, sin_t, qg, kg, hsum, B, S):
    T, D = x2d.shape
    tm = min(ROW_TILE, S)
    ns = S // tm
    n = w_qkv.shape[1]
    full = lambda shape: pl.BlockSpec(shape, lambda i: (0,) * len(shape))
    return pl.pallas_call(
        _qkv_kernel,
        grid=(T // tm,),
        in_specs=[
            pl.BlockSpec((tm, D), lambda i: (i, 0)),
            full((1, D)), full((D, n)), full((1, n)),
            pl.BlockSpec((tm, LANES), lambda i: (i % ns, 0)),
            pl.BlockSpec((tm, LANES), lambda i: (i % ns, 0)),
            full((1, LANES)), full((1, LANES)), full((LANES, LANES)),
        ],
        out_specs=[
            pl.BlockSpec((1, N_Q_HEADS, tm, HEAD_DIM), lambda i: (i // ns, 0, i % ns, 0)),
            pl.BlockSpec((1, N_KV_HEADS, tm, HEAD_DIM), lambda i: (i // ns, 0, i % ns, 0)),
            pl.BlockSpec((1, N_KV_HEADS, tm, LANES), lambda i: (i // ns, 0, i % ns, 0)),
        ],
        out_shape=[
            jax.ShapeDtypeStruct((B, N_Q_HEADS, S, HEAD_DIM), BF16),
            jax.ShapeDtypeStruct((B, N_KV_HEADS, S, HEAD_DIM), BF16),
            jax.ShapeDtypeStruct((B, N_KV_HEADS, S, LANES), BF16),
        ],
        compiler_params=_cparams("parallel"),
        name="qkv",
    )(x2d, g1, w_qkv, b_qkv, cos_t, sin_t, qg, kg, hsum)


def _proj_kernel(x_ref, g_ref, w_ref, b_ref, o_ref):
    h = _rms(x_ref[...], g_ref[...]).astype(BF16)
    o_ref[...] = jnp.dot(h, w_ref[...], preferred_element_type=F32) + b_ref[...]


def _proj(x2d, g, w, b):
    T, D = x2d.shape
    n = w.shape[1]
    tm = min(ROW_TILE, T)
    return pl.pallas_call(
        _proj_kernel,
        grid=(T // tm,),
        in_specs=[
            pl.BlockSpec((tm, D), lambda i: (i, 0)),
            pl.BlockSpec((1, D), lambda i: (0, 0)),
            pl.BlockSpec((D, n), lambda i: (0, 0)),
            pl.BlockSpec((1, n), lambda i: (0, 0)),
        ],
        out_specs=pl.BlockSpec((tm, n), lambda i: (i, 0)),
        out_shape=jax.ShapeDtypeStruct((T, n), F32),
        compiler_params=_cparams("parallel"),
        name="proj",
    )(x2d, g, w, b)


def _attn_kernel(q_ref, k_ref, v_ref, o_ref, *, tk):
    tq = q_ref.shape[2]
    S = k_ref.shape[2]
    q = q_ref[0].reshape(Q_GROUP * tq, HEAD_DIM)
    M = q.shape[0]

    def step(j, carry):
        m, acc = carry
        off = pl.multiple_of(j * tk, tk)
        kc = k_ref[0, 0, pl.ds(off, tk), :]
        vc = v_ref[0, 0, pl.ds(off, tk), :]
        s = lax.dot_general(q, kc, (((1,), (1,)), ((), ())), preferred_element_type=F32)
        m_new = jnp.maximum(m, jnp.max(s, axis=-1, keepdims=True))
        alpha = jnp.exp2(m - m_new)
        p = jnp.exp2(s - m_new).astype(BF16)
        acc = alpha * acc + jnp.dot(p, vc, preferred_element_type=F32)
        return m_new, acc

    m0 = jnp.full((M, 1), NEG_BIG, F32)
    acc0 = jnp.zeros((M, LANES), F32)
    _, acc = lax.fori_loop(0, S // tk, step, (m0, acc0))
    out = acc[:, :HEAD_DIM] / acc[:, HEAD_DIM:HEAD_DIM + 1]
    o_ref[...] = jnp.concatenate(
        [out[g * tq:(g + 1) * tq] for g in range(Q_GROUP)], axis=1).astype(BF16)


def _attn(q, k, v):
    B, _, S, _ = q.shape
    tq = min(ATT_TQ, S)
    tk = min(ATT_TK, S)
    nq = S // tq
    return pl.pallas_call(
        functools.partial(_attn_kernel, tk=tk),
        grid=(B, N_KV_HEADS, nq),
        in_specs=[
            pl.BlockSpec((1, Q_GROUP, tq, HEAD_DIM), lambda b, g, i: (b, g, i, 0)),
            pl.BlockSpec((1, 1, S, HEAD_DIM), lambda b, g, i: (b, g, 0, 0)),
            pl.BlockSpec((1, 1, S, LANES), lambda b, g, i: (b, g, 0, 0)),
        ],
        out_specs=pl.BlockSpec((tq, Q_GROUP * HEAD_DIM), lambda b, g, i: (b * nq + i, g)),
        out_shape=jax.ShapeDtypeStruct((B * S, Q_WIDTH), BF16),
        compiler_params=_cparams("parallel", "parallel", "parallel"),
        name="attn",
    )(q, k, v)


def _shift_chunks(v, down):
    row = lax.broadcasted_iota(I32, v.shape, 0)
    if down:
        return jnp.where(row == 0, 0.0, pltpu.roll(v, 1, 0))
    return jnp.where(row == SUBLANES - 1, 0.0, pltpu.roll(v, SUBLANES - 1, 0))


def _lru_kernel(xr_ref, yr_ref, cw_ref, cb_ref, wg_ref, bg_ref, lam_ref, o_ref,
                xe_ref, af_ref, uf_ref, ab_ref, ub_ref, *, rows):
    S, ct = xr_ref.shape
    lc = S // TIME_CHUNKS
    halo = SUBLANES

    ng = ct // LANES

    def xe_rows(r0, n):
        return jnp.concatenate([xe_ref[g, pl.ds(r0, n), :] for g in range(ng)], axis=1)

    def xe_store(r0, n, val):
        for g in range(ng):
            xe_ref[g, pl.ds(r0, n), :] = val[:, g * LANES:(g + 1) * LANES]

    for s in range(TIME_CHUNKS):
        for g in range(ng):
            xe_ref[g, pl.ds(halo + s, lc, stride=SUBLANES), :] = (
                xr_ref[pl.ds(s * lc, lc), g * LANES:(g + 1) * LANES])
    xe_store(0, halo, _shift_chunks(xe_rows(S, halo), True))
    first = xe_rows(halo, halo)
    second = xe_rows(2 * halo, halo)
    xe_store(S + halo, halo, _shift_chunks(first, False))
    xe_store(S + 2 * halo, halo, _shift_chunks(second, False))

    cw = cw_ref[...]
    cb = cb_ref[...]
    lam = lam_ref[...]
    log_sig = jnp.minimum(lam, 0.0) - jnp.log(1.0 + jnp.exp(-jnp.abs(lam)))
    c_log = RG_C * log_sig
    wg = wg_ref[0]
    bg = bg_ref[0]

    def gate_chunk(i, _):
        r0 = pl.multiple_of(i * rows, rows)
        xc = cb
        for j in range(4):
            xc = xc + cw[j:j + 1, :] * xe_rows(r0 + j * halo, rows)
        g = jnp.dot(xc.astype(BF16), wg, preferred_element_type=F32) + bg
        for d, (a_ref, u_ref) in enumerate(((af_ref, uf_ref), (ab_ref, ub_ref))):
            r = jax.nn.sigmoid(g[:, (2 * d) * ct:(2 * d + 1) * ct])
            ig = jax.nn.sigmoid(g[:, (2 * d + 1) * ct:(2 * d + 2) * ct])
            log_a = r * c_log[d:d + 1, :]
            a = jnp.exp(log_a)
            a_ref[pl.ds(r0, rows), :] = a
            u_ref[pl.ds(r0, rows), :] = jnp.sqrt(jnp.maximum(1.0 - a * a, 0.0)) * (ig * xc)
        return 0

    lax.fori_loop(0, S // rows, gate_chunk, 0)

    unroll = 8

    def scan_dir(a_ref, u_ref, reverse):
        def body(i, carry):
            h, acc = carry
            for k in range(unroll):
                t = i * unroll + k
                t = (lc - 1 - t) if reverse else t
                r0 = pl.multiple_of(t * SUBLANES, SUBLANES)
                a = a_ref[pl.ds(r0, SUBLANES), :]
                h = a * h + u_ref[pl.ds(r0, SUBLANES), :]
                acc = a * acc
                u_ref[pl.ds(r0, SUBLANES), :] = h
                a_ref[pl.ds(r0, SUBLANES), :] = acc
            return h, acc

        h_end, a_end = lax.fori_loop(
            0, lc // unroll, body,
            (jnp.zeros((SUBLANES, ct), F32), jnp.ones((SUBLANES, ct), F32)))
        f = h_end
        for _ in range(TIME_CHUNKS - 1):
            f = h_end + a_end * _shift_chunks(f, not reverse)
        return _shift_chunks(f, not reverse)

    cin_f = scan_dir(af_ref, uf_ref, False)
    cin_b = scan_dir(ab_ref, ub_ref, True)

    def fix_chunk(i, _):
        r0 = pl.multiple_of(i * rows, rows)
        reps = rows // SUBLANES
        hf = uf_ref[pl.ds(r0, rows), :] + af_ref[pl.ds(r0, rows), :] * jnp.tile(cin_f, (reps, 1))
        hb = ub_ref[pl.ds(r0, rows), :] + ab_ref[pl.ds(r0, rows), :] * jnp.tile(cin_b, (reps, 1))
        xe_store(r0, rows, hf + hb)
        return 0

    lax.fori_loop(0, S // rows, fix_chunk, 0)

    for s in range(TIME_CHUNKS):
        hsum = jnp.concatenate(
            [xe_ref[g, pl.ds(s, lc, stride=SUBLANES), :] for g in range(ng)], axis=1)
        y = yr_ref[pl.ds(s * lc, lc), :]
        o_ref[pl.ds(s * lc, lc), :] = (hsum * jax.nn.gelu(y, approximate=True)).astype(BF16)


def _lru(proj, conv_w, conv_b, wg, bg, lam, B, S, C):
    ct = LRU_CT
    nct = C // ct
    rows = min(LRU_ROWS, S)
    return pl.pallas_call(
        functools.partial(_lru_kernel, rows=rows),
        grid=(B, nct),
        in_specs=[
            pl.BlockSpec((S, ct), lambda b, c: (b, c)),
            pl.BlockSpec((S, ct), lambda b, c: (b, nct + c)),
            pl.BlockSpec((4, ct), lambda b, c: (0, c)),
            pl.BlockSpec((1, ct), lambda b, c: (0, c)),
            pl.BlockSpec((1, ct, 4 * ct), lambda b, c: (c, 0, 0)),
            pl.BlockSpec((1, 1, 4 * ct), lambda b, c: (c, 0, 0)),
            pl.BlockSpec((2, ct), lambda b, c: (0, c)),
        ],
        out_specs=pl.BlockSpec((S, ct), lambda b, c: (b, c)),
        out_shape=jax.ShapeDtypeStruct((B * S, C), BF16),
        scratch_shapes=[
            pltpu.VMEM((ct // LANES, S + 3 * SUBLANES, LANES), F32),
            pltpu.VMEM((S, ct), F32), pltpu.VMEM((S, ct), F32),
            pltpu.VMEM((S, ct), F32), pltpu.VMEM((S, ct), F32),
        ],
        compiler_params=_cparams("parallel", "parallel"),
        name="lru",
    )(proj, proj, conv_w, conv_b, wg, bg, lam)


def _merge_kernel(x_ref, attn_ref, rec_ref, gl_ref, wa_ref, wr_ref, wo_ref, g2_ref,
                  rw_hi_ref, rw_lo_ref, rb_ref, x2_ref, h2_ref, lg_ref):
    D = x_ref.shape[1]
    gl = gl_ref[...]
    ma = jnp.dot(attn_ref[...], wa_ref[...], preferred_element_type=F32)
    mr = jnp.dot(rec_ref[...], wr_ref[...], preferred_element_type=F32)
    merged = jax.nn.sigmoid(gl[:, :D]) * ma + jax.nn.sigmoid(gl[:, D:]) * mr
    x2 = x_ref[...] + jnp.dot(merged.astype(BF16), wo_ref[...], preferred_element_type=F32)
    x2_ref[...] = x2
    h2 = _rms(x2, g2_ref[...])
    hi = h2.astype(BF16)
    lo = (h2 - hi.astype(F32)).astype(BF16)
    lg_ref[...] = (jnp.dot(hi, rw_hi_ref[...], preferred_element_type=F32)
                   + jnp.dot(lo, rw_hi_ref[...], preferred_element_type=F32)
                   + jnp.dot(hi, rw_lo_ref[...], preferred_element_type=F32)) + rb_ref[...]
    tm = h2.shape[0]
    for j in range(D // LANES):
        h2_ref[pl.ds(j, tm, stride=D // LANES), :] = h2[:, j * LANES:(j + 1) * LANES]


def _merge(x2d, attn, rec, proj, wa, wr, wo, g2, rw_hi, rw_lo, rb):
    T, D = x2d.shape
    tm = min(ROW_TILE, T)
    nj = D // LANES
    full = lambda a: pl.BlockSpec(a.shape, lambda i: (0,) * a.ndim)
    return pl.pallas_call(
        _merge_kernel,
        grid=(T // tm,),
        in_specs=[
            pl.BlockSpec((tm, D), lambda i: (i, 0)),
            pl.BlockSpec((tm, Q_WIDTH), lambda i: (i, 0)),
            pl.BlockSpec((tm, D), lambda i: (i, 0)),
            pl.BlockSpec((tm, 2 * D), lambda i: (i, 1)),
            full(wa), full(wr), full(wo), full(g2), full(rw_hi), full(rw_lo), full(rb),
        ],
        out_specs=[
            pl.BlockSpec((tm, D), lambda i: (i, 0)),
            pl.BlockSpec((tm * nj, LANES), lambda i: (i, 0)),
            pl.BlockSpec((tm, LANES), lambda i: (i, 0)),
        ],
        out_shape=[
            jax.ShapeDtypeStruct((T, D), F32),
            jax.ShapeDtypeStruct((T * nj, LANES), F32),
            jax.ShapeDtypeStruct((T, LANES), F32),
        ],
        compiler_params=_cparams("parallel"),
        name="merge",
    )(x2d, attn, rec, proj, wa, wr, wo, g2, rw_hi, rw_lo, rb)


def _route_kernel(lg_ref, gate_ref, eidx_ref, rank_ref, cnt_ref):
    i = pl.program_id(0)

    @pl.when(i == 0)
    def _():
        cnt_ref[...] = jnp.zeros_like(cnt_ref)

    lg = lg_ref[...]
    tm = lg.shape[0]
    lane = lax.broadcasted_iota(I32, lg.shape, 1)
    work = lg
    sels, vals, idxs = [], [], []
    for _ in range(TOP_K):
        mx = jnp.max(work, axis=-1, keepdims=True)
        idx = jnp.min(jnp.where(work == mx, lane, LANES), axis=-1, keepdims=True)
        sel = lane == idx
        sels.append(sel)
        vals.append(mx)
        idxs.append(idx)
        work = jnp.where(sel, NEG_BIG * 2.0, work)
    ex = [jnp.exp(v - vals[0]) for v in vals]
    den = ex[0] + ex[1] + ex[2] + ex[3]
    member = (sels[0] | sels[1] | sels[2] | sels[3]).astype(BF16)
    r = lax.broadcasted_iota(I32, (tm, tm), 0)
    c = lax.broadcasted_iota(I32, (tm, tm), 1)
    tri = (c < r).astype(BF16)
    base = cnt_ref[0:1, :]
    before = jnp.dot(tri, member, preferred_element_type=F32) + base
    gate = jnp.zeros(lg.shape, F32)
    eidx = jnp.zeros(lg.shape, I32)
    rank = jnp.zeros(lg.shape, F32)
    for k in range(TOP_K):
        rk = jnp.sum(jnp.where(sels[k], before, 0.0), axis=-1, keepdims=True)
        gate = jnp.where(lane == k, ex[k] / den, gate)
        eidx = jnp.where(lane == k, idxs[k], eidx)
        rank = jnp.where(lane == k, rk, rank)
    gate_ref[...] = gate
    eidx_ref[...] = eidx
    rank_ref[...] = rank.astype(I32)
    cnt_ref[...] = jnp.broadcast_to(
        base + jnp.sum(member.astype(F32), axis=0, keepdims=True), cnt_ref.shape)


def _route(logits):
    T = logits.shape[0]
    tm = min(ROW_TILE, T)
    spec = pl.BlockSpec((tm, LANES), lambda i: (i, 0))
    return pl.pallas_call(
        _route_kernel,
        grid=(T // tm,),
        in_specs=[spec],
        out_specs=[spec, spec, spec, pl.BlockSpec((SUBLANES, LANES), lambda i: (0, 0))],
        out_shape=[
            jax.ShapeDtypeStruct((T, LANES), F32),
            jax.ShapeDtypeStruct((T, LANES), I32),
            jax.ShapeDtypeStruct((T, LANES), I32),
            jax.ShapeDtypeStruct((SUBLANES, LANES), F32),
        ],
        compiler_params=_cparams("arbitrary"),
        name="route",
    )(logits)


def _dispatch_kernel(dest_ref, h2_ref, buf_in_ref, buf_ref, sem):
    del buf_in_ref
    tm = h2_ref.shape[0]

    def issue(r, _):
        for k in range(TOP_K):
            d = dest_ref[r * TOP_K + k]
            pltpu.make_async_copy(h2_ref.at[r], buf_ref.at[d], sem).start()
        return 0

    lax.fori_loop(0, tm, issue, 0)

    def drain(r, _):
        for k in range(TOP_K):
            pltpu.make_async_copy(h2_ref.at[0], buf_ref.at[0], sem).wait()
        return 0

    lax.fori_loop(0, tm, drain, 0)


def _dispatch(dest_flat, h2_tiles, buf0):
    T = h2_tiles.shape[0]
    tm = min(DMA_TILE, T)
    return pl.pallas_call(
        _dispatch_kernel,
        grid=(T // tm,),
        in_specs=[
            pl.BlockSpec((tm * TOP_K,), lambda i: (i,), memory_space=pltpu.SMEM),
            pl.BlockSpec((tm,) + h2_tiles.shape[1:], lambda i: (i, 0, 0)),
            pl.BlockSpec(memory_space=pl.ANY),
        ],
        out_specs=pl.BlockSpec(memory_space=pl.ANY),
        out_shape=jax.ShapeDtypeStruct(buf0.shape, buf0.dtype),
        scratch_shapes=[pltpu.SemaphoreType.DMA(())],
        input_output_aliases={2: 0},
        compiler_params=_cparams("arbitrary"),
        name="dispatch",
    )(dest_flat, h2_tiles, buf0)


def _ffn_kernel(blk_exp_ref, nused_ref, x_ref, wgu_ref, bgu_ref, wd_ref, bd_ref, o_ref):
    b = pl.program_id(0)
    tm = o_ref.shape[0] // SUBLANES
    nj = SUBLANES
    F = wd_ref.shape[1]

    @pl.when(b < nused_ref[0])
    def _():
        x = jnp.concatenate(
            [x_ref[pl.ds(j, tm, stride=nj), :] for j in range(nj)], axis=1).astype(BF16)
        gu = jnp.dot(x, wgu_ref[0], preferred_element_type=F32) + bgu_ref[0]
        g = jnp.minimum(gu[:, :F], SWIGLU_LIMIT)
        u = jnp.clip(gu[:, F:], -SWIGLU_LIMIT, SWIGLU_LIMIT)
        act = g * jax.nn.sigmoid(SWIGLU_ALPHA * g) * (u + 1.0)
        y = jnp.dot(act.astype(BF16), wd_ref[0], preferred_element_type=F32) + bd_ref[0]
        for j in range(nj):
            o_ref[pl.ds(j, tm, stride=nj), :] = y[:, j * LANES:(j + 1) * LANES]

    @pl.when(b >= nused_ref[0])
    def _():
        o_ref[...] = jnp.zeros_like(o_ref)


def _ffn(blk_exp, nused, buf2d, wgu, bgu, wd, bd, tm):
    rows = buf2d.shape[0]
    nb = rows // (tm * SUBLANES)
    D = wgu.shape[1]
    F = wd.shape[1]

    def xmap(b, be, nu):
        return (jnp.minimum(b, nu[0] - 1), 0)

    def wmap(b, be, nu):
        return (be[jnp.minimum(b, nu[0] - 1)], 0, 0)

    grid_spec = pltpu.PrefetchScalarGridSpec(
        num_scalar_prefetch=2,
        grid=(nb,),
        in_specs=[
            pl.BlockSpec((tm * SUBLANES, LANES), xmap),
            pl.BlockSpec((1, D, 2 * F), wmap),
            pl.BlockSpec((1, 1, 2 * F), wmap),
            pl.BlockSpec((1, F, D), wmap),
            pl.BlockSpec((1, 1, D), wmap),
        ],
        out_specs=pl.BlockSpec((tm * SUBLANES, LANES), lambda b, be, nu: (b, 0)),
    )
    return pl.pallas_call(
        _ffn_kernel,
        grid_spec=grid_spec,
        out_shape=jax.ShapeDtypeStruct((rows, LANES), F32),
        compiler_params=_cparams("arbitrary"),
        name="ffn",
    )(blk_exp, nused, buf2d, wgu, bgu, wd, bd)


def _combine_kernel(dest_ref, gate_ref, x2_ref, fg_ref, obuf_ref, o_ref, rows_ref, sem):
    tm, D = x2_ref.shape
    nj = D // LANES

    def row_slot(k, r):
        return rows_ref.at[pl.ds(pl.multiple_of((k * tm + r) * nj, nj), nj)]

    def issue(r, _):
        for k in range(TOP_K):
            d = dest_ref[r * TOP_K + k]
            pltpu.make_async_copy(obuf_ref.at[d], row_slot(k, r), sem).start()
        return 0

    lax.fori_loop(0, tm, issue, 0)

    def drain(r, _):
        for k in range(TOP_K):
            pltpu.make_async_copy(obuf_ref.at[0], row_slot(0, 0), sem).wait()
        return 0

    lax.fori_loop(0, tm, drain, 0)

    gate = gate_ref[...]
    cols = []
    for j in range(nj):
        acc = None
        for k in range(TOP_K):
            part = gate[:, k:k + 1] * rows_ref[pl.ds(k * tm * nj + j, tm, stride=nj), :]
            acc = part if acc is None else acc + part
        cols.append(acc)
    y = jnp.concatenate(cols, axis=1)
    o_ref[...] = _rms(x2_ref[...] + y, fg_ref[...])


def _combine(dest_flat, gate4, x2, fg, obuf_tiles):
    T, D = x2.shape
    tm = min(DMA_TILE, T)
    return pl.pallas_call(
        _combine_kernel,
        grid=(T // tm,),
        in_specs=[
            pl.BlockSpec((tm * TOP_K,), lambda i: (i,), memory_space=pltpu.SMEM),
            pl.BlockSpec((tm, LANES), lambda i: (i, 0)),
            pl.BlockSpec((tm, D), lambda i: (i, 0)),
            pl.BlockSpec((1, D), lambda i: (0, 0)),
            pl.BlockSpec(memory_space=pl.ANY),
        ],
        out_specs=pl.BlockSpec((tm, D), lambda i: (i, 0)),
        out_shape=jax.ShapeDtypeStruct((T, D), F32),
        scratch_shapes=[
            pltpu.VMEM((TOP_K * tm * (D // LANES), LANES), F32),
            pltpu.SemaphoreType.DMA(()),
        ],
        compiler_params=_cparams("arbitrary"),
        name="combine",
    )(dest_flat, gate4, x2, fg, obuf_tiles)


def _rope_tables(S):
    rows = S // GRID_W
    row = jnp.repeat(jnp.arange(rows, dtype=I32), GRID_W).astype(F32)
    col = jnp.tile(jnp.arange(GRID_W, dtype=I32), rows).astype(F32)
    inv = ROPE_THETA ** (-jnp.arange(0, AXIS_DIM, 2, dtype=F32) / AXIS_DIM)
    ang_r = row[:, None] * inv[None, :]
    ang_c = col[:, None] * inv[None, :]
    cos = jnp.concatenate([jnp.cos(ang_r)] * 2 + [jnp.cos(ang_c)] * 2, axis=1)
    sin = jnp.concatenate([-jnp.sin(ang_r), jnp.sin(ang_r), -jnp.sin(ang_c), jnp.sin(ang_c)], axis=1)
    return jnp.tile(cos, (1, 2)), jnp.tile(sin, (1, 2))


def _block_diag_gates(wa, ba, wi, bi, ct):
    nb, bw = wa.shape[1], wa.shape[2]
    per = ct // bw
    nct = nb // per
    eye = jnp.eye(per, dtype=wa.dtype)

    def tiles(w):
        w = w.reshape(nct, per, bw, bw)
        return jnp.einsum('cpij,pq->cpiqj', w, eye).reshape(nct, ct, ct)

    wg = jnp.concatenate([tiles(wa[0]), tiles(wi[0]), tiles(wa[1]), tiles(wi[1])], axis=2)
    bias = lambda b: b.reshape(nct, 1, ct)
    bg = jnp.concatenate([bias(ba[0]), bias(bi[0]), bias(ba[1]), bias(bi[1])], axis=2)
    return wg.astype(BF16), bg


def _layer(x2d, B, S, norm1_g, w_in, b_in, q_norm_g, k_norm_g, conv_w, conv_b, lru_wa, lru_ba,
           lru_wi, lru_bi, lru_lambda, w_attn_o, w_lru_o, w_out, norm2_g, w_router, b_router,
           w_gu, b_gu, w_down, b_down, out_g):
    T, D = x2d.shape
    C = conv_w.shape[1]
    nqkv = Q_WIDTH + 2 * KV_WIDTH
    row2 = lambda v: v.reshape(1, -1)

    cos_t, sin_t = _rope_tables(S)
    head = jnp.arange(LANES) // HEAD_DIM
    hsum = (head[:, None] == head[None, :]).astype(BF16)
    q, k, v = _qkv(x2d, row2(norm1_g), w_in[:, :nqkv].astype(BF16), row2(b_in[:nqkv]),
                   cos_t, sin_t, row2(jnp.tile(q_norm_g, 2)), row2(jnp.tile(k_norm_g, 2)),
                   hsum, B, S)
    proj = _proj(x2d, row2(norm1_g), w_in[:, nqkv:].astype(BF16), row2(b_in[nqkv:]))
    attn = _attn(q, k, v)
    wg, bg = _block_diag_gates(lru_wa, lru_ba, lru_wi, lru_bi, LRU_CT)
    rec = _lru(proj, conv_w, row2(conv_b), wg, bg, lru_lambda, B, S, C)

    pad = LANES - N_EXPERTS
    rw = jnp.pad(w_router, ((0, 0), (0, pad)))
    rw_hi = rw.astype(BF16)
    rw_lo = (rw - rw_hi.astype(F32)).astype(BF16)
    rb = jnp.pad(b_router, (0, pad), constant_values=NEG_BIG).reshape(1, LANES)
    x2, h2_rows, logits = _merge(x2d, attn, rec, proj, w_attn_o.astype(BF16),
                                 w_lru_o.astype(BF16), w_out.astype(BF16), row2(norm2_g),
                                 rw_hi, rw_lo, rb)

    gate4, eidx4, rank4, cnt = _route(logits)
    counts = cnt[0, :N_EXPERTS].astype(I32)
    tm = FFN_TM
    nblk = (counts + tm - 1) // tm
    pend_blk = jnp.cumsum(nblk)
    pstart = (pend_blk - nblk) * tm
    A = T * TOP_K
    nb = (A + N_EXPERTS * (tm - 1) + tm - 1) // tm
    blk_exp = jnp.minimum(
        jnp.searchsorted(pend_blk, jnp.arange(nb, dtype=I32), side='right'),
        N_EXPERTS - 1).astype(I32)
    nused = pend_blk[-1:].astype(I32)
    dest = (pstart[eidx4[:, :TOP_K]] + rank4[:, :TOP_K]).astype(I32).reshape(A)

    nj = D // LANES
    buf0 = jnp.zeros((nb * tm, nj, LANES), F32)
    buf = _dispatch(dest, h2_rows.reshape(T, nj, LANES), buf0)
    obuf = _ffn(blk_exp, nused, buf.reshape(nb * tm * nj, LANES), w_gu.astype(BF16),
                b_gu.reshape(N_EXPERTS, 1, -1), w_down.astype(BF16),
                b_down.reshape(N_EXPERTS, 1, -1), tm)
    return _combine(dest, gate4, x2, row2(out_g), obuf.reshape(nb * tm, nj, LANES))


def kernel(x, norm1_g, w_in, b_in, q_norm_g, k_norm_g, conv_w, conv_b, lru_wa, lru_ba, lru_wi,
           lru_bi, lru_lambda, w_attn_o, w_lru_o, w_out, norm2_g, w_router, b_router, w_gu, b_gu,
           w_down, b_down, final_g):
    B, S, D = x.shape
    depth = norm1_g.shape[0]
    assert depth == 1, "the fused final RMSNorm assumes a single layer"
    assert S % (TIME_CHUNKS * SUBLANES) == 0 and S % GRID_W == 0
    out = _layer(x.reshape(B * S, D), B, S, norm1_g[0], w_in[0], b_in[0], q_norm_g[0],
                 k_norm_g[0], conv_w[0], conv_b[0], lru_wa[0], lru_ba[0], lru_wi[0], lru_bi[0],
                 lru_lambda[0], w_attn_o[0], w_lru_o[0], w_out[0], norm2_g[0], w_router[0],
                 b_router[0], w_gu[0], b_gu[0], w_down[0], b_down[0], final_g)
    return out.reshape(B, S, D)
```

```python
import functools
import math

import jax
import jax.numpy as jnp
from jax import lax
from jax.experimental import pallas as pl
from jax.experimental.pallas import tpu as pltpu

F32 = jnp.float32
BF16 = jnp.bfloat16
I32 = jnp.int32

LANES = 128
SUBLANES = 8
VMEM_LIMIT_BYTES = 56 * 1024 * 1024

HEAD_DIM = 64
N_Q_HEADS = 8
N_KV_HEADS = 2
Q_GROUP = N_Q_HEADS // N_KV_HEADS
Q_WIDTH = N_Q_HEADS * HEAD_DIM
KV_WIDTH = N_KV_HEADS * HEAD_DIM
AXIS_DIM = HEAD_DIM // 2
ROT_HALF = AXIS_DIM // 2
ROPE_THETA = 10000.0
GRID_W = 64
LRU_BLOCKS = 16
RG_C = 8.0
N_EXPERTS = 32
TOP_K = 4
SWIGLU_LIMIT = 7.0
SWIGLU_ALPHA = 1.702
NORM_EPS = 1e-6
LOG2E = 1.4426950408889634
NEG_BIG = -1e30

ROW_TILE = 512
ATT_TQ = 128
ATT_TK = 512
LRU_CT = 256
LRU_ROWS = 256
TIME_CHUNKS = SUBLANES
FFN_TM = 512
DMA_TILE = 256


def _cparams(*sem):
    return pltpu.CompilerParams(dimension_semantics=sem, vmem_limit_bytes=VMEM_LIMIT_BYTES)


def _rms(x, g):
    return x * lax.rsqrt(jnp.mean(x * x, axis=-1, keepdims=True) + NORM_EPS) * g


def _qkv_kernel(x_ref, g1_ref, w_ref, b_ref, cos_ref, sin_ref, qg_ref, kg_ref, hsum_ref,
                q_ref, k_ref, v_ref):
    h = _rms(x_ref[...], g1_ref[...]).astype(BF16)
    p = jnp.dot(h, w_ref[...], preferred_element_type=F32) + b_ref[...]
    cos = cos_ref[...]
    sin = sin_ref[...]
    hsum = hsum_ref[...]
    lane = lax.broadcasted_iota(I32, cos.shape, 1)
    first_half = (lane % AXIS_DIM) < ROT_HALF

    def norm_rope(c, gain):
        sq = c * c
        hi = sq.astype(BF16)
        lo = (sq - hi.astype(F32)).astype(BF16)
        ms = (jnp.dot(hi, hsum, preferred_element_type=F32)
              + jnp.dot(lo, hsum, preferred_element_type=F32)) * (1.0 / HEAD_DIM)
        y = c * lax.rsqrt(ms + NORM_EPS) * gain
        partner = jnp.where(first_half, pltpu.roll(y, LANES - ROT_HALF, 1),
                            pltpu.roll(y, ROT_HALF, 1))
        return y * cos + partner * sin

    qg = qg_ref[...]
    for c in range(Q_WIDTH // LANES):
        y = norm_rope(p[:, c * LANES:(c + 1) * LANES], qg) * (HEAD_DIM ** -0.5 * LOG2E)
        yt = y.T.astype(BF16)
        q_ref[0, 2 * c] = yt[:HEAD_DIM]
        q_ref[0, 2 * c + 1] = yt[HEAD_DIM:]
    yk = norm_rope(p[:, Q_WIDTH:Q_WIDTH + KV_WIDTH], kg_ref[...])
    k_ref[0, 0] = yk[:, :HEAD_DIM].astype(BF16)
    k_ref[0, 1] = yk[:, HEAD_DIM:].astype(BF16)
    vt = p[:, Q_WIDTH + KV_WIDTH:].T
    tm = vt.shape[1]
    ones_row = (lax.broadcasted_iota(I32, (HEAD_DIM, tm), 0) == 0).astype(BF16)
    v_ref[0, 0] = jnp.concatenate([vt[:HEAD_DIM].astype(BF16), ones_row], axis=0)
    v_ref[0, 1] = jnp.concatenate([vt[HEAD_DIM:].astype(BF16), ones_row], axis=0)


def _qkv(x2d, g1, w_qkv, b_qkv, cos_t, sin_t, qg, kg, hsum, B, S):
    T, D = x2d.shape
    tm = min(ROW_TILE, S)
    ns = S // tm
    n = w_qkv.shape[1]
    full = lambda shape: pl.BlockSpec(shape, lambda i: (0,) * len(shape))
    return pl.pallas_call(
        _qkv_kernel,
        grid=(T // tm,),
        in_specs=[
            pl.BlockSpec((tm, D), lambda i: (i, 0)),
            full((1, D)), full((D, n)), full((1, n)),
            pl.BlockSpec((tm, LANES), lambda i: (i % ns, 0)),
            pl.BlockSpec((tm, LANES), lambda i: (i % ns, 0)),
            full((1, LANES)), full((1, LANES)), full((LANES, LANES)),
        ],
        out_specs=[
            pl.BlockSpec((1, N_Q_HEADS, HEAD_DIM, tm), lambda i: (i // ns, 0, 0, i % ns)),
            pl.BlockSpec((1, N_KV_HEADS, tm, HEAD_DIM), lambda i: (i // ns, 0, i % ns, 0)),
            pl.BlockSpec((1, N_KV_HEADS, LANES, tm), lambda i: (i // ns, 0, 0, i % ns)),
        ],
        out_shape=[
            jax.ShapeDtypeStruct((B, N_Q_HEADS, HEAD_DIM, S), BF16),
            jax.ShapeDtypeStruct((B, N_KV_HEADS, S, HEAD_DIM), BF16),
            jax.ShapeDtypeStruct((B, N_KV_HEADS, LANES, S), BF16),
        ],
        compiler_params=_cparams("parallel"),
        name="qkv",
    )(x2d, g1, w_qkv, b_qkv, cos_t, sin_t, qg, kg, hsum)


def _proj_kernel(x_ref, g_ref, w_ref, b_ref, o_ref):
    h = _rms(x_ref[...], g_ref[...]).astype(BF16)
    o_ref[...] = jnp.dot(h, w_ref[...], preferred_element_type=F32) + b_ref[...]


def _proj(x2d, g, w, b):
    T, D = x2d.shape
    n = w.shape[1]
    tm = min(ROW_TILE, T)
    return pl.pallas_call(
        _proj_kernel,
        grid=(T // tm,),
        in_specs=[
            pl.BlockSpec((tm, D), lambda i: (i, 0)),
            pl.BlockSpec((1, D), lambda i: (0, 0)),
            pl.BlockSpec((D, n), lambda i: (0, 0)),
            pl.BlockSpec((1, n), lambda i: (0, 0)),
        ],
        out_specs=pl.BlockSpec((tm, n), lambda i: (i, 0)),
        out_shape=jax.ShapeDtypeStruct((T, n), F32),
        compiler_params=_cparams("parallel"),
        name="proj",
    )(x2d, g, w, b)


def _attn_kernel(q_ref, k_ref, v_ref, o_ref, s0_ref, s1_ref, *, tk):
    tq = q_ref.shape[3]
    S = k_ref.shape[2]
    M = Q_GROUP * tq
    n = S // tk
    qT = jnp.concatenate([q_ref[0, h] for h in range(Q_GROUP)], axis=1)

    def scores(j, s_ref):
        off = pl.multiple_of(j * tk, tk)
        s_ref[...] = jnp.dot(k_ref[0, 0, pl.ds(off, tk), :], qT, preferred_element_type=F32)

    def absorb(j, s_ref, carry):
        m, acc = carry
        off = pl.multiple_of(j * tk, tk)
        vc = v_ref[0, 0, :, pl.ds(off, tk)]
        s = s_ref[...]
        m_new = jnp.maximum(m, jnp.max(s, axis=0, keepdims=True))
        alpha = jnp.exp2(m - m_new)
        p = jnp.exp2(s - m_new).astype(BF16)
        return m_new, alpha * acc + jnp.dot(vc, p, preferred_element_type=F32)

    def pair(i, carry):
        scores(2 * i + 1, s1_ref)
        carry = absorb(2 * i, s0_ref, carry)
        scores(2 * i + 2, s0_ref)
        return absorb(2 * i + 1, s1_ref, carry)

    scores(0, s0_ref)
    carry = (jnp.full((1, M), NEG_BIG, F32), jnp.zeros((LANES, M), F32))
    carry = lax.fori_loop(0, n // 2 - 1, pair, carry)
    scores(n - 1, s1_ref)
    carry = absorb(n - 2, s0_ref, carry)
    _, acc = absorb(n - 1, s1_ref, carry)
    outT = acc[:HEAD_DIM] / acc[HEAD_DIM:HEAD_DIM + 1]
    stacked = jnp.concatenate(
        [outT[:, g * tq:(g + 1) * tq] for g in range(Q_GROUP)], axis=0)
    o_ref[...] = stacked.T.astype(BF16)


def _attn(qT, k, vT):
    B, _, _, S = qT.shape
    tq = min(ATT_TQ, S)
    tk = min(ATT_TK, S)
    nq = S // tq
    assert (S // tk) % 2 == 0, "key chunks are processed in pairs"
    score_buf = pltpu.VMEM((tk, Q_GROUP * tq), F32)
    return pl.pallas_call(
        functools.partial(_attn_kernel, tk=tk),
        grid=(B, N_KV_HEADS, nq),
        in_specs=[
            pl.BlockSpec((1, Q_GROUP, HEAD_DIM, tq), lambda b, g, i: (b, g, 0, i)),
            pl.BlockSpec((1, 1, S, HEAD_DIM), lambda b, g, i: (b, g, 0, 0)),
            pl.BlockSpec((1, 1, LANES, S), lambda b, g, i: (b, g, 0, 0)),
        ],
        out_specs=pl.BlockSpec((tq, Q_GROUP * HEAD_DIM), lambda b, g, i: (b * nq + i, g)),
        out_shape=jax.ShapeDtypeStruct((B * S, Q_WIDTH), BF16),
        scratch_shapes=[score_buf, score_buf],
        compiler_params=_cparams("parallel", "parallel", "parallel"),
        name="attn",
    )(qT, k, vT)


def _shift_chunks(v, down):
    row = lax.broadcasted_iota(I32, v.shape, 0)
    if down:
        return jnp.where(row == 0, 0.0, pltpu.roll(v, 1, 0))
    return jnp.where(row == SUBLANES - 1, 0.0, pltpu.roll(v, SUBLANES - 1, 0))


def _lru_kernel(xr_ref, yr_ref, cw_ref, cb_ref, wg_ref, bg_ref, lam_ref, o_ref,
                xe_ref, af_ref, uf_ref, ab_ref, ub_ref, *, rows):
    S, ct = xr_ref.shape
    lc = S // TIME_CHUNKS
    halo = SUBLANES

    ng = ct // LANES

    def xe_rows(r0, n):
        return jnp.concatenate([xe_ref[g, pl.ds(r0, n), :] for g in range(ng)], axis=1)

    def xe_store(r0, n, val):
        for g in range(ng):
            xe_ref[g, pl.ds(r0, n), :] = val[:, g * LANES:(g + 1) * LANES]

    for s in range(TIME_CHUNKS):
        for g in range(ng):
            xe_ref[g, pl.ds(halo + s, lc, stride=SUBLANES), :] = (
                xr_ref[pl.ds(s * lc, lc), g * LANES:(g + 1) * LANES])
    xe_store(0, halo, _shift_chunks(xe_rows(S, halo), True))
    first = xe_rows(halo, halo)
    second = xe_rows(2 * halo, halo)
    xe_store(S + halo, halo, _shift_chunks(first, False))
    xe_store(S + 2 * halo, halo, _shift_chunks(second, False))

    cw = cw_ref[...]
    cb = cb_ref[...]
    lam = lam_ref[...]
    log_sig = jnp.minimum(lam, 0.0) - jnp.log(1.0 + jnp.exp(-jnp.abs(lam)))
    c_log = RG_C * log_sig
    wg = wg_ref[0]
    bg = bg_ref[0]

    def gate_chunk(i, _):
        r0 = pl.multiple_of(i * rows, rows)
        xc = cb
        for j in range(4):
            xc = xc + cw[j:j + 1, :] * xe_rows(r0 + j * halo, rows)
        g = jnp.dot(xc.astype(BF16), wg, preferred_element_type=F32) + bg
        for d, (a_ref, u_ref) in enumerate(((af_ref, uf_ref), (ab_ref, ub_ref))):
            r = jax.nn.sigmoid(g[:, (2 * d) * ct:(2 * d + 1) * ct])
            ig = jax.nn.sigmoid(g[:, (2 * d + 1) * ct:(2 * d + 2) * ct])
            log_a = r * c_log[d:d + 1, :]
            a = jnp.exp(log_a)
            a_ref[pl.ds(r0, rows), :] = a
            u_ref[pl.ds(r0, rows), :] = jnp.sqrt(jnp.maximum(1.0 - a * a, 0.0)) * (ig * xc)
        return 0

    lax.fori_loop(0, S // rows, gate_chunk, 0)

    unroll = 8

    def scan_dir(a_ref, u_ref, reverse):
        def body(i, carry):
            h, acc = carry
            for k in range(unroll):
                t = i * unroll + k
                t = (lc - 1 - t) if reverse else t
                r0 = pl.multiple_of(t * SUBLANES, SUBLANES)
                a = a_ref[pl.ds(r0, SUBLANES), :]
                h = a * h + u_ref[pl.ds(r0, SUBLANES), :]
                acc = a * acc
                u_ref[pl.ds(r0, SUBLANES), :] = h
                a_ref[pl.ds(r0, SUBLANES), :] = acc
            return h, acc

        h_end, a_end = lax.fori_loop(
            0, lc // unroll, body,
            (jnp.zeros((SUBLANES, ct), F32), jnp.ones((SUBLANES, ct), F32)))
        f = h_end
        for _ in range(TIME_CHUNKS - 1):
            f = h_end + a_end * _shift_chunks(f, not reverse)
        return _shift_chunks(f, not reverse)

    cin_f = scan_dir(af_ref, uf_ref, False)
    cin_b = scan_dir(ab_ref, ub_ref, True)

    def fix_chunk(i, _):
        r0 = pl.multiple_of(i * rows, rows)
        reps = rows // SUBLANES
        hf = uf_ref[pl.ds(r0, rows), :] + af_ref[pl.ds(r0, rows), :] * jnp.tile(cin_f, (reps, 1))
        hb = ub_ref[pl.ds(r0, rows), :] + ab_ref[pl.ds(r0, rows), :] * jnp.tile(cin_b, (reps, 1))
        xe_store(r0, rows, hf + hb)
        return 0

    lax.fori_loop(0, S // rows, fix_chunk, 0)

    for s in range(TIME_CHUNKS):
        hsum = jnp.concatenate(
            [xe_ref[g, pl.ds(s, lc, stride=SUBLANES), :] for g in range(ng)], axis=1)
        y = yr_ref[pl.ds(s * lc, lc), :]
        o_ref[pl.ds(s * lc, lc), :] = (hsum * jax.nn.gelu(y, approximate=True)).astype(BF16)


def _lru(proj, conv_w, conv_b, wg, bg, lam, B, S, C):
    ct = LRU_CT
    nct = C // ct
    rows = min(LRU_ROWS, S)
    return pl.pallas_call(
        functools.partial(_lru_kernel, rows=rows),
        grid=(B, nct),
        in_specs=[
            pl.BlockSpec((S, ct), lambda b, c: (b, c)),
            pl.BlockSpec((S, ct), lambda b, c: (b, nct + c)),
            pl.BlockSpec((4, ct), lambda b, c: (0, c)),
            pl.BlockSpec((1, ct), lambda b, c: (0, c)),
            pl.BlockSpec((1, ct, 4 * ct), lambda b, c: (c, 0, 0)),
            pl.BlockSpec((1, 1, 4 * ct), lambda b, c: (c, 0, 0)),
            pl.BlockSpec((2, ct), lambda b, c: (0, c)),
        ],
        out_specs=pl.BlockSpec((S, ct), lambda b, c: (b, c)),
        out_shape=jax.ShapeDtypeStruct((B * S, C), BF16),
        scratch_shapes=[
            pltpu.VMEM((ct // LANES, S + 3 * SUBLANES, LANES), F32),
            pltpu.VMEM((S, ct), F32), pltpu.VMEM((S, ct), F32),
            pltpu.VMEM((S, ct), F32), pltpu.VMEM((S, ct), F32),
        ],
        compiler_params=_cparams("parallel", "parallel"),
        name="lru",
    )(proj, proj, conv_w, conv_b, wg, bg, lam)


def _merge_kernel(x_ref, attn_ref, rec_ref, gl_ref, wa_ref, wr_ref, wo_ref, g2_ref,
                  rw_hi_ref, rw_lo_ref, rb_ref, x2_ref, h2_ref, lg_ref):
    D = x_ref.shape[1]
    gl = gl_ref[...]
    ma = jnp.dot(attn_ref[...], wa_ref[...], preferred_element_type=F32)
    mr = jnp.dot(rec_ref[...], wr_ref[...], preferred_element_type=F32)
    merged = jax.nn.sigmoid(gl[:, :D]) * ma + jax.nn.sigmoid(gl[:, D:]) * mr
    x2 = x_ref[...] + jnp.dot(merged.astype(BF16), wo_ref[...], preferred_element_type=F32)
    x2_ref[...] = x2
    h2 = _rms(x2, g2_ref[...])
    hi = h2.astype(BF16)
    lo = (h2 - hi.astype(F32)).astype(BF16)
    lg_ref[...] = (jnp.dot(hi, rw_hi_ref[...], preferred_element_type=F32)
                   + jnp.dot(lo, rw_hi_ref[...], preferred_element_type=F32)
                   + jnp.dot(hi, rw_lo_ref[...], preferred_element_type=F32)) + rb_ref[...]
    tm = h2.shape[0]
    for j in range(D // LANES):
        h2_ref[pl.ds(j, tm, stride=D // LANES), :] = h2[:, j * LANES:(j + 1) * LANES]


def _merge(x2d, attn, rec, proj, wa, wr, wo, g2, rw_hi, rw_lo, rb):
    T, D = x2d.shape
    tm = min(ROW_TILE, T)
    nj = D // LANES
    full = lambda a: pl.BlockSpec(a.shape, lambda i: (0,) * a.ndim)
    return pl.pallas_call(
        _merge_kernel,
        grid=(T // tm,),
        in_specs=[
            pl.BlockSpec((tm, D), lambda i: (i, 0)),
            pl.BlockSpec((tm, Q_WIDTH), lambda i: (i, 0)),
            pl.BlockSpec((tm, D), lambda i: (i, 0)),
            pl.BlockSpec((tm, 2 * D), lambda i: (i, 1)),
            full(wa), full(wr), full(wo), full(g2), full(rw_hi), full(rw_lo), full(rb),
        ],
        out_specs=[
            pl.BlockSpec((tm, D), lambda i: (i, 0)),
            pl.BlockSpec((tm * nj, LANES), lambda i: (i, 0)),
            pl.BlockSpec((tm, LANES), lambda i: (i, 0)),
        ],
        out_shape=[
            jax.ShapeDtypeStruct((T, D), F32),
            jax.ShapeDtypeStruct((T * nj, LANES), F32),
            jax.ShapeDtypeStruct((T, LANES), F32),
        ],
        compiler_params=_cparams("parallel"),
        name="merge",
    )(x2d, attn, rec, proj, wa, wr, wo, g2, rw_hi, rw_lo, rb)


def _route_kernel(lg_ref, gate_ref, eidx_ref, rank_ref, cnt_ref):
    i = pl.program_id(0)

    @pl.when(i == 0)
    def _():
        cnt_ref[...] = jnp.zeros_like(cnt_ref)

    lg = lg_ref[...]
    tm = lg.shape[0]
    lane = lax.broadcasted_iota(I32, lg.shape, 1)
    work = lg
    sels, vals, idxs = [], [], []
    for _ in range(TOP_K):
        mx = jnp.max(work, axis=-1, keepdims=True)
        idx = jnp.min(jnp.where(work == mx, lane, LANES), axis=-1, keepdims=True)
        sel = lane == idx
        sels.append(sel)
        vals.append(mx)
        idxs.append(idx)
        work = jnp.where(sel, NEG_BIG * 2.0, work)
    ex = [jnp.exp(v - vals[0]) for v in vals]
    den = ex[0] + ex[1] + ex[2] + ex[3]
    member = (sels[0] | sels[1] | sels[2] | sels[3]).astype(BF16)
    r = lax.broadcasted_iota(I32, (tm, tm), 0)
    c = lax.broadcasted_iota(I32, (tm, tm), 1)
    tri = (c < r).astype(BF16)
    base = cnt_ref[0:1, :]
    before = jnp.dot(tri, member, preferred_element_type=F32) + base
    gate = jnp.zeros(lg.shape, F32)
    eidx = jnp.zeros(lg.shape, I32)
    rank = jnp.zeros(lg.shape, F32)
    for k in range(TOP_K):
        rk = jnp.sum(jnp.where(sels[k], before, 0.0), axis=-1, keepdims=True)
        gate = jnp.where(lane == k, ex[k] / den, gate)
        eidx = jnp.where(lane == k, idxs[k], eidx)
        rank = jnp.where(lane == k, rk, rank)
    gate_ref[...] = gate
    eidx_ref[...] = eidx
    rank_ref[...] = rank.astype(I32)
    cnt_ref[...] = jnp.broadcast_to(
        base + jnp.sum(member.astype(F32), axis=0, keepdims=True), cnt_ref.shape)


def _route(logits):
    T = logits.shape[0]
    tm = min(ROW_TILE, T)
    spec = pl.BlockSpec((tm, LANES), lambda i: (i, 0))
    return pl.pallas_call(
        _route_kernel,
        grid=(T // tm,),
        in_specs=[spec],
        out_specs=[spec, spec, spec, pl.BlockSpec((SUBLANES, LANES), lambda i: (0, 0))],
        out_shape=[
            jax.ShapeDtypeStruct((T, LANES), F32),
            jax.ShapeDtypeStruct((T, LANES), I32),
            jax.ShapeDtypeStruct((T, LANES), I32),
            jax.ShapeDtypeStruct((SUBLANES, LANES), F32),
        ],
        compiler_params=_cparams("arbitrary"),
        name="route",
    )(logits)


def _dispatch_kernel(dest_ref, pad_start_ref, pad_len_ref, nused_ref, h2_ref, buf_ref,
                     zero_ref, sem, zsem):
    tm = h2_ref.shape[0]
    blk = zero_ref.shape[0]
    nblocks = buf_ref.shape[0] // blk

    @pl.when(pl.program_id(0) == 0)
    def _():
        zero_ref[...] = jnp.zeros_like(zero_ref)

        def per_expert(e, _):
            start = pad_start_ref[e]
            npad = pad_len_ref[e]

            def put(r, _):
                pltpu.make_async_copy(zero_ref.at[0], buf_ref.at[start + r], zsem).start()
                return 0

            def done(r, _):
                pltpu.make_async_copy(zero_ref.at[0], buf_ref.at[0], zsem).wait()
                return 0

            lax.fori_loop(0, npad, put, 0)
            lax.fori_loop(0, npad, done, 0)
            return 0

        lax.fori_loop(0, N_EXPERTS, per_expert, 0)

        def tail_block(b):
            return pltpu.make_async_copy(
                zero_ref, buf_ref.at[pl.ds(pl.multiple_of(b * blk, blk), blk)], zsem)

        def tail_put(b, _):
            tail_block(b).start()
            return 0

        def tail_done(b, _):
            tail_block(b).wait()
            return 0

        lax.fori_loop(nused_ref[0], nblocks, tail_put, 0)
        lax.fori_loop(nused_ref[0], nblocks, tail_done, 0)

    def issue(r, _):
        for k in range(TOP_K):
            d = dest_ref[r * TOP_K + k]
            pltpu.make_async_copy(h2_ref.at[r], buf_ref.at[d], sem).start()
        return 0

    lax.fori_loop(0, tm, issue, 0)
    for _ in range(TOP_K):
        pltpu.make_async_copy(h2_ref, buf_ref.at[pl.ds(0, tm)], sem).wait()


def _dispatch(dest_flat, pad_start, pad_len, nused, h2_tiles, n_rows, blk):
    T = h2_tiles.shape[0]
    tm = min(DMA_TILE, T)
    smem = pl.BlockSpec(memory_space=pltpu.SMEM)
    return pl.pallas_call(
        _dispatch_kernel,
        grid=(T // tm,),
        in_specs=[
            pl.BlockSpec((tm * TOP_K,), lambda i: (i,), memory_space=pltpu.SMEM),
            smem, smem, smem,
            pl.BlockSpec((tm,) + h2_tiles.shape[1:], lambda i: (i, 0, 0)),
        ],
        out_specs=pl.BlockSpec(memory_space=pl.ANY),
        out_shape=jax.ShapeDtypeStruct((n_rows,) + h2_tiles.shape[1:], F32),
        scratch_shapes=[pltpu.VMEM((blk,) + h2_tiles.shape[1:], F32),
                        pltpu.SemaphoreType.DMA(()), pltpu.SemaphoreType.DMA(())],
        compiler_params=_cparams("arbitrary"),
        name="dispatch",
    )(dest_flat, pad_start, pad_len, nused, h2_tiles)


def _ffn_kernel(blk_exp_ref, nused_ref, x_ref, wgu_ref, bgu_ref, wd_ref, bd_ref, o_ref,
                wgu_bf_ref, wd_bf_ref):
    b = pl.program_id(0)
    tm = o_ref.shape[0] // SUBLANES
    nj = SUBLANES
    F = wd_ref.shape[1]
    used = b < nused_ref[0]
    new_expert = (b == 0) | (blk_exp_ref[b] != blk_exp_ref[jnp.maximum(b - 1, 0)])

    @pl.when(used & new_expert)
    def _():
        wgu_bf_ref[...] = wgu_ref[0].astype(BF16)
        wd_bf_ref[...] = wd_ref[0].astype(BF16)

    @pl.when(used)
    def _():
        x = jnp.concatenate(
            [x_ref[pl.ds(j, tm, stride=nj), :] for j in range(nj)], axis=1).astype(BF16)
        gu = jnp.dot(x, wgu_bf_ref[...], preferred_element_type=F32) + bgu_ref[0]
        g = jnp.minimum(gu[:, :F], SWIGLU_LIMIT)
        u = jnp.clip(gu[:, F:], -SWIGLU_LIMIT, SWIGLU_LIMIT)
        act = g * jax.nn.sigmoid(SWIGLU_ALPHA * g) * (u + 1.0)
        y = jnp.dot(act.astype(BF16), wd_bf_ref[...], preferred_element_type=F32) + bd_ref[0]
        for j in range(nj):
            o_ref[pl.ds(j, tm, stride=nj), :] = y[:, j * LANES:(j + 1) * LANES]

    @pl.when(b >= nused_ref[0])
    def _():
        o_ref[...] = jnp.zeros_like(o_ref)


def _ffn(blk_exp, nused, buf2d, wgu, bgu, wd, bd, tm):
    rows = buf2d.shape[0]
    nb = rows // (tm * SUBLANES)
    D = wgu.shape[1]
    F = wd.shape[1]

    def xmap(b, be, nu):
        return (jnp.minimum(b, nu[0] - 1), 0)

    def wmap(b, be, nu):
        return (be[jnp.minimum(b, nu[0] - 1)], 0, 0)

    grid_spec = pltpu.PrefetchScalarGridSpec(
        num_scalar_prefetch=2,
        grid=(nb,),
        in_specs=[
            pl.BlockSpec((tm * SUBLANES, LANES), xmap),
            pl.BlockSpec((1, D, 2 * F), wmap),
            pl.BlockSpec((1, 1, 2 * F), wmap),
            pl.BlockSpec((1, F, D), wmap),
            pl.BlockSpec((1, 1, D), wmap),
        ],
        out_specs=pl.BlockSpec((tm * SUBLANES, LANES), lambda b, be, nu: (b, 0)),
        scratch_shapes=[pltpu.VMEM((D, 2 * F), BF16), pltpu.VMEM((F, D), BF16)],
    )
    return pl.pallas_call(
        _ffn_kernel,
        grid_spec=grid_spec,
        out_shape=jax.ShapeDtypeStruct((rows, LANES), F32),
        compiler_params=_cparams("arbitrary"),
        name="ffn",
    )(blk_exp, nused, buf2d, wgu, bgu, wd, bd)


def _combine_kernel(dest_ref, gate_ref, x2_ref, fg_ref, obuf_ref, o_ref, rows_ref, sem):
    tm, D = x2_ref.shape
    nj = D // LANES

    def row_slot(k, r):
        return rows_ref.at[pl.ds(pl.multiple_of((k * tm + r) * nj, nj), nj)]

    def issue(r, _):
        for k in range(TOP_K):
            d = pl.multiple_of(dest_ref[r * TOP_K + k] * nj, nj)
            pltpu.make_async_copy(obuf_ref.at[pl.ds(d, nj)], row_slot(k, r), sem).start()
        return 0

    lax.fori_loop(0, tm, issue, 0)
    for k in range(TOP_K):
        pltpu.make_async_copy(obuf_ref.at[pl.ds(0, tm * nj)],
                              rows_ref.at[pl.ds(k * tm * nj, tm * nj)], sem).wait()

    gate = gate_ref[...]
    cols = []
    for j in range(nj):
        acc = None
        for k in range(TOP_K):
            part = gate[:, k:k + 1] * rows_ref[pl.ds(k * tm * nj + j, tm, stride=nj), :]
            acc = part if acc is None else acc + part
        cols.append(acc)
    y = jnp.concatenate(cols, axis=1)
    o_ref[...] = _rms(x2_ref[...] + y, fg_ref[...])


def _combine(dest_flat, gate4, x2, fg, obuf_tiles):
    T, D = x2.shape
    tm = min(DMA_TILE, T)
    return pl.pallas_call(
        _combine_kernel,
        grid=(T // tm,),
        in_specs=[
            pl.BlockSpec((tm * TOP_K,), lambda i: (i,), memory_space=pltpu.SMEM),
            pl.BlockSpec((tm, LANES), lambda i: (i, 0)),
            pl.BlockSpec((tm, D), lambda i: (i, 0)),
            pl.BlockSpec((1, D), lambda i: (0, 0)),
            pl.BlockSpec(memory_space=pl.ANY),
        ],
        out_specs=pl.BlockSpec((tm, D), lambda i: (i, 0)),
        out_shape=jax.ShapeDtypeStruct((T, D), F32),
        scratch_shapes=[
            pltpu.VMEM((TOP_K * tm * (D // LANES), LANES), F32),
            pltpu.SemaphoreType.DMA(()),
        ],
        compiler_params=_cparams("arbitrary"),
        name="combine",
    )(dest_flat, gate4, x2, fg, obuf_tiles)


def _rope_tables(S):
    rows = S // GRID_W
    row = jnp.repeat(jnp.arange(rows, dtype=I32), GRID_W).astype(F32)
    col = jnp.tile(jnp.arange(GRID_W, dtype=I32), rows).astype(F32)
    inv = ROPE_THETA ** (-jnp.arange(0, AXIS_DIM, 2, dtype=F32) / AXIS_DIM)
    ang_r = row[:, None] * inv[None, :]
    ang_c = col[:, None] * inv[None, :]
    cos = jnp.concatenate([jnp.cos(ang_r)] * 2 + [jnp.cos(ang_c)] * 2, axis=1)
    sin = jnp.concatenate([-jnp.sin(ang_r), jnp.sin(ang_r), -jnp.sin(ang_c), jnp.sin(ang_c)], axis=1)
    return jnp.tile(cos, (1, 2)), jnp.tile(sin, (1, 2))


def _block_diag_gates(wa, ba, wi, bi, ct):
    nb, bw = wa.shape[1], wa.shape[2]
    per = ct // bw
    nct = nb // per
    eye = jnp.eye(per, dtype=wa.dtype)

    def tiles(w):
        w = w.reshape(nct, per, bw, bw)
        return jnp.einsum('cpij,pq->cpiqj', w, eye).reshape(nct, ct, ct)

    wg = jnp.concatenate([tiles(wa[0]), tiles(wi[0]), tiles(wa[1]), tiles(wi[1])], axis=2)
    bias = lambda b: b.reshape(nct, 1, ct)
    bg = jnp.concatenate([bias(ba[0]), bias(bi[0]), bias(ba[1]), bias(bi[1])], axis=2)
    return wg.astype(BF16), bg


def _layer(x2d, B, S, norm1_g, w_in, b_in, q_norm_g, k_norm_g, conv_w, conv_b, lru_wa, lru_ba,
           lru_wi, lru_bi, lru_lambda, w_attn_o, w_lru_o, w_out, norm2_g, w_router, b_router,
           w_gu, b_gu, w_down, b_down, out_g):
    T, D = x2d.shape
    C = conv_w.shape[1]
    nqkv = Q_WIDTH + 2 * KV_WIDTH
    row2 = lambda v: v.reshape(1, -1)

    cos_t, sin_t = _rope_tables(S)
    head = jnp.arange(LANES) // HEAD_DIM
    hsum = (head[:, None] == head[None, :]).astype(BF16)
    q, k, v = _qkv(x2d, row2(norm1_g), w_in[:, :nqkv].astype(BF16), row2(b_in[:nqkv]),
                   cos_t, sin_t, row2(jnp.tile(q_norm_g, 2)), row2(jnp.tile(k_norm_g, 2)),
                   hsum, B, S)
    proj = _proj(x2d, row2(norm1_g), w_in[:, nqkv:].astype(BF16), row2(b_in[nqkv:]))
    attn = _attn(q, k, v)
    wg, bg = _block_diag_gates(lru_wa, lru_ba, lru_wi, lru_bi, LRU_CT)
    rec = _lru(proj, conv_w, row2(conv_b), wg, bg, lru_lambda, B, S, C)

    pad = LANES - N_EXPERTS
    rw = jnp.pad(w_router, ((0, 0), (0, pad)))
    rw_hi = rw.astype(BF16)
    rw_lo = (rw - rw_hi.astype(F32)).astype(BF16)
    rb = jnp.pad(b_router, (0, pad), constant_values=NEG_BIG).reshape(1, LANES)
    x2, h2_rows, logits = _merge(x2d, attn, rec, proj, w_attn_o.astype(BF16),
                                 w_lru_o.astype(BF16), w_out.astype(BF16), row2(norm2_g),
                                 rw_hi, rw_lo, rb)

    gate4, eidx4, rank4, cnt = _route(logits)
    counts = cnt[0, :N_EXPERTS].astype(I32)
    tm = FFN_TM
    nblk = (counts + tm - 1) // tm
    pend_blk = jnp.cumsum(nblk)
    pstart = (pend_blk - nblk) * tm
    A = T * TOP_K
    nb = (A + N_EXPERTS * (tm - 1) + tm - 1) // tm
    blk_exp = jnp.minimum(
        jnp.sum(pend_blk[None, :] <= jnp.arange(nb, dtype=I32)[:, None], axis=1),
        N_EXPERTS - 1).astype(I32)
    nused = pend_blk[-1:].astype(I32)
    dest = (pstart[eidx4[:, :TOP_K]] + rank4[:, :TOP_K]).astype(I32).reshape(A)
    pad_start = (pstart + counts).astype(I32)
    pad_len = (nblk * tm - counts).astype(I32)

    nj = D // LANES
    buf = _dispatch(dest, pad_start, pad_len, nused, h2_rows.reshape(T, nj, LANES), nb * tm, tm)
    obuf = _ffn(blk_exp, nused, buf.reshape(nb * tm * nj, LANES), w_gu,
                b_gu.reshape(N_EXPERTS, 1, -1), w_down, b_down.reshape(N_EXPERTS, 1, -1), tm)
    return _combine(dest, gate4, x2, row2(out_g), obuf)


def kernel(x, norm1_g, w_in, b_in, q_norm_g, k_norm_g, conv_w, conv_b, lru_wa, lru_ba, lru_wi,
           lru_bi, lru_lambda, w_attn_o, w_lru_o, w_out, norm2_g, w_router, b_router, w_gu, b_gu,
           w_down, b_down, final_g):
    B, S, D = x.shape
    depth = norm1_g.shape[0]
    assert depth == 1, "the fused final RMSNorm assumes a single layer"
    assert S % (TIME_CHUNKS * SUBLANES) == 0 and S % GRID_W == 0
    out = _layer(x.reshape(B * S, D), B, S, norm1_g[0], w_in[0], b_in[0], q_norm_g[0],
                 k_norm_g[0], conv_w[0], conv_b[0], lru_wa[0], lru_ba[0], lru_wi[0], lru_bi[0],
                 lru_lambda[0], w_attn_o[0], w_lru_o[0], w_out[0], norm2_g[0], w_router[0],
                 b_router[0], w_gu[0], b_gu[0], w_down[0], b_down[0], final_g)
    return out.reshape(B, S, D)
```

```python
import functools
import math

import jax
import jax.numpy as jnp
from jax import lax
from jax.experimental import pallas as pl
from jax.experimental.pallas import tpu as pltpu

F32 = jnp.float32
BF16 = jnp.bfloat16
I32 = jnp.int32

LANES = 128
SUBLANES = 8
VMEM_LIMIT_BYTES = 56 * 1024 * 1024

HEAD_DIM = 64
N_Q_HEADS = 8
N_KV_HEADS = 2
Q_GROUP = N_Q_HEADS // N_KV_HEADS
Q_WIDTH = N_Q_HEADS * HEAD_DIM
KV_WIDTH = N_KV_HEADS * HEAD_DIM
AXIS_DIM = HEAD_DIM // 2
ROT_HALF = AXIS_DIM // 2
ROPE_THETA = 10000.0
GRID_W = 64
LRU_BLOCKS = 16
RG_C = 8.0
N_EXPERTS = 32
TOP_K = 4
SWIGLU_LIMIT = 7.0
SWIGLU_ALPHA = 1.702
NORM_EPS = 1e-6
LOG2E = 1.4426950408889634
NEG_BIG = -1e30
TINY = 1e-30

ROW_TILE = 512
ATT_TQ = 512
ATT_TK = 512
LRU_CT = 256
LRU_ROWS = 256
TIME_CHUNKS = SUBLANES
FFN_TM = 512
DMA_TILE = 256


def _cparams(*sem):
    return pltpu.CompilerParams(dimension_semantics=sem, vmem_limit_bytes=VMEM_LIMIT_BYTES)


def _rms(x, g):
    return x * lax.rsqrt(jnp.mean(x * x, axis=-1, keepdims=True) + NORM_EPS) * g


def _qkv_kernel(x_ref, g1_ref, w_ref, b_ref, cos_ref, sin_ref, qg_ref, kg_ref, hsum_ref,
                q_ref, k_ref, v_ref):
    h = _rms(x_ref[...], g1_ref[...]).astype(BF16)
    p = jnp.dot(h, w_ref[...], preferred_element_type=F32) + b_ref[...]
    cos = cos_ref[...]
    sin = sin_ref[...]
    hsum = hsum_ref[...]
    lane = lax.broadcasted_iota(I32, cos.shape, 1)
    first_half = (lane % AXIS_DIM) < ROT_HALF

    def norm_rope(c, gain):
        sq = c * c
        hi = sq.astype(BF16)
        lo = (sq - hi.astype(F32)).astype(BF16)
        ms = (jnp.dot(hi, hsum, preferred_element_type=F32)
              + jnp.dot(lo, hsum, preferred_element_type=F32)) * (1.0 / HEAD_DIM)
        y = c * lax.rsqrt(ms + NORM_EPS) * gain
        partner = jnp.where(first_half, pltpu.roll(y, LANES - ROT_HALF, 1),
                            pltpu.roll(y, ROT_HALF, 1))
        return y * cos + partner * sin

    qg = qg_ref[...]
    for c in range(Q_WIDTH // LANES):
        y = norm_rope(p[:, c * LANES:(c + 1) * LANES], qg) * (HEAD_DIM ** -0.5 * LOG2E)
        yt = y.T.astype(BF16)
        q_ref[0, 2 * c] = yt[:HEAD_DIM]
        q_ref[0, 2 * c + 1] = yt[HEAD_DIM:]
    yk = norm_rope(p[:, Q_WIDTH:Q_WIDTH + KV_WIDTH], kg_ref[...])
    k_ref[0, 0] = yk[:, :HEAD_DIM].astype(BF16)
    k_ref[0, 1] = yk[:, HEAD_DIM:].astype(BF16)
    vt = p[:, Q_WIDTH + KV_WIDTH:].T
    tm = vt.shape[1]
    ones_row = (lax.broadcasted_iota(I32, (HEAD_DIM, tm), 0) == 0).astype(BF16)
    v_ref[0, 0] = jnp.concatenate([vt[:HEAD_DIM].astype(BF16), ones_row], axis=0)
    v_ref[0, 1] = jnp.concatenate([vt[HEAD_DIM:].astype(BF16), ones_row], axis=0)


def _qkv(x2d, g1, w_qkv, b_qkv, cos_t, sin_t, qg, kg, hsum, B, S):
    T, D = x2d.shape
    tm = min(ROW_TILE, S)
    ns = S // tm
    n = w_qkv.shape[1]
    full = lambda shape: pl.BlockSpec(shape, lambda i: (0,) * len(shape))
    return pl.pallas_call(
        _qkv_kernel,
        grid=(T // tm,),
        in_specs=[
            pl.BlockSpec((tm, D), lambda i: (i, 0)),
            full((1, D)), full((D, n)), full((1, n)),
            pl.BlockSpec((tm, LANES), lambda i: (i % ns, 0)),
            pl.BlockSpec((tm, LANES), lambda i: (i % ns, 0)),
            full((1, LANES)), full((1, LANES)), full((LANES, LANES)),
        ],
        out_specs=[
            pl.BlockSpec((1, N_Q_HEADS, HEAD_DIM, tm), lambda i: (i // ns, 0, 0, i % ns)),
            pl.BlockSpec((1, N_KV_HEADS, tm, HEAD_DIM), lambda i: (i // ns, 0, i % ns, 0)),
            pl.BlockSpec((1, N_KV_HEADS, LANES, tm), lambda i: (i // ns, 0, 0, i % ns)),
        ],
        out_shape=[
            jax.ShapeDtypeStruct((B, N_Q_HEADS, HEAD_DIM, S), BF16),
            jax.ShapeDtypeStruct((B, N_KV_HEADS, S, HEAD_DIM), BF16),
            jax.ShapeDtypeStruct((B, N_KV_HEADS, LANES, S), BF16),
        ],
        compiler_params=_cparams("parallel"),
        name="qkv",
    )(x2d, g1, w_qkv, b_qkv, cos_t, sin_t, qg, kg, hsum)


def _proj_kernel(x_ref, g_ref, w_ref, b_ref, o_ref):
    h = _rms(x_ref[...], g_ref[...]).astype(BF16)
    o_ref[...] = jnp.dot(h, w_ref[...], preferred_element_type=F32) + b_ref[...]


def _proj(x2d, g, w, b):
    T, D = x2d.shape
    n = w.shape[1]
    tm = min(ROW_TILE, T)
    return pl.pallas_call(
        _proj_kernel,
        grid=(T // tm,),
        in_specs=[
            pl.BlockSpec((tm, D), lambda i: (i, 0)),
            pl.BlockSpec((1, D), lambda i: (0, 0)),
            pl.BlockSpec((D, n), lambda i: (0, 0)),
            pl.BlockSpec((1, n), lambda i: (0, 0)),
        ],
        out_specs=pl.BlockSpec((tm, n), lambda i: (i, 0)),
        out_shape=jax.ShapeDtypeStruct((T, n), F32),
        compiler_params=_cparams("parallel"),
        name="proj",
    )(x2d, g, w, b)


def _attn_kernel(q_ref, k_ref, v_ref, o_ref, s0_ref, s1_ref, *, tk):
    tq = q_ref.shape[3]
    S = k_ref.shape[2]
    M = Q_GROUP * tq
    n = S // tk
    qT = jnp.concatenate([q_ref[0, h] for h in range(Q_GROUP)], axis=1)

    def scores(j, s_ref):
        off = pl.multiple_of(j * tk, tk)
        s_ref[...] = jnp.dot(k_ref[0, 0, pl.ds(off, tk), :], qT, preferred_element_type=F32)

    def absorb(j, s_ref, carry):
        m, acc = carry
        off = pl.multiple_of(j * tk, tk)
        vc = v_ref[0, 0, :, pl.ds(off, tk)]
        s = s_ref[...]
        m_new = jnp.maximum(m, jnp.max(s, axis=0, keepdims=True))
        alpha = jnp.exp2(m - m_new)
        p = jnp.exp2(s - m_new).astype(BF16)
        return m_new, alpha * acc + jnp.dot(vc, p, preferred_element_type=F32)

    def pair(i, carry):
        scores(2 * i + 1, s1_ref)
        carry = absorb(2 * i, s0_ref, carry)
        scores(2 * i + 2, s0_ref)
        return absorb(2 * i + 1, s1_ref, carry)

    scores(0, s0_ref)
    carry = (jnp.full((1, M), NEG_BIG, F32), jnp.zeros((LANES, M), F32))
    carry = lax.fori_loop(0, n // 2 - 1, pair, carry)
    scores(n - 1, s1_ref)
    carry = absorb(n - 2, s0_ref, carry)
    _, acc = absorb(n - 1, s1_ref, carry)
    outT = acc[:HEAD_DIM] / acc[HEAD_DIM:HEAD_DIM + 1]
    stacked = jnp.concatenate(
        [outT[:, g * tq:(g + 1) * tq] for g in range(Q_GROUP)], axis=0)
    o_ref[...] = stacked.T.astype(BF16)


def _attn(qT, k, vT):
    B, _, _, S = qT.shape
    tq = min(ATT_TQ, S)
    tk = min(ATT_TK, S)
    nq = S // tq
    assert (S // tk) % 2 == 0, "key chunks are processed in pairs"
    score_buf = pltpu.VMEM((tk, Q_GROUP * tq), F32)
    return pl.pallas_call(
        functools.partial(_attn_kernel, tk=tk),
        grid=(B, N_KV_HEADS, nq),
        in_specs=[
            pl.BlockSpec((1, Q_GROUP, HEAD_DIM, tq), lambda b, g, i: (b, g, 0, i)),
            pl.BlockSpec((1, 1, S, HEAD_DIM), lambda b, g, i: (b, g, 0, 0)),
            pl.BlockSpec((1, 1, LANES, S), lambda b, g, i: (b, g, 0, 0)),
        ],
        out_specs=pl.BlockSpec((tq, Q_GROUP * HEAD_DIM), lambda b, g, i: (b * nq + i, g)),
        out_shape=jax.ShapeDtypeStruct((B * S, Q_WIDTH), BF16),
        scratch_shapes=[score_buf, score_buf],
        compiler_params=_cparams("parallel", "parallel", "parallel"),
        name="attn",
    )(qT, k, vT)


def _shift_chunks(v, down):
    row = lax.broadcasted_iota(I32, v.shape, 0)
    if down:
        return jnp.where(row == 0, 0.0, pltpu.roll(v, 1, 0))
    return jnp.where(row == SUBLANES - 1, 0.0, pltpu.roll(v, SUBLANES - 1, 0))


def _lru_kernel(xr_ref, yr_ref, cw_ref, cb_ref, wg_ref, bg_ref, lam_ref, o_ref,
                xe_ref, af_ref, uf_ref, ab_ref, ub_ref, *, rows):
    S, ct = xr_ref.shape
    lc = S // TIME_CHUNKS
    halo = SUBLANES

    ng = ct // LANES

    def xe_rows(r0, n):
        return jnp.concatenate([xe_ref[g, pl.ds(r0, n), :] for g in range(ng)], axis=1)

    def xe_store(r0, n, val):
        for g in range(ng):
            xe_ref[g, pl.ds(r0, n), :] = val[:, g * LANES:(g + 1) * LANES]

    for s in range(TIME_CHUNKS):
        for g in range(ng):
            xe_ref[g, pl.ds(halo + s, lc, stride=SUBLANES), :] = (
                xr_ref[pl.ds(s * lc, lc), g * LANES:(g + 1) * LANES])
    xe_store(0, halo, _shift_chunks(xe_rows(S, halo), True))
    first = xe_rows(halo, halo)
    second = xe_rows(2 * halo, halo)
    xe_store(S + halo, halo, _shift_chunks(first, False))
    xe_store(S + 2 * halo, halo, _shift_chunks(second, False))

    cw = cw_ref[...]
    cb = cb_ref[...]
    lam = lam_ref[...]
    log_sig = jnp.minimum(lam, 0.0) - jnp.log(1.0 + jnp.exp(-jnp.abs(lam)))
    c_half = (0.5 * RG_C * LOG2E) * log_sig
    wg = wg_ref[0]
    bg = bg_ref[0]

    def gate_chunk(i, _):
        r0 = pl.multiple_of(i * rows, rows)
        xc = cb
        for j in range(4):
            xc = xc + cw[j:j + 1, :] * xe_rows(r0 + j * halo, rows)
        t = jnp.tanh(jnp.dot(xc.astype(BF16), wg, preferred_element_type=F32) + bg)
        x_half = 0.5 * xc
        for d, (a_ref, u_ref) in enumerate(((af_ref, uf_ref), (ab_ref, ub_ref))):
            t_r = t[:, (2 * d) * ct:(2 * d + 1) * ct]
            t_i = t[:, (2 * d + 1) * ct:(2 * d + 2) * ct]
            ch = c_half[d:d + 1, :]
            a = jnp.exp2(t_r * ch + ch)
            a_ref[pl.ds(r0, rows), :] = a
            v = 1.0 - a * a
            root = jnp.maximum(v, 0.0) * lax.rsqrt(jnp.maximum(v, TINY))
            u_ref[pl.ds(r0, rows), :] = root * ((t_i + 1.0) * x_half)
        return 0

    lax.fori_loop(0, S // rows, gate_chunk, 0)

    unroll = 8

    def scan_body(i, carry):
        hf, pf, hb, pb = carry
        for k in range(unroll):
            tf = i * unroll + k
            rf = pl.multiple_of(tf * SUBLANES, SUBLANES)
            rb = pl.multiple_of((lc - 1 - tf) * SUBLANES, SUBLANES)
            a = af_ref[pl.ds(rf, SUBLANES), :]
            hf = a * hf + uf_ref[pl.ds(rf, SUBLANES), :]
            pf = a * pf
            uf_ref[pl.ds(rf, SUBLANES), :] = hf
            af_ref[pl.ds(rf, SUBLANES), :] = pf
            a = ab_ref[pl.ds(rb, SUBLANES), :]
            hb = a * hb + ub_ref[pl.ds(rb, SUBLANES), :]
            pb = a * pb
            ub_ref[pl.ds(rb, SUBLANES), :] = hb
            ab_ref[pl.ds(rb, SUBLANES), :] = pb
        return hf, pf, hb, pb

    zero = jnp.zeros((SUBLANES, ct), F32)
    one = jnp.ones((SUBLANES, ct), F32)
    hf_end, pf_end, hb_end, pb_end = lax.fori_loop(0, lc // unroll, scan_body,
                                                   (zero, one, zero, one))

    def chunk_carry(h_end, p_end, down):
        f = h_end
        for _ in range(TIME_CHUNKS - 1):
            f = h_end + p_end * _shift_chunks(f, down)
        return _shift_chunks(f, down)

    cin_f = chunk_carry(hf_end, pf_end, True)
    cin_b = chunk_carry(hb_end, pb_end, False)

    def fix_chunk(i, _):
        r0 = pl.multiple_of(i * rows, rows)
        reps = rows // SUBLANES
        hf = uf_ref[pl.ds(r0, rows), :] + af_ref[pl.ds(r0, rows), :] * jnp.tile(cin_f, (reps, 1))
        hb = ub_ref[pl.ds(r0, rows), :] + ab_ref[pl.ds(r0, rows), :] * jnp.tile(cin_b, (reps, 1))
        xe_store(r0, rows, hf + hb)
        return 0

    lax.fori_loop(0, S // rows, fix_chunk, 0)

    for s in range(TIME_CHUNKS):
        hsum = jnp.concatenate(
            [xe_ref[g, pl.ds(s, lc, stride=SUBLANES), :] for g in range(ng)], axis=1)
        y = yr_ref[pl.ds(s * lc, lc), :]
        o_ref[pl.ds(s * lc, lc), :] = (hsum * jax.nn.gelu(y, approximate=True)).astype(BF16)


def _lru(proj, conv_w, conv_b, wg, bg, lam, B, S, C):
    ct = LRU_CT
    nct = C // ct
    rows = min(LRU_ROWS, S)
    return pl.pallas_call(
        functools.partial(_lru_kernel, rows=rows),
        grid=(B, nct),
        in_specs=[
            pl.BlockSpec((S, ct), lambda b, c: (b, c)),
            pl.BlockSpec((S, ct), lambda b, c: (b, nct + c)),
            pl.BlockSpec((4, ct), lambda b, c: (0, c)),
            pl.BlockSpec((1, ct), lambda b, c: (0, c)),
            pl.BlockSpec((1, ct, 4 * ct), lambda b, c: (c, 0, 0)),
            pl.BlockSpec((1, 1, 4 * ct), lambda b, c: (c, 0, 0)),
            pl.BlockSpec((2, ct), lambda b, c: (0, c)),
        ],
        out_specs=pl.BlockSpec((S, ct), lambda b, c: (b, c)),
        out_shape=jax.ShapeDtypeStruct((B * S, C), BF16),
        scratch_shapes=[
            pltpu.VMEM((ct // LANES, S + 3 * SUBLANES, LANES), F32),
            pltpu.VMEM((S, ct), F32), pltpu.VMEM((S, ct), F32),
            pltpu.VMEM((S, ct), F32), pltpu.VMEM((S, ct), F32),
        ],
        compiler_params=_cparams("parallel", "parallel"),
        name="lru",
    )(proj, proj, conv_w, conv_b, wg, bg, lam)


def _merge_kernel(x_ref, attn_ref, rec_ref, gl_ref, wa_ref, wr_ref, wo_ref, g2_ref,
                  rw_hi_ref, rw_lo_ref, rb_ref, x2_ref, h2_ref, lg_ref):
    D = x_ref.shape[1]
    gl = gl_ref[...]
    ma = jnp.dot(attn_ref[...], wa_ref[...], preferred_element_type=F32)
    mr = jnp.dot(rec_ref[...], wr_ref[...], preferred_element_type=F32)
    merged = jax.nn.sigmoid(gl[:, :D]) * ma + jax.nn.sigmoid(gl[:, D:]) * mr
    x2 = x_ref[...] + jnp.dot(merged.astype(BF16), wo_ref[...], preferred_element_type=F32)
    x2_ref[...] = x2
    h2 = _rms(x2, g2_ref[...])
    hi = h2.astype(BF16)
    lo = (h2 - hi.astype(F32)).astype(BF16)
    lg_ref[...] = (jnp.dot(hi, rw_hi_ref[...], preferred_element_type=F32)
                   + jnp.dot(lo, rw_hi_ref[...], preferred_element_type=F32)
                   + jnp.dot(hi, rw_lo_ref[...], preferred_element_type=F32)) + rb_ref[...]
    tm = h2.shape[0]
    for j in range(D // LANES):
        h2_ref[pl.ds(j, tm, stride=D // LANES), :] = h2[:, j * LANES:(j + 1) * LANES]


def _merge(x2d, attn, rec, proj, wa, wr, wo, g2, rw_hi, rw_lo, rb):
    T, D = x2d.shape
    tm = min(ROW_TILE, T)
    nj = D // LANES
    full = lambda a: pl.BlockSpec(a.shape, lambda i: (0,) * a.ndim)
    return pl.pallas_call(
        _merge_kernel,
        grid=(T // tm,),
        in_specs=[
            pl.BlockSpec((tm, D), lambda i: (i, 0)),
            pl.BlockSpec((tm, Q_WIDTH), lambda i: (i, 0)),
            pl.BlockSpec((tm, D), lambda i: (i, 0)),
            pl.BlockSpec((tm, 2 * D), lambda i: (i, 1)),
            full(wa), full(wr), full(wo), full(g2), full(rw_hi), full(rw_lo), full(rb),
        ],
        out_specs=[
            pl.BlockSpec((tm, D), lambda i: (i, 0)),
            pl.BlockSpec((tm * nj, LANES), lambda i: (i, 0)),
            pl.BlockSpec((tm, LANES), lambda i: (i, 0)),
        ],
        out_shape=[
            jax.ShapeDtypeStruct((T, D), F32),
            jax.ShapeDtypeStruct((T * nj, LANES), F32),
            jax.ShapeDtypeStruct((T, LANES), F32),
        ],
        compiler_params=_cparams("parallel"),
        name="merge",
    )(x2d, attn, rec, proj, wa, wr, wo, g2, rw_hi, rw_lo, rb)


def _route_kernel(lg_ref, gate_ref, eidx_ref, rank_ref, cnt_ref):
    i = pl.program_id(0)

    @pl.when(i == 0)
    def _():
        cnt_ref[...] = jnp.zeros_like(cnt_ref)

    lg = lg_ref[...]
    tm = lg.shape[0]
    lane = lax.broadcasted_iota(I32, lg.shape, 1)
    work = lg
    sels, vals, idxs = [], [], []
    for _ in range(TOP_K):
        mx = jnp.max(work, axis=-1, keepdims=True)
        idx = jnp.min(jnp.where(work == mx, lane, LANES), axis=-1, keepdims=True)
        sel = lane == idx
        sels.append(sel)
        vals.append(mx)
        idxs.append(idx)
        work = jnp.where(sel, NEG_BIG * 2.0, work)
    ex = [jnp.exp(v - vals[0]) for v in vals]
    den = ex[0] + ex[1] + ex[2] + ex[3]
    member = (sels[0] | sels[1] | sels[2] | sels[3]).astype(BF16)
    r = lax.broadcasted_iota(I32, (tm, tm), 0)
    c = lax.broadcasted_iota(I32, (tm, tm), 1)
    tri = (c < r).astype(BF16)
    base = cnt_ref[0:1, :]
    before = jnp.dot(tri, member, preferred_element_type=F32) + base
    gate = jnp.zeros(lg.shape, F32)
    eidx = jnp.zeros(lg.shape, I32)
    rank = jnp.zeros(lg.shape, F32)
    for k in range(TOP_K):
        rk = jnp.sum(jnp.where(sels[k], before, 0.0), axis=-1, keepdims=True)
        gate = jnp.where(lane == k, ex[k] / den, gate)
        eidx = jnp.where(lane == k, idxs[k], eidx)
        rank = jnp.where(lane == k, rk, rank)
    gate_ref[...] = gate
    eidx_ref[...] = eidx
    rank_ref[...] = rank.astype(I32)
    cnt_ref[...] = jnp.broadcast_to(
        base + jnp.sum(member.astype(F32), axis=0, keepdims=True), cnt_ref.shape)


def _route(logits):
    T = logits.shape[0]
    tm = min(ROW_TILE, T)
    spec = pl.BlockSpec((tm, LANES), lambda i: (i, 0))
    return pl.pallas_call(
        _route_kernel,
        grid=(T // tm,),
        in_specs=[spec],
        out_specs=[spec, spec, spec, pl.BlockSpec((SUBLANES, LANES), lambda i: (0, 0))],
        out_shape=[
            jax.ShapeDtypeStruct((T, LANES), F32),
            jax.ShapeDtypeStruct((T, LANES), I32),
            jax.ShapeDtypeStruct((T, LANES), I32),
            jax.ShapeDtypeStruct((SUBLANES, LANES), F32),
        ],
        compiler_params=_cparams("arbitrary"),
        name="route",
    )(logits)


def _dispatch_kernel(dest_ref, pad_start_ref, pad_len_ref, nused_ref, h2_ref, buf_ref,
                     zero_ref, sem, zsem):
    tm = h2_ref.shape[0]
    blk = zero_ref.shape[0]
    nblocks = buf_ref.shape[0] // blk

    @pl.when(pl.program_id(0) == 0)
    def _():
        zero_ref[...] = jnp.zeros_like(zero_ref)

        def per_expert(e, _):
            start = pad_start_ref[e]
            npad = pad_len_ref[e]

            def put(r, _):
                pltpu.make_async_copy(zero_ref.at[0], buf_ref.at[start + r], zsem).start()
                return 0

            def done(r, _):
                pltpu.make_async_copy(zero_ref.at[0], buf_ref.at[0], zsem).wait()
                return 0

            lax.fori_loop(0, npad, put, 0)
            lax.fori_loop(0, npad, done, 0)
            return 0

        lax.fori_loop(0, N_EXPERTS, per_expert, 0)

        def tail_block(b):
            return pltpu.make_async_copy(
                zero_ref, buf_ref.at[pl.ds(pl.multiple_of(b * blk, blk), blk)], zsem)

        def tail_put(b, _):
            tail_block(b).start()
            return 0

        def tail_done(b, _):
            tail_block(b).wait()
            return 0

        lax.fori_loop(nused_ref[0], nblocks, tail_put, 0)
        lax.fori_loop(nused_ref[0], nblocks, tail_done, 0)

    def issue(r, _):
        for k in range(TOP_K):
            d = dest_ref[r * TOP_K + k]
            pltpu.make_async_copy(h2_ref.at[r], buf_ref.at[d], sem).start(priority=k % 2)
        return 0

    lax.fori_loop(0, tm, issue, 0)
    for _ in range(TOP_K):
        pltpu.make_async_copy(h2_ref, buf_ref.at[pl.ds(0, tm)], sem).wait()


def _dispatch(dest_flat, pad_start, pad_len, nused, h2_tiles, n_rows, blk):
    T = h2_tiles.shape[0]
    tm = min(DMA_TILE, T)
    smem = pl.BlockSpec(memory_space=pltpu.SMEM)
    return pl.pallas_call(
        _dispatch_kernel,
        grid=(T // tm,),
        in_specs=[
            pl.BlockSpec((tm * TOP_K,), lambda i: (i,), memory_space=pltpu.SMEM),
            smem, smem, smem,
            pl.BlockSpec((tm,) + h2_tiles.shape[1:], lambda i: (i, 0, 0)),
        ],
        out_specs=pl.BlockSpec(memory_space=pl.ANY),
        out_shape=jax.ShapeDtypeStruct((n_rows,) + h2_tiles.shape[1:], F32),
        scratch_shapes=[pltpu.VMEM((blk,) + h2_tiles.shape[1:], F32),
                        pltpu.SemaphoreType.DMA(()), pltpu.SemaphoreType.DMA(())],
        compiler_params=_cparams("arbitrary"),
        name="dispatch",
    )(dest_flat, pad_start, pad_len, nused, h2_tiles)


def _ffn_kernel(blk_exp_ref, nused_ref, x_ref, wgu_ref, bgu_ref, wd_ref, bd_ref, o_ref,
                wgu_bf_ref, wd_bf_ref):
    b = pl.program_id(0)
    tm = o_ref.shape[0] // SUBLANES
    nj = SUBLANES
    F = wd_ref.shape[1]
    used = b < nused_ref[0]
    new_expert = (b == 0) | (blk_exp_ref[b] != blk_exp_ref[jnp.maximum(b - 1, 0)])

    @pl.when(used & new_expert)
    def _():
        wgu_bf_ref[...] = wgu_ref[0].astype(BF16)
        wd_bf_ref[...] = wd_ref[0].astype(BF16)

    @pl.when(used)
    def _():
        x = jnp.concatenate(
            [x_ref[pl.ds(j, tm, stride=nj), :] for j in range(nj)], axis=1).astype(BF16)
        gu = jnp.dot(x, wgu_bf_ref[...], preferred_element_type=F32) + bgu_ref[0]
        g = jnp.minimum(gu[:, :F], SWIGLU_LIMIT)
        u = jnp.clip(gu[:, F:], -SWIGLU_LIMIT, SWIGLU_LIMIT)
        act = g * jax.nn.sigmoid(SWIGLU_ALPHA * g) * (u + 1.0)
        y = jnp.dot(act.astype(BF16), wd_bf_ref[...], preferred_element_type=F32) + bd_ref[0]
        for j in range(nj):
            o_ref[pl.ds(j, tm, stride=nj), :] = y[:, j * LANES:(j + 1) * LANES]

    @pl.when(b >= nused_ref[0])
    def _():
        o_ref[...] = jnp.zeros_like(o_ref)


def _ffn(blk_exp, nused, buf2d, wgu, bgu, wd, bd, tm):
    rows = buf2d.shape[0]
    nb = rows // (tm * SUBLANES)
    D = wgu.shape[1]
    F = wd.shape[1]

    def xmap(b, be, nu):
        return (jnp.minimum(b, nu[0] - 1), 0)

    def wmap(b, be, nu):
        return (be[jnp.minimum(b, nu[0] - 1)], 0, 0)

    grid_spec = pltpu.PrefetchScalarGridSpec(
        num_scalar_prefetch=2,
        grid=(nb,),
        in_specs=[
            pl.BlockSpec((tm * SUBLANES, LANES), xmap),
            pl.BlockSpec((1, D, 2 * F), wmap),
            pl.BlockSpec((1, 1, 2 * F), wmap),
            pl.BlockSpec((1, F, D), wmap),
            pl.BlockSpec((1, 1, D), wmap),
        ],
        out_specs=pl.BlockSpec((tm * SUBLANES, LANES), lambda b, be, nu: (b, 0)),
        scratch_shapes=[pltpu.VMEM((D, 2 * F), BF16), pltpu.VMEM((F, D), BF16)],
    )
    return pl.pallas_call(
        _ffn_kernel,
        grid_spec=grid_spec,
        out_shape=jax.ShapeDtypeStruct((rows, LANES), F32),
        compiler_params=_cparams("arbitrary"),
        name="ffn",
    )(blk_exp, nused, buf2d, wgu, bgu, wd, bd)


def _combine_kernel(dest_ref, gate_ref, x2_ref, fg_ref, obuf_ref, o_ref, rows_ref, sem):
    tm, D = x2_ref.shape
    nj = D // LANES

    def row_slot(k, r):
        return rows_ref.at[pl.ds(pl.multiple_of((k * tm + r) * nj, nj), nj)]

    def issue(r, _):
        for k in range(TOP_K):
            d = pl.multiple_of(dest_ref[r * TOP_K + k] * nj, nj)
            pltpu.make_async_copy(obuf_ref.at[pl.ds(d, nj)], row_slot(k, r), sem).start(
                priority=k % 2)
        return 0

    lax.fori_loop(0, tm, issue, 0)
    for k in range(TOP_K):
        pltpu.make_async_copy(obuf_ref.at[pl.ds(0, tm * nj)],
                              rows_ref.at[pl.ds(k * tm * nj, tm * nj)], sem).wait()

    gate = gate_ref[...]
    cols = []
    for j in range(nj):
        acc = None
        for k in range(TOP_K):
            part = gate[:, k:k + 1] * rows_ref[pl.ds(k * tm * nj + j, tm, stride=nj), :]
            acc = part if acc is None else acc + part
        cols.append(acc)
    y = jnp.concatenate(cols, axis=1)
    o_ref[...] = _rms(x2_ref[...] + y, fg_ref[...])


def _combine(dest_flat, gate4, x2, fg, obuf_tiles):
    T, D = x2.shape
    tm = min(DMA_TILE, T)
    return pl.pallas_call(
        _combine_kernel,
        grid=(T // tm,),
        in_specs=[
            pl.BlockSpec((tm * TOP_K,), lambda i: (i,), memory_space=pltpu.SMEM),
            pl.BlockSpec((tm, LANES), lambda i: (i, 0)),
            pl.BlockSpec((tm, D), lambda i: (i, 0)),
            pl.BlockSpec((1, D), lambda i: (0, 0)),
            pl.BlockSpec(memory_space=pl.ANY),
        ],
        out_specs=pl.BlockSpec((tm, D), lambda i: (i, 0)),
        out_shape=jax.ShapeDtypeStruct((T, D), F32),
        scratch_shapes=[
            pltpu.VMEM((TOP_K * tm * (D // LANES), LANES), F32),
            pltpu.SemaphoreType.DMA(()),
        ],
        compiler_params=_cparams("arbitrary"),
        name="combine",
    )(dest_flat, gate4, x2, fg, obuf_tiles)


def _rope_tables(S):
    rows = S // GRID_W
    row = jnp.repeat(jnp.arange(rows, dtype=I32), GRID_W).astype(F32)
    col = jnp.tile(jnp.arange(GRID_W, dtype=I32), rows).astype(F32)
    inv = ROPE_THETA ** (-jnp.arange(0, AXIS_DIM, 2, dtype=F32) / AXIS_DIM)
    ang_r = row[:, None] * inv[None, :]
    ang_c = col[:, None] * inv[None, :]
    cos = jnp.concatenate([jnp.cos(ang_r)] * 2 + [jnp.cos(ang_c)] * 2, axis=1)
    sin = jnp.concatenate([-jnp.sin(ang_r), jnp.sin(ang_r), -jnp.sin(ang_c), jnp.sin(ang_c)], axis=1)
    return jnp.tile(cos, (1, 2)), jnp.tile(sin, (1, 2))


def _block_diag_gates(wa, ba, wi, bi, ct):
    nb, bw = wa.shape[1], wa.shape[2]
    per = ct // bw
    nct = nb // per
    eye = jnp.eye(per, dtype=wa.dtype)

    def tiles(w):
        w = w.reshape(nct, per, bw, bw)
        return jnp.einsum('cpij,pq->cpiqj', w, eye).reshape(nct, ct, ct)

    wg = jnp.concatenate([tiles(wa[0]), tiles(wi[0]), tiles(wa[1]), tiles(wi[1])], axis=2)
    bias = lambda b: b.reshape(nct, 1, ct)
    bg = jnp.concatenate([bias(ba[0]), bias(bi[0]), bias(ba[1]), bias(bi[1])], axis=2)
    return (0.5 * wg).astype(BF16), 0.5 * bg


def _layer(x2d, B, S, norm1_g, w_in, b_in, q_norm_g, k_norm_g, conv_w, conv_b, lru_wa, lru_ba,
           lru_wi, lru_bi, lru_lambda, w_attn_o, w_lru_o, w_out, norm2_g, w_router, b_router,
           w_gu, b_gu, w_down, b_down, out_g):
    T, D = x2d.shape
    C = conv_w.shape[1]
    nqkv = Q_WIDTH + 2 * KV_WIDTH
    row2 = lambda v: v.reshape(1, -1)

    cos_t, sin_t = _rope_tables(S)
    head = jnp.arange(LANES) // HEAD_DIM
    hsum = (head[:, None] == head[None, :]).astype(BF16)
    q, k, v = _qkv(x2d, row2(norm1_g), w_in[:, :nqkv].astype(BF16), row2(b_in[:nqkv]),
                   cos_t, sin_t, row2(jnp.tile(q_norm_g, 2)), row2(jnp.tile(k_norm_g, 2)),
                   hsum, B, S)
    proj = _proj(x2d, row2(norm1_g), w_in[:, nqkv:].astype(BF16), row2(b_in[nqkv:]))
    attn = _attn(q, k, v)
    wg, bg = _block_diag_gates(lru_wa, lru_ba, lru_wi, lru_bi, LRU_CT)
    rec = _lru(proj, conv_w, row2(conv_b), wg, bg, lru_lambda, B, S, C)

    pad = LANES - N_EXPERTS
    rw = jnp.pad(w_router, ((0, 0), (0, pad)))
    rw_hi = rw.astype(BF16)
    rw_lo = (rw - rw_hi.astype(F32)).astype(BF16)
    rb = jnp.pad(b_router, (0, pad), constant_values=NEG_BIG).reshape(1, LANES)
    x2, h2_rows, logits = _merge(x2d, attn, rec, proj, w_attn_o.astype(BF16),
                                 w_lru_o.astype(BF16), w_out.astype(BF16), row2(norm2_g),
                                 rw_hi, rw_lo, rb)

    gate4, eidx4, rank4, cnt = _route(logits)
    counts = cnt[0, :N_EXPERTS].astype(I32)
    tm = FFN_TM
    nblk = (counts + tm - 1) // tm
    pend_blk = jnp.cumsum(nblk)
    pstart = (pend_blk - nblk) * tm
    A = T * TOP_K
    nb = (A + N_EXPERTS * (tm - 1) + tm - 1) // tm
    blk_exp = jnp.minimum(
        jnp.sum(pend_blk[None, :] <= jnp.arange(nb, dtype=I32)[:, None], axis=1),
        N_EXPERTS - 1).astype(I32)
    nused = pend_blk[-1:].astype(I32)
    dest = (pstart[eidx4[:, :TOP_K]] + rank4[:, :TOP_K]).astype(I32).reshape(A)
    pad_start = (pstart + counts).astype(I32)
    pad_len = (nblk * tm - counts).astype(I32)

    nj = D // LANES
    buf = _dispatch(dest, pad_start, pad_len, nused, h2_rows.reshape(T, nj, LANES), nb * tm, tm)
    obuf = _ffn(blk_exp, nused, buf.reshape(nb * tm * nj, LANES), w_gu,
                b_gu.reshape(N_EXPERTS, 1, -1), w_down, b_down.reshape(N_EXPERTS, 1, -1), tm)
    return _combine(dest, gate4, x2, row2(out_g), obuf)


def kernel(x, norm1_g, w_in, b_in, q_norm_g, k_norm_g, conv_w, conv_b, lru_wa, lru_ba, lru_wi,
           lru_bi, lru_lambda, w_attn_o, w_lru_o, w_out, norm2_g, w_router, b_router, w_gu, b_gu,
           w_down, b_down, final_g):
    B, S, D = x.shape
    depth = norm1_g.shape[0]
    assert depth == 1, "the fused final RMSNorm assumes a single layer"
    assert S % (TIME_CHUNKS * SUBLANES) == 0 and S % GRID_W == 0
    out = _layer(x.reshape(B * S, D), B, S, norm1_g[0], w_in[0], b_in[0], q_norm_g[0],
                 k_norm_g[0], conv_w[0], conv_b[0], lru_wa[0], lru_ba[0], lru_wi[0], lru_bi[0],
                 lru_lambda[0], w_attn_o[0], w_lru_o[0], w_out[0], norm2_g[0], w_router[0],
                 b_router[0], w_gu[0], b_gu[0], w_down[0], b_down[0], final_g)
    return out.reshape(B, S, D)
```

```python
import functools
import math

import jax
import jax.numpy as jnp
from jax import lax
from jax.experimental import pallas as pl
from jax.experimental.pallas import tpu as pltpu

F32 = jnp.float32
BF16 = jnp.bfloat16
I32 = jnp.int32
U32 = jnp.uint32

LANES = 128
SUBLANES = 8
PACK_LINES = 4
VMEM_LIMIT_BYTES = 56 * 1024 * 1024

HEAD_DIM = 64
N_Q_HEADS = 8
N_KV_HEADS = 2
Q_GROUP = N_Q_HEADS // N_KV_HEADS
Q_WIDTH = N_Q_HEADS * HEAD_DIM
KV_WIDTH = N_KV_HEADS * HEAD_DIM
AXIS_DIM = HEAD_DIM // 2
ROT_HALF = AXIS_DIM // 2
ROPE_THETA = 10000.0
GRID_W = 64
LRU_BLOCKS = 16
RG_C = 8.0
N_EXPERTS = 32
TOP_K = 4
SWIGLU_LIMIT = 7.0
SWIGLU_ALPHA = 1.702
NORM_EPS = 1e-6
LOG2E = 1.4426950408889634
NEG_BIG = -1e30
TINY = 1e-30

ROW_TILE = 512
ATT_TQ = 512
ATT_TK = 512
LRU_CT = 256
LRU_ROWS = 256
TIME_CHUNKS = SUBLANES
FFN_TM = 512
DMA_TILE = 256


def _cparams(*sem):
    return pltpu.CompilerParams(dimension_semantics=sem, vmem_limit_bytes=VMEM_LIMIT_BYTES)


def _rms(x, g):
    return x * lax.rsqrt(jnp.mean(x * x, axis=-1, keepdims=True) + NORM_EPS) * g


def _qkv_kernel(x_ref, g1_ref, w_ref, b_ref, cos_ref, sin_ref, qg_ref, kg_ref, hsum_ref,
                q_ref, k_ref, v_ref):
    h = _rms(x_ref[...], g1_ref[...]).astype(BF16)
    p = jnp.dot(h, w_ref[...], preferred_element_type=F32) + b_ref[...]
    cos = cos_ref[...]
    sin = sin_ref[...]
    hsum = hsum_ref[...]
    lane = lax.broadcasted_iota(I32, cos.shape, 1)
    first_half = (lane % AXIS_DIM) < ROT_HALF

    def norm_rope(c, gain):
        sq = c * c
        hi = sq.astype(BF16)
        lo = (sq - hi.astype(F32)).astype(BF16)
        ms = (jnp.dot(hi, hsum, preferred_element_type=F32)
              + jnp.dot(lo, hsum, preferred_element_type=F32)) * (1.0 / HEAD_DIM)
        y = c * lax.rsqrt(ms + NORM_EPS) * gain
        partner = jnp.where(first_half, pltpu.roll(y, LANES - ROT_HALF, 1),
                            pltpu.roll(y, ROT_HALF, 1))
        return y * cos + partner * sin

    qg = qg_ref[...]
    for c in range(Q_WIDTH // LANES):
        y = norm_rope(p[:, c * LANES:(c + 1) * LANES], qg) * (HEAD_DIM ** -0.5 * LOG2E)
        yt = y.T.astype(BF16)
        q_ref[0, 2 * c] = yt[:HEAD_DIM]
        q_ref[0, 2 * c + 1] = yt[HEAD_DIM:]
    yk = norm_rope(p[:, Q_WIDTH:Q_WIDTH + KV_WIDTH], kg_ref[...])
    k_ref[0, 0] = yk[:, :HEAD_DIM].astype(BF16)
    k_ref[0, 1] = yk[:, HEAD_DIM:].astype(BF16)
    vt = p[:, Q_WIDTH + KV_WIDTH:].T
    tm = vt.shape[1]
    ones_row = (lax.broadcasted_iota(I32, (HEAD_DIM, tm), 0) == 0).astype(BF16)
    v_ref[0, 0] = jnp.concatenate([vt[:HEAD_DIM].astype(BF16), ones_row], axis=0)
    v_ref[0, 1] = jnp.concatenate([vt[HEAD_DIM:].astype(BF16), ones_row], axis=0)


def _qkv(x2d, g1, w_qkv, b_qkv, cos_t, sin_t, qg, kg, hsum, B, S):
    T, D = x2d.shape
    tm = min(ROW_TILE, S)
    ns = S // tm
    n = w_qkv.shape[1]
    full = lambda shape: pl.BlockSpec(shape, lambda i: (0,) * len(shape))
    return pl.pallas_call(
        _qkv_kernel,
        grid=(T // tm,),
        in_specs=[
            pl.BlockSpec((tm, D), lambda i: (i, 0)),
            full((1, D)), full((D, n)), full((1, n)),
            pl.BlockSpec((tm, LANES), lambda i: (i % ns, 0)),
            pl.BlockSpec((tm, LANES), lambda i: (i % ns, 0)),
            full((1, LANES)), full((1, LANES)), full((LANES, LANES)),
        ],
        out_specs=[
            pl.BlockSpec((1, N_Q_HEADS, HEAD_DIM, tm), lambda i: (i // ns, 0, 0, i % ns)),
            pl.BlockSpec((1, N_KV_HEADS, tm, HEAD_DIM), lambda i: (i // ns, 0, i % ns, 0)),
            pl.BlockSpec((1, N_KV_HEADS, LANES, tm), lambda i: (i // ns, 0, 0, i % ns)),
        ],
        out_shape=[
            jax.ShapeDtypeStruct((B, N_Q_HEADS, HEAD_DIM, S), BF16),
            jax.ShapeDtypeStruct((B, N_KV_HEADS, S, HEAD_DIM), BF16),
            jax.ShapeDtypeStruct((B, N_KV_HEADS, LANES, S), BF16),
        ],
        compiler_params=_cparams("parallel"),
        name="qkv",
    )(x2d, g1, w_qkv, b_qkv, cos_t, sin_t, qg, kg, hsum)


def _proj_kernel(x_ref, g_ref, w_ref, b_ref, o_ref):
    h = _rms(x_ref[...], g_ref[...]).astype(BF16)
    o_ref[...] = jnp.dot(h, w_ref[...], preferred_element_type=F32) + b_ref[...]


def _proj(x2d, g, w, b):
    T, D = x2d.shape
    n = w.shape[1]
    tm = min(ROW_TILE, T)
    return pl.pallas_call(
        _proj_kernel,
        grid=(T // tm,),
        in_specs=[
            pl.BlockSpec((tm, D), lambda i: (i, 0)),
            pl.BlockSpec((1, D), lambda i: (0, 0)),
            pl.BlockSpec((D, n), lambda i: (0, 0)),
            pl.BlockSpec((1, n), lambda i: (0, 0)),
        ],
        out_specs=pl.BlockSpec((tm, n), lambda i: (i, 0)),
        out_shape=jax.ShapeDtypeStruct((T, n), F32),
        compiler_params=_cparams("parallel"),
        name="proj",
    )(x2d, g, w, b)


def _attn_kernel(q_ref, k_ref, v_ref, o_ref, s0_ref, s1_ref, *, tk):
    tq = q_ref.shape[3]
    S = k_ref.shape[2]
    M = Q_GROUP * tq
    n = S // tk
    qT = jnp.concatenate([q_ref[0, h] for h in range(Q_GROUP)], axis=1)

    def scores(j, s_ref):
        off = pl.multiple_of(j * tk, tk)
        s_ref[...] = jnp.dot(k_ref[0, 0, pl.ds(off, tk), :], qT, preferred_element_type=F32)

    def absorb(j, s_ref, carry):
        m, acc = carry
        off = pl.multiple_of(j * tk, tk)
        vc = v_ref[0, 0, :, pl.ds(off, tk)]
        s = s_ref[...]
        m_new = jnp.maximum(m, jnp.max(s, axis=0, keepdims=True))
        alpha = jnp.exp2(m - m_new)
        p = jnp.exp2(s - m_new).astype(BF16)
        return m_new, alpha * acc + jnp.dot(vc, p, preferred_element_type=F32)

    def pair(i, carry):
        scores(2 * i + 1, s1_ref)
        carry = absorb(2 * i, s0_ref, carry)
        scores(2 * i + 2, s0_ref)
        return absorb(2 * i + 1, s1_ref, carry)

    scores(0, s0_ref)
    carry = (jnp.full((1, M), NEG_BIG, F32), jnp.zeros((LANES, M), F32))
    carry = lax.fori_loop(0, n // 2 - 1, pair, carry)
    scores(n - 1, s1_ref)
    carry = absorb(n - 2, s0_ref, carry)
    _, acc = absorb(n - 1, s1_ref, carry)
    outT = acc[:HEAD_DIM] / acc[HEAD_DIM:HEAD_DIM + 1]
    stacked = jnp.concatenate(
        [outT[:, g * tq:(g + 1) * tq] for g in range(Q_GROUP)], axis=0)
    o_ref[...] = stacked.T.astype(BF16)


def _attn(qT, k, vT):
    B, _, _, S = qT.shape
    tq = min(ATT_TQ, S)
    tk = min(ATT_TK, S)
    nq = S // tq
    assert (S // tk) % 2 == 0, "key chunks are processed in pairs"
    score_buf = pltpu.VMEM((tk, Q_GROUP * tq), F32)
    return pl.pallas_call(
        functools.partial(_attn_kernel, tk=tk),
        grid=(B, N_KV_HEADS, nq),
        in_specs=[
            pl.BlockSpec((1, Q_GROUP, HEAD_DIM, tq), lambda b, g, i: (b, g, 0, i)),
            pl.BlockSpec((1, 1, S, HEAD_DIM), lambda b, g, i: (b, g, 0, 0)),
            pl.BlockSpec((1, 1, LANES, S), lambda b, g, i: (b, g, 0, 0)),
        ],
        out_specs=pl.BlockSpec((tq, Q_GROUP * HEAD_DIM), lambda b, g, i: (b * nq + i, g)),
        out_shape=jax.ShapeDtypeStruct((B * S, Q_WIDTH), BF16),
        scratch_shapes=[score_buf, score_buf],
        compiler_params=_cparams("parallel", "parallel", "parallel"),
        name="attn",
    )(qT, k, vT)


def _shift_chunks(v, down):
    row = lax.broadcasted_iota(I32, v.shape, 0)
    if down:
        return jnp.where(row == 0, 0.0, pltpu.roll(v, 1, 0))
    return jnp.where(row == SUBLANES - 1, 0.0, pltpu.roll(v, SUBLANES - 1, 0))


def _lru_kernel(xr_ref, yr_ref, cw_ref, cb_ref, wg_ref, bg_ref, lam_ref, o_ref,
                xe_ref, af_ref, uf_ref, ab_ref, ub_ref, *, rows):
    S, ct = xr_ref.shape
    lc = S // TIME_CHUNKS
    halo = SUBLANES

    ng = ct // LANES

    def xe_rows(r0, n):
        return jnp.concatenate([xe_ref[g, pl.ds(r0, n), :] for g in range(ng)], axis=1)

    def xe_store(r0, n, val):
        for g in range(ng):
            xe_ref[g, pl.ds(r0, n), :] = val[:, g * LANES:(g + 1) * LANES]

    for s in range(TIME_CHUNKS):
        for g in range(ng):
            xe_ref[g, pl.ds(halo + s, lc, stride=SUBLANES), :] = (
                xr_ref[pl.ds(s * lc, lc), g * LANES:(g + 1) * LANES])
    xe_store(0, halo, _shift_chunks(xe_rows(S, halo), True))
    first = xe_rows(halo, halo)
    second = xe_rows(2 * halo, halo)
    xe_store(S + halo, halo, _shift_chunks(first, False))
    xe_store(S + 2 * halo, halo, _shift_chunks(second, False))

    cw = cw_ref[...]
    cb = cb_ref[...]
    lam = lam_ref[...]
    log_sig = jnp.minimum(lam, 0.0) - jnp.log(1.0 + jnp.exp(-jnp.abs(lam)))
    c_half = (0.5 * RG_C * LOG2E) * log_sig
    wg = wg_ref[0]
    bg = bg_ref[0]

    def gate_chunk(i, _):
        r0 = pl.multiple_of(i * rows, rows)
        xc = cb
        for j in range(4):
            xc = xc + cw[j:j + 1, :] * xe_rows(r0 + j * halo, rows)
        t = jnp.tanh(jnp.dot(xc.astype(BF16), wg, preferred_element_type=F32) + bg)
        x_half = 0.5 * xc
        for d, (a_ref, u_ref) in enumerate(((af_ref, uf_ref), (ab_ref, ub_ref))):
            t_r = t[:, (2 * d) * ct:(2 * d + 1) * ct]
            t_i = t[:, (2 * d + 1) * ct:(2 * d + 2) * ct]
            ch = c_half[d:d + 1, :]
            a = jnp.exp2(t_r * ch + ch)
            a_ref[pl.ds(r0, rows), :] = a
            v = 1.0 - a * a
            root = jnp.maximum(v, 0.0) * lax.rsqrt(jnp.maximum(v, TINY))
            u_ref[pl.ds(r0, rows), :] = root * ((t_i + 1.0) * x_half)
        return 0

    lax.fori_loop(0, S // rows, gate_chunk, 0)

    unroll = 8

    def scan_body(i, carry):
        hf, pf, hb, pb = carry
        for k in range(unroll):
            tf = i * unroll + k
            rf = pl.multiple_of(tf * SUBLANES, SUBLANES)
            rb = pl.multiple_of((lc - 1 - tf) * SUBLANES, SUBLANES)
            a = af_ref[pl.ds(rf, SUBLANES), :]
            hf = a * hf + uf_ref[pl.ds(rf, SUBLANES), :]
            pf = a * pf
            uf_ref[pl.ds(rf, SUBLANES), :] = hf
            af_ref[pl.ds(rf, SUBLANES), :] = pf
            a = ab_ref[pl.ds(rb, SUBLANES), :]
            hb = a * hb + ub_ref[pl.ds(rb, SUBLANES), :]
            pb = a * pb
            ub_ref[pl.ds(rb, SUBLANES), :] = hb
            ab_ref[pl.ds(rb, SUBLANES), :] = pb
        return hf, pf, hb, pb

    zero = jnp.zeros((SUBLANES, ct), F32)
    one = jnp.ones((SUBLANES, ct), F32)
    hf_end, pf_end, hb_end, pb_end = lax.fori_loop(0, lc // unroll, scan_body,
                                                   (zero, one, zero, one))

    def chunk_carry(h_end, p_end, down):
        f = h_end
        for _ in range(TIME_CHUNKS - 1):
            f = h_end + p_end * _shift_chunks(f, down)
        return _shift_chunks(f, down)

    cin_f = chunk_carry(hf_end, pf_end, True)
    cin_b = chunk_carry(hb_end, pb_end, False)

    def fix_chunk(i, _):
        r0 = pl.multiple_of(i * rows, rows)
        reps = rows // SUBLANES
        hf = uf_ref[pl.ds(r0, rows), :] + af_ref[pl.ds(r0, rows), :] * jnp.tile(cin_f, (reps, 1))
        hb = ub_ref[pl.ds(r0, rows), :] + ab_ref[pl.ds(r0, rows), :] * jnp.tile(cin_b, (reps, 1))
        xe_store(r0, rows, hf + hb)
        return 0

    lax.fori_loop(0, S // rows, fix_chunk, 0)

    for s in range(TIME_CHUNKS):
        hsum = jnp.concatenate(
            [xe_ref[g, pl.ds(s, lc, stride=SUBLANES), :] for g in range(ng)], axis=1)
        y = yr_ref[pl.ds(s * lc, lc), :]
        o_ref[pl.ds(s * lc, lc), :] = (hsum * jax.nn.gelu(y, approximate=True)).astype(BF16)


def _lru(proj, conv_w, conv_b, wg, bg, lam, B, S, C):
    ct = LRU_CT
    nct = C // ct
    rows = min(LRU_ROWS, S)
    return pl.pallas_call(
        functools.partial(_lru_kernel, rows=rows),
        grid=(B, nct),
        in_specs=[
            pl.BlockSpec((S, ct), lambda b, c: (b, c)),
            pl.BlockSpec((S, ct), lambda b, c: (b, nct + c)),
            pl.BlockSpec((4, ct), lambda b, c: (0, c)),
            pl.BlockSpec((1, ct), lambda b, c: (0, c)),
            pl.BlockSpec((1, ct, 4 * ct), lambda b, c: (c, 0, 0)),
            pl.BlockSpec((1, 1, 4 * ct), lambda b, c: (c, 0, 0)),
            pl.BlockSpec((2, ct), lambda b, c: (0, c)),
        ],
        out_specs=pl.BlockSpec((S, ct), lambda b, c: (b, c)),
        out_shape=jax.ShapeDtypeStruct((B * S, C), BF16),
        scratch_shapes=[
            pltpu.VMEM((ct // LANES, S + 3 * SUBLANES, LANES), F32),
            pltpu.VMEM((S, ct), F32), pltpu.VMEM((S, ct), F32),
            pltpu.VMEM((S, ct), F32), pltpu.VMEM((S, ct), F32),
        ],
        compiler_params=_cparams("parallel", "parallel"),
        name="lru",
    )(proj, proj, conv_w, conv_b, wg, bg, lam)


def _merge_kernel(x_ref, attn_ref, rec_ref, gl_ref, wa_ref, wr_ref, wo_ref, g2_ref,
                  rw_hi_ref, rw_lo_ref, rb_ref, x2_ref, h2_ref, lg_ref):
    D = x_ref.shape[1]
    gl = gl_ref[...]
    ma = jnp.dot(attn_ref[...], wa_ref[...], preferred_element_type=F32)
    mr = jnp.dot(rec_ref[...], wr_ref[...], preferred_element_type=F32)
    merged = jax.nn.sigmoid(gl[:, :D]) * ma + jax.nn.sigmoid(gl[:, D:]) * mr
    x2 = x_ref[...] + jnp.dot(merged.astype(BF16), wo_ref[...], preferred_element_type=F32)
    x2_ref[...] = x2
    h2 = _rms(x2, g2_ref[...])
    hi = h2.astype(BF16)
    lo = (h2 - hi.astype(F32)).astype(BF16)
    lg_ref[...] = (jnp.dot(hi, rw_hi_ref[...], preferred_element_type=F32)
                   + jnp.dot(lo, rw_hi_ref[...], preferred_element_type=F32)
                   + jnp.dot(hi, rw_lo_ref[...], preferred_element_type=F32)) + rb_ref[...]
    tm = h2.shape[0]
    bits = lax.bitcast_convert_type(hi.astype(F32), U32)
    words = bits[:, :D // 2] | (bits[:, D // 2:] >> 16)
    for j in range(PACK_LINES):
        h2_ref[pl.ds(j, tm, stride=SUBLANES), :] = words[:, j * LANES:(j + 1) * LANES]
    for j in range(PACK_LINES, SUBLANES):
        h2_ref[pl.ds(j, tm, stride=SUBLANES), :] = jnp.zeros((tm, LANES), U32)


def _merge(x2d, attn, rec, proj, wa, wr, wo, g2, rw_hi, rw_lo, rb):
    T, D = x2d.shape
    tm = min(ROW_TILE, T)
    nj = D // LANES
    full = lambda a: pl.BlockSpec(a.shape, lambda i: (0,) * a.ndim)
    return pl.pallas_call(
        _merge_kernel,
        grid=(T // tm,),
        in_specs=[
            pl.BlockSpec((tm, D), lambda i: (i, 0)),
            pl.BlockSpec((tm, Q_WIDTH), lambda i: (i, 0)),
            pl.BlockSpec((tm, D), lambda i: (i, 0)),
            pl.BlockSpec((tm, 2 * D), lambda i: (i, 1)),
            full(wa), full(wr), full(wo), full(g2), full(rw_hi), full(rw_lo), full(rb),
        ],
        out_specs=[
            pl.BlockSpec((tm, D), lambda i: (i, 0)),
            pl.BlockSpec((tm * nj, LANES), lambda i: (i, 0)),
            pl.BlockSpec((tm, LANES), lambda i: (i, 0)),
        ],
        out_shape=[
            jax.ShapeDtypeStruct((T, D), F32),
            jax.ShapeDtypeStruct((T * nj, LANES), U32),
            jax.ShapeDtypeStruct((T, LANES), F32),
        ],
        compiler_params=_cparams("parallel"),
        name="merge",
    )(x2d, attn, rec, proj, wa, wr, wo, g2, rw_hi, rw_lo, rb)


def _route_kernel(lg_ref, gate_ref, eidx_ref, rank_ref, cnt_ref):
    i = pl.program_id(0)

    @pl.when(i == 0)
    def _():
        cnt_ref[...] = jnp.zeros_like(cnt_ref)

    lg = lg_ref[...]
    tm = lg.shape[0]
    lane = lax.broadcasted_iota(I32, lg.shape, 1)
    work = lg
    sels, vals, idxs = [], [], []
    for _ in range(TOP_K):
        mx = jnp.max(work, axis=-1, keepdims=True)
        idx = jnp.min(jnp.where(work == mx, lane, LANES), axis=-1, keepdims=True)
        sel = lane == idx
        sels.append(sel)
        vals.append(mx)
        idxs.append(idx)
        work = jnp.where(sel, NEG_BIG * 2.0, work)
    ex = [jnp.exp(v - vals[0]) for v in vals]
    den = ex[0] + ex[1] + ex[2] + ex[3]
    member = (sels[0] | sels[1] | sels[2] | sels[3]).astype(BF16)
    r = lax.broadcasted_iota(I32, (tm, tm), 0)
    c = lax.broadcasted_iota(I32, (tm, tm), 1)
    tri = (c < r).astype(BF16)
    base = cnt_ref[0:1, :]
    before = jnp.dot(tri, member, preferred_element_type=F32) + base
    gate = jnp.zeros(lg.shape, F32)
    eidx = jnp.zeros(lg.shape, I32)
    rank = jnp.zeros(lg.shape, F32)
    for k in range(TOP_K):
        rk = jnp.sum(jnp.where(sels[k], before, 0.0), axis=-1, keepdims=True)
        gate = jnp.where(lane == k, ex[k] / den, gate)
        eidx = jnp.where(lane == k, idxs[k], eidx)
        rank = jnp.where(lane == k, rk, rank)
    gate_ref[...] = gate
    eidx_ref[...] = eidx
    rank_ref[...] = rank.astype(I32)
    cnt_ref[...] = jnp.broadcast_to(
        base + jnp.sum(member.astype(F32), axis=0, keepdims=True), cnt_ref.shape)


def _route(logits):
    T = logits.shape[0]
    tm = min(ROW_TILE, T)
    spec = pl.BlockSpec((tm, LANES), lambda i: (i, 0))
    return pl.pallas_call(
        _route_kernel,
        grid=(T // tm,),
        in_specs=[spec],
        out_specs=[spec, spec, spec, pl.BlockSpec((SUBLANES, LANES), lambda i: (0, 0))],
        out_shape=[
            jax.ShapeDtypeStruct((T, LANES), F32),
            jax.ShapeDtypeStruct((T, LANES), I32),
            jax.ShapeDtypeStruct((T, LANES), I32),
            jax.ShapeDtypeStruct((SUBLANES, LANES), F32),
        ],
        compiler_params=_cparams("arbitrary"),
        name="route",
    )(logits)


def _tile(ref, row):
    return ref.at[pl.ds(pl.multiple_of(row * SUBLANES, SUBLANES), SUBLANES)]


def _dispatch_kernel(dest_ref, pad_start_ref, pad_len_ref, nused_ref, h2_ref, buf_ref,
                     src_ref, pad_ref, sem, zsem, *, n_tokens):
    i = pl.program_id(0)
    tm = h2_ref.shape[0] // SUBLANES
    blk = pad_ref.shape[0] // SUBLANES
    nblocks = buf_ref.shape[0] // pad_ref.shape[0]

    @pl.when(i == 0)
    def _():
        li = lax.broadcasted_iota(I32, pad_ref.shape, 0)
        trash = (TOP_K * n_tokens + (li >> 3)).astype(U32)
        pad_ref[...] = jnp.where((li & (SUBLANES - 1)) >= PACK_LINES, trash, jnp.uint32(0))

        def per_expert(e, _):
            start = pad_start_ref[e]
            npad = pad_len_ref[e]

            def put(r, _):
                pltpu.make_async_copy(_tile(pad_ref, lax.rem(start + r, blk)),
                                      _tile(buf_ref, start + r), zsem).start()
                return 0

            def done(r, _):
                pltpu.make_async_copy(_tile(pad_ref, 0), _tile(buf_ref, 0), zsem).wait()
                return 0

            lax.fori_loop(0, npad, put, 0)
            lax.fori_loop(0, npad, done, 0)
            return 0

        lax.fori_loop(0, N_EXPERTS, per_expert, 0)

        def tail_block(b):
            rows = pad_ref.shape[0]
            return pltpu.make_async_copy(
                pad_ref, buf_ref.at[pl.ds(pl.multiple_of(b * rows, rows), rows)], zsem)

        def tail_put(b, _):
            tail_block(b).start()
            return 0

        def tail_done(b, _):
            tail_block(b).wait()
            return 0

        lax.fori_loop(nused_ref[0], nblocks, tail_put, 0)
        lax.fori_loop(nused_ref[0], nblocks, tail_done, 0)

    parity = lax.rem(i, 2)
    half = parity * TOP_K
    li = lax.broadcasted_iota(I32, h2_ref.shape, 0)
    tok = i * tm + (li >> 3)
    is_id = (li & (SUBLANES - 1)) >= PACK_LINES
    data = h2_ref[...]
    for k in range(TOP_K):
        src_ref[half + k] = jnp.where(is_id, (k * n_tokens + tok).astype(U32), data)

    def issue(r, _):
        for k in range(TOP_K):
            pltpu.make_async_copy(_tile(src_ref.at[half + k], r),
                                  _tile(buf_ref, dest_ref[r * TOP_K + k]),
                                  sem.at[parity]).start(priority=k % 2)
        return 0

    lax.fori_loop(0, tm, issue, 0)

    def wait_step(which):
        for k in range(TOP_K):
            pltpu.make_async_copy(src_ref.at[k], buf_ref.at[pl.ds(0, tm * SUBLANES)],
                                  sem.at[which]).wait()

    pl.when(i >= 1)(functools.partial(wait_step, 1 - parity))
    pl.when(i == pl.num_programs(0) - 1)(functools.partial(wait_step, parity))


def _dispatch(dest_flat, pad_start, pad_len, nused, h2_rows, n_rows, blk):
    T = h2_rows.shape[0] // SUBLANES
    tm = min(DMA_TILE, T)
    smem = pl.BlockSpec(memory_space=pltpu.SMEM)
    return pl.pallas_call(
        functools.partial(_dispatch_kernel, n_tokens=T),
        grid=(T // tm,),
        in_specs=[
            pl.BlockSpec((tm * TOP_K,), lambda i: (i,), memory_space=pltpu.SMEM),
            smem, smem, smem,
            pl.BlockSpec((tm * SUBLANES, LANES), lambda i: (i, 0)),
        ],
        out_specs=pl.BlockSpec(memory_space=pl.ANY),
        out_shape=jax.ShapeDtypeStruct((n_rows * SUBLANES, LANES), U32),
        scratch_shapes=[pltpu.VMEM((2 * TOP_K, tm * SUBLANES, LANES), U32),
                        pltpu.VMEM((blk * SUBLANES, LANES), U32),
                        pltpu.SemaphoreType.DMA((2,)), pltpu.SemaphoreType.DMA(())],
        compiler_params=_cparams("arbitrary"),
        name="dispatch",
    )(dest_flat, pad_start, pad_len, nused, h2_rows)


def _ffn_kernel(blk_exp_ref, nused_ref, x_ref, wgu_ref, bgu_ref, wd_ref, bd_ref, y_ref,
                wgu_bf_ref, wd_bf_ref, out_ref, idv_ref, ids_ref, ssem, isem, *, n_tokens):
    b = pl.program_id(0)
    nb = pl.num_programs(0)
    tm = x_ref.shape[0] // SUBLANES
    groups = tm // LANES
    F = wd_ref.shape[1]
    nused = nused_ref[0]
    used = b < nused
    slot = lax.rem(b, 2)
    prev = 1 - slot
    new_expert = (b == 0) | (blk_exp_ref[b] != blk_exp_ref[jnp.maximum(b - 1, 0)])

    def send_row(s, r_hi, r_lo, row, priority):
        rid = ids_ref[s * SUBLANES + r_hi, r_lo]
        pltpu.make_async_copy(_tile(out_ref, s * tm + row), _tile(y_ref, rid),
                              ssem).start(priority=priority)

    def send_block_unrolled(s):
        for r in range(tm):
            send_row(s, r // LANES, r % LANES, r, r % 2)

    def send_block_rolled(s):
        def body(g, _):
            def inner(l, _):
                send_row(s, g, l, g * LANES + l, 0)
                return 0
            return lax.fori_loop(0, LANES, inner, 0)
        lax.fori_loop(0, groups, body, 0)

    def wait_block():
        rows = tm * SUBLANES
        pltpu.make_async_copy(out_ref.at[pl.ds(0, rows)], y_ref.at[pl.ds(0, rows)], ssem).wait()

    def id_copy(s):
        return pltpu.make_async_copy(
            idv_ref, ids_ref.at[pl.ds(pl.multiple_of(s * SUBLANES, SUBLANES), SUBLANES)], isem)

    @pl.when(b == 0)
    def _():
        out_ref[...] = jnp.zeros_like(out_ref)
        pos = (lax.broadcasted_iota(I32, idv_ref.shape, 0) * LANES
               + lax.broadcasted_iota(I32, idv_ref.shape, 1))
        idv_ref[...] = TOP_K * n_tokens + jnp.minimum(pos, tm - 1)
        id_copy(1).start()
        id_copy(1).wait()

    @pl.when((b >= 1) & (b <= nused))
    def _():
        wait_block()
        id_copy(prev).wait()

    @pl.when(used & new_expert)
    def _():
        wgu_bf_ref[...] = wgu_ref[0].astype(BF16)
        wd_bf_ref[...] = wd_ref[0].astype(BF16)

    def block(slot, prev):
        send_block_unrolled(prev)
        words = [x_ref[pl.ds(j, tm, stride=SUBLANES), :] for j in range(PACK_LINES)]
        hi = [lax.bitcast_convert_type(w & jnp.uint32(0xFFFF0000), F32) for w in words]
        lo = [lax.bitcast_convert_type(w << 16, F32) for w in words]
        x = jnp.concatenate(hi + lo, axis=1).astype(BF16)
        gu = jnp.dot(x, wgu_bf_ref[...], preferred_element_type=F32) + bgu_ref[0]
        g = jnp.minimum(gu[:, :F], SWIGLU_LIMIT)
        u = jnp.clip(gu[:, F:], -SWIGLU_LIMIT, SWIGLU_LIMIT)
        act = g * jax.nn.sigmoid(SWIGLU_ALPHA * g) * (u + 1.0)
        y = jnp.dot(act.astype(BF16), wd_bf_ref[...], preferred_element_type=F32) + bd_ref[0]
        base = slot * (tm * SUBLANES)
        for j in range(SUBLANES):
            out_ref[pl.ds(base + j, tm, stride=SUBLANES), :] = y[:, j * LANES:(j + 1) * LANES]
        idl = x_ref[pl.ds(PACK_LINES, tm, stride=SUBLANES), :].astype(I32)
        rr = lax.broadcasted_iota(I32, idl.shape, 0)
        ll = lax.broadcasted_iota(I32, idl.shape, 1)
        diag = jnp.where((rr & (LANES - 1)) == ll, idl, 0)
        idv_ref[pl.ds(0, groups), :] = jnp.sum(diag.reshape(groups, LANES, LANES), axis=1)
        id_copy(slot).start()

    for parity in range(2):
        pl.when(used & (slot == parity))(functools.partial(block, parity, 1 - parity))

    def flush(s):
        id_copy(s).wait()
        send_block_rolled(s)
        wait_block()

    @pl.when(b == nused)
    def _():
        send_block_rolled(prev)
        wait_block()

    @pl.when(used & (b == nb - 1))
    def _():
        wait_block()
        flush(slot)


def _ffn(blk_exp, nused, buf, wgu, bgu, wd, bd, tm, n_tokens):
    nb = buf.shape[0] // (tm * SUBLANES)
    D = wgu.shape[1]
    F = wd.shape[1]

    def xmap(b, be, nu):
        return (jnp.minimum(b, nu[0] - 1), 0)

    def wmap(b, be, nu):
        return (be[jnp.minimum(b, nu[0] - 1)], 0, 0)

    grid_spec = pltpu.PrefetchScalarGridSpec(
        num_scalar_prefetch=2,
        grid=(nb,),
        in_specs=[
            pl.BlockSpec((tm * SUBLANES, LANES), xmap),
            pl.BlockSpec((1, D, 2 * F), wmap),
            pl.BlockSpec((1, 1, 2 * F), wmap),
            pl.BlockSpec((1, F, D), wmap),
            pl.BlockSpec((1, 1, D), wmap),
        ],
        out_specs=pl.BlockSpec(memory_space=pl.ANY),
        scratch_shapes=[
            pltpu.VMEM((D, 2 * F), BF16), pltpu.VMEM((F, D), BF16),
            pltpu.VMEM((2 * tm * SUBLANES, LANES), F32),
            pltpu.VMEM((SUBLANES, LANES), I32),
            pltpu.SMEM((2 * SUBLANES, LANES), I32),
            pltpu.SemaphoreType.DMA(()), pltpu.SemaphoreType.DMA(()),
        ],
    )
    return pl.pallas_call(
        functools.partial(_ffn_kernel, n_tokens=n_tokens),
        grid_spec=grid_spec,
        out_shape=jax.ShapeDtypeStruct(((TOP_K * n_tokens + tm) * SUBLANES, LANES), F32),
        compiler_params=_cparams("arbitrary"),
        name="ffn",
    )(blk_exp, nused, buf, wgu, bgu, wd, bd)


def _combine_kernel(gate_ref, x2_ref, fg_ref, *refs):
    y_refs, o_ref = refs[:TOP_K], refs[TOP_K]
    tm, D = x2_ref.shape
    nj = D // LANES
    gate = gate_ref[...]
    cols = []
    for j in range(nj):
        acc = None
        for k in range(TOP_K):
            part = gate[:, k:k + 1] * y_refs[k][pl.ds(j, tm, stride=nj), :]
            acc = part if acc is None else acc + part
        cols.append(acc)
    y = jnp.concatenate(cols, axis=1)
    o_ref[...] = _rms(x2_ref[...] + y, fg_ref[...])


def _combine(gate4, x2, fg, y_rows):
    T, D = x2.shape
    tm = min(ROW_TILE, T)
    nj = D // LANES
    nt = T // tm

    def slot_spec(k):
        return pl.BlockSpec((tm * nj, LANES), lambda i: (k * nt + i, 0))

    return pl.pallas_call(
        _combine_kernel,
        grid=(nt,),
        in_specs=[
            pl.BlockSpec((tm, LANES), lambda i: (i, 0)),
            pl.BlockSpec((tm, D), lambda i: (i, 0)),
            pl.BlockSpec((1, D), lambda i: (0, 0)),
        ] + [slot_spec(k) for k in range(TOP_K)],
        out_specs=pl.BlockSpec((tm, D), lambda i: (i, 0)),
        out_shape=jax.ShapeDtypeStruct((T, D), F32),
        compiler_params=_cparams("parallel"),
        name="combine",
    )(gate4, x2, fg, *([y_rows] * TOP_K))


def _rope_tables(S):
    rows = S // GRID_W
    row = jnp.repeat(jnp.arange(rows, dtype=I32), GRID_W).astype(F32)
    col = jnp.tile(jnp.arange(GRID_W, dtype=I32), rows).astype(F32)
    inv = ROPE_THETA ** (-jnp.arange(0, AXIS_DIM, 2, dtype=F32) / AXIS_DIM)
    ang_r = row[:, None] * inv[None, :]
    ang_c = col[:, None] * inv[None, :]
    cos = jnp.concatenate([jnp.cos(ang_r)] * 2 + [jnp.cos(ang_c)] * 2, axis=1)
    sin = jnp.concatenate([-jnp.sin(ang_r), jnp.sin(ang_r), -jnp.sin(ang_c), jnp.sin(ang_c)], axis=1)
    return jnp.tile(cos, (1, 2)), jnp.tile(sin, (1, 2))


def _block_diag_gates(wa, ba, wi, bi, ct):
    nb, bw = wa.shape[1], wa.shape[2]
    per = ct // bw
    nct = nb // per
    eye = jnp.eye(per, dtype=wa.dtype)

    def tiles(w):
        w = w.reshape(nct, per, bw, bw)
        return jnp.einsum('cpij,pq->cpiqj', w, eye).reshape(nct, ct, ct)

    wg = jnp.concatenate([tiles(wa[0]), tiles(wi[0]), tiles(wa[1]), tiles(wi[1])], axis=2)
    bias = lambda b: b.reshape(nct, 1, ct)
    bg = jnp.concatenate([bias(ba[0]), bias(bi[0]), bias(ba[1]), bias(bi[1])], axis=2)
    return (0.5 * wg).astype(BF16), 0.5 * bg


def _layer(x2d, B, S, norm1_g, w_in, b_in, q_norm_g, k_norm_g, conv_w, conv_b, lru_wa, lru_ba,
           lru_wi, lru_bi, lru_lambda, w_attn_o, w_lru_o, w_out, norm2_g, w_router, b_router,
           w_gu, b_gu, w_down, b_down, out_g):
    T, D = x2d.shape
    C = conv_w.shape[1]
    nqkv = Q_WIDTH + 2 * KV_WIDTH
    row2 = lambda v: v.reshape(1, -1)

    cos_t, sin_t = _rope_tables(S)
    head = jnp.arange(LANES) // HEAD_DIM
    hsum = (head[:, None] == head[None, :]).astype(BF16)
    q, k, v = _qkv(x2d, row2(norm1_g), w_in[:, :nqkv].astype(BF16), row2(b_in[:nqkv]),
                   cos_t, sin_t, row2(jnp.tile(q_norm_g, 2)), row2(jnp.tile(k_norm_g, 2)),
                   hsum, B, S)
    proj = _proj(x2d, row2(norm1_g), w_in[:, nqkv:].astype(BF16), row2(b_in[nqkv:]))
    attn = _attn(q, k, v)
    wg, bg = _block_diag_gates(lru_wa, lru_ba, lru_wi, lru_bi, LRU_CT)
    rec = _lru(proj, conv_w, row2(conv_b), wg, bg, lru_lambda, B, S, C)

    pad = LANES - N_EXPERTS
    rw = jnp.pad(w_router, ((0, 0), (0, pad)))
    rw_hi = rw.astype(BF16)
    rw_lo = (rw - rw_hi.astype(F32)).astype(BF16)
    rb = jnp.pad(b_router, (0, pad), constant_values=NEG_BIG).reshape(1, LANES)
    x2, h2_rows, logits = _merge(x2d, attn, rec, proj, w_attn_o.astype(BF16),
                                 w_lru_o.astype(BF16), w_out.astype(BF16), row2(norm2_g),
                                 rw_hi, rw_lo, rb)

    gate4, eidx4, rank4, cnt = _route(logits)
    counts = cnt[0, :N_EXPERTS].astype(I32)
    tm = FFN_TM
    nblk = (counts + tm - 1) // tm
    pend_blk = jnp.cumsum(nblk)
    pstart = (pend_blk - nblk) * tm
    A = T * TOP_K
    nb = (A + N_EXPERTS * (tm - 1) + tm - 1) // tm
    blk_exp = jnp.minimum(
        jnp.sum(pend_blk[None, :] <= jnp.arange(nb, dtype=I32)[:, None], axis=1),
        N_EXPERTS - 1).astype(I32)
    nused = pend_blk[-1:].astype(I32)
    dest = (pstart[eidx4[:, :TOP_K]] + rank4[:, :TOP_K]).astype(I32).reshape(A)
    pad_start = (pstart + counts).astype(I32)
    pad_len = (nblk * tm - counts).astype(I32)

    buf = _dispatch(dest, pad_start, pad_len, nused, h2_rows, nb * tm, tm)
    y_rows = _ffn(blk_exp, nused, buf, w_gu, b_gu.reshape(N_EXPERTS, 1, -1), w_down,
                  b_down.reshape(N_EXPERTS, 1, -1), tm, T)
    return _combine(gate4, x2, row2(out_g), y_rows)


def kernel(x, norm1_g, w_in, b_in, q_norm_g, k_norm_g, conv_w, conv_b, lru_wa, lru_ba, lru_wi,
           lru_bi, lru_lambda, w_attn_o, w_lru_o, w_out, norm2_g, w_router, b_router, w_gu, b_gu,
           w_down, b_down, final_g):
    B, S, D = x.shape
    depth = norm1_g.shape[0]
    assert depth == 1, "the fused final RMSNorm assumes a single layer"
    assert S % (TIME_CHUNKS * SUBLANES) == 0 and S % GRID_W == 0
    assert D == 2 * PACK_LINES * LANES, "a token's bf16 row must fill PACK_LINES word lines"
    out = _layer(x.reshape(B * S, D), B, S, norm1_g[0], w_in[0], b_in[0], q_norm_g[0],
                 k_norm_g[0], conv_w[0], conv_b[0], lru_wa[0], lru_ba[0], lru_wi[0], lru_bi[0],
                 lru_lambda[0], w_attn_o[0], w_lru_o[0], w_out[0], norm2_g[0], w_router[0],
                 b_router[0], w_gu[0], b_gu[0], w_down[0], b_down[0], final_g)
    return out.reshape(B, S, D)
```

```python
import functools
import math

import jax
import jax.numpy as jnp
from jax import lax
from jax.experimental import pallas as pl
from jax.experimental.pallas import tpu as pltpu

F32 = jnp.float32
BF16 = jnp.bfloat16
I32 = jnp.int32
U32 = jnp.uint32

LANES = 128
SUBLANES = 8
PACK_LINES = 4
VMEM_LIMIT_BYTES = 56 * 1024 * 1024

HEAD_DIM = 64
N_Q_HEADS = 8
N_KV_HEADS = 2
Q_GROUP = N_Q_HEADS // N_KV_HEADS
Q_WIDTH = N_Q_HEADS * HEAD_DIM
KV_WIDTH = N_KV_HEADS * HEAD_DIM
AXIS_DIM = HEAD_DIM // 2
ROT_HALF = AXIS_DIM // 2
ROPE_THETA = 10000.0
GRID_W = 64
LRU_BLOCKS = 16
RG_C = 8.0
N_EXPERTS = 32
TOP_K = 4
SWIGLU_LIMIT = 7.0
SWIGLU_ALPHA = 1.702
NORM_EPS = 1e-6
LOG2E = 1.4426950408889634
NEG_BIG = -1e30
TINY = 1e-30

ROW_TILE = 512
ATT_TQ = 512
ATT_TK = 512
LRU_CT = 256
LRU_ROWS = 256
TIME_CHUNKS = SUBLANES
FFN_TM = 512
FFN_SLICES = 4
DMA_TILE = 256


def _cparams(*sem):
    return pltpu.CompilerParams(dimension_semantics=sem, vmem_limit_bytes=VMEM_LIMIT_BYTES)


def _rms(x, g):
    return x * lax.rsqrt(jnp.mean(x * x, axis=-1, keepdims=True) + NORM_EPS) * g


def _qkv_kernel(x_ref, g1_ref, w_ref, b_ref, cos_ref, sin_ref, qg_ref, kg_ref, hsum_ref,
                q_ref, k_ref, v_ref):
    h = _rms(x_ref[...], g1_ref[...]).astype(BF16)
    p = jnp.dot(h, w_ref[...], preferred_element_type=F32) + b_ref[...]
    cos = cos_ref[...]
    sin = sin_ref[...]
    hsum = hsum_ref[...]
    lane = lax.broadcasted_iota(I32, cos.shape, 1)
    first_half = (lane % AXIS_DIM) < ROT_HALF

    def norm_rope(c, gain):
        sq = c * c
        hi = sq.astype(BF16)
        lo = (sq - hi.astype(F32)).astype(BF16)
        ms = (jnp.dot(hi, hsum, preferred_element_type=F32)
              + jnp.dot(lo, hsum, preferred_element_type=F32)) * (1.0 / HEAD_DIM)
        y = c * lax.rsqrt(ms + NORM_EPS) * gain
        partner = jnp.where(first_half, pltpu.roll(y, LANES - ROT_HALF, 1),
                            pltpu.roll(y, ROT_HALF, 1))
        return y * cos + partner * sin

    qg = qg_ref[...]
    for c in range(Q_WIDTH // LANES):
        y = norm_rope(p[:, c * LANES:(c + 1) * LANES], qg) * (HEAD_DIM ** -0.5 * LOG2E)
        yt = y.T.astype(BF16)
        q_ref[0, 2 * c] = yt[:HEAD_DIM]
        q_ref[0, 2 * c + 1] = yt[HEAD_DIM:]
    yk = norm_rope(p[:, Q_WIDTH:Q_WIDTH + KV_WIDTH], kg_ref[...])
    k_ref[0, 0] = yk[:, :HEAD_DIM].astype(BF16)
    k_ref[0, 1] = yk[:, HEAD_DIM:].astype(BF16)
    vt = p[:, Q_WIDTH + KV_WIDTH:].T
    tm = vt.shape[1]
    ones_row = (lax.broadcasted_iota(I32, (HEAD_DIM, tm), 0) == 0).astype(BF16)
    v_ref[0, 0] = jnp.concatenate([vt[:HEAD_DIM].astype(BF16), ones_row], axis=0)
    v_ref[0, 1] = jnp.concatenate([vt[HEAD_DIM:].astype(BF16), ones_row], axis=0)


def _qkv(x2d, g1, w_qkv, b_qkv, cos_t, sin_t, qg, kg, hsum, B, S):
    T, D = x2d.shape
    tm = min(ROW_TILE, S)
    ns = S // tm
    n = w_qkv.shape[1]
    full = lambda shape: pl.BlockSpec(shape, lambda i: (0,) * len(shape))
    return pl.pallas_call(
        _qkv_kernel,
        grid=(T // tm,),
        in_specs=[
            pl.BlockSpec((tm, D), lambda i: (i, 0)),
            full((1, D)), full((D, n)), full((1, n)),
            pl.BlockSpec((tm, LANES), lambda i: (i % ns, 0)),
            pl.BlockSpec((tm, LANES), lambda i: (i % ns, 0)),
            full((1, LANES)), full((1, LANES)), full((LANES, LANES)),
        ],
        out_specs=[
            pl.BlockSpec((1, N_Q_HEADS, HEAD_DIM, tm), lambda i: (i // ns, 0, 0, i % ns)),
            pl.BlockSpec((1, N_KV_HEADS, tm, HEAD_DIM), lambda i: (i // ns, 0, i % ns, 0)),
            pl.BlockSpec((1, N_KV_HEADS, LANES, tm), lambda i: (i // ns, 0, 0, i % ns)),
        ],
        out_shape=[
            jax.ShapeDtypeStruct((B, N_Q_HEADS, HEAD_DIM, S), BF16),
            jax.ShapeDtypeStruct((B, N_KV_HEADS, S, HEAD_DIM), BF16),
            jax.ShapeDtypeStruct((B, N_KV_HEADS, LANES, S), BF16),
        ],
        compiler_params=_cparams("parallel"),
        name="qkv",
    )(x2d, g1, w_qkv, b_qkv, cos_t, sin_t, qg, kg, hsum)


def _proj_kernel(x_ref, g_ref, w_ref, b_ref, o_ref):
    h = _rms(x_ref[...], g_ref[...]).astype(BF16)
    o_ref[...] = jnp.dot(h, w_ref[...], preferred_element_type=F32) + b_ref[...]


def _proj(x2d, g, w, b):
    T, D = x2d.shape
    n = w.shape[1]
    tm = min(ROW_TILE, T)
    return pl.pallas_call(
        _proj_kernel,
        grid=(T // tm,),
        in_specs=[
            pl.BlockSpec((tm, D), lambda i: (i, 0)),
            pl.BlockSpec((1, D), lambda i: (0, 0)),
            pl.BlockSpec((D, n), lambda i: (0, 0)),
            pl.BlockSpec((1, n), lambda i: (0, 0)),
        ],
        out_specs=pl.BlockSpec((tm, n), lambda i: (i, 0)),
        out_shape=jax.ShapeDtypeStruct((T, n), F32),
        compiler_params=_cparams("parallel"),
        name="proj",
    )(x2d, g, w, b)


def _attn_kernel(q_ref, k_ref, v_ref, o_ref, s0_ref, s1_ref, *, tk):
    tq = q_ref.shape[3]
    S = k_ref.shape[2]
    M = Q_GROUP * tq
    n = S // tk
    qT = jnp.concatenate([q_ref[0, h] for h in range(Q_GROUP)], axis=1)

    def scores(j, s_ref):
        off = pl.multiple_of(j * tk, tk)
        s_ref[...] = jnp.dot(k_ref[0, 0, pl.ds(off, tk), :], qT, preferred_element_type=F32)

    def absorb(j, s_ref, carry):
        m, acc = carry
        off = pl.multiple_of(j * tk, tk)
        vc = v_ref[0, 0, :, pl.ds(off, tk)]
        s = s_ref[...]
        m_new = jnp.maximum(m, jnp.max(s, axis=0, keepdims=True))
        alpha = jnp.exp2(m - m_new)
        p = jnp.exp2(s - m_new).astype(BF16)
        return m_new, alpha * acc + jnp.dot(vc, p, preferred_element_type=F32)

    def pair(i, carry):
        scores(2 * i + 1, s1_ref)
        carry = absorb(2 * i, s0_ref, carry)
        scores(2 * i + 2, s0_ref)
        return absorb(2 * i + 1, s1_ref, carry)

    scores(0, s0_ref)
    carry = (jnp.full((1, M), NEG_BIG, F32), jnp.zeros((LANES, M), F32))
    carry = lax.fori_loop(0, n // 2 - 1, pair, carry)
    scores(n - 1, s1_ref)
    carry = absorb(n - 2, s0_ref, carry)
    _, acc = absorb(n - 1, s1_ref, carry)
    outT = acc[:HEAD_DIM] / acc[HEAD_DIM:HEAD_DIM + 1]
    stacked = jnp.concatenate(
        [outT[:, g * tq:(g + 1) * tq] for g in range(Q_GROUP)], axis=0)
    o_ref[...] = stacked.T.astype(BF16)


def _attn(qT, k, vT):
    B, _, _, S = qT.shape
    tq = min(ATT_TQ, S)
    tk = min(ATT_TK, S)
    nq = S // tq
    assert (S // tk) % 2 == 0, "key chunks are processed in pairs"
    score_buf = pltpu.VMEM((tk, Q_GROUP * tq), F32)
    return pl.pallas_call(
        functools.partial(_attn_kernel, tk=tk),
        grid=(B, N_KV_HEADS, nq),
        in_specs=[
            pl.BlockSpec((1, Q_GROUP, HEAD_DIM, tq), lambda b, g, i: (b, g, 0, i)),
            pl.BlockSpec((1, 1, S, HEAD_DIM), lambda b, g, i: (b, g, 0, 0)),
            pl.BlockSpec((1, 1, LANES, S), lambda b, g, i: (b, g, 0, 0)),
        ],
        out_specs=pl.BlockSpec((tq, Q_GROUP * HEAD_DIM), lambda b, g, i: (b * nq + i, g)),
        out_shape=jax.ShapeDtypeStruct((B * S, Q_WIDTH), BF16),
        scratch_shapes=[score_buf, score_buf],
        compiler_params=_cparams("parallel", "parallel", "parallel"),
        name="attn",
    )(qT, k, vT)


def _shift_chunks(v, down):
    row = lax.broadcasted_iota(I32, v.shape, 0)
    if down:
        return jnp.where(row == 0, 0.0, pltpu.roll(v, 1, 0))
    return jnp.where(row == SUBLANES - 1, 0.0, pltpu.roll(v, SUBLANES - 1, 0))


def _lru_kernel(xr_ref, yr_ref, cw_ref, cb_ref, wg_ref, bg_ref, lam_ref, o_ref,
                xe_ref, af_ref, uf_ref, ab_ref, ub_ref, *, rows):
    S, ct = xr_ref.shape
    lc = S // TIME_CHUNKS
    halo = SUBLANES

    ng = ct // LANES

    def xe_rows(r0, n):
        return jnp.concatenate([xe_ref[g, pl.ds(r0, n), :] for g in range(ng)], axis=1)

    def xe_store(r0, n, val):
        for g in range(ng):
            xe_ref[g, pl.ds(r0, n), :] = val[:, g * LANES:(g + 1) * LANES]

    for s in range(TIME_CHUNKS):
        for g in range(ng):
            xe_ref[g, pl.ds(halo + s, lc, stride=SUBLANES), :] = (
                xr_ref[pl.ds(s * lc, lc), g * LANES:(g + 1) * LANES])
    xe_store(0, halo, _shift_chunks(xe_rows(S, halo), True))
    first = xe_rows(halo, halo)
    second = xe_rows(2 * halo, halo)
    xe_store(S + halo, halo, _shift_chunks(first, False))
    xe_store(S + 2 * halo, halo, _shift_chunks(second, False))

    cw = cw_ref[...]
    cb = cb_ref[...]
    lam = lam_ref[...]
    log_sig = jnp.minimum(lam, 0.0) - jnp.log(1.0 + jnp.exp(-jnp.abs(lam)))
    c_half = (0.5 * RG_C * LOG2E) * log_sig
    wg = wg_ref[0]
    bg = bg_ref[0]

    def gate_chunk(i, _):
        r0 = pl.multiple_of(i * rows, rows)
        xc = cb
        for j in range(4):
            xc = xc + cw[j:j + 1, :] * xe_rows(r0 + j * halo, rows)
        t = jnp.tanh(jnp.dot(xc.astype(BF16), wg, preferred_element_type=F32) + bg)
        x_half = 0.5 * xc
        for d, (a_ref, u_ref) in enumerate(((af_ref, uf_ref), (ab_ref, ub_ref))):
            t_r = t[:, (2 * d) * ct:(2 * d + 1) * ct]
            t_i = t[:, (2 * d + 1) * ct:(2 * d + 2) * ct]
            ch = c_half[d:d + 1, :]
            a = jnp.exp2(t_r * ch + ch)
            a_ref[pl.ds(r0, rows), :] = a
            v = 1.0 - a * a
            root = jnp.maximum(v, 0.0) * lax.rsqrt(jnp.maximum(v, TINY))
            u_ref[pl.ds(r0, rows), :] = root * ((t_i + 1.0) * x_half)
        return 0

    lax.fori_loop(0, S // rows, gate_chunk, 0)

    unroll = 8

    def scan_body(i, carry):
        hf, pf, hb, pb = carry
        for k in range(unroll):
            tf = i * unroll + k
            rf = pl.multiple_of(tf * SUBLANES, SUBLANES)
            rb = pl.multiple_of((lc - 1 - tf) * SUBLANES, SUBLANES)
            a = af_ref[pl.ds(rf, SUBLANES), :]
            hf = a * hf + uf_ref[pl.ds(rf, SUBLANES), :]
            pf = a * pf
            uf_ref[pl.ds(rf, SUBLANES), :] = hf
            af_ref[pl.ds(rf, SUBLANES), :] = pf
            a = ab_ref[pl.ds(rb, SUBLANES), :]
            hb = a * hb + ub_ref[pl.ds(rb, SUBLANES), :]
            pb = a * pb
            ub_ref[pl.ds(rb, SUBLANES), :] = hb
            ab_ref[pl.ds(rb, SUBLANES), :] = pb
        return hf, pf, hb, pb

    zero = jnp.zeros((SUBLANES, ct), F32)
    one = jnp.ones((SUBLANES, ct), F32)
    hf_end, pf_end, hb_end, pb_end = lax.fori_loop(0, lc // unroll, scan_body,
                                                   (zero, one, zero, one))

    def chunk_carry(h_end, p_end, down):
        f = h_end
        for _ in range(TIME_CHUNKS - 1):
            f = h_end + p_end * _shift_chunks(f, down)
        return _shift_chunks(f, down)

    cin_f = chunk_carry(hf_end, pf_end, True)
    cin_b = chunk_carry(hb_end, pb_end, False)

    def fix_chunk(i, _):
        r0 = pl.multiple_of(i * rows, rows)
        reps = rows // SUBLANES
        hf = uf_ref[pl.ds(r0, rows), :] + af_ref[pl.ds(r0, rows), :] * jnp.tile(cin_f, (reps, 1))
        hb = ub_ref[pl.ds(r0, rows), :] + ab_ref[pl.ds(r0, rows), :] * jnp.tile(cin_b, (reps, 1))
        xe_store(r0, rows, hf + hb)
        return 0

    lax.fori_loop(0, S // rows, fix_chunk, 0)

    for s in range(TIME_CHUNKS):
        hsum = jnp.concatenate(
            [xe_ref[g, pl.ds(s, lc, stride=SUBLANES), :] for g in range(ng)], axis=1)
        y = yr_ref[pl.ds(s * lc, lc), :]
        o_ref[pl.ds(s * lc, lc), :] = (hsum * jax.nn.gelu(y, approximate=True)).astype(BF16)


def _lru(proj, conv_w, conv_b, wg, bg, lam, B, S, C):
    ct = LRU_CT
    nct = C // ct
    rows = min(LRU_ROWS, S)
    return pl.pallas_call(
        functools.partial(_lru_kernel, rows=rows),
        grid=(B, nct),
        in_specs=[
            pl.BlockSpec((S, ct), lambda b, c: (b, c)),
            pl.BlockSpec((S, ct), lambda b, c: (b, nct + c)),
            pl.BlockSpec((4, ct), lambda b, c: (0, c)),
            pl.BlockSpec((1, ct), lambda b, c: (0, c)),
            pl.BlockSpec((1, ct, 4 * ct), lambda b, c: (c, 0, 0)),
            pl.BlockSpec((1, 1, 4 * ct), lambda b, c: (c, 0, 0)),
            pl.BlockSpec((2, ct), lambda b, c: (0, c)),
        ],
        out_specs=pl.BlockSpec((S, ct), lambda b, c: (b, c)),
        out_shape=jax.ShapeDtypeStruct((B * S, C), BF16),
        scratch_shapes=[
            pltpu.VMEM((ct // LANES, S + 3 * SUBLANES, LANES), F32),
            pltpu.VMEM((S, ct), F32), pltpu.VMEM((S, ct), F32),
            pltpu.VMEM((S, ct), F32), pltpu.VMEM((S, ct), F32),
        ],
        compiler_params=_cparams("parallel", "parallel"),
        name="lru",
    )(proj, proj, conv_w, conv_b, wg, bg, lam)


def _merge_kernel(x_ref, attn_ref, rec_ref, gl_ref, wa_ref, wr_ref, wo_ref, g2_ref,
                  rw_hi_ref, rw_lo_ref, rb_ref, x2_ref, h2_ref, lg_ref):
    D = x_ref.shape[1]
    gl = gl_ref[...]
    ma = jnp.dot(attn_ref[...], wa_ref[...], preferred_element_type=F32)
    mr = jnp.dot(rec_ref[...], wr_ref[...], preferred_element_type=F32)
    merged = jax.nn.sigmoid(gl[:, :D]) * ma + jax.nn.sigmoid(gl[:, D:]) * mr
    x2 = x_ref[...] + jnp.dot(merged.astype(BF16), wo_ref[...], preferred_element_type=F32)
    x2_ref[...] = x2
    h2 = _rms(x2, g2_ref[...])
    hi = h2.astype(BF16)
    lo = (h2 - hi.astype(F32)).astype(BF16)
    lg_ref[...] = (jnp.dot(hi, rw_hi_ref[...], preferred_element_type=F32)
                   + jnp.dot(lo, rw_hi_ref[...], preferred_element_type=F32)
                   + jnp.dot(hi, rw_lo_ref[...], preferred_element_type=F32)) + rb_ref[...]
    tm = h2.shape[0]
    bits = lax.bitcast_convert_type(hi.astype(F32), U32)
    words = bits[:, :D // 2] | (bits[:, D // 2:] >> 16)
    for j in range(PACK_LINES):
        h2_ref[pl.ds(j, tm, stride=SUBLANES), :] = words[:, j * LANES:(j + 1) * LANES]
    for j in range(PACK_LINES, SUBLANES):
        h2_ref[pl.ds(j, tm, stride=SUBLANES), :] = jnp.zeros((tm, LANES), U32)


def _merge(x2d, attn, rec, proj, wa, wr, wo, g2, rw_hi, rw_lo, rb):
    T, D = x2d.shape
    tm = min(ROW_TILE, T)
    nj = D // LANES
    full = lambda a: pl.BlockSpec(a.shape, lambda i: (0,) * a.ndim)
    return pl.pallas_call(
        _merge_kernel,
        grid=(T // tm,),
        in_specs=[
            pl.BlockSpec((tm, D), lambda i: (i, 0)),
            pl.BlockSpec((tm, Q_WIDTH), lambda i: (i, 0)),
            pl.BlockSpec((tm, D), lambda i: (i, 0)),
            pl.BlockSpec((tm, 2 * D), lambda i: (i, 1)),
            full(wa), full(wr), full(wo), full(g2), full(rw_hi), full(rw_lo), full(rb),
        ],
        out_specs=[
            pl.BlockSpec((tm, D), lambda i: (i, 0)),
            pl.BlockSpec((tm * nj, LANES), lambda i: (i, 0)),
            pl.BlockSpec((tm, LANES), lambda i: (i, 0)),
        ],
        out_shape=[
            jax.ShapeDtypeStruct((T, D), F32),
            jax.ShapeDtypeStruct((T * nj, LANES), U32),
            jax.ShapeDtypeStruct((T, LANES), F32),
        ],
        compiler_params=_cparams("parallel"),
        name="merge",
    )(x2d, attn, rec, proj, wa, wr, wo, g2, rw_hi, rw_lo, rb)


def _route_kernel(lg_ref, gate_ref, eidx_ref, rank_ref, cnt_ref):
    i = pl.program_id(0)

    @pl.when(i == 0)
    def _():
        cnt_ref[...] = jnp.zeros_like(cnt_ref)

    lg = lg_ref[...]
    tm = lg.shape[0]
    lane = lax.broadcasted_iota(I32, lg.shape, 1)
    work = lg
    sels, vals, idxs = [], [], []
    for _ in range(TOP_K):
        mx = jnp.max(work, axis=-1, keepdims=True)
        idx = jnp.min(jnp.where(work == mx, lane, LANES), axis=-1, keepdims=True)
        sel = lane == idx
        sels.append(sel)
        vals.append(mx)
        idxs.append(idx)
        work = jnp.where(sel, NEG_BIG * 2.0, work)
    ex = [jnp.exp(v - vals[0]) for v in vals]
    den = ex[0] + ex[1] + ex[2] + ex[3]
    member = (sels[0] | sels[1] | sels[2] | sels[3]).astype(BF16)
    r = lax.broadcasted_iota(I32, (tm, tm), 0)
    c = lax.broadcasted_iota(I32, (tm, tm), 1)
    tri = (c < r).astype(BF16)
    base = cnt_ref[0:1, :]
    before = jnp.dot(tri, member, preferred_element_type=F32) + base
    gate = jnp.zeros(lg.shape, F32)
    eidx = jnp.zeros(lg.shape, I32)
    rank = jnp.zeros(lg.shape, F32)
    for k in range(TOP_K):
        rk = jnp.sum(jnp.where(sels[k], before, 0.0), axis=-1, keepdims=True)
        gate = jnp.where(lane == k, ex[k] / den, gate)
        eidx = jnp.where(lane == k, idxs[k], eidx)
        rank = jnp.where(lane == k, rk, rank)
    gate_ref[...] = gate
    eidx_ref[...] = eidx
    rank_ref[...] = rank.astype(I32)
    cnt_ref[...] = jnp.broadcast_to(
        base + jnp.sum(member.astype(F32), axis=0, keepdims=True), cnt_ref.shape)


def _route(logits):
    T = logits.shape[0]
    tm = min(ROW_TILE, T)
    spec = pl.BlockSpec((tm, LANES), lambda i: (i, 0))
    return pl.pallas_call(
        _route_kernel,
        grid=(T // tm,),
        in_specs=[spec],
        out_specs=[spec, spec, spec, pl.BlockSpec((SUBLANES, LANES), lambda i: (0, 0))],
        out_shape=[
            jax.ShapeDtypeStruct((T, LANES), F32),
            jax.ShapeDtypeStruct((T, LANES), I32),
            jax.ShapeDtypeStruct((T, LANES), I32),
            jax.ShapeDtypeStruct((SUBLANES, LANES), F32),
        ],
        compiler_params=_cparams("arbitrary"),
        name="route",
    )(logits)


def _tile(ref, row):
    return ref.at[pl.ds(pl.multiple_of(row * SUBLANES, SUBLANES), SUBLANES)]


def _dispatch_kernel(dest_ref, pad_start_ref, pad_len_ref, nused_ref, h2_ref, buf_ref,
                     src_ref, pad_ref, sem, zsem, *, n_tokens):
    i = pl.program_id(0)
    tm = h2_ref.shape[0] // SUBLANES
    blk = pad_ref.shape[0] // (2 * SUBLANES)
    nblocks = buf_ref.shape[0] // (blk * SUBLANES)

    @pl.when(i == 0)
    def _():
        li = lax.broadcasted_iota(I32, pad_ref.shape, 0)
        trash = (TOP_K * n_tokens + (li >> 3)).astype(U32)
        pad_ref[...] = jnp.where((li & (SUBLANES - 1)) >= PACK_LINES, trash, jnp.uint32(0))

        def per_expert(e, _):
            start = pad_start_ref[e]
            npad = pad_len_ref[e]

            def put(r, _):
                pltpu.make_async_copy(_tile(pad_ref, lax.rem(start + r, 2 * blk)),
                                      _tile(buf_ref, start + r), zsem).start()
                return 0

            def done(r, _):
                pltpu.make_async_copy(_tile(pad_ref, 0), _tile(buf_ref, 0), zsem).wait()
                return 0

            lax.fori_loop(0, npad, put, 0)
            lax.fori_loop(0, npad, done, 0)
            return 0

        lax.fori_loop(0, N_EXPERTS, per_expert, 0)

        def tail_block(b):
            rows = blk * SUBLANES
            return pltpu.make_async_copy(
                pad_ref.at[pl.ds(0, rows)],
                buf_ref.at[pl.ds(pl.multiple_of(b * rows, rows), rows)], zsem)

        def tail_put(b, _):
            tail_block(b).start()
            return 0

        def tail_done(b, _):
            tail_block(b).wait()
            return 0

        lax.fori_loop(nused_ref[0], nblocks, tail_put, 0)
        lax.fori_loop(nused_ref[0], nblocks, tail_done, 0)

    parity = lax.rem(i, 2)
    half = parity * TOP_K
    li = lax.broadcasted_iota(I32, h2_ref.shape, 0)
    tok = i * tm + (li >> 3)
    is_id = (li & (SUBLANES - 1)) >= PACK_LINES
    data = h2_ref[...]
    for k in range(TOP_K):
        src_ref[half + k] = jnp.where(is_id, (k * n_tokens + tok).astype(U32), data)

    def issue(r, _):
        for k in range(TOP_K):
            pltpu.make_async_copy(_tile(src_ref.at[half + k], r),
                                  _tile(buf_ref, dest_ref[r * TOP_K + k]),
                                  sem.at[parity]).start(priority=k % 2)
        return 0

    lax.fori_loop(0, tm, issue, 0)

    def wait_step(which):
        for k in range(TOP_K):
            pltpu.make_async_copy(src_ref.at[k], buf_ref.at[pl.ds(0, tm * SUBLANES)],
                                  sem.at[which]).wait()

    pl.when(i >= 1)(functools.partial(wait_step, 1 - parity))
    pl.when(i == pl.num_programs(0) - 1)(functools.partial(wait_step, parity))


def _dispatch(dest_flat, pad_start, pad_len, nused, h2_rows, n_rows, blk):
    T = h2_rows.shape[0] // SUBLANES
    tm = min(DMA_TILE, T)
    smem = pl.BlockSpec(memory_space=pltpu.SMEM)
    return pl.pallas_call(
        functools.partial(_dispatch_kernel, n_tokens=T),
        grid=(T // tm,),
        in_specs=[
            pl.BlockSpec((tm * TOP_K,), lambda i: (i,), memory_space=pltpu.SMEM),
            smem, smem, smem,
            pl.BlockSpec((tm * SUBLANES, LANES), lambda i: (i, 0)),
        ],
        out_specs=pl.BlockSpec(memory_space=pl.ANY),
        out_shape=jax.ShapeDtypeStruct((n_rows * SUBLANES, LANES), U32),
        scratch_shapes=[pltpu.VMEM((2 * TOP_K, tm * SUBLANES, LANES), U32),
                        pltpu.VMEM((2 * blk * SUBLANES, LANES), U32),
                        pltpu.SemaphoreType.DMA((2,)), pltpu.SemaphoreType.DMA(())],
        compiler_params=_cparams("arbitrary"),
        name="dispatch",
    )(dest_flat, pad_start, pad_len, nused, h2_rows)


def _ffn_kernel(blk_exp_ref, nused_ref, x_ref, wgu_ref, bgu_ref, wd_ref, bd_ref, y_ref,
                wgu_bf_ref, wd_bf_ref, out_ref, idv_ref, ids_ref, ssem, isem, *, n_tokens):
    b = pl.program_id(0)
    nb = pl.num_programs(0)
    tm = x_ref.shape[0] // SUBLANES
    groups = tm // LANES
    F = wd_ref.shape[1]
    nused = nused_ref[0]
    used = b < nused
    slot = lax.rem(b, 2)
    prev = 1 - slot
    new_expert = (b == 0) | (blk_exp_ref[b] != blk_exp_ref[jnp.maximum(b - 1, 0)])

    def send_row(s, r_hi, r_lo, row, priority):
        rid = ids_ref[s * SUBLANES + r_hi, r_lo]
        pltpu.make_async_copy(_tile(out_ref, s * tm + row), _tile(y_ref, rid),
                              ssem.at[s]).start(priority=priority)

    def send_rows_unrolled(s, lo, hi):
        for r in range(lo, hi):
            send_row(s, r // LANES, r % LANES, r, r % 2)

    def send_block_rolled(s):
        def body(g, _):
            def inner(l, _):
                send_row(s, g, l, g * LANES + l, 0)
                return 0
            return lax.fori_loop(0, LANES, inner, 0)
        lax.fori_loop(0, groups, body, 0)

    def wait_block(s):
        rows = tm * SUBLANES
        pltpu.make_async_copy(out_ref.at[pl.ds(0, rows)], y_ref.at[pl.ds(0, rows)],
                              ssem.at[s]).wait()

    def id_copy(s):
        return pltpu.make_async_copy(
            idv_ref, ids_ref.at[pl.ds(pl.multiple_of(s * SUBLANES, SUBLANES), SUBLANES)], isem)

    @pl.when(b == 0)
    def _():
        out_ref[...] = jnp.zeros_like(out_ref)
        rows = tm * SUBLANES
        pltpu.make_async_copy(out_ref.at[pl.ds(0, rows)],
                              y_ref.at[pl.ds(TOP_K * n_tokens * SUBLANES, rows)],
                              ssem.at[0]).start()
        pos = (lax.broadcasted_iota(I32, idv_ref.shape, 0) * LANES
               + lax.broadcasted_iota(I32, idv_ref.shape, 1))
        idv_ref[...] = TOP_K * n_tokens + tm + jnp.minimum(pos, tm - 1)
        id_copy(1).start()
        id_copy(1).wait()

    @pl.when((b >= 1) & (b <= nused))
    def _():
        id_copy(prev).wait()

    @pl.when(used & new_expert)
    def _():
        wgu_bf_ref[...] = wgu_ref[0].astype(BF16)
        wd_bf_ref[...] = wd_ref[0].astype(BF16)

    def block(slot, prev):
        words = [x_ref[pl.ds(j, tm, stride=SUBLANES), :] for j in range(PACK_LINES)]
        hi = [lax.bitcast_convert_type(w & jnp.uint32(0xFFFF0000), F32) for w in words]
        lo = [lax.bitcast_convert_type(w << 16, F32) for w in words]
        x = jnp.concatenate(hi + lo, axis=1).astype(BF16)
        acts = []
        for c in range(FFN_SLICES):
            send_rows_unrolled(prev, c * tm // FFN_SLICES, (c + 1) * tm // FFN_SLICES)
            w = F // FFN_SLICES
            g = jnp.dot(x, wgu_bf_ref[:, c * w:(c + 1) * w], preferred_element_type=F32)
            u = jnp.dot(x, wgu_bf_ref[:, F + c * w:F + (c + 1) * w],
                        preferred_element_type=F32)
            g = jnp.minimum(g + bgu_ref[0, :, c * w:(c + 1) * w], SWIGLU_LIMIT)
            u = jnp.clip(u + bgu_ref[0, :, F + c * w:F + (c + 1) * w],
                         -SWIGLU_LIMIT, SWIGLU_LIMIT)
            acts.append((g * jax.nn.sigmoid(SWIGLU_ALPHA * g) * (u + 1.0)).astype(BF16))
        act = jnp.concatenate(acts, axis=1)
        y = jnp.dot(act, wd_bf_ref[...], preferred_element_type=F32) + bd_ref[0]
        wait_block(slot)
        base = slot * (tm * SUBLANES)
        for j in range(SUBLANES):
            out_ref[pl.ds(base + j, tm, stride=SUBLANES), :] = y[:, j * LANES:(j + 1) * LANES]
        idl = x_ref[pl.ds(PACK_LINES, tm, stride=SUBLANES), :].astype(I32)
        rr = lax.broadcasted_iota(I32, idl.shape, 0)
        ll = lax.broadcasted_iota(I32, idl.shape, 1)
        diag = jnp.where((rr & (LANES - 1)) == ll, idl, 0)
        idv_ref[pl.ds(0, groups), :] = jnp.sum(diag.reshape(groups, LANES, LANES), axis=1)
        id_copy(slot).start()

    for parity in range(2):
        pl.when(used & (slot == parity))(functools.partial(block, parity, 1 - parity))

    @pl.when(b == nused)
    def _():
        wait_block(slot)
        send_block_rolled(prev)
        wait_block(prev)

    @pl.when(used & (b == nb - 1))
    def _():
        wait_block(prev)
        id_copy(slot).wait()
        send_block_rolled(slot)
        wait_block(slot)


def _ffn(blk_exp, nused, buf, wgu, bgu, wd, bd, tm, n_tokens):
    nb = buf.shape[0] // (tm * SUBLANES)
    D = wgu.shape[1]
    F = wd.shape[1]

    def xmap(b, be, nu):
        return (jnp.minimum(b, nu[0] - 1), 0)

    def wmap(b, be, nu):
        return (be[jnp.minimum(b, nu[0] - 1)], 0, 0)

    grid_spec = pltpu.PrefetchScalarGridSpec(
        num_scalar_prefetch=2,
        grid=(nb,),
        in_specs=[
            pl.BlockSpec((tm * SUBLANES, LANES), xmap),
            pl.BlockSpec((1, D, 2 * F), wmap),
            pl.BlockSpec((1, 1, 2 * F), wmap),
            pl.BlockSpec((1, F, D), wmap),
            pl.BlockSpec((1, 1, D), wmap),
        ],
        out_specs=pl.BlockSpec(memory_space=pl.ANY),
        scratch_shapes=[
            pltpu.VMEM((D, 2 * F), BF16), pltpu.VMEM((F, D), BF16),
            pltpu.VMEM((2 * tm * SUBLANES, LANES), F32),
            pltpu.VMEM((SUBLANES, LANES), I32),
            pltpu.SMEM((2 * SUBLANES, LANES), I32),
            pltpu.SemaphoreType.DMA((2,)), pltpu.SemaphoreType.DMA(()),
        ],
    )
    return pl.pallas_call(
        functools.partial(_ffn_kernel, n_tokens=n_tokens),
        grid_spec=grid_spec,
        out_shape=jax.ShapeDtypeStruct(((TOP_K * n_tokens + 2 * tm) * SUBLANES, LANES), F32),
        compiler_params=_cparams("arbitrary"),
        name="ffn",
    )(blk_exp, nused, buf, wgu, bgu, wd, bd)


def _combine_kernel(gate_ref, x2_ref, fg_ref, *refs):
    y_refs, o_ref = refs[:TOP_K], refs[TOP_K]
    tm, D = x2_ref.shape
    nj = D // LANES
    gate = gate_ref[...]
    cols = []
    for j in range(nj):
        acc = None
        for k in range(TOP_K):
            part = gate[:, k:k + 1] * y_refs[k][pl.ds(j, tm, stride=nj), :]
            acc = part if acc is None else acc + part
        cols.append(acc)
    y = jnp.concatenate(cols, axis=1)
    o_ref[...] = _rms(x2_ref[...] + y, fg_ref[...])


def _combine(gate4, x2, fg, y_rows):
    T, D = x2.shape
    tm = min(ROW_TILE, T)
    nj = D // LANES
    nt = T // tm

    def slot_spec(k):
        return pl.BlockSpec((tm * nj, LANES), lambda i: (k * nt + i, 0))

    return pl.pallas_call(
        _combine_kernel,
        grid=(nt,),
        in_specs=[
            pl.BlockSpec((tm, LANES), lambda i: (i, 0)),
            pl.BlockSpec((tm, D), lambda i: (i, 0)),
            pl.BlockSpec((1, D), lambda i: (0, 0)),
        ] + [slot_spec(k) for k in range(TOP_K)],
        out_specs=pl.BlockSpec((tm, D), lambda i: (i, 0)),
        out_shape=jax.ShapeDtypeStruct((T, D), F32),
        compiler_params=_cparams("parallel"),
        name="combine",
    )(gate4, x2, fg, *([y_rows] * TOP_K))


def _rope_tables(S):
    rows = S // GRID_W
    row = jnp.repeat(jnp.arange(rows, dtype=I32), GRID_W).astype(F32)
    col = jnp.tile(jnp.arange(GRID_W, dtype=I32), rows).astype(F32)
    inv = ROPE_THETA ** (-jnp.arange(0, AXIS_DIM, 2, dtype=F32) / AXIS_DIM)
    ang_r = row[:, None] * inv[None, :]
    ang_c = col[:, None] * inv[None, :]
    cos = jnp.concatenate([jnp.cos(ang_r)] * 2 + [jnp.cos(ang_c)] * 2, axis=1)
    sin = jnp.concatenate([-jnp.sin(ang_r), jnp.sin(ang_r), -jnp.sin(ang_c), jnp.sin(ang_c)], axis=1)
    return jnp.tile(cos, (1, 2)), jnp.tile(sin, (1, 2))


def _block_diag_gates(wa, ba, wi, bi, ct):
    nb, bw = wa.shape[1], wa.shape[2]
    per = ct // bw
    nct = nb // per
    eye = jnp.eye(per, dtype=wa.dtype)

    def tiles(w):
        w = w.reshape(nct, per, bw, bw)
        return jnp.einsum('cpij,pq->cpiqj', w, eye).reshape(nct, ct, ct)

    wg = jnp.concatenate([tiles(wa[0]), tiles(wi[0]), tiles(wa[1]), tiles(wi[1])], axis=2)
    bias = lambda b: b.reshape(nct, 1, ct)
    bg = jnp.concatenate([bias(ba[0]), bias(bi[0]), bias(ba[1]), bias(bi[1])], axis=2)
    return (0.5 * wg).astype(BF16), 0.5 * bg


def _layer(x2d, B, S, norm1_g, w_in, b_in, q_norm_g, k_norm_g, conv_w, conv_b, lru_wa, lru_ba,
           lru_wi, lru_bi, lru_lambda, w_attn_o, w_lru_o, w_out, norm2_g, w_router, b_router,
           w_gu, b_gu, w_down, b_down, out_g):
    T, D = x2d.shape
    C = conv_w.shape[1]
    nqkv = Q_WIDTH + 2 * KV_WIDTH
    row2 = lambda v: v.reshape(1, -1)

    cos_t, sin_t = _rope_tables(S)
    head = jnp.arange(LANES) // HEAD_DIM
    hsum = (head[:, None] == head[None, :]).astype(BF16)
    q, k, v = _qkv(x2d, row2(norm1_g), w_in[:, :nqkv].astype(BF16), row2(b_in[:nqkv]),
                   cos_t, sin_t, row2(jnp.tile(q_norm_g, 2)), row2(jnp.tile(k_norm_g, 2)),
                   hsum, B, S)
    proj = _proj(x2d, row2(norm1_g), w_in[:, nqkv:].astype(BF16), row2(b_in[nqkv:]))
    attn = _attn(q, k, v)
    wg, bg = _block_diag_gates(lru_wa, lru_ba, lru_wi, lru_bi, LRU_CT)
    rec = _lru(proj, conv_w, row2(conv_b), wg, bg, lru_lambda, B, S, C)

    pad = LANES - N_EXPERTS
    rw = jnp.pad(w_router, ((0, 0), (0, pad)))
    rw_hi = rw.astype(BF16)
    rw_lo = (rw - rw_hi.astype(F32)).astype(BF16)
    rb = jnp.pad(b_router, (0, pad), constant_values=NEG_BIG).reshape(1, LANES)
    x2, h2_rows, logits = _merge(x2d, attn, rec, proj, w_attn_o.astype(BF16),
                                 w_lru_o.astype(BF16), w_out.astype(BF16), row2(norm2_g),
                                 rw_hi, rw_lo, rb)

    gate4, eidx4, rank4, cnt = _route(logits)
    counts = cnt[0, :N_EXPERTS].astype(I32)
    tm = FFN_TM
    nblk = (counts + tm - 1) // tm
    pend_blk = jnp.cumsum(nblk)
    pstart = (pend_blk - nblk) * tm
    A = T * TOP_K
    nb = (A + N_EXPERTS * (tm - 1) + tm - 1) // tm
    blk_exp = jnp.minimum(
        jnp.sum(pend_blk[None, :] <= jnp.arange(nb, dtype=I32)[:, None], axis=1),
        N_EXPERTS - 1).astype(I32)
    nused = pend_blk[-1:].astype(I32)
    dest = (pstart[eidx4[:, :TOP_K]] + rank4[:, :TOP_K]).astype(I32).reshape(A)
    pad_start = (pstart + counts).astype(I32)
    pad_len = (nblk * tm - counts).astype(I32)

    buf = _dispatch(dest, pad_start, pad_len, nused, h2_rows, nb * tm, tm)
    y_rows = _ffn(blk_exp, nused, buf, w_gu, b_gu.reshape(N_EXPERTS, 1, -1), w_down,
                  b_down.reshape(N_EXPERTS, 1, -1), tm, T)
    return _combine(gate4, x2, row2(out_g), y_rows)


def kernel(x, norm1_g, w_in, b_in, q_norm_g, k_norm_g, conv_w, conv_b, lru_wa, lru_ba, lru_wi,
           lru_bi, lru_lambda, w_attn_o, w_lru_o, w_out, norm2_g, w_router, b_router, w_gu, b_gu,
           w_down, b_down, final_g):
    B, S, D = x.shape
    depth = norm1_g.shape[0]
    assert depth == 1, "the fused final RMSNorm assumes a single layer"
    assert S % (TIME_CHUNKS * SUBLANES) == 0 and S % GRID_W == 0
    assert D == 2 * PACK_LINES * LANES, "a token's bf16 row must fill PACK_LINES word lines"
    out = _layer(x.reshape(B * S, D), B, S, norm1_g[0], w_in[0], b_in[0], q_norm_g[0],
                 k_norm_g[0], conv_w[0], conv_b[0], lru_wa[0], lru_ba[0], lru_wi[0], lru_bi[0],
                 lru_lambda[0], w_attn_o[0], w_lru_o[0], w_out[0], norm2_g[0], w_router[0],
                 b_router[0], w_gu[0], b_gu[0], w_down[0], b_down[0], final_g)
    return out.reshape(B, S, D)
```

```python
import functools
import math

import jax
import jax.numpy as jnp
from jax import lax
from jax.experimental import pallas as pl
from jax.experimental.pallas import tpu as pltpu

F32 = jnp.float32
BF16 = jnp.bfloat16
I32 = jnp.int32
U32 = jnp.uint32

LANES = 128
SUBLANES = 8
PACK_LINES = 4
VMEM_LIMIT_BYTES = 56 * 1024 * 1024

HEAD_DIM = 64
N_Q_HEADS = 8
N_KV_HEADS = 2
Q_GROUP = N_Q_HEADS // N_KV_HEADS
Q_WIDTH = N_Q_HEADS * HEAD_DIM
KV_WIDTH = N_KV_HEADS * HEAD_DIM
AXIS_DIM = HEAD_DIM // 2
ROT_HALF = AXIS_DIM // 2
ROPE_THETA = 10000.0
GRID_W = 64
LRU_BLOCKS = 16
RG_C = 8.0
N_EXPERTS = 32
TOP_K = 4
SWIGLU_LIMIT = 7.0
SWIGLU_ALPHA = 1.702
NORM_EPS = 1e-6
LOG2E = 1.4426950408889634
NEG_BIG = -1e30
TINY = 1e-30

ROW_TILE = 512
ATT_TQ = 512
ATT_TK = 512
LRU_CT = 256
LRU_ROWS = 256
TIME_CHUNKS = SUBLANES
FFN_TM = 512
FFN_SLICES = 4
DMA_TILE = 256


def _cparams(*sem):
    return pltpu.CompilerParams(dimension_semantics=sem, vmem_limit_bytes=VMEM_LIMIT_BYTES)


def _rms(x, g):
    return x * lax.rsqrt(jnp.mean(x * x, axis=-1, keepdims=True) + NORM_EPS) * g


def _qkv_kernel(x_ref, g1_ref, w_ref, b_ref, cos_ref, sin_ref, qg_ref, kg_ref, hsum_ref,
                q_ref, k_ref, v_ref):
    h = _rms(x_ref[...], g1_ref[...]).astype(BF16)
    p = jnp.dot(h, w_ref[...], preferred_element_type=F32) + b_ref[...]
    cos = cos_ref[...]
    sin = sin_ref[...]
    hsum = hsum_ref[...]
    lane = lax.broadcasted_iota(I32, cos.shape, 1)
    first_half = (lane % AXIS_DIM) < ROT_HALF

    def norm_rope(c, gain):
        sq = c * c
        hi = sq.astype(BF16)
        lo = (sq - hi.astype(F32)).astype(BF16)
        ms = (jnp.dot(hi, hsum, preferred_element_type=F32)
              + jnp.dot(lo, hsum, preferred_element_type=F32)) * (1.0 / HEAD_DIM)
        y = c * lax.rsqrt(ms + NORM_EPS) * gain
        partner = jnp.where(first_half, pltpu.roll(y, LANES - ROT_HALF, 1),
                            pltpu.roll(y, ROT_HALF, 1))
        return y * cos + partner * sin

    qg = qg_ref[...]
    for c in range(Q_WIDTH // LANES):
        y = norm_rope(p[:, c * LANES:(c + 1) * LANES], qg) * (HEAD_DIM ** -0.5 * LOG2E)
        yt = y.T.astype(BF16)
        q_ref[0, 2 * c] = yt[:HEAD_DIM]
        q_ref[0, 2 * c + 1] = yt[HEAD_DIM:]
    yk = norm_rope(p[:, Q_WIDTH:Q_WIDTH + KV_WIDTH], kg_ref[...])
    k_ref[0, 0] = yk[:, :HEAD_DIM].astype(BF16)
    k_ref[0, 1] = yk[:, HEAD_DIM:].astype(BF16)
    vt = p[:, Q_WIDTH + KV_WIDTH:].T
    tm = vt.shape[1]
    ones_row = (lax.broadcasted_iota(I32, (HEAD_DIM, tm), 0) == 0).astype(BF16)
    v_ref[0, 0] = jnp.concatenate([vt[:HEAD_DIM].astype(BF16), ones_row], axis=0)
    v_ref[0, 1] = jnp.concatenate([vt[HEAD_DIM:].astype(BF16), ones_row], axis=0)


def _qkv(x2d, g1, w_qkv, b_qkv, cos_t, sin_t, qg, kg, hsum, B, S):
    T, D = x2d.shape
    tm = min(ROW_TILE, S)
    ns = S // tm
    n = w_qkv.shape[1]
    full = lambda shape: pl.BlockSpec(shape, lambda i: (0,) * len(shape))
    return pl.pallas_call(
        _qkv_kernel,
        grid=(T // tm,),
        in_specs=[
            pl.BlockSpec((tm, D), lambda i: (i, 0)),
            full((1, D)), full((D, n)), full((1, n)),
            pl.BlockSpec((tm, LANES), lambda i: (i % ns, 0)),
            pl.BlockSpec((tm, LANES), lambda i: (i % ns, 0)),
            full((1, LANES)), full((1, LANES)), full((LANES, LANES)),
        ],
        out_specs=[
            pl.BlockSpec((1, N_Q_HEADS, HEAD_DIM, tm), lambda i: (i // ns, 0, 0, i % ns)),
            pl.BlockSpec((1, N_KV_HEADS, tm, HEAD_DIM), lambda i: (i // ns, 0, i % ns, 0)),
            pl.BlockSpec((1, N_KV_HEADS, LANES, tm), lambda i: (i // ns, 0, 0, i % ns)),
        ],
        out_shape=[
            jax.ShapeDtypeStruct((B, N_Q_HEADS, HEAD_DIM, S), BF16),
            jax.ShapeDtypeStruct((B, N_KV_HEADS, S, HEAD_DIM), BF16),
            jax.ShapeDtypeStruct((B, N_KV_HEADS, LANES, S), BF16),
        ],
        compiler_params=_cparams("parallel"),
        name="qkv",
    )(x2d, g1, w_qkv, b_qkv, cos_t, sin_t, qg, kg, hsum)


def _proj_kernel(x_ref, g_ref, w_ref, b_ref, o_ref):
    h = _rms(x_ref[...], g_ref[...]).astype(BF16)
    o_ref[...] = jnp.dot(h, w_ref[...], preferred_element_type=F32) + b_ref[...]


def _proj(x2d, g, w, b):
    T, D = x2d.shape
    n = w.shape[1]
    tm = min(ROW_TILE, T)
    return pl.pallas_call(
        _proj_kernel,
        grid=(T // tm,),
        in_specs=[
            pl.BlockSpec((tm, D), lambda i: (i, 0)),
            pl.BlockSpec((1, D), lambda i: (0, 0)),
            pl.BlockSpec((D, n), lambda i: (0, 0)),
            pl.BlockSpec((1, n), lambda i: (0, 0)),
        ],
        out_specs=pl.BlockSpec((tm, n), lambda i: (i, 0)),
        out_shape=jax.ShapeDtypeStruct((T, n), F32),
        compiler_params=_cparams("parallel"),
        name="proj",
    )(x2d, g, w, b)


def _attn_kernel(q_ref, k_ref, v_ref, o_ref, s0_ref, s1_ref, p0_ref, p1_ref, *, tk):
    tq = q_ref.shape[3]
    S = k_ref.shape[2]
    M = Q_GROUP * tq
    n = S // tk
    qT = jnp.concatenate([q_ref[0, h] for h in range(Q_GROUP)], axis=1)

    def scores(j, s_ref):
        off = pl.multiple_of(j * tk, tk)
        s_ref[...] = jnp.dot(k_ref[0, 0, pl.ds(off, tk), :], qT, preferred_element_type=F32)

    def soften(s_ref, p_ref, m):
        s = s_ref[...]
        m_new = jnp.maximum(m, jnp.max(s, axis=0, keepdims=True))
        p_ref[...] = jnp.exp2(s - m_new).astype(BF16)
        return m_new, jnp.exp2(m - m_new)

    def accumulate(j, p_ref, alpha, acc):
        off = pl.multiple_of(j * tk, tk)
        vc = v_ref[0, 0, :, pl.ds(off, tk)]
        return alpha * acc + jnp.dot(vc, p_ref[...], preferred_element_type=F32)

    def pair(i, carry):
        m, alpha, acc = carry
        scores(2 * i + 2, s0_ref)
        m, alpha_odd = soften(s1_ref, p1_ref, m)
        acc = accumulate(2 * i, p0_ref, alpha, acc)
        scores(2 * i + 3, s1_ref)
        m, alpha_even = soften(s0_ref, p0_ref, m)
        acc = accumulate(2 * i + 1, p1_ref, alpha_odd, acc)
        return m, alpha_even, acc

    scores(0, s0_ref)
    scores(1, s1_ref)
    m, alpha = soften(s0_ref, p0_ref, jnp.full((1, M), NEG_BIG, F32))
    carry = (m, alpha, jnp.zeros((LANES, M), F32))
    m, alpha, acc = lax.fori_loop(0, n // 2 - 1, pair, carry)
    _, alpha_last = soften(s1_ref, p1_ref, m)
    acc = accumulate(n - 2, p0_ref, alpha, acc)
    acc = accumulate(n - 1, p1_ref, alpha_last, acc)
    outT = acc[:HEAD_DIM] / acc[HEAD_DIM:HEAD_DIM + 1]
    stacked = jnp.concatenate(
        [outT[:, g * tq:(g + 1) * tq] for g in range(Q_GROUP)], axis=0)
    o_ref[...] = stacked.T.astype(BF16)


def _attn(qT, k, vT):
    B, _, _, S = qT.shape
    tq = min(ATT_TQ, S)
    tk = min(ATT_TK, S)
    nq = S // tq
    assert (S // tk) % 2 == 0, "key chunks are processed in pairs"
    score_buf = pltpu.VMEM((tk, Q_GROUP * tq), F32)
    prob_buf = pltpu.VMEM((tk, Q_GROUP * tq), BF16)
    return pl.pallas_call(
        functools.partial(_attn_kernel, tk=tk),
        grid=(B, N_KV_HEADS, nq),
        in_specs=[
            pl.BlockSpec((1, Q_GROUP, HEAD_DIM, tq), lambda b, g, i: (b, g, 0, i)),
            pl.BlockSpec((1, 1, S, HEAD_DIM), lambda b, g, i: (b, g, 0, 0)),
            pl.BlockSpec((1, 1, LANES, S), lambda b, g, i: (b, g, 0, 0)),
        ],
        out_specs=pl.BlockSpec((tq, Q_GROUP * HEAD_DIM), lambda b, g, i: (b * nq + i, g)),
        out_shape=jax.ShapeDtypeStruct((B * S, Q_WIDTH), BF16),
        scratch_shapes=[score_buf, score_buf, prob_buf, prob_buf],
        compiler_params=_cparams("parallel", "parallel", "parallel"),
        name="attn",
    )(qT, k, vT)


def _shift_chunks(v, down):
    row = lax.broadcasted_iota(I32, v.shape, 0)
    if down:
        return jnp.where(row == 0, 0.0, pltpu.roll(v, 1, 0))
    return jnp.where(row == SUBLANES - 1, 0.0, pltpu.roll(v, SUBLANES - 1, 0))


def _lru_kernel(xr_ref, yr_ref, cw_ref, cb_ref, wg_ref, bg_ref, lam_ref, o_ref,
                xe_ref, af_ref, uf_ref, ab_ref, ub_ref, *, rows):
    S, ct = xr_ref.shape
    lc = S // TIME_CHUNKS
    halo = SUBLANES

    ng = ct // LANES

    def xe_rows(r0, n):
        return jnp.concatenate([xe_ref[g, pl.ds(r0, n), :] for g in range(ng)], axis=1)

    def xe_store(r0, n, val):
        for g in range(ng):
            xe_ref[g, pl.ds(r0, n), :] = val[:, g * LANES:(g + 1) * LANES]

    for s in range(TIME_CHUNKS):
        for g in range(ng):
            xe_ref[g, pl.ds(halo + s, lc, stride=SUBLANES), :] = (
                xr_ref[pl.ds(s * lc, lc), g * LANES:(g + 1) * LANES])
    xe_store(0, halo, _shift_chunks(xe_rows(S, halo), True))
    first = xe_rows(halo, halo)
    second = xe_rows(2 * halo, halo)
    xe_store(S + halo, halo, _shift_chunks(first, False))
    xe_store(S + 2 * halo, halo, _shift_chunks(second, False))

    cw = cw_ref[...]
    cb = cb_ref[...]
    lam = lam_ref[...]
    log_sig = jnp.minimum(lam, 0.0) - jnp.log(1.0 + jnp.exp(-jnp.abs(lam)))
    c_half = (0.5 * RG_C * LOG2E) * log_sig
    wg = wg_ref[0]
    bg = bg_ref[0]

    def gate_chunk(i, _):
        r0 = pl.multiple_of(i * rows, rows)
        xc = cb
        for j in range(4):
            xc = xc + cw[j:j + 1, :] * xe_rows(r0 + j * halo, rows)
        t = jnp.tanh(jnp.dot(xc.astype(BF16), wg, preferred_element_type=F32) + bg)
        x_half = 0.5 * xc
        for d, (a_ref, u_ref) in enumerate(((af_ref, uf_ref), (ab_ref, ub_ref))):
            t_r = t[:, (2 * d) * ct:(2 * d + 1) * ct]
            t_i = t[:, (2 * d + 1) * ct:(2 * d + 2) * ct]
            ch = c_half[d:d + 1, :]
            a = jnp.exp2(t_r * ch + ch)
            a_ref[pl.ds(r0, rows), :] = a
            v = 1.0 - a * a
            root = jnp.maximum(v, 0.0) * lax.rsqrt(jnp.maximum(v, TINY))
            u_ref[pl.ds(r0, rows), :] = root * ((t_i + 1.0) * x_half)
        return 0

    lax.fori_loop(0, S // rows, gate_chunk, 0)

    unroll = 8

    def scan_body(i, carry):
        hf, pf, hb, pb = carry
        for k in range(unroll):
            tf = i * unroll + k
            rf = pl.multiple_of(tf * SUBLANES, SUBLANES)
            rb = pl.multiple_of((lc - 1 - tf) * SUBLANES, SUBLANES)
            a = af_ref[pl.ds(rf, SUBLANES), :]
            hf = a * hf + uf_ref[pl.ds(rf, SUBLANES), :]
            pf = a * pf
            uf_ref[pl.ds(rf, SUBLANES), :] = hf
            af_ref[pl.ds(rf, SUBLANES), :] = pf
            a = ab_ref[pl.ds(rb, SUBLANES), :]
            hb = a * hb + ub_ref[pl.ds(rb, SUBLANES), :]
            pb = a * pb
            ub_ref[pl.ds(rb, SUBLANES), :] = hb
            ab_ref[pl.ds(rb, SUBLANES), :] = pb
        return hf, pf, hb, pb

    zero = jnp.zeros((SUBLANES, ct), F32)
    one = jnp.ones((SUBLANES, ct), F32)
    hf_end, pf_end, hb_end, pb_end = lax.fori_loop(0, lc // unroll, scan_body,
                                                   (zero, one, zero, one))

    def chunk_carry(h_end, p_end, down):
        f = h_end
        for _ in range(TIME_CHUNKS - 1):
            f = h_end + p_end * _shift_chunks(f, down)
        return _shift_chunks(f, down)

    cin_f = chunk_carry(hf_end, pf_end, True)
    cin_b = chunk_carry(hb_end, pb_end, False)

    def fix_chunk(i, _):
        r0 = pl.multiple_of(i * rows, rows)
        reps = rows // SUBLANES
        hf = uf_ref[pl.ds(r0, rows), :] + af_ref[pl.ds(r0, rows), :] * jnp.tile(cin_f, (reps, 1))
        hb = ub_ref[pl.ds(r0, rows), :] + ab_ref[pl.ds(r0, rows), :] * jnp.tile(cin_b, (reps, 1))
        xe_store(r0, rows, hf + hb)
        return 0

    lax.fori_loop(0, S // rows, fix_chunk, 0)

    for s in range(TIME_CHUNKS):
        hsum = jnp.concatenate(
            [xe_ref[g, pl.ds(s, lc, stride=SUBLANES), :] for g in range(ng)], axis=1)
        y = yr_ref[pl.ds(s * lc, lc), :]
        o_ref[pl.ds(s * lc, lc), :] = (hsum * jax.nn.gelu(y, approximate=True)).astype(BF16)


def _lru(proj, conv_w, conv_b, wg, bg, lam, B, S, C):
    ct = LRU_CT
    nct = C // ct
    rows = min(LRU_ROWS, S)
    return pl.pallas_call(
        functools.partial(_lru_kernel, rows=rows),
        grid=(B, nct),
        in_specs=[
            pl.BlockSpec((S, ct), lambda b, c: (b, c)),
            pl.BlockSpec((S, ct), lambda b, c: (b, nct + c)),
            pl.BlockSpec((4, ct), lambda b, c: (0, c)),
            pl.BlockSpec((1, ct), lambda b, c: (0, c)),
            pl.BlockSpec((1, ct, 4 * ct), lambda b, c: (c, 0, 0)),
            pl.BlockSpec((1, 1, 4 * ct), lambda b, c: (c, 0, 0)),
            pl.BlockSpec((2, ct), lambda b, c: (0, c)),
        ],
        out_specs=pl.BlockSpec((S, ct), lambda b, c: (b, c)),
        out_shape=jax.ShapeDtypeStruct((B * S, C), BF16),
        scratch_shapes=[
            pltpu.VMEM((ct // LANES, S + 3 * SUBLANES, LANES), F32),
            pltpu.VMEM((S, ct), F32), pltpu.VMEM((S, ct), F32),
            pltpu.VMEM((S, ct), F32), pltpu.VMEM((S, ct), F32),
        ],
        compiler_params=_cparams("parallel", "parallel"),
        name="lru",
    )(proj, proj, conv_w, conv_b, wg, bg, lam)


def _merge_kernel(x_ref, attn_ref, rec_ref, gl_ref, wa_ref, wr_ref, wo_ref, g2_ref,
                  rw_hi_ref, rw_lo_ref, rb_ref, x2_ref, h2_ref, lg_ref):
    D = x_ref.shape[1]
    gl = gl_ref[...]
    ma = jnp.dot(attn_ref[...], wa_ref[...], preferred_element_type=F32)
    mr = jnp.dot(rec_ref[...], wr_ref[...], preferred_element_type=F32)
    merged = jax.nn.sigmoid(gl[:, :D]) * ma + jax.nn.sigmoid(gl[:, D:]) * mr
    x2 = x_ref[...] + jnp.dot(merged.astype(BF16), wo_ref[...], preferred_element_type=F32)
    x2_ref[...] = x2
    h2 = _rms(x2, g2_ref[...])
    hi = h2.astype(BF16)
    lo = (h2 - hi.astype(F32)).astype(BF16)
    lg_ref[...] = (jnp.dot(hi, rw_hi_ref[...], preferred_element_type=F32)
                   + jnp.dot(lo, rw_hi_ref[...], preferred_element_type=F32)
                   + jnp.dot(hi, rw_lo_ref[...], preferred_element_type=F32)) + rb_ref[...]
    tm = h2.shape[0]
    bits = lax.bitcast_convert_type(hi.astype(F32), U32)
    words = bits[:, :D // 2] | (bits[:, D // 2:] >> 16)
    for j in range(PACK_LINES):
        h2_ref[pl.ds(j, tm, stride=SUBLANES), :] = words[:, j * LANES:(j + 1) * LANES]
    for j in range(PACK_LINES, SUBLANES):
        h2_ref[pl.ds(j, tm, stride=SUBLANES), :] = jnp.zeros((tm, LANES), U32)


def _merge(x2d, attn, rec, proj, wa, wr, wo, g2, rw_hi, rw_lo, rb):
    T, D = x2d.shape
    tm = min(ROW_TILE, T)
    nj = D // LANES
    full = lambda a: pl.BlockSpec(a.shape, lambda i: (0,) * a.ndim)
    return pl.pallas_call(
        _merge_kernel,
        grid=(T // tm,),
        in_specs=[
            pl.BlockSpec((tm, D), lambda i: (i, 0)),
            pl.BlockSpec((tm, Q_WIDTH), lambda i: (i, 0)),
            pl.BlockSpec((tm, D), lambda i: (i, 0)),
            pl.BlockSpec((tm, 2 * D), lambda i: (i, 1)),
            full(wa), full(wr), full(wo), full(g2), full(rw_hi), full(rw_lo), full(rb),
        ],
        out_specs=[
            pl.BlockSpec((tm, D), lambda i: (i, 0)),
            pl.BlockSpec((tm * nj, LANES), lambda i: (i, 0)),
            pl.BlockSpec((tm, LANES), lambda i: (i, 0)),
        ],
        out_shape=[
            jax.ShapeDtypeStruct((T, D), F32),
            jax.ShapeDtypeStruct((T * nj, LANES), U32),
            jax.ShapeDtypeStruct((T, LANES), F32),
        ],
        compiler_params=_cparams("parallel"),
        name="merge",
    )(x2d, attn, rec, proj, wa, wr, wo, g2, rw_hi, rw_lo, rb)


def _route_kernel(lg_ref, gate_ref, gate_t_ref, eidx_t_ref, rank_t_ref, cnt_ref):
    i = pl.program_id(0)

    @pl.when(i == 0)
    def _():
        cnt_ref[...] = jnp.zeros_like(cnt_ref)

    tm = lg_ref.shape[0]
    work = lg_ref[...].T[:N_EXPERTS]
    expert = lax.broadcasted_iota(I32, work.shape, 0)
    sels, vals, idxs = [], [], []
    for _ in range(TOP_K):
        mx = jnp.max(work, axis=0, keepdims=True)
        idx = jnp.min(jnp.where(work == mx, expert, N_EXPERTS), axis=0, keepdims=True)
        sel = expert == idx
        sels.append(sel)
        vals.append(mx)
        idxs.append(idx)
        work = jnp.where(sel, NEG_BIG, work)
    ex = [jnp.exp(v - vals[0]) for v in vals]
    den = ex[0] + ex[1] + ex[2] + ex[3]
    member = (sels[0] | sels[1] | sels[2] | sels[3]).astype(BF16)
    r = lax.broadcasted_iota(I32, (tm, tm), 0)
    c = lax.broadcasted_iota(I32, (tm, tm), 1)
    tri = (r < c).astype(BF16)
    base = cnt_ref[:, 0:1]
    before = jnp.dot(member, tri, preferred_element_type=F32) + base
    ranks = [jnp.sum(jnp.where(sels[k], before, 0.0), axis=0, keepdims=True)
             for k in range(TOP_K)]
    gates = [ex[k] / den for k in range(TOP_K)]
    pad = SUBLANES - TOP_K
    gate_t = jnp.concatenate(gates + [jnp.zeros((pad, tm), F32)], axis=0)
    gate_t_ref[...] = gate_t
    eidx_t_ref[...] = jnp.concatenate(idxs + [jnp.zeros((pad, tm), I32)], axis=0)
    rank_t_ref[...] = jnp.concatenate(ranks + [jnp.zeros((pad, tm), F32)], axis=0).astype(I32)
    gate_ref[...] = jnp.concatenate(
        [gate_t, jnp.zeros((LANES - SUBLANES, tm), F32)], axis=0).T
    cnt_ref[...] = jnp.broadcast_to(
        base + jnp.sum(member.astype(F32), axis=1, keepdims=True), cnt_ref.shape)


def _route(logits):
    T = logits.shape[0]
    tm = min(ROW_TILE, T)
    spec = pl.BlockSpec((tm, LANES), lambda i: (i, 0))
    spec_t = pl.BlockSpec((SUBLANES, tm), lambda i: (0, i))
    return pl.pallas_call(
        _route_kernel,
        grid=(T // tm,),
        in_specs=[spec],
        out_specs=[spec, spec_t, spec_t, spec_t,
                   pl.BlockSpec((N_EXPERTS, LANES), lambda i: (0, 0))],
        out_shape=[
            jax.ShapeDtypeStruct((T, LANES), F32),
            jax.ShapeDtypeStruct((SUBLANES, T), F32),
            jax.ShapeDtypeStruct((SUBLANES, T), I32),
            jax.ShapeDtypeStruct((SUBLANES, T), I32),
            jax.ShapeDtypeStruct((N_EXPERTS, LANES), F32),
        ],
        compiler_params=_cparams("arbitrary"),
        name="route",
    )(logits)


def _tile(ref, row):
    return ref.at[pl.ds(pl.multiple_of(row * SUBLANES, SUBLANES), SUBLANES)]


def _dispatch_kernel(dest_ref, pad_start_ref, pad_len_ref, nused_ref, h2_ref, buf_ref,
                     src_ref, pad_ref, sem, zsem, *, n_tokens):
    i = pl.program_id(0)
    tm = h2_ref.shape[0] // SUBLANES
    blk = pad_ref.shape[0] // (2 * SUBLANES)
    nblocks = buf_ref.shape[0] // (blk * SUBLANES)

    @pl.when(i == 0)
    def _():
        li = lax.broadcasted_iota(I32, pad_ref.shape, 0)
        trash = (TOP_K * n_tokens + (li >> 3)).astype(U32)
        pad_ref[...] = jnp.where((li & (SUBLANES - 1)) >= PACK_LINES, trash, jnp.uint32(0))

        def per_expert(e, _):
            start = pad_start_ref[e]
            npad = pad_len_ref[e]

            def put(r, _):
                pltpu.make_async_copy(_tile(pad_ref, lax.rem(start + r, 2 * blk)),
                                      _tile(buf_ref, start + r), zsem).start()
                return 0

            def done(r, _):
                pltpu.make_async_copy(_tile(pad_ref, 0), _tile(buf_ref, 0), zsem).wait()
                return 0

            lax.fori_loop(0, npad, put, 0)
            lax.fori_loop(0, npad, done, 0)
            return 0

        lax.fori_loop(0, N_EXPERTS, per_expert, 0)

        def tail_block(b):
            rows = blk * SUBLANES
            return pltpu.make_async_copy(
                pad_ref.at[pl.ds(0, rows)],
                buf_ref.at[pl.ds(pl.multiple_of(b * rows, rows), rows)], zsem)

        def tail_put(b, _):
            tail_block(b).start()
            return 0

        def tail_done(b, _):
            tail_block(b).wait()
            return 0

        lax.fori_loop(nused_ref[0], nblocks, tail_put, 0)
        lax.fori_loop(nused_ref[0], nblocks, tail_done, 0)

    parity = lax.rem(i, 2)
    half = parity * TOP_K
    li = lax.broadcasted_iota(I32, h2_ref.shape, 0)
    tok = i * tm + (li >> 3)
    is_id = (li & (SUBLANES - 1)) >= PACK_LINES
    data = h2_ref[...]
    for k in range(TOP_K):
        src_ref[half + k] = jnp.where(is_id, (k * n_tokens + tok).astype(U32), data)

    def issue(r, _):
        for k in range(TOP_K):
            pltpu.make_async_copy(_tile(src_ref.at[half + k], r),
                                  _tile(buf_ref, dest_ref[k * tm + r]),
                                  sem.at[parity]).start(priority=k % 2)
        return 0

    lax.fori_loop(0, tm, issue, 0)

    def wait_step(which):
        for k in range(TOP_K):
            pltpu.make_async_copy(src_ref.at[k], buf_ref.at[pl.ds(0, tm * SUBLANES)],
                                  sem.at[which]).wait()

    pl.when(i >= 1)(functools.partial(wait_step, 1 - parity))
    pl.when(i == pl.num_programs(0) - 1)(functools.partial(wait_step, parity))


def _dispatch(dest_flat, pad_start, pad_len, nused, h2_rows, n_rows, blk):
    T = h2_rows.shape[0] // SUBLANES
    tm = min(DMA_TILE, T)
    smem = pl.BlockSpec(memory_space=pltpu.SMEM)
    return pl.pallas_call(
        functools.partial(_dispatch_kernel, n_tokens=T),
        grid=(T // tm,),
        in_specs=[
            pl.BlockSpec((TOP_K * tm,), lambda i: (i,), memory_space=pltpu.SMEM),
            smem, smem, smem,
            pl.BlockSpec((tm * SUBLANES, LANES), lambda i: (i, 0)),
        ],
        out_specs=pl.BlockSpec(memory_space=pl.ANY),
        out_shape=jax.ShapeDtypeStruct((n_rows * SUBLANES, LANES), U32),
        scratch_shapes=[pltpu.VMEM((2 * TOP_K, tm * SUBLANES, LANES), U32),
                        pltpu.VMEM((2 * blk * SUBLANES, LANES), U32),
                        pltpu.SemaphoreType.DMA((2,)), pltpu.SemaphoreType.DMA(())],
        compiler_params=_cparams("arbitrary"),
        name="dispatch",
    )(dest_flat, pad_start, pad_len, nused, h2_rows)


def _ffn_kernel(blk_exp_ref, nused_ref, x_ref, wgu_ref, bgu_ref, wd_ref, bd_ref, y_ref,
                wgu_bf_ref, wd_bf_ref, out_ref, idv_ref, ids_ref, ssem, isem, *, n_tokens):
    b = pl.program_id(0)
    nb = pl.num_programs(0)
    tm = x_ref.shape[0] // SUBLANES
    groups = tm // LANES
    F = wd_ref.shape[1]
    nused = nused_ref[0]
    used = b < nused
    slot = lax.rem(b, 2)
    prev = 1 - slot
    new_expert = (b == 0) | (blk_exp_ref[b] != blk_exp_ref[jnp.maximum(b - 1, 0)])

    def send_row(s, r_hi, r_lo, row, priority):
        rid = ids_ref[s * SUBLANES + r_hi, r_lo]
        pltpu.make_async_copy(_tile(out_ref, s * tm + row), _tile(y_ref, rid),
                              ssem.at[s]).start(priority=priority)

    def send_rows_unrolled(s, lo, hi):
        for r in range(lo, hi):
            send_row(s, r // LANES, r % LANES, r, r % 2)

    def send_block_rolled(s):
        def body(g, _):
            def inner(l, _):
                send_row(s, g, l, g * LANES + l, 0)
                return 0
            return lax.fori_loop(0, LANES, inner, 0)
        lax.fori_loop(0, groups, body, 0)

    def wait_block(s):
        rows = tm * SUBLANES
        pltpu.make_async_copy(out_ref.at[pl.ds(0, rows)], y_ref.at[pl.ds(0, rows)],
                              ssem.at[s]).wait()

    def id_copy(s):
        return pltpu.make_async_copy(
            idv_ref, ids_ref.at[pl.ds(pl.multiple_of(s * SUBLANES, SUBLANES), SUBLANES)], isem)

    @pl.when(b == 0)
    def _():
        out_ref[...] = jnp.zeros_like(out_ref)
        rows = tm * SUBLANES
        pltpu.make_async_copy(out_ref.at[pl.ds(0, rows)],
                              y_ref.at[pl.ds(TOP_K * n_tokens * SUBLANES, rows)],
                              ssem.at[0]).start()
        pos = (lax.broadcasted_iota(I32, idv_ref.shape, 0) * LANES
               + lax.broadcasted_iota(I32, idv_ref.shape, 1))
        idv_ref[...] = TOP_K * n_tokens + tm + jnp.minimum(pos, tm - 1)
        id_copy(1).start()
        id_copy(1).wait()

    @pl.when((b >= 1) & (b <= nused))
    def _():
        id_copy(prev).wait()

    @pl.when(used & new_expert)
    def _():
        wgu_bf_ref[...] = wgu_ref[0].astype(BF16)
        wd_bf_ref[...] = wd_ref[0].astype(BF16)

    def block(slot, prev):
        words = [x_ref[pl.ds(j, tm, stride=SUBLANES), :] for j in range(PACK_LINES)]
        hi = [lax.bitcast_convert_type(w & jnp.uint32(0xFFFF0000), F32) for w in words]
        lo = [lax.bitcast_convert_type(w << 16, F32) for w in words]
        x = jnp.concatenate(hi + lo, axis=1).astype(BF16)
        acts = []
        for c in range(FFN_SLICES):
            send_rows_unrolled(prev, c * tm // FFN_SLICES, (c + 1) * tm // FFN_SLICES)
            w = F // FFN_SLICES
            g = jnp.dot(x, wgu_bf_ref[:, c * w:(c + 1) * w], preferred_element_type=F32)
            u = jnp.dot(x, wgu_bf_ref[:, F + c * w:F + (c + 1) * w],
                        preferred_element_type=F32)
            g = jnp.minimum(g + bgu_ref[0, :, c * w:(c + 1) * w], SWIGLU_LIMIT)
            u = jnp.clip(u + bgu_ref[0, :, F + c * w:F + (c + 1) * w],
                         -SWIGLU_LIMIT, SWIGLU_LIMIT)
            acts.append((g * jax.nn.sigmoid(SWIGLU_ALPHA * g) * (u + 1.0)).astype(BF16))
        act = jnp.concatenate(acts, axis=1)
        y = jnp.dot(act, wd_bf_ref[...], preferred_element_type=F32) + bd_ref[0]
        wait_block(slot)
        base = slot * (tm * SUBLANES)
        for j in range(SUBLANES):
            out_ref[pl.ds(base + j, tm, stride=SUBLANES), :] = y[:, j * LANES:(j + 1) * LANES]
        idl = x_ref[pl.ds(PACK_LINES, tm, stride=SUBLANES), :].astype(I32)
        rr = lax.broadcasted_iota(I32, idl.shape, 0)
        ll = lax.broadcasted_iota(I32, idl.shape, 1)
        diag = jnp.where((rr & (LANES - 1)) == ll, idl, 0)
        idv_ref[pl.ds(0, groups), :] = jnp.sum(diag.reshape(groups, LANES, LANES), axis=1)
        id_copy(slot).start()

    for parity in range(2):
        pl.when(used & (slot == parity))(functools.partial(block, parity, 1 - parity))

    @pl.when(b == nused)
    def _():
        wait_block(slot)
        send_block_rolled(prev)
        wait_block(prev)

    @pl.when(used & (b == nb - 1))
    def _():
        wait_block(prev)
        id_copy(slot).wait()
        send_block_rolled(slot)
        wait_block(slot)


def _ffn(blk_exp, nused, buf, wgu, bgu, wd, bd, tm, n_tokens):
    nb = buf.shape[0] // (tm * SUBLANES)
    D = wgu.shape[1]
    F = wd.shape[1]

    def xmap(b, be, nu):
        return (jnp.minimum(b, nu[0] - 1), 0)

    def wmap(b, be, nu):
        return (be[jnp.minimum(b, nu[0] - 1)], 0, 0)

    grid_spec = pltpu.PrefetchScalarGridSpec(
        num_scalar_prefetch=2,
        grid=(nb,),
        in_specs=[
            pl.BlockSpec((tm * SUBLANES, LANES), xmap),
            pl.BlockSpec((1, D, 2 * F), wmap),
            pl.BlockSpec((1, 1, 2 * F), wmap),
            pl.BlockSpec((1, F, D), wmap),
            pl.BlockSpec((1, 1, D), wmap),
        ],
        out_specs=pl.BlockSpec(memory_space=pl.ANY),
        scratch_shapes=[
            pltpu.VMEM((D, 2 * F), BF16), pltpu.VMEM((F, D), BF16),
            pltpu.VMEM((2 * tm * SUBLANES, LANES), F32),
            pltpu.VMEM((SUBLANES, LANES), I32),
            pltpu.SMEM((2 * SUBLANES, LANES), I32),
            pltpu.SemaphoreType.DMA((2,)), pltpu.SemaphoreType.DMA(()),
        ],
    )
    return pl.pallas_call(
        functools.partial(_ffn_kernel, n_tokens=n_tokens),
        grid_spec=grid_spec,
        out_shape=jax.ShapeDtypeStruct(((TOP_K * n_tokens + 2 * tm) * SUBLANES, LANES), F32),
        compiler_params=_cparams("arbitrary"),
        name="ffn",
    )(blk_exp, nused, buf, wgu, bgu, wd, bd)


def _combine_kernel(gate_ref, x2_ref, fg_ref, *refs):
    y_refs, o_ref = refs[:TOP_K], refs[TOP_K]
    tm, D = x2_ref.shape
    nj = D // LANES
    gate = gate_ref[...]
    cols = []
    for j in range(nj):
        acc = None
        for k in range(TOP_K):
            part = gate[:, k:k + 1] * y_refs[k][pl.ds(j, tm, stride=nj), :]
            acc = part if acc is None else acc + part
        cols.append(acc)
    y = jnp.concatenate(cols, axis=1)
    o_ref[...] = _rms(x2_ref[...] + y, fg_ref[...])


def _combine(gate4, x2, fg, y_rows):
    T, D = x2.shape
    tm = min(ROW_TILE, T)
    nj = D // LANES
    nt = T // tm

    def slot_spec(k):
        return pl.BlockSpec((tm * nj, LANES), lambda i: (k * nt + i, 0))

    return pl.pallas_call(
        _combine_kernel,
        grid=(nt,),
        in_specs=[
            pl.BlockSpec((tm, LANES), lambda i: (i, 0)),
            pl.BlockSpec((tm, D), lambda i: (i, 0)),
            pl.BlockSpec((1, D), lambda i: (0, 0)),
        ] + [slot_spec(k) for k in range(TOP_K)],
        out_specs=pl.BlockSpec((tm, D), lambda i: (i, 0)),
        out_shape=jax.ShapeDtypeStruct((T, D), F32),
        compiler_params=_cparams("parallel"),
        name="combine",
    )(gate4, x2, fg, *([y_rows] * TOP_K))


def _rope_tables(S):
    rows = S // GRID_W
    row = jnp.repeat(jnp.arange(rows, dtype=I32), GRID_W).astype(F32)
    col = jnp.tile(jnp.arange(GRID_W, dtype=I32), rows).astype(F32)
    inv = ROPE_THETA ** (-jnp.arange(0, AXIS_DIM, 2, dtype=F32) / AXIS_DIM)
    ang_r = row[:, None] * inv[None, :]
    ang_c = col[:, None] * inv[None, :]
    cos = jnp.concatenate([jnp.cos(ang_r)] * 2 + [jnp.cos(ang_c)] * 2, axis=1)
    sin = jnp.concatenate([-jnp.sin(ang_r), jnp.sin(ang_r), -jnp.sin(ang_c), jnp.sin(ang_c)], axis=1)
    return jnp.tile(cos, (1, 2)), jnp.tile(sin, (1, 2))


def _block_diag_gates(wa, ba, wi, bi, ct):
    nb, bw = wa.shape[1], wa.shape[2]
    per = ct // bw
    nct = nb // per
    eye = jnp.eye(per, dtype=wa.dtype)

    def tiles(w):
        w = w.reshape(nct, per, bw, bw)
        return jnp.einsum('cpij,pq->cpiqj', w, eye).reshape(nct, ct, ct)

    wg = jnp.concatenate([tiles(wa[0]), tiles(wi[0]), tiles(wa[1]), tiles(wi[1])], axis=2)
    bias = lambda b: b.reshape(nct, 1, ct)
    bg = jnp.concatenate([bias(ba[0]), bias(bi[0]), bias(ba[1]), bias(bi[1])], axis=2)
    return (0.5 * wg).astype(BF16), 0.5 * bg


def _layer(x2d, B, S, norm1_g, w_in, b_in, q_norm_g, k_norm_g, conv_w, conv_b, lru_wa, lru_ba,
           lru_wi, lru_bi, lru_lambda, w_attn_o, w_lru_o, w_out, norm2_g, w_router, b_router,
           w_gu, b_gu, w_down, b_down, out_g):
    T, D = x2d.shape
    C = conv_w.shape[1]
    nqkv = Q_WIDTH + 2 * KV_WIDTH
    row2 = lambda v: v.reshape(1, -1)

    cos_t, sin_t = _rope_tables(S)
    head = jnp.arange(LANES) // HEAD_DIM
    hsum = (head[:, None] == head[None, :]).astype(BF16)
    q, k, v = _qkv(x2d, row2(norm1_g), w_in[:, :nqkv].astype(BF16), row2(b_in[:nqkv]),
                   cos_t, sin_t, row2(jnp.tile(q_norm_g, 2)), row2(jnp.tile(k_norm_g, 2)),
                   hsum, B, S)
    proj = _proj(x2d, row2(norm1_g), w_in[:, nqkv:].astype(BF16), row2(b_in[nqkv:]))
    attn = _attn(q, k, v)
    wg, bg = _block_diag_gates(lru_wa, lru_ba, lru_wi, lru_bi, LRU_CT)
    rec = _lru(proj, conv_w, row2(conv_b), wg, bg, lru_lambda, B, S, C)

    pad = LANES - N_EXPERTS
    rw = jnp.pad(w_router, ((0, 0), (0, pad)))
    rw_hi = rw.astype(BF16)
    rw_lo = (rw - rw_hi.astype(F32)).astype(BF16)
    rb = jnp.pad(b_router, (0, pad), constant_values=NEG_BIG).reshape(1, LANES)
    x2, h2_rows, logits = _merge(x2d, attn, rec, proj, w_attn_o.astype(BF16),
                                 w_lru_o.astype(BF16), w_out.astype(BF16), row2(norm2_g),
                                 rw_hi, rw_lo, rb)

    gate4, _, eidx_t, rank_t, cnt = _route(logits)
    counts = cnt[:, 0].astype(I32)
    tm = FFN_TM
    nblk = (counts + tm - 1) // tm
    pend_blk = jnp.cumsum(nblk)
    pstart = (pend_blk - nblk) * tm
    A = T * TOP_K
    nb = (A + N_EXPERTS * (tm - 1) + tm - 1) // tm
    blk_exp = jnp.minimum(
        jnp.sum(pend_blk[None, :] <= jnp.arange(nb, dtype=I32)[:, None], axis=1),
        N_EXPERTS - 1).astype(I32)
    nused = pend_blk[-1:].astype(I32)
    first_row = jnp.sum(jnp.where(eidx_t[:TOP_K, :, None] == jnp.arange(N_EXPERTS), pstart, 0),
                        axis=-1)
    dest = (first_row + rank_t[:TOP_K]).astype(I32)
    dest = dest.reshape(TOP_K, T // DMA_TILE, DMA_TILE).transpose(1, 0, 2).reshape(A)
    pad_start = (pstart + counts).astype(I32)
    pad_len = (nblk * tm - counts).astype(I32)

    buf = _dispatch(dest, pad_start, pad_len, nused, h2_rows, nb * tm, tm)
    y_rows = _ffn(blk_exp, nused, buf, w_gu, b_gu.reshape(N_EXPERTS, 1, -1), w_down,
                  b_down.reshape(N_EXPERTS, 1, -1), tm, T)
    return _combine(gate4, x2, row2(out_g), y_rows)


def kernel(x, norm1_g, w_in, b_in, q_norm_g, k_norm_g, conv_w, conv_b, lru_wa, lru_ba, lru_wi,
           lru_bi, lru_lambda, w_attn_o, w_lru_o, w_out, norm2_g, w_router, b_router, w_gu, b_gu,
           w_down, b_down, final_g):
    B, S, D = x.shape
    depth = norm1_g.shape[0]
    assert depth == 1, "the fused final RMSNorm assumes a single layer"
    assert S % (TIME_CHUNKS * SUBLANES) == 0 and S % GRID_W == 0
    assert D == 2 * PACK_LINES * LANES, "a token's bf16 row must fill PACK_LINES word lines"
    out = _layer(x.reshape(B * S, D), B, S, norm1_g[0], w_in[0], b_in[0], q_norm_g[0],
                 k_norm_g[0], conv_w[0], conv_b[0], lru_wa[0], lru_ba[0], lru_wi[0], lru_bi[0],
                 lru_lambda[0], w_attn_o[0], w_lru_o[0], w_out[0], norm2_g[0], w_router[0],
                 b_router[0], w_gu[0], b_gu[0], w_down[0], b_down[0], final_g)
    return out.reshape(B, S, D)
```

```python
import functools
import math

import jax
import jax.numpy as jnp
from jax import lax
from jax.experimental import pallas as pl
from jax.experimental.pallas import tpu as pltpu

F32 = jnp.float32
BF16 = jnp.bfloat16
I32 = jnp.int32
U32 = jnp.uint32

LANES = 128
SUBLANES = 8
PACK_LINES = 4
SMEM_1D_TILE = 1024
VMEM_LIMIT_BYTES = 56 * 1024 * 1024

HEAD_DIM = 64
N_Q_HEADS = 8
N_KV_HEADS = 2
Q_GROUP = N_Q_HEADS // N_KV_HEADS
Q_WIDTH = N_Q_HEADS * HEAD_DIM
KV_WIDTH = N_KV_HEADS * HEAD_DIM
AXIS_DIM = HEAD_DIM // 2
ROT_HALF = AXIS_DIM // 2
ROPE_THETA = 10000.0
GRID_W = 64
LRU_BLOCKS = 16
RG_C = 8.0
N_EXPERTS = 32
TOP_K = 4
SWIGLU_LIMIT = 7.0
SWIGLU_ALPHA = 1.702
NORM_EPS = 1e-6
LOG2E = 1.4426950408889634
NEG_BIG = -1e30
TINY = 1e-30

ROW_TILE = 512
ATT_TQ = 512
ATT_TK = 512
LRU_CT = 256
LRU_ROWS = 256
TIME_CHUNKS = SUBLANES
FFN_TM = 512
FFN_SLICES = 4
DMA_TILE = 256


def _cparams(*sem):
    return pltpu.CompilerParams(dimension_semantics=sem, vmem_limit_bytes=VMEM_LIMIT_BYTES)


def _rms(x, g):
    return x * lax.rsqrt(jnp.mean(x * x, axis=-1, keepdims=True) + NORM_EPS) * g


def _qkv_kernel(x_ref, g1_ref, w_ref, b_ref, cos_ref, sin_ref, qg_ref, kg_ref, hsum_ref,
                q_ref, k_ref, v_ref):
    h = _rms(x_ref[...], g1_ref[...]).astype(BF16)
    p = jnp.dot(h, w_ref[...], preferred_element_type=F32) + b_ref[...]
    cos = cos_ref[...]
    sin = sin_ref[...]
    hsum = hsum_ref[...]
    lane = lax.broadcasted_iota(I32, cos.shape, 1)
    first_half = (lane % AXIS_DIM) < ROT_HALF

    def norm_rope(c, gain):
        sq = c * c
        hi = sq.astype(BF16)
        lo = (sq - hi.astype(F32)).astype(BF16)
        ms = (jnp.dot(hi, hsum, preferred_element_type=F32)
              + jnp.dot(lo, hsum, preferred_element_type=F32)) * (1.0 / HEAD_DIM)
        y = c * lax.rsqrt(ms + NORM_EPS) * gain
        partner = jnp.where(first_half, pltpu.roll(y, LANES - ROT_HALF, 1),
                            pltpu.roll(y, ROT_HALF, 1))
        return y * cos + partner * sin

    qg = qg_ref[...]
    for c in range(Q_WIDTH // LANES):
        y = norm_rope(p[:, c * LANES:(c + 1) * LANES], qg) * (HEAD_DIM ** -0.5 * LOG2E)
        yt = y.T.astype(BF16)
        q_ref[0, 2 * c] = yt[:HEAD_DIM]
        q_ref[0, 2 * c + 1] = yt[HEAD_DIM:]
    yk = norm_rope(p[:, Q_WIDTH:Q_WIDTH + KV_WIDTH], kg_ref[...])
    k_ref[0, 0] = yk[:, :HEAD_DIM].astype(BF16)
    k_ref[0, 1] = yk[:, HEAD_DIM:].astype(BF16)
    vt = p[:, Q_WIDTH + KV_WIDTH:].T
    tm = vt.shape[1]
    ones_row = (lax.broadcasted_iota(I32, (HEAD_DIM, tm), 0) == 0).astype(BF16)
    v_ref[0, 0] = jnp.concatenate([vt[:HEAD_DIM].astype(BF16), ones_row], axis=0)
    v_ref[0, 1] = jnp.concatenate([vt[HEAD_DIM:].astype(BF16), ones_row], axis=0)


def _qkv(x2d, g1, w_qkv, b_qkv, cos_t, sin_t, qg, kg, hsum, B, S):
    T, D = x2d.shape
    tm = min(ROW_TILE, S)
    ns = S // tm
    n = w_qkv.shape[1]
    full = lambda shape: pl.BlockSpec(shape, lambda i: (0,) * len(shape))
    return pl.pallas_call(
        _qkv_kernel,
        grid=(T // tm,),
        in_specs=[
            pl.BlockSpec((tm, D), lambda i: (i, 0)),
            full((1, D)), full((D, n)), full((1, n)),
            pl.BlockSpec((tm, LANES), lambda i: (i % ns, 0)),
            pl.BlockSpec((tm, LANES), lambda i: (i % ns, 0)),
            full((1, LANES)), full((1, LANES)), full((LANES, LANES)),
        ],
        out_specs=[
            pl.BlockSpec((1, N_Q_HEADS, HEAD_DIM, tm), lambda i: (i // ns, 0, 0, i % ns)),
            pl.BlockSpec((1, N_KV_HEADS, tm, HEAD_DIM), lambda i: (i // ns, 0, i % ns, 0)),
            pl.BlockSpec((1, N_KV_HEADS, LANES, tm), lambda i: (i // ns, 0, 0, i % ns)),
        ],
        out_shape=[
            jax.ShapeDtypeStruct((B, N_Q_HEADS, HEAD_DIM, S), BF16),
            jax.ShapeDtypeStruct((B, N_KV_HEADS, S, HEAD_DIM), BF16),
            jax.ShapeDtypeStruct((B, N_KV_HEADS, LANES, S), BF16),
        ],
        compiler_params=_cparams("parallel"),
        name="qkv",
    )(x2d, g1, w_qkv, b_qkv, cos_t, sin_t, qg, kg, hsum)


def _proj_kernel(x_ref, g_ref, w_ref, b_ref, o_ref):
    h = _rms(x_ref[...], g_ref[...]).astype(BF16)
    o_ref[...] = jnp.dot(h, w_ref[...], preferred_element_type=F32) + b_ref[...]


def _proj(x2d, g, w, b):
    T, D = x2d.shape
    n = w.shape[1]
    tm = min(ROW_TILE, T)
    return pl.pallas_call(
        _proj_kernel,
        grid=(T // tm,),
        in_specs=[
            pl.BlockSpec((tm, D), lambda i: (i, 0)),
            pl.BlockSpec((1, D), lambda i: (0, 0)),
            pl.BlockSpec((D, n), lambda i: (0, 0)),
            pl.BlockSpec((1, n), lambda i: (0, 0)),
        ],
        out_specs=pl.BlockSpec((tm, n), lambda i: (i, 0)),
        out_shape=jax.ShapeDtypeStruct((T, n), F32),
        compiler_params=_cparams("parallel"),
        name="proj",
    )(x2d, g, w, b)


def _attn_kernel(q_ref, k_ref, v_ref, o_ref, s0_ref, s1_ref, *, tk):
    tq = q_ref.shape[3]
    S = k_ref.shape[2]
    M = Q_GROUP * tq
    n = S // tk
    qT = jnp.concatenate([q_ref[0, h] for h in range(Q_GROUP)], axis=1)

    def scores(j, s_ref):
        off = pl.multiple_of(j * tk, tk)
        s_ref[...] = jnp.dot(k_ref[0, 0, pl.ds(off, tk), :], qT, preferred_element_type=F32)

    def absorb(j, s_ref, carry):
        m, acc = carry
        off = pl.multiple_of(j * tk, tk)
        vc = v_ref[0, 0, :, pl.ds(off, tk)]
        s = s_ref[...]
        m_new = jnp.maximum(m, jnp.max(s, axis=0, keepdims=True))
        alpha = jnp.exp2(m - m_new)
        p = jnp.exp2(s - m_new).astype(BF16)
        return m_new, alpha * acc + jnp.dot(vc, p, preferred_element_type=F32)

    def pair(i, carry):
        scores(2 * i + 1, s1_ref)
        carry = absorb(2 * i, s0_ref, carry)
        scores(2 * i + 2, s0_ref)
        return absorb(2 * i + 1, s1_ref, carry)

    scores(0, s0_ref)
    carry = (jnp.full((1, M), NEG_BIG, F32), jnp.zeros((LANES, M), F32))
    carry = lax.fori_loop(0, n // 2 - 1, pair, carry)
    scores(n - 1, s1_ref)
    carry = absorb(n - 2, s0_ref, carry)
    _, acc = absorb(n - 1, s1_ref, carry)
    outT = acc[:HEAD_DIM] / acc[HEAD_DIM:HEAD_DIM + 1]
    stacked = jnp.concatenate(
        [outT[:, g * tq:(g + 1) * tq] for g in range(Q_GROUP)], axis=0)
    o_ref[...] = stacked.T.astype(BF16)


def _attn(qT, k, vT):
    B, _, _, S = qT.shape
    tq = min(ATT_TQ, S)
    tk = min(ATT_TK, S)
    nq = S // tq
    assert (S // tk) % 2 == 0, "key chunks are processed in pairs"
    score_buf = pltpu.VMEM((tk, Q_GROUP * tq), F32)
    return pl.pallas_call(
        functools.partial(_attn_kernel, tk=tk),
        grid=(B, N_KV_HEADS, nq),
        in_specs=[
            pl.BlockSpec((1, Q_GROUP, HEAD_DIM, tq), lambda b, g, i: (b, g, 0, i)),
            pl.BlockSpec((1, 1, S, HEAD_DIM), lambda b, g, i: (b, g, 0, 0)),
            pl.BlockSpec((1, 1, LANES, S), lambda b, g, i: (b, g, 0, 0)),
        ],
        out_specs=pl.BlockSpec((tq, Q_GROUP * HEAD_DIM), lambda b, g, i: (b * nq + i, g)),
        out_shape=jax.ShapeDtypeStruct((B * S, Q_WIDTH), BF16),
        scratch_shapes=[score_buf, score_buf],
        compiler_params=_cparams("parallel", "parallel", "parallel"),
        name="attn",
    )(qT, k, vT)


def _shift_chunks(v, down):
    row = lax.broadcasted_iota(I32, v.shape, 0)
    if down:
        return jnp.where(row == 0, 0.0, pltpu.roll(v, 1, 0))
    return jnp.where(row == SUBLANES - 1, 0.0, pltpu.roll(v, SUBLANES - 1, 0))


def _lru_kernel(xr_ref, yr_ref, cw_ref, cb_ref, wg_ref, bg_ref, lam_ref, o_ref,
                xe_ref, af_ref, uf_ref, ab_ref, ub_ref, *, rows):
    S, ct = xr_ref.shape
    lc = S // TIME_CHUNKS
    halo = SUBLANES

    ng = ct // LANES

    def xe_rows(r0, n):
        return jnp.concatenate([xe_ref[g, pl.ds(r0, n), :] for g in range(ng)], axis=1)

    def xe_store(r0, n, val):
        for g in range(ng):
            xe_ref[g, pl.ds(r0, n), :] = val[:, g * LANES:(g + 1) * LANES]

    for s in range(TIME_CHUNKS):
        for g in range(ng):
            xe_ref[g, pl.ds(halo + s, lc, stride=SUBLANES), :] = (
                xr_ref[pl.ds(s * lc, lc), g * LANES:(g + 1) * LANES])
    xe_store(0, halo, _shift_chunks(xe_rows(S, halo), True))
    first = xe_rows(halo, halo)
    second = xe_rows(2 * halo, halo)
    xe_store(S + halo, halo, _shift_chunks(first, False))
    xe_store(S + 2 * halo, halo, _shift_chunks(second, False))

    cw = cw_ref[...]
    cb = cb_ref[...]
    lam = lam_ref[...]
    log_sig = jnp.minimum(lam, 0.0) - jnp.log(1.0 + jnp.exp(-jnp.abs(lam)))
    c_half = (0.5 * RG_C * LOG2E) * log_sig
    wg = wg_ref[0]
    bg = bg_ref[0]

    def gate_chunk(i, _):
        r0 = pl.multiple_of(i * rows, rows)
        xc = cb
        for j in range(4):
            xc = xc + cw[j:j + 1, :] * xe_rows(r0 + j * halo, rows)
        t = jnp.tanh(jnp.dot(xc.astype(BF16), wg, preferred_element_type=F32) + bg)
        x_half = 0.5 * xc
        for d, (a_ref, u_ref) in enumerate(((af_ref, uf_ref), (ab_ref, ub_ref))):
            t_r = t[:, (2 * d) * ct:(2 * d + 1) * ct]
            t_i = t[:, (2 * d + 1) * ct:(2 * d + 2) * ct]
            ch = c_half[d:d + 1, :]
            a = jnp.exp2(t_r * ch + ch)
            a_ref[pl.ds(r0, rows), :] = a
            v = 1.0 - a * a
            root = jnp.maximum(v, 0.0) * lax.rsqrt(jnp.maximum(v, TINY))
            u_ref[pl.ds(r0, rows), :] = root * ((t_i + 1.0) * x_half)
        return 0

    lax.fori_loop(0, S // rows, gate_chunk, 0)

    unroll = 8

    def scan_body(i, carry):
        hf, pf, hb, pb = carry
        for k in range(unroll):
            tf = i * unroll + k
            rf = pl.multiple_of(tf * SUBLANES, SUBLANES)
            rb = pl.multiple_of((lc - 1 - tf) * SUBLANES, SUBLANES)
            a = af_ref[pl.ds(rf, SUBLANES), :]
            hf = a * hf + uf_ref[pl.ds(rf, SUBLANES), :]
            pf = a * pf
            uf_ref[pl.ds(rf, SUBLANES), :] = hf
            af_ref[pl.ds(rf, SUBLANES), :] = pf
            a = ab_ref[pl.ds(rb, SUBLANES), :]
            hb = a * hb + ub_ref[pl.ds(rb, SUBLANES), :]
            pb = a * pb
            ub_ref[pl.ds(rb, SUBLANES), :] = hb
            ab_ref[pl.ds(rb, SUBLANES), :] = pb
        return hf, pf, hb, pb

    zero = jnp.zeros((SUBLANES, ct), F32)
    one = jnp.ones((SUBLANES, ct), F32)
    hf_end, pf_end, hb_end, pb_end = lax.fori_loop(0, lc // unroll, scan_body,
                                                   (zero, one, zero, one))

    def chunk_carry(h_end, p_end, down):
        f = h_end
        for _ in range(TIME_CHUNKS - 1):
            f = h_end + p_end * _shift_chunks(f, down)
        return _shift_chunks(f, down)

    cin_f = chunk_carry(hf_end, pf_end, True)
    cin_b = chunk_carry(hb_end, pb_end, False)

    def fix_chunk(i, _):
        r0 = pl.multiple_of(i * rows, rows)
        reps = rows // SUBLANES
        hf = uf_ref[pl.ds(r0, rows), :] + af_ref[pl.ds(r0, rows), :] * jnp.tile(cin_f, (reps, 1))
        hb = ub_ref[pl.ds(r0, rows), :] + ab_ref[pl.ds(r0, rows), :] * jnp.tile(cin_b, (reps, 1))
        xe_store(r0, rows, hf + hb)
        return 0

    lax.fori_loop(0, S // rows, fix_chunk, 0)

    for s in range(TIME_CHUNKS):
        hsum = jnp.concatenate(
            [xe_ref[g, pl.ds(s, lc, stride=SUBLANES), :] for g in range(ng)], axis=1)
        y = yr_ref[pl.ds(s * lc, lc), :]
        o_ref[pl.ds(s * lc, lc), :] = (hsum * jax.nn.gelu(y, approximate=True)).astype(BF16)


def _lru(proj, conv_w, conv_b, wg, bg, lam, B, S, C):
    ct = LRU_CT
    nct = C // ct
    rows = min(LRU_ROWS, S)
    return pl.pallas_call(
        functools.partial(_lru_kernel, rows=rows),
        grid=(B, nct),
        in_specs=[
            pl.BlockSpec((S, ct), lambda b, c: (b, c)),
            pl.BlockSpec((S, ct), lambda b, c: (b, nct + c)),
            pl.BlockSpec((4, ct), lambda b, c: (0, c)),
            pl.BlockSpec((1, ct), lambda b, c: (0, c)),
            pl.BlockSpec((1, ct, 4 * ct), lambda b, c: (c, 0, 0)),
            pl.BlockSpec((1, 1, 4 * ct), lambda b, c: (c, 0, 0)),
            pl.BlockSpec((2, ct), lambda b, c: (0, c)),
        ],
        out_specs=pl.BlockSpec((S, ct), lambda b, c: (b, c)),
        out_shape=jax.ShapeDtypeStruct((B * S, C), BF16),
        scratch_shapes=[
            pltpu.VMEM((ct // LANES, S + 3 * SUBLANES, LANES), F32),
            pltpu.VMEM((S, ct), F32), pltpu.VMEM((S, ct), F32),
            pltpu.VMEM((S, ct), F32), pltpu.VMEM((S, ct), F32),
        ],
        compiler_params=_cparams("parallel", "parallel"),
        name="lru",
    )(proj, proj, conv_w, conv_b, wg, bg, lam)


def _merge_kernel(x_ref, attn_ref, rec_ref, gl_ref, wa_ref, wr_ref, wo_ref, g2_ref,
                  rw_hi_ref, rw_lo_ref, rb_ref, x2_ref, h2_ref, lg_ref):
    D = x_ref.shape[1]
    gl = gl_ref[...]
    ma = jnp.dot(attn_ref[...], wa_ref[...], preferred_element_type=F32)
    mr = jnp.dot(rec_ref[...], wr_ref[...], preferred_element_type=F32)
    merged = jax.nn.sigmoid(gl[:, :D]) * ma + jax.nn.sigmoid(gl[:, D:]) * mr
    x2 = x_ref[...] + jnp.dot(merged.astype(BF16), wo_ref[...], preferred_element_type=F32)
    x2_ref[...] = x2
    h2 = _rms(x2, g2_ref[...])
    hi = h2.astype(BF16)
    lo = (h2 - hi.astype(F32)).astype(BF16)
    lg_ref[...] = (jnp.dot(hi, rw_hi_ref[...], preferred_element_type=F32)
                   + jnp.dot(lo, rw_hi_ref[...], preferred_element_type=F32)
                   + jnp.dot(hi, rw_lo_ref[...], preferred_element_type=F32)) + rb_ref[...]
    tm = h2.shape[0]
    bits = lax.bitcast_convert_type(hi.astype(F32), U32)
    words = bits[:, :D // 2] | (bits[:, D // 2:] >> 16)
    for j in range(PACK_LINES):
        h2_ref[pl.ds(j, tm, stride=SUBLANES), :] = words[:, j * LANES:(j + 1) * LANES]
    for j in range(PACK_LINES, SUBLANES):
        h2_ref[pl.ds(j, tm, stride=SUBLANES), :] = jnp.zeros((tm, LANES), U32)


def _merge(x2d, attn, rec, proj, wa, wr, wo, g2, rw_hi, rw_lo, rb):
    T, D = x2d.shape
    tm = min(ROW_TILE, T)
    nj = D // LANES
    full = lambda a: pl.BlockSpec(a.shape, lambda i: (0,) * a.ndim)
    return pl.pallas_call(
        _merge_kernel,
        grid=(T // tm,),
        in_specs=[
            pl.BlockSpec((tm, D), lambda i: (i, 0)),
            pl.BlockSpec((tm, Q_WIDTH), lambda i: (i, 0)),
            pl.BlockSpec((tm, D), lambda i: (i, 0)),
            pl.BlockSpec((tm, 2 * D), lambda i: (i, 1)),
            full(wa), full(wr), full(wo), full(g2), full(rw_hi), full(rw_lo), full(rb),
        ],
        out_specs=[
            pl.BlockSpec((tm, D), lambda i: (i, 0)),
            pl.BlockSpec((tm * nj, LANES), lambda i: (i, 0)),
            pl.BlockSpec((tm, LANES), lambda i: (i, 0)),
        ],
        out_shape=[
            jax.ShapeDtypeStruct((T, D), F32),
            jax.ShapeDtypeStruct((T * nj, LANES), U32),
            jax.ShapeDtypeStruct((T, LANES), F32),
        ],
        compiler_params=_cparams("parallel"),
        name="merge",
    )(x2d, attn, rec, proj, wa, wr, wo, g2, rw_hi, rw_lo, rb)


def _route_kernel(lg_ref, gate_ref, gate_t_ref, eidx_t_ref, rank_t_ref, cnt_ref):
    i = pl.program_id(0)

    @pl.when(i == 0)
    def _():
        cnt_ref[...] = jnp.zeros_like(cnt_ref)

    tm = lg_ref.shape[0]
    work = lg_ref[...].T[:N_EXPERTS]
    expert = lax.broadcasted_iota(I32, work.shape, 0)
    sels, vals, idxs = [], [], []
    for _ in range(TOP_K):
        mx = jnp.max(work, axis=0, keepdims=True)
        idx = jnp.min(jnp.where(work == mx, expert, N_EXPERTS), axis=0, keepdims=True)
        sel = expert == idx
        sels.append(sel)
        vals.append(mx)
        idxs.append(idx)
        work = jnp.where(sel, NEG_BIG, work)
    ex = [jnp.exp(v - vals[0]) for v in vals]
    den = ex[0] + ex[1] + ex[2] + ex[3]
    member = (sels[0] | sels[1] | sels[2] | sels[3]).astype(BF16)
    r = lax.broadcasted_iota(I32, (tm, tm), 0)
    c = lax.broadcasted_iota(I32, (tm, tm), 1)
    tri = (r < c).astype(BF16)
    base = cnt_ref[:, 0:1]
    before = jnp.dot(member, tri, preferred_element_type=F32) + base
    ranks = [jnp.sum(jnp.where(sels[k], before, 0.0), axis=0, keepdims=True)
             for k in range(TOP_K)]
    gates = [ex[k] / den for k in range(TOP_K)]
    pad = SUBLANES - TOP_K
    gate_t = jnp.concatenate(gates + [jnp.zeros((pad, tm), F32)], axis=0)
    gate_t_ref[...] = gate_t
    eidx_t_ref[...] = jnp.concatenate(idxs + [jnp.zeros((pad, tm), I32)], axis=0)
    rank_t_ref[...] = jnp.concatenate(ranks + [jnp.zeros((pad, tm), F32)], axis=0).astype(I32)
    gate_ref[...] = jnp.concatenate(
        [gate_t, jnp.zeros((LANES - SUBLANES, tm), F32)], axis=0).T
    cnt_ref[...] = jnp.broadcast_to(
        base + jnp.sum(member.astype(F32), axis=1, keepdims=True), cnt_ref.shape)


def _route(logits):
    T = logits.shape[0]
    tm = min(ROW_TILE, T)
    spec = pl.BlockSpec((tm, LANES), lambda i: (i, 0))
    spec_t = pl.BlockSpec((SUBLANES, tm), lambda i: (0, i))
    return pl.pallas_call(
        _route_kernel,
        grid=(T // tm,),
        in_specs=[spec],
        out_specs=[spec, spec_t, spec_t, spec_t,
                   pl.BlockSpec((N_EXPERTS, LANES), lambda i: (0, 0))],
        out_shape=[
            jax.ShapeDtypeStruct((T, LANES), F32),
            jax.ShapeDtypeStruct((SUBLANES, T), F32),
            jax.ShapeDtypeStruct((SUBLANES, T), I32),
            jax.ShapeDtypeStruct((SUBLANES, T), I32),
            jax.ShapeDtypeStruct((N_EXPERTS, LANES), F32),
        ],
        compiler_params=_cparams("arbitrary"),
        name="route",
    )(logits)


def _tile(ref, row):
    return ref.at[pl.ds(pl.multiple_of(row * SUBLANES, SUBLANES), SUBLANES)]


def _dispatch_kernel(dest_ref, pad_start_ref, pad_len_ref, nused_ref, h2_ref, buf_ref,
                     src_ref, pad_ref, sem, zsem, *, n_tokens):
    i = pl.program_id(0)
    tm = h2_ref.shape[0] // SUBLANES
    blk = pad_ref.shape[0] // (2 * SUBLANES)
    nblocks = buf_ref.shape[0] // (blk * SUBLANES)

    @pl.when(i == 0)
    def _():
        li = lax.broadcasted_iota(I32, pad_ref.shape, 0)
        trash = (TOP_K * n_tokens + (li >> 3)).astype(U32)
        pad_ref[...] = jnp.where((li & (SUBLANES - 1)) >= PACK_LINES, trash, jnp.uint32(0))

        def per_expert(e, _):
            start = pad_start_ref[e]
            npad = pad_len_ref[e]

            def put(r, _):
                pltpu.make_async_copy(_tile(pad_ref, lax.rem(start + r, 2 * blk)),
                                      _tile(buf_ref, start + r), zsem).start()
                return 0

            def done(r, _):
                pltpu.make_async_copy(_tile(pad_ref, 0), _tile(buf_ref, 0), zsem).wait()
                return 0

            lax.fori_loop(0, npad, put, 0)
            lax.fori_loop(0, npad, done, 0)
            return 0

        lax.fori_loop(0, N_EXPERTS, per_expert, 0)

        def tail_block(b):
            rows = blk * SUBLANES
            return pltpu.make_async_copy(
                pad_ref.at[pl.ds(0, rows)],
                buf_ref.at[pl.ds(pl.multiple_of(b * rows, rows), rows)], zsem)

        def tail_put(b, _):
            tail_block(b).start()
            return 0

        def tail_done(b, _):
            tail_block(b).wait()
            return 0

        lax.fori_loop(nused_ref[0], nblocks, tail_put, 0)
        lax.fori_loop(nused_ref[0], nblocks, tail_done, 0)

    parity = lax.rem(i, 2)
    half = parity * TOP_K
    li = lax.broadcasted_iota(I32, h2_ref.shape, 0)
    tok = i * tm + (li >> 3)
    is_id = (li & (SUBLANES - 1)) >= PACK_LINES
    data = h2_ref[...]
    for k in range(TOP_K):
        src_ref[half + k] = jnp.where(is_id, (k * n_tokens + tok).astype(U32), data)

    def issue(r, _):
        for k in range(TOP_K):
            pltpu.make_async_copy(_tile(src_ref.at[half + k], r),
                                  _tile(buf_ref, dest_ref[k * (tm + 1) + r]),
                                  sem.at[parity]).start(priority=k % 2)
        return 0

    lax.fori_loop(0, tm, issue, 0)

    def wait_step(which):
        for k in range(TOP_K):
            pltpu.make_async_copy(src_ref.at[k], buf_ref.at[pl.ds(0, tm * SUBLANES)],
                                  sem.at[which]).wait()

    pl.when(i >= 1)(functools.partial(wait_step, 1 - parity))
    pl.when(i == pl.num_programs(0) - 1)(functools.partial(wait_step, parity))


def _dispatch(dest, pad_start, pad_len, nused, h2_rows, n_rows, blk):
    T = h2_rows.shape[0] // SUBLANES
    tm = min(DMA_TILE, T)
    nt = T // tm
    words = pl.cdiv(TOP_K * (tm + 1), SMEM_1D_TILE) * SMEM_1D_TILE
    d = jnp.pad(dest.reshape(TOP_K, nt, tm).transpose(1, 0, 2), ((0, 0), (0, 0), (0, 1)))
    d = jnp.pad(d.reshape(nt, TOP_K * (tm + 1)), ((0, 0), (0, words - TOP_K * (tm + 1))))
    dest_flat = d.reshape(nt * words)
    smem = pl.BlockSpec(memory_space=pltpu.SMEM)
    return pl.pallas_call(
        functools.partial(_dispatch_kernel, n_tokens=T),
        grid=(nt,),
        in_specs=[
            pl.BlockSpec((words,), lambda i: (i,), memory_space=pltpu.SMEM),
            smem, smem, smem,
            pl.BlockSpec((tm * SUBLANES, LANES), lambda i: (i, 0)),
        ],
        out_specs=pl.BlockSpec(memory_space=pl.ANY),
        out_shape=jax.ShapeDtypeStruct((n_rows * SUBLANES, LANES), U32),
        scratch_shapes=[pltpu.VMEM((2 * TOP_K, tm * SUBLANES, LANES), U32),
                        pltpu.VMEM((2 * blk * SUBLANES, LANES), U32),
                        pltpu.SemaphoreType.DMA((2,)), pltpu.SemaphoreType.DMA(())],
        compiler_params=_cparams("arbitrary"),
        name="dispatch",
    )(dest_flat, pad_start, pad_len, nused, h2_rows)


def _ffn_kernel(blk_exp_ref, nused_ref, x_ref, wgu_ref, bgu_ref, wd_ref, bd_ref, y_ref,
                wgu_bf_ref, wd_bf_ref, out_ref, idv_ref, ids_ref, ssem, isem, *, n_tokens):
    b = pl.program_id(0)
    nb = pl.num_programs(0)
    tm = x_ref.shape[0] // SUBLANES
    groups = tm // LANES
    F = wd_ref.shape[1]
    nused = nused_ref[0]
    used = b < nused
    slot = lax.rem(b, 2)
    prev = 1 - slot
    new_expert = (b == 0) | (blk_exp_ref[b] != blk_exp_ref[jnp.maximum(b - 1, 0)])

    def send_row(s, r_hi, r_lo, row, priority):
        rid = ids_ref[s * SUBLANES + r_hi, r_lo]
        pltpu.make_async_copy(_tile(out_ref, s * tm + row), _tile(y_ref, rid),
                              ssem.at[s]).start(priority=priority)

    def send_rows_unrolled(s, lo, hi):
        for r in range(lo, hi):
            send_row(s, r // LANES, r % LANES, r, r % 2)

    def send_block_rolled(s):
        def body(g, _):
            def inner(l, _):
                send_row(s, g, l, g * LANES + l, 0)
                return 0
            return lax.fori_loop(0, LANES, inner, 0)
        lax.fori_loop(0, groups, body, 0)

    def wait_block(s):
        rows = tm * SUBLANES
        pltpu.make_async_copy(out_ref.at[pl.ds(0, rows)], y_ref.at[pl.ds(0, rows)],
                              ssem.at[s]).wait()

    def id_copy(s):
        return pltpu.make_async_copy(
            idv_ref, ids_ref.at[pl.ds(pl.multiple_of(s * SUBLANES, SUBLANES), SUBLANES)], isem)

    @pl.when(b == 0)
    def _():
        out_ref[...] = jnp.zeros_like(out_ref)
        rows = tm * SUBLANES
        pltpu.make_async_copy(out_ref.at[pl.ds(0, rows)],
                              y_ref.at[pl.ds(TOP_K * n_tokens * SUBLANES, rows)],
                              ssem.at[0]).start()
        pos = (lax.broadcasted_iota(I32, idv_ref.shape, 0) * LANES
               + lax.broadcasted_iota(I32, idv_ref.shape, 1))
        idv_ref[...] = TOP_K * n_tokens + tm + jnp.minimum(pos, tm - 1)
        id_copy(1).start()
        id_copy(1).wait()

    @pl.when((b >= 1) & (b <= nused))
    def _():
        id_copy(prev).wait()

    @pl.when(used & new_expert)
    def _():
        wgu_bf_ref[...] = wgu_ref[0].astype(BF16)
        wd_bf_ref[...] = wd_ref[0].astype(BF16)

    def block(slot, prev):
        words = [x_ref[pl.ds(j, tm, stride=SUBLANES), :] for j in range(PACK_LINES)]
        hi = [lax.bitcast_convert_type(w & jnp.uint32(0xFFFF0000), F32) for w in words]
        lo = [lax.bitcast_convert_type(w << 16, F32) for w in words]
        x = jnp.concatenate(hi + lo, axis=1).astype(BF16)
        acts = []
        for c in range(FFN_SLICES):
            send_rows_unrolled(prev, c * tm // FFN_SLICES, (c + 1) * tm // FFN_SLICES)
            w = F // FFN_SLICES
            g = jnp.dot(x, wgu_bf_ref[:, c * w:(c + 1) * w], preferred_element_type=F32)
            u = jnp.dot(x, wgu_bf_ref[:, F + c * w:F + (c + 1) * w],
                        preferred_element_type=F32)
            g = jnp.minimum(g + bgu_ref[0, :, c * w:(c + 1) * w], SWIGLU_LIMIT)
            u = jnp.clip(u + bgu_ref[0, :, F + c * w:F + (c + 1) * w],
                         -SWIGLU_LIMIT, SWIGLU_LIMIT)
            acts.append((g * jax.nn.sigmoid(SWIGLU_ALPHA * g) * (u + 1.0)).astype(BF16))
        act = jnp.concatenate(acts, axis=1)
        y = jnp.dot(act, wd_bf_ref[...], preferred_element_type=F32) + bd_ref[0]
        wait_block(slot)
        base = slot * (tm * SUBLANES)
        for j in range(SUBLANES):
            out_ref[pl.ds(base + j, tm, stride=SUBLANES), :] = y[:, j * LANES:(j + 1) * LANES]
        idl = x_ref[pl.ds(PACK_LINES, tm, stride=SUBLANES), :].astype(I32)
        rr = lax.broadcasted_iota(I32, idl.shape, 0)
        ll = lax.broadcasted_iota(I32, idl.shape, 1)
        diag = jnp.where((rr & (LANES - 1)) == ll, idl, 0)
        idv_ref[pl.ds(0, groups), :] = jnp.sum(diag.reshape(groups, LANES, LANES), axis=1)
        id_copy(slot).start()

    for parity in range(2):
        pl.when(used & (slot == parity))(functools.partial(block, parity, 1 - parity))

    @pl.when(b == nused)
    def _():
        wait_block(slot)
        send_block_rolled(prev)
        wait_block(prev)

    @pl.when(used & (b == nb - 1))
    def _():
        wait_block(prev)
        id_copy(slot).wait()
        send_block_rolled(slot)
        wait_block(slot)


def _ffn(blk_exp, nused, buf, wgu, bgu, wd, bd, tm, n_tokens):
    nb = buf.shape[0] // (tm * SUBLANES)
    D = wgu.shape[1]
    F = wd.shape[1]

    def xmap(b, be, nu):
        return (jnp.minimum(b, nu[0] - 1), 0)

    def wmap(b, be, nu):
        return (be[jnp.minimum(b, nu[0] - 1)], 0, 0)

    grid_spec = pltpu.PrefetchScalarGridSpec(
        num_scalar_prefetch=2,
        grid=(nb,),
        in_specs=[
            pl.BlockSpec((tm * SUBLANES, LANES), xmap),
            pl.BlockSpec((1, D, 2 * F), wmap),
            pl.BlockSpec((1, 1, 2 * F), wmap),
            pl.BlockSpec((1, F, D), wmap),
            pl.BlockSpec((1, 1, D), wmap),
        ],
        out_specs=pl.BlockSpec(memory_space=pl.ANY),
        scratch_shapes=[
            pltpu.VMEM((D, 2 * F), BF16), pltpu.VMEM((F, D), BF16),
            pltpu.VMEM((2 * tm * SUBLANES, LANES), F32),
            pltpu.VMEM((SUBLANES, LANES), I32),
            pltpu.SMEM((2 * SUBLANES, LANES), I32),
            pltpu.SemaphoreType.DMA((2,)), pltpu.SemaphoreType.DMA(()),
        ],
    )
    return pl.pallas_call(
        functools.partial(_ffn_kernel, n_tokens=n_tokens),
        grid_spec=grid_spec,
        out_shape=jax.ShapeDtypeStruct(((TOP_K * n_tokens + 2 * tm) * SUBLANES, LANES), F32),
        compiler_params=_cparams("arbitrary"),
        name="ffn",
    )(blk_exp, nused, buf, wgu, bgu, wd, bd)


def _combine_kernel(gate_ref, x2_ref, fg_ref, *refs):
    y_refs, o_ref = refs[:TOP_K], refs[TOP_K]
    tm, D = x2_ref.shape
    nj = D // LANES
    gate = gate_ref[...]
    cols = []
    for j in range(nj):
        acc = None
        for k in range(TOP_K):
            part = gate[:, k:k + 1] * y_refs[k][pl.ds(j, tm, stride=nj), :]
            acc = part if acc is None else acc + part
        cols.append(acc)
    y = jnp.concatenate(cols, axis=1)
    o_ref[...] = _rms(x2_ref[...] + y, fg_ref[...])


def _combine(gate4, x2, fg, y_rows):
    T, D = x2.shape
    tm = min(ROW_TILE, T)
    nj = D // LANES
    nt = T // tm

    def slot_spec(k):
        return pl.BlockSpec((tm * nj, LANES), lambda i: (k * nt + i, 0))

    return pl.pallas_call(
        _combine_kernel,
        grid=(nt,),
        in_specs=[
            pl.BlockSpec((tm, LANES), lambda i: (i, 0)),
            pl.BlockSpec((tm, D), lambda i: (i, 0)),
            pl.BlockSpec((1, D), lambda i: (0, 0)),
        ] + [slot_spec(k) for k in range(TOP_K)],
        out_specs=pl.BlockSpec((tm, D), lambda i: (i, 0)),
        out_shape=jax.ShapeDtypeStruct((T, D), F32),
        compiler_params=_cparams("parallel"),
        name="combine",
    )(gate4, x2, fg, *([y_rows] * TOP_K))


def _rope_tables(S):
    rows = S // GRID_W
    row = jnp.repeat(jnp.arange(rows, dtype=I32), GRID_W).astype(F32)
    col = jnp.tile(jnp.arange(GRID_W, dtype=I32), rows).astype(F32)
    inv = ROPE_THETA ** (-jnp.arange(0, AXIS_DIM, 2, dtype=F32) / AXIS_DIM)
    ang_r = row[:, None] * inv[None, :]
    ang_c = col[:, None] * inv[None, :]
    cos = jnp.concatenate([jnp.cos(ang_r)] * 2 + [jnp.cos(ang_c)] * 2, axis=1)
    sin = jnp.concatenate([-jnp.sin(ang_r), jnp.sin(ang_r), -jnp.sin(ang_c), jnp.sin(ang_c)], axis=1)
    return jnp.tile(cos, (1, 2)), jnp.tile(sin, (1, 2))


def _block_diag_gates(wa, ba, wi, bi, ct):
    nb, bw = wa.shape[1], wa.shape[2]
    per = ct // bw
    nct = nb // per
    eye = jnp.eye(per, dtype=wa.dtype)

    def tiles(w):
        w = w.reshape(nct, per, bw, bw)
        return jnp.einsum('cpij,pq->cpiqj', w, eye).reshape(nct, ct, ct)

    wg = jnp.concatenate([tiles(wa[0]), tiles(wi[0]), tiles(wa[1]), tiles(wi[1])], axis=2)
    bias = lambda b: b.reshape(nct, 1, ct)
    bg = jnp.concatenate([bias(ba[0]), bias(bi[0]), bias(ba[1]), bias(bi[1])], axis=2)
    return (0.5 * wg).astype(BF16), 0.5 * bg


def _layer(x2d, B, S, norm1_g, w_in, b_in, q_norm_g, k_norm_g, conv_w, conv_b, lru_wa, lru_ba,
           lru_wi, lru_bi, lru_lambda, w_attn_o, w_lru_o, w_out, norm2_g, w_router, b_router,
           w_gu, b_gu, w_down, b_down, out_g):
    T, D = x2d.shape
    C = conv_w.shape[1]
    nqkv = Q_WIDTH + 2 * KV_WIDTH
    row2 = lambda v: v.reshape(1, -1)

    cos_t, sin_t = _rope_tables(S)
    head = jnp.arange(LANES) // HEAD_DIM
    hsum = (head[:, None] == head[None, :]).astype(BF16)
    q, k, v = _qkv(x2d, row2(norm1_g), w_in[:, :nqkv].astype(BF16), row2(b_in[:nqkv]),
                   cos_t, sin_t, row2(jnp.tile(q_norm_g, 2)), row2(jnp.tile(k_norm_g, 2)),
                   hsum, B, S)
    proj = _proj(x2d, row2(norm1_g), w_in[:, nqkv:].astype(BF16), row2(b_in[nqkv:]))
    attn = _attn(q, k, v)
    wg, bg = _block_diag_gates(lru_wa, lru_ba, lru_wi, lru_bi, LRU_CT)
    rec = _lru(proj, conv_w, row2(conv_b), wg, bg, lru_lambda, B, S, C)

    pad = LANES - N_EXPERTS
    rw = jnp.pad(w_router, ((0, 0), (0, pad)))
    rw_hi = rw.astype(BF16)
    rw_lo = (rw - rw_hi.astype(F32)).astype(BF16)
    rb = jnp.pad(b_router, (0, pad), constant_values=NEG_BIG).reshape(1, LANES)
    x2, h2_rows, logits = _merge(x2d, attn, rec, proj, w_attn_o.astype(BF16),
                                 w_lru_o.astype(BF16), w_out.astype(BF16), row2(norm2_g),
                                 rw_hi, rw_lo, rb)

    gate4, _, eidx_t, rank_t, cnt = _route(logits)
    counts = cnt[:, 0].astype(I32)
    tm = FFN_TM
    nblk = (counts + tm - 1) // tm
    pend_blk = jnp.cumsum(nblk)
    pstart = (pend_blk - nblk) * tm
    A = T * TOP_K
    nb = (A + N_EXPERTS * (tm - 1) + tm - 1) // tm
    blk_exp = jnp.minimum(
        jnp.sum(pend_blk[None, :] <= jnp.arange(nb, dtype=I32)[:, None], axis=1),
        N_EXPERTS - 1).astype(I32)
    nused = pend_blk[-1:].astype(I32)
    first_row = jnp.sum(jnp.where(eidx_t[:TOP_K, :, None] == jnp.arange(N_EXPERTS), pstart, 0),
                        axis=-1)
    dest = (first_row + rank_t[:TOP_K]).astype(I32)
    pad_start = (pstart + counts).astype(I32)
    pad_len = (nblk * tm - counts).astype(I32)

    buf = _dispatch(dest, pad_start, pad_len, nused, h2_rows, nb * tm, tm)
    y_rows = _ffn(blk_exp, nused, buf, w_gu, b_gu.reshape(N_EXPERTS, 1, -1), w_down,
                  b_down.reshape(N_EXPERTS, 1, -1), tm, T)
    return _combine(gate4, x2, row2(out_g), y_rows)


def kernel(x, norm1_g, w_in, b_in, q_norm_g, k_norm_g, conv_w, conv_b, lru_wa, lru_ba, lru_wi,
           lru_bi, lru_lambda, w_attn_o, w_lru_o, w_out, norm2_g, w_router, b_router, w_gu, b_gu,
           w_down, b_down, final_g):
    B, S, D = x.shape
    depth = norm1_g.shape[0]
    assert depth == 1, "the fused final RMSNorm assumes a single layer"
    assert S % (TIME_CHUNKS * SUBLANES) == 0 and S % GRID_W == 0
    assert D == 2 * PACK_LINES * LANES, "a token's bf16 row must fill PACK_LINES word lines"
    out = _layer(x.reshape(B * S, D), B, S, norm1_g[0], w_in[0], b_in[0], q_norm_g[0],
                 k_norm_g[0], conv_w[0], conv_b[0], lru_wa[0], lru_ba[0], lru_wi[0], lru_bi[0],
                 lru_lambda[0], w_attn_o[0], w_lru_o[0], w_out[0], norm2_g[0], w_router[0],
                 b_router[0], w_gu[0], b_gu[0], w_down[0], b_down[0], final_g)
    return out.reshape(B, S, D)
```

```python
import functools
import math

import jax
import jax.numpy as jnp
from jax import lax
from jax.experimental import pallas as pl
from jax.experimental.pallas import tpu as pltpu

F32 = jnp.float32
BF16 = jnp.bfloat16
I32 = jnp.int32
U32 = jnp.uint32

LANES = 128
SUBLANES = 8
PACK_LINES = 4
SMEM_1D_TILE = 1024
VMEM_LIMIT_BYTES = 56 * 1024 * 1024

HEAD_DIM = 64
N_Q_HEADS = 8
N_KV_HEADS = 2
Q_GROUP = N_Q_HEADS // N_KV_HEADS
Q_WIDTH = N_Q_HEADS * HEAD_DIM
KV_WIDTH = N_KV_HEADS * HEAD_DIM
AXIS_DIM = HEAD_DIM // 2
ROT_HALF = AXIS_DIM // 2
ROPE_THETA = 10000.0
GRID_W = 64
LRU_BLOCKS = 16
RG_C = 8.0
N_EXPERTS = 32
TOP_K = 4
SWIGLU_LIMIT = 7.0
SWIGLU_ALPHA = 1.702
NORM_EPS = 1e-6
LOG2E = 1.4426950408889634
NEG_BIG = -1e30
TINY = 1e-30

ROW_TILE = 512
ATT_TQ = 512
ATT_TK = 512
LRU_CT = 256
LRU_ROWS = 256
TIME_CHUNKS = SUBLANES
FFN_TM = 512
FFN_SLICES = 4
DMA_TILE = 256


def _cparams(*sem):
    return pltpu.CompilerParams(dimension_semantics=sem, vmem_limit_bytes=VMEM_LIMIT_BYTES)


def _rms(x, g):
    return x * lax.rsqrt(jnp.mean(x * x, axis=-1, keepdims=True) + NORM_EPS) * g


def _qkv_kernel(x_ref, g1_ref, w_ref, b_ref, cos_ref, sin_ref, qg_ref, kg_ref, hsum_ref,
                q_ref, k_ref, v_ref):
    h = _rms(x_ref[...], g1_ref[...]).astype(BF16)
    p = jnp.dot(h, w_ref[...], preferred_element_type=F32) + b_ref[...]
    cos = cos_ref[...]
    sin = sin_ref[...]
    hsum = hsum_ref[...]
    lane = lax.broadcasted_iota(I32, cos.shape, 1)
    first_half = (lane % AXIS_DIM) < ROT_HALF

    def norm_rope(c, gain):
        sq = c * c
        hi = sq.astype(BF16)
        lo = (sq - hi.astype(F32)).astype(BF16)
        ms = (jnp.dot(hi, hsum, preferred_element_type=F32)
              + jnp.dot(lo, hsum, preferred_element_type=F32)) * (1.0 / HEAD_DIM)
        y = c * lax.rsqrt(ms + NORM_EPS) * gain
        partner = jnp.where(first_half, pltpu.roll(y, LANES - ROT_HALF, 1),
                            pltpu.roll(y, ROT_HALF, 1))
        return y * cos + partner * sin

    qg = qg_ref[...]
    for c in range(Q_WIDTH // LANES):
        y = norm_rope(p[:, c * LANES:(c + 1) * LANES], qg) * (HEAD_DIM ** -0.5 * LOG2E)
        yt = y.T.astype(BF16)
        q_ref[0, 2 * c] = yt[:HEAD_DIM]
        q_ref[0, 2 * c + 1] = yt[HEAD_DIM:]
    yk = norm_rope(p[:, Q_WIDTH:Q_WIDTH + KV_WIDTH], kg_ref[...])
    k_ref[0, 0] = yk[:, :HEAD_DIM].astype(BF16)
    k_ref[0, 1] = yk[:, HEAD_DIM:].astype(BF16)
    vt = p[:, Q_WIDTH + KV_WIDTH:].T
    tm = vt.shape[1]
    ones_row = (lax.broadcasted_iota(I32, (HEAD_DIM, tm), 0) == 0).astype(BF16)
    v_ref[0, 0] = jnp.concatenate([vt[:HEAD_DIM].astype(BF16), ones_row], axis=0)
    v_ref[0, 1] = jnp.concatenate([vt[HEAD_DIM:].astype(BF16), ones_row], axis=0)


def _qkv(x2d, g1, w_qkv, b_qkv, cos_t, sin_t, qg, kg, hsum, B, S):
    T, D = x2d.shape
    tm = min(ROW_TILE, S)
    ns = S // tm
    n = w_qkv.shape[1]
    full = lambda shape: pl.BlockSpec(shape, lambda i: (0,) * len(shape))
    return pl.pallas_call(
        _qkv_kernel,
        grid=(T // tm,),
        in_specs=[
            pl.BlockSpec((tm, D), lambda i: (i, 0)),
            full((1, D)), full((D, n)), full((1, n)),
            pl.BlockSpec((tm, LANES), lambda i: (i % ns, 0)),
            pl.BlockSpec((tm, LANES), lambda i: (i % ns, 0)),
            full((1, LANES)), full((1, LANES)), full((LANES, LANES)),
        ],
        out_specs=[
            pl.BlockSpec((1, N_Q_HEADS, HEAD_DIM, tm), lambda i: (i // ns, 0, 0, i % ns)),
            pl.BlockSpec((1, N_KV_HEADS, tm, HEAD_DIM), lambda i: (i // ns, 0, i % ns, 0)),
            pl.BlockSpec((1, N_KV_HEADS, LANES, tm), lambda i: (i // ns, 0, 0, i % ns)),
        ],
        out_shape=[
            jax.ShapeDtypeStruct((B, N_Q_HEADS, HEAD_DIM, S), BF16),
            jax.ShapeDtypeStruct((B, N_KV_HEADS, S, HEAD_DIM), BF16),
            jax.ShapeDtypeStruct((B, N_KV_HEADS, LANES, S), BF16),
        ],
        compiler_params=_cparams("parallel"),
        name="qkv",
    )(x2d, g1, w_qkv, b_qkv, cos_t, sin_t, qg, kg, hsum)


def _proj_kernel(x_ref, g_ref, w_ref, b_ref, o_ref):
    h = _rms(x_ref[...], g_ref[...]).astype(BF16)
    o_ref[...] = jnp.dot(h, w_ref[...], preferred_element_type=F32) + b_ref[...]


def _proj(x2d, g, w, b):
    T, D = x2d.shape
    n = w.shape[1]
    tm = min(ROW_TILE, T)
    return pl.pallas_call(
        _proj_kernel,
        grid=(T // tm,),
        in_specs=[
            pl.BlockSpec((tm, D), lambda i: (i, 0)),
            pl.BlockSpec((1, D), lambda i: (0, 0)),
            pl.BlockSpec((D, n), lambda i: (0, 0)),
            pl.BlockSpec((1, n), lambda i: (0, 0)),
        ],
        out_specs=pl.BlockSpec((tm, n), lambda i: (i, 0)),
        out_shape=jax.ShapeDtypeStruct((T, n), F32),
        compiler_params=_cparams("parallel"),
        name="proj",
    )(x2d, g, w, b)


def _attn_kernel(q_ref, k_ref, v_ref, o_ref, s0_ref, s1_ref, *, tk):
    tq = q_ref.shape[3]
    S = k_ref.shape[2]
    M = Q_GROUP * tq
    n = S // tk
    qT = jnp.concatenate([q_ref[0, h] for h in range(Q_GROUP)], axis=1)

    def scores(j, s_ref):
        off = pl.multiple_of(j * tk, tk)
        s_ref[...] = jnp.dot(k_ref[0, 0, pl.ds(off, tk), :], qT, preferred_element_type=F32)

    def absorb(j, s_ref, carry):
        m, acc = carry
        off = pl.multiple_of(j * tk, tk)
        vc = v_ref[0, 0, :, pl.ds(off, tk)]
        s = s_ref[...]
        m_new = jnp.maximum(m, jnp.max(s, axis=0, keepdims=True))
        alpha = jnp.exp2(m - m_new)
        p = jnp.exp2(s - m_new).astype(BF16)
        return m_new, alpha * acc + jnp.dot(vc, p, preferred_element_type=F32)

    def pair(i, carry):
        scores(2 * i + 1, s1_ref)
        carry = absorb(2 * i, s0_ref, carry)
        scores(2 * i + 2, s0_ref)
        return absorb(2 * i + 1, s1_ref, carry)

    scores(0, s0_ref)
    carry = (jnp.full((1, M), NEG_BIG, F32), jnp.zeros((LANES, M), F32))
    carry = lax.fori_loop(0, n // 2 - 1, pair, carry)
    scores(n - 1, s1_ref)
    carry = absorb(n - 2, s0_ref, carry)
    _, acc = absorb(n - 1, s1_ref, carry)
    outT = acc[:HEAD_DIM] / acc[HEAD_DIM:HEAD_DIM + 1]
    stacked = jnp.concatenate(
        [outT[:, g * tq:(g + 1) * tq] for g in range(Q_GROUP)], axis=0)
    o_ref[...] = stacked.T.astype(BF16)


def _attn(qT, k, vT):
    B, _, _, S = qT.shape
    tq = min(ATT_TQ, S)
    tk = min(ATT_TK, S)
    nq = S // tq
    assert (S // tk) % 2 == 0, "key chunks are processed in pairs"
    score_buf = pltpu.VMEM((tk, Q_GROUP * tq), F32)
    return pl.pallas_call(
        functools.partial(_attn_kernel, tk=tk),
        grid=(B, N_KV_HEADS, nq),
        in_specs=[
            pl.BlockSpec((1, Q_GROUP, HEAD_DIM, tq), lambda b, g, i: (b, g, 0, i)),
            pl.BlockSpec((1, 1, S, HEAD_DIM), lambda b, g, i: (b, g, 0, 0)),
            pl.BlockSpec((1, 1, LANES, S), lambda b, g, i: (b, g, 0, 0)),
        ],
        out_specs=pl.BlockSpec((tq, Q_GROUP * HEAD_DIM), lambda b, g, i: (b * nq + i, g)),
        out_shape=jax.ShapeDtypeStruct((B * S, Q_WIDTH), BF16),
        scratch_shapes=[score_buf, score_buf],
        compiler_params=_cparams("parallel", "parallel", "parallel"),
        name="attn",
    )(qT, k, vT)


def _shift_chunks(v, down):
    row = lax.broadcasted_iota(I32, v.shape, 0)
    if down:
        return jnp.where(row == 0, 0.0, pltpu.roll(v, 1, 0))
    return jnp.where(row == SUBLANES - 1, 0.0, pltpu.roll(v, SUBLANES - 1, 0))


def _lru_kernel(xr_ref, yr_ref, cw_ref, cb_ref, wg_ref, bg_ref, lam_ref, o_ref,
                xe_ref, af_ref, uf_ref, ab_ref, ub_ref, *, rows):
    S, ct = xr_ref.shape
    lc = S // TIME_CHUNKS
    halo = SUBLANES

    ng = ct // LANES

    def xe_rows(r0, n):
        return jnp.concatenate([xe_ref[g, pl.ds(r0, n), :] for g in range(ng)], axis=1)

    def xe_store(r0, n, val):
        for g in range(ng):
            xe_ref[g, pl.ds(r0, n), :] = val[:, g * LANES:(g + 1) * LANES]

    for s in range(TIME_CHUNKS):
        for g in range(ng):
            xe_ref[g, pl.ds(halo + s, lc, stride=SUBLANES), :] = (
                xr_ref[pl.ds(s * lc, lc), g * LANES:(g + 1) * LANES])
    xe_store(0, halo, _shift_chunks(xe_rows(S, halo), True))
    first = xe_rows(halo, halo)
    second = xe_rows(2 * halo, halo)
    xe_store(S + halo, halo, _shift_chunks(first, False))
    xe_store(S + 2 * halo, halo, _shift_chunks(second, False))

    cw = cw_ref[...]
    cb = cb_ref[...]
    lam = lam_ref[...]
    log_sig = jnp.minimum(lam, 0.0) - jnp.log(1.0 + jnp.exp(-jnp.abs(lam)))
    c_half = (0.5 * RG_C * LOG2E) * log_sig
    wg = wg_ref[0]
    bg = bg_ref[0]

    def gate_chunk(i, _):
        r0 = pl.multiple_of(i * rows, rows)
        xc = cb
        for j in range(4):
            xc = xc + cw[j:j + 1, :] * xe_rows(r0 + j * halo, rows)
        t = jnp.tanh(jnp.dot(xc.astype(BF16), wg, preferred_element_type=F32) + bg)
        x_half = 0.5 * xc
        for d, (a_ref, u_ref) in enumerate(((af_ref, uf_ref), (ab_ref, ub_ref))):
            t_r = t[:, (2 * d) * ct:(2 * d + 1) * ct]
            t_i = t[:, (2 * d + 1) * ct:(2 * d + 2) * ct]
            ch = c_half[d:d + 1, :]
            a = jnp.exp2(t_r * ch + ch)
            a_ref[pl.ds(r0, rows), :] = a
            v = 1.0 - a * a
            root = jnp.maximum(v, 0.0) * lax.rsqrt(jnp.maximum(v, TINY))
            u_ref[pl.ds(r0, rows), :] = root * ((t_i + 1.0) * x_half)
        return 0

    lax.fori_loop(0, S // rows, gate_chunk, 0)

    unroll = 8

    def scan_body(i, carry):
        hf, pf, hb, pb = carry
        for k in range(unroll):
            tf = i * unroll + k
            rf = pl.multiple_of(tf * SUBLANES, SUBLANES)
            rb = pl.multiple_of((lc - 1 - tf) * SUBLANES, SUBLANES)
            a = af_ref[pl.ds(rf, SUBLANES), :]
            hf = a * hf + uf_ref[pl.ds(rf, SUBLANES), :]
            pf = a * pf
            uf_ref[pl.ds(rf, SUBLANES), :] = hf
            af_ref[pl.ds(rf, SUBLANES), :] = pf
            a = ab_ref[pl.ds(rb, SUBLANES), :]
            hb = a * hb + ub_ref[pl.ds(rb, SUBLANES), :]
            pb = a * pb
            ub_ref[pl.ds(rb, SUBLANES), :] = hb
            ab_ref[pl.ds(rb, SUBLANES), :] = pb
        return hf, pf, hb, pb

    zero = jnp.zeros((SUBLANES, ct), F32)
    one = jnp.ones((SUBLANES, ct), F32)
    hf_end, pf_end, hb_end, pb_end = lax.fori_loop(0, lc // unroll, scan_body,
                                                   (zero, one, zero, one))

    def chunk_carry(h_end, p_end, down):
        f = h_end
        for _ in range(TIME_CHUNKS - 1):
            f = h_end + p_end * _shift_chunks(f, down)
        return _shift_chunks(f, down)

    cin_f = chunk_carry(hf_end, pf_end, True)
    cin_b = chunk_carry(hb_end, pb_end, False)

    def fix_chunk(i, _):
        r0 = pl.multiple_of(i * rows, rows)
        reps = rows // SUBLANES
        hf = uf_ref[pl.ds(r0, rows), :] + af_ref[pl.ds(r0, rows), :] * jnp.tile(cin_f, (reps, 1))
        hb = ub_ref[pl.ds(r0, rows), :] + ab_ref[pl.ds(r0, rows), :] * jnp.tile(cin_b, (reps, 1))
        xe_store(r0, rows, hf + hb)
        return 0

    lax.fori_loop(0, S // rows, fix_chunk, 0)

    for s in range(TIME_CHUNKS):
        hsum = jnp.concatenate(
            [xe_ref[g, pl.ds(s, lc, stride=SUBLANES), :] for g in range(ng)], axis=1)
        y = yr_ref[pl.ds(s * lc, lc), :]
        o_ref[pl.ds(s * lc, lc), :] = (hsum * jax.nn.gelu(y, approximate=True)).astype(BF16)


def _lru(proj, conv_w, conv_b, wg, bg, lam, B, S, C):
    ct = LRU_CT
    nct = C // ct
    rows = min(LRU_ROWS, S)
    return pl.pallas_call(
        functools.partial(_lru_kernel, rows=rows),
        grid=(B, nct),
        in_specs=[
            pl.BlockSpec((S, ct), lambda b, c: (b, c)),
            pl.BlockSpec((S, ct), lambda b, c: (b, nct + c)),
            pl.BlockSpec((4, ct), lambda b, c: (0, c)),
            pl.BlockSpec((1, ct), lambda b, c: (0, c)),
            pl.BlockSpec((1, ct, 4 * ct), lambda b, c: (c, 0, 0)),
            pl.BlockSpec((1, 1, 4 * ct), lambda b, c: (c, 0, 0)),
            pl.BlockSpec((2, ct), lambda b, c: (0, c)),
        ],
        out_specs=pl.BlockSpec((S, ct), lambda b, c: (b, c)),
        out_shape=jax.ShapeDtypeStruct((B * S, C), BF16),
        scratch_shapes=[
            pltpu.VMEM((ct // LANES, S + 3 * SUBLANES, LANES), F32),
            pltpu.VMEM((S, ct), F32), pltpu.VMEM((S, ct), F32),
            pltpu.VMEM((S, ct), F32), pltpu.VMEM((S, ct), F32),
        ],
        compiler_params=_cparams("parallel", "parallel"),
        name="lru",
    )(proj, proj, conv_w, conv_b, wg, bg, lam)


def _merge_kernel(x_ref, attn_ref, rec_ref, gl_ref, wa_ref, wr_ref, wo_ref, g2_ref,
                  rw_hi_ref, rw_lo_ref, rb_ref, x2_ref, h2_ref, lg_ref):
    D = x_ref.shape[1]
    gl = gl_ref[...]
    ma = jnp.dot(attn_ref[...], wa_ref[...], preferred_element_type=F32)
    mr = jnp.dot(rec_ref[...], wr_ref[...], preferred_element_type=F32)
    merged = jax.nn.sigmoid(gl[:, :D]) * ma + jax.nn.sigmoid(gl[:, D:]) * mr
    x2 = x_ref[...] + jnp.dot(merged.astype(BF16), wo_ref[...], preferred_element_type=F32)
    x2_ref[...] = x2
    h2 = _rms(x2, g2_ref[...])
    hi = h2.astype(BF16)
    lo = (h2 - hi.astype(F32)).astype(BF16)
    lg_ref[...] = (jnp.dot(hi, rw_hi_ref[...], preferred_element_type=F32)
                   + jnp.dot(lo, rw_hi_ref[...], preferred_element_type=F32)
                   + jnp.dot(hi, rw_lo_ref[...], preferred_element_type=F32)) + rb_ref[...]
    tm = h2.shape[0]
    bits = lax.bitcast_convert_type(hi.astype(F32), U32)
    words = bits[:, :D // 2] | (bits[:, D // 2:] >> 16)
    for j in range(PACK_LINES):
        h2_ref[pl.ds(j, tm, stride=SUBLANES), :] = words[:, j * LANES:(j + 1) * LANES]
    for j in range(PACK_LINES, SUBLANES):
        h2_ref[pl.ds(j, tm, stride=SUBLANES), :] = jnp.zeros((tm, LANES), U32)


def _merge(x2d, attn, rec, proj, wa, wr, wo, g2, rw_hi, rw_lo, rb):
    T, D = x2d.shape
    tm = min(ROW_TILE, T)
    nj = D // LANES
    full = lambda a: pl.BlockSpec(a.shape, lambda i: (0,) * a.ndim)
    return pl.pallas_call(
        _merge_kernel,
        grid=(T // tm,),
        in_specs=[
            pl.BlockSpec((tm, D), lambda i: (i, 0)),
            pl.BlockSpec((tm, Q_WIDTH), lambda i: (i, 0)),
            pl.BlockSpec((tm, D), lambda i: (i, 0)),
            pl.BlockSpec((tm, 2 * D), lambda i: (i, 1)),
            full(wa), full(wr), full(wo), full(g2), full(rw_hi), full(rw_lo), full(rb),
        ],
        out_specs=[
            pl.BlockSpec((tm, D), lambda i: (i, 0)),
            pl.BlockSpec((tm * nj, LANES), lambda i: (i, 0)),
            pl.BlockSpec((tm, LANES), lambda i: (i, 0)),
        ],
        out_shape=[
            jax.ShapeDtypeStruct((T, D), F32),
            jax.ShapeDtypeStruct((T * nj, LANES), U32),
            jax.ShapeDtypeStruct((T, LANES), F32),
        ],
        compiler_params=_cparams("parallel"),
        name="merge",
    )(x2d, attn, rec, proj, wa, wr, wo, g2, rw_hi, rw_lo, rb)


def _route_kernel(lg_ref, gate_ref, gate_t_ref, eidx_t_ref, rank_t_ref, cnt_ref):
    i = pl.program_id(0)

    @pl.when(i == 0)
    def _():
        cnt_ref[...] = jnp.zeros_like(cnt_ref)

    tm = lg_ref.shape[0]
    work = lg_ref[...].T[:N_EXPERTS]
    expert = lax.broadcasted_iota(I32, work.shape, 0)
    sels, vals, idxs = [], [], []
    for _ in range(TOP_K):
        mx = jnp.max(work, axis=0, keepdims=True)
        idx = jnp.min(jnp.where(work == mx, expert, N_EXPERTS), axis=0, keepdims=True)
        sel = expert == idx
        sels.append(sel)
        vals.append(mx)
        idxs.append(idx)
        work = jnp.where(sel, NEG_BIG, work)
    ex = [jnp.exp(v - vals[0]) for v in vals]
    den = ex[0] + ex[1] + ex[2] + ex[3]
    member = (sels[0] | sels[1] | sels[2] | sels[3]).astype(BF16)
    r = lax.broadcasted_iota(I32, (tm, tm), 0)
    c = lax.broadcasted_iota(I32, (tm, tm), 1)
    tri = (r < c).astype(BF16)
    base = cnt_ref[:, 0:1]
    before = jnp.dot(member, tri, preferred_element_type=F32) + base
    ranks = [jnp.sum(jnp.where(sels[k], before, 0.0), axis=0, keepdims=True)
             for k in range(TOP_K)]
    gates = [ex[k] / den for k in range(TOP_K)]
    pad = SUBLANES - TOP_K
    gate_t = jnp.concatenate(gates + [jnp.zeros((pad, tm), F32)], axis=0)
    gate_t_ref[...] = gate_t
    eidx_t_ref[...] = jnp.concatenate(idxs + [jnp.zeros((pad, tm), I32)], axis=0)
    rank_t_ref[...] = jnp.concatenate(ranks + [jnp.zeros((pad, tm), F32)], axis=0).astype(I32)
    gate_ref[...] = jnp.concatenate(
        [gate_t, jnp.zeros((LANES - SUBLANES, tm), F32)], axis=0).T
    cnt_ref[...] = jnp.broadcast_to(
        base + jnp.sum(member.astype(F32), axis=1, keepdims=True), cnt_ref.shape)


def _route(logits):
    T = logits.shape[0]
    tm = min(ROW_TILE, T)
    spec = pl.BlockSpec((tm, LANES), lambda i: (i, 0))
    spec_t = pl.BlockSpec((SUBLANES, tm), lambda i: (0, i))
    return pl.pallas_call(
        _route_kernel,
        grid=(T // tm,),
        in_specs=[spec],
        out_specs=[spec, spec_t, spec_t, spec_t,
                   pl.BlockSpec((N_EXPERTS, LANES), lambda i: (0, 0))],
        out_shape=[
            jax.ShapeDtypeStruct((T, LANES), F32),
            jax.ShapeDtypeStruct((SUBLANES, T), F32),
            jax.ShapeDtypeStruct((SUBLANES, T), I32),
            jax.ShapeDtypeStruct((SUBLANES, T), I32),
            jax.ShapeDtypeStruct((N_EXPERTS, LANES), F32),
        ],
        compiler_params=_cparams("arbitrary"),
        name="route",
    )(logits)


def _tile(ref, row):
    return ref.at[pl.ds(pl.multiple_of(row * SUBLANES, SUBLANES), SUBLANES)]


def _dispatch_kernel(dest_ref, pad_start_ref, pad_len_ref, nused_ref, h2_ref, buf_ref,
                     src_ref, pad_ref, sem, zsem, *, n_tokens):
    i = pl.program_id(0)
    tm = h2_ref.shape[0] // SUBLANES
    blk = pad_ref.shape[0] // (2 * SUBLANES)
    nblocks = buf_ref.shape[0] // (blk * SUBLANES)

    @pl.when(i == 0)
    def _():
        li = lax.broadcasted_iota(I32, pad_ref.shape, 0)
        trash = (TOP_K * n_tokens + (li >> 3)).astype(U32)
        pad_ref[...] = jnp.where((li & (SUBLANES - 1)) >= PACK_LINES, trash, jnp.uint32(0))

        def per_expert(e, _):
            start = pad_start_ref[e]
            npad = pad_len_ref[e]

            def put(r, _):
                pltpu.make_async_copy(_tile(pad_ref, lax.rem(start + r, 2 * blk)),
                                      _tile(buf_ref, start + r), zsem).start()
                return 0

            def done(r, _):
                pltpu.make_async_copy(_tile(pad_ref, 0), _tile(buf_ref, 0), zsem).wait()
                return 0

            lax.fori_loop(0, npad, put, 0)
            lax.fori_loop(0, npad, done, 0)
            return 0

        lax.fori_loop(0, N_EXPERTS, per_expert, 0)

        def tail_block(b):
            rows = blk * SUBLANES
            return pltpu.make_async_copy(
                pad_ref.at[pl.ds(0, rows)],
                buf_ref.at[pl.ds(pl.multiple_of(b * rows, rows), rows)], zsem)

        def tail_put(b, _):
            tail_block(b).start()
            return 0

        def tail_done(b, _):
            tail_block(b).wait()
            return 0

        lax.fori_loop(nused_ref[0], nblocks, tail_put, 0)
        lax.fori_loop(nused_ref[0], nblocks, tail_done, 0)

    parity = lax.rem(i, 2)
    half = parity * TOP_K
    li = lax.broadcasted_iota(I32, h2_ref.shape, 0)
    tok = i * tm + (li >> 3)
    is_id = (li & (SUBLANES - 1)) >= PACK_LINES
    data = h2_ref[...]
    for k in range(TOP_K):
        src_ref[half + k] = jnp.where(is_id, (k * n_tokens + tok).astype(U32), data)

    def issue(r, _):
        for k in range(TOP_K):
            pltpu.make_async_copy(_tile(src_ref.at[half + k], r),
                                  _tile(buf_ref, dest_ref[k * (tm + 1) + r]),
                                  sem.at[parity]).start(priority=k % 2)
        return 0

    lax.fori_loop(0, tm, issue, 0)

    def wait_step(which):
        for k in range(TOP_K):
            pltpu.make_async_copy(src_ref.at[k], buf_ref.at[pl.ds(0, tm * SUBLANES)],
                                  sem.at[which]).wait()

    pl.when(i >= 1)(functools.partial(wait_step, 1 - parity))
    pl.when(i == pl.num_programs(0) - 1)(functools.partial(wait_step, parity))


def _dispatch(dest, pad_start, pad_len, nused, h2_rows, n_rows, blk):
    T = h2_rows.shape[0] // SUBLANES
    tm = min(DMA_TILE, T)
    nt = T // tm
    words = pl.cdiv(TOP_K * (tm + 1), SMEM_1D_TILE) * SMEM_1D_TILE
    d = jnp.pad(dest.reshape(TOP_K, nt, tm).transpose(1, 0, 2), ((0, 0), (0, 0), (0, 1)))
    d = jnp.pad(d.reshape(nt, TOP_K * (tm + 1)), ((0, 0), (0, words - TOP_K * (tm + 1))))
    dest_flat = d.reshape(nt * words)
    smem = pl.BlockSpec(memory_space=pltpu.SMEM)
    return pl.pallas_call(
        functools.partial(_dispatch_kernel, n_tokens=T),
        grid=(nt,),
        in_specs=[
            pl.BlockSpec((words,), lambda i: (i,), memory_space=pltpu.SMEM),
            smem, smem, smem,
            pl.BlockSpec((tm * SUBLANES, LANES), lambda i: (i, 0)),
        ],
        out_specs=pl.BlockSpec(memory_space=pl.ANY),
        out_shape=jax.ShapeDtypeStruct((n_rows * SUBLANES, LANES), U32),
        scratch_shapes=[pltpu.VMEM((2 * TOP_K, tm * SUBLANES, LANES), U32),
                        pltpu.VMEM((2 * blk * SUBLANES, LANES), U32),
                        pltpu.SemaphoreType.DMA((2,)), pltpu.SemaphoreType.DMA(())],
        compiler_params=_cparams("arbitrary"),
        name="dispatch",
    )(dest_flat, pad_start, pad_len, nused, h2_rows)


def _ffn_kernel(blk_exp_ref, nused_ref, x_ref, wgu_ref, bgu_ref, wd_ref, bd_ref, y_ref,
                wgu_bf_ref, wd_bf_ref, out_ref, idv_ref, ids_ref, ssem, isem, *, n_tokens):
    b = pl.program_id(0)
    nb = pl.num_programs(0)
    tm = x_ref.shape[0] // SUBLANES
    groups = tm // LANES
    F = wd_ref.shape[1]
    nused = nused_ref[0]
    used = b < nused
    slot = lax.rem(b, 2)
    prev = 1 - slot
    new_expert = (b == 0) | (blk_exp_ref[b] != blk_exp_ref[jnp.maximum(b - 1, 0)])

    def send_row(s, r_hi, r_lo, row, priority):
        rid = ids_ref[s * SUBLANES + r_hi, r_lo]
        pltpu.make_async_copy(_tile(out_ref, s * tm + row), _tile(y_ref, rid),
                              ssem.at[s]).start(priority=priority)

    def send_rows_unrolled(s, lo, hi):
        for r in range(lo, hi):
            send_row(s, r // LANES, r % LANES, r, r % 2)

    def send_block_rolled(s):
        def body(g, _):
            def inner(l, _):
                send_row(s, g, l, g * LANES + l, 0)
                return 0
            return lax.fori_loop(0, LANES, inner, 0)
        lax.fori_loop(0, groups, body, 0)

    def wait_block(s):
        rows = tm * SUBLANES
        pltpu.make_async_copy(out_ref.at[pl.ds(0, rows)], y_ref.at[pl.ds(0, rows)],
                              ssem.at[s]).wait()

    def id_copy(s):
        return pltpu.make_async_copy(
            idv_ref, ids_ref.at[pl.ds(pl.multiple_of(s * SUBLANES, SUBLANES), SUBLANES)], isem)

    @pl.when(b == 0)
    def _():
        out_ref[...] = jnp.zeros_like(out_ref)
        rows = tm * SUBLANES
        pltpu.make_async_copy(out_ref.at[pl.ds(0, rows)],
                              y_ref.at[pl.ds(TOP_K * n_tokens * SUBLANES, rows)],
                              ssem.at[0]).start()
        pos = (lax.broadcasted_iota(I32, idv_ref.shape, 0) * LANES
               + lax.broadcasted_iota(I32, idv_ref.shape, 1))
        idv_ref[...] = TOP_K * n_tokens + tm + jnp.minimum(pos, tm - 1)
        id_copy(1).start()
        id_copy(1).wait()

    @pl.when((b >= 1) & (b <= nused))
    def _():
        id_copy(prev).wait()

    @pl.when(used)
    def _():
        idl = x_ref[pl.ds(PACK_LINES, tm, stride=SUBLANES), :].astype(I32)
        rr = lax.broadcasted_iota(I32, idl.shape, 0)
        ll = lax.broadcasted_iota(I32, idl.shape, 1)
        diag = jnp.where((rr & (LANES - 1)) == ll, idl, 0)
        idv_ref[pl.ds(0, groups), :] = jnp.sum(diag.reshape(groups, LANES, LANES), axis=1)
        id_copy(slot).start()

    @pl.when(used & new_expert)
    def _():
        wgu_bf_ref[...] = wgu_ref[0].astype(BF16)
        wd_bf_ref[...] = wd_ref[0].astype(BF16)

    def block(slot, prev):
        words = [x_ref[pl.ds(j, tm, stride=SUBLANES), :] for j in range(PACK_LINES)]
        hi = [lax.bitcast_convert_type(w & jnp.uint32(0xFFFF0000), F32) for w in words]
        lo = [lax.bitcast_convert_type(w << 16, F32) for w in words]
        x = jnp.concatenate(hi + lo, axis=1).astype(BF16)
        acts = []
        for c in range(FFN_SLICES):
            send_rows_unrolled(prev, c * tm // FFN_SLICES, (c + 1) * tm // FFN_SLICES)
            w = F // FFN_SLICES
            g = jnp.dot(x, wgu_bf_ref[:, c * w:(c + 1) * w], preferred_element_type=F32)
            u = jnp.dot(x, wgu_bf_ref[:, F + c * w:F + (c + 1) * w],
                        preferred_element_type=F32)
            g = jnp.minimum(g + bgu_ref[0, :, c * w:(c + 1) * w], SWIGLU_LIMIT)
            u = jnp.clip(u + bgu_ref[0, :, F + c * w:F + (c + 1) * w],
                         -SWIGLU_LIMIT, SWIGLU_LIMIT)
            acts.append((g * jax.nn.sigmoid(SWIGLU_ALPHA * g) * (u + 1.0)).astype(BF16))
        act = jnp.concatenate(acts, axis=1)
        y = jnp.dot(act, wd_bf_ref[...], preferred_element_type=F32) + bd_ref[0]
        wait_block(slot)
        base = slot * (tm * SUBLANES)
        for j in range(SUBLANES):
            out_ref[pl.ds(base + j, tm, stride=SUBLANES), :] = y[:, j * LANES:(j + 1) * LANES]

    for parity in range(2):
        pl.when(used & (slot == parity))(functools.partial(block, parity, 1 - parity))

    @pl.when(b == nused)
    def _():
        wait_block(slot)
        send_block_rolled(prev)
        wait_block(prev)

    @pl.when(used & (b == nb - 1))
    def _():
        wait_block(prev)
        id_copy(slot).wait()
        send_block_rolled(slot)
        wait_block(slot)


def _ffn(blk_exp, nused, buf, wgu, bgu, wd, bd, tm, n_tokens):
    nb = buf.shape[0] // (tm * SUBLANES)
    D = wgu.shape[1]
    F = wd.shape[1]

    def xmap(b, be, nu):
        return (jnp.minimum(b, nu[0] - 1), 0)

    def wmap(b, be, nu):
        return (be[jnp.minimum(b, nu[0] - 1)], 0, 0)

    grid_spec = pltpu.PrefetchScalarGridSpec(
        num_scalar_prefetch=2,
        grid=(nb,),
        in_specs=[
            pl.BlockSpec((tm * SUBLANES, LANES), xmap),
            pl.BlockSpec((1, D, 2 * F), wmap),
            pl.BlockSpec((1, 1, 2 * F), wmap),
            pl.BlockSpec((1, F, D), wmap),
            pl.BlockSpec((1, 1, D), wmap),
        ],
        out_specs=pl.BlockSpec(memory_space=pl.ANY),
        scratch_shapes=[
            pltpu.VMEM((D, 2 * F), BF16), pltpu.VMEM((F, D), BF16),
            pltpu.VMEM((2 * tm * SUBLANES, LANES), F32),
            pltpu.VMEM((SUBLANES, LANES), I32),
            pltpu.SMEM((2 * SUBLANES, LANES), I32),
            pltpu.SemaphoreType.DMA((2,)), pltpu.SemaphoreType.DMA(()),
        ],
    )
    return pl.pallas_call(
        functools.partial(_ffn_kernel, n_tokens=n_tokens),
        grid_spec=grid_spec,
        out_shape=jax.ShapeDtypeStruct(((TOP_K * n_tokens + 2 * tm) * SUBLANES, LANES), F32),
        compiler_params=_cparams("arbitrary"),
        name="ffn",
    )(blk_exp, nused, buf, wgu, bgu, wd, bd)


def _combine_kernel(gate_ref, x2_ref, fg_ref, *refs):
    y_refs, o_ref = refs[:TOP_K], refs[TOP_K]
    tm, D = x2_ref.shape
    nj = D // LANES
    gate = gate_ref[...]
    cols = []
    for j in range(nj):
        acc = None
        for k in range(TOP_K):
            part = gate[:, k:k + 1] * y_refs[k][pl.ds(j, tm, stride=nj), :]
            acc = part if acc is None else acc + part
        cols.append(acc)
    y = jnp.concatenate(cols, axis=1)
    o_ref[...] = _rms(x2_ref[...] + y, fg_ref[...])


def _combine(gate4, x2, fg, y_rows):
    T, D = x2.shape
    tm = min(ROW_TILE, T)
    nj = D // LANES
    nt = T // tm

    def slot_spec(k):
        return pl.BlockSpec((tm * nj, LANES), lambda i: (k * nt + i, 0))

    return pl.pallas_call(
        _combine_kernel,
        grid=(nt,),
        in_specs=[
            pl.BlockSpec((tm, LANES), lambda i: (i, 0)),
            pl.BlockSpec((tm, D), lambda i: (i, 0)),
            pl.BlockSpec((1, D), lambda i: (0, 0)),
        ] + [slot_spec(k) for k in range(TOP_K)],
        out_specs=pl.BlockSpec((tm, D), lambda i: (i, 0)),
        out_shape=jax.ShapeDtypeStruct((T, D), F32),
        compiler_params=_cparams("parallel"),
        name="combine",
    )(gate4, x2, fg, *([y_rows] * TOP_K))


def _rope_tables(S):
    rows = S // GRID_W
    row = jnp.repeat(jnp.arange(rows, dtype=I32), GRID_W).astype(F32)
    col = jnp.tile(jnp.arange(GRID_W, dtype=I32), rows).astype(F32)
    inv = ROPE_THETA ** (-jnp.arange(0, AXIS_DIM, 2, dtype=F32) / AXIS_DIM)
    ang_r = row[:, None] * inv[None, :]
    ang_c = col[:, None] * inv[None, :]
    cos = jnp.concatenate([jnp.cos(ang_r)] * 2 + [jnp.cos(ang_c)] * 2, axis=1)
    sin = jnp.concatenate([-jnp.sin(ang_r), jnp.sin(ang_r), -jnp.sin(ang_c), jnp.sin(ang_c)], axis=1)
    return jnp.tile(cos, (1, 2)), jnp.tile(sin, (1, 2))


def _block_diag_gates(wa, ba, wi, bi, ct):
    nb, bw = wa.shape[1], wa.shape[2]
    per = ct // bw
    nct = nb // per
    eye = jnp.eye(per, dtype=wa.dtype)

    def tiles(w):
        w = w.reshape(nct, per, bw, bw)
        return jnp.einsum('cpij,pq->cpiqj', w, eye).reshape(nct, ct, ct)

    wg = jnp.concatenate([tiles(wa[0]), tiles(wi[0]), tiles(wa[1]), tiles(wi[1])], axis=2)
    bias = lambda b: b.reshape(nct, 1, ct)
    bg = jnp.concatenate([bias(ba[0]), bias(bi[0]), bias(ba[1]), bias(bi[1])], axis=2)
    return (0.5 * wg).astype(BF16), 0.5 * bg


def _layer(x2d, B, S, norm1_g, w_in, b_in, q_norm_g, k_norm_g, conv_w, conv_b, lru_wa, lru_ba,
           lru_wi, lru_bi, lru_lambda, w_attn_o, w_lru_o, w_out, norm2_g, w_router, b_router,
           w_gu, b_gu, w_down, b_down, out_g):
    T, D = x2d.shape
    C = conv_w.shape[1]
    nqkv = Q_WIDTH + 2 * KV_WIDTH
    row2 = lambda v: v.reshape(1, -1)

    cos_t, sin_t = _rope_tables(S)
    head = jnp.arange(LANES) // HEAD_DIM
    hsum = (head[:, None] == head[None, :]).astype(BF16)
    q, k, v = _qkv(x2d, row2(norm1_g), w_in[:, :nqkv].astype(BF16), row2(b_in[:nqkv]),
                   cos_t, sin_t, row2(jnp.tile(q_norm_g, 2)), row2(jnp.tile(k_norm_g, 2)),
                   hsum, B, S)
    proj = _proj(x2d, row2(norm1_g), w_in[:, nqkv:].astype(BF16), row2(b_in[nqkv:]))
    attn = _attn(q, k, v)
    wg, bg = _block_diag_gates(lru_wa, lru_ba, lru_wi, lru_bi, LRU_CT)
    rec = _lru(proj, conv_w, row2(conv_b), wg, bg, lru_lambda, B, S, C)

    pad = LANES - N_EXPERTS
    rw = jnp.pad(w_router, ((0, 0), (0, pad)))
    rw_hi = rw.astype(BF16)
    rw_lo = (rw - rw_hi.astype(F32)).astype(BF16)
    rb = jnp.pad(b_router, (0, pad), constant_values=NEG_BIG).reshape(1, LANES)
    x2, h2_rows, logits = _merge(x2d, attn, rec, proj, w_attn_o.astype(BF16),
                                 w_lru_o.astype(BF16), w_out.astype(BF16), row2(norm2_g),
                                 rw_hi, rw_lo, rb)

    gate4, _, eidx_t, rank_t, cnt = _route(logits)
    counts = cnt[:, 0].astype(I32)
    tm = FFN_TM
    nblk = (counts + tm - 1) // tm
    pend_blk = jnp.cumsum(nblk)
    pstart = (pend_blk - nblk) * tm
    A = T * TOP_K
    nb = (A + N_EXPERTS * (tm - 1) + tm - 1) // tm
    blk_exp = jnp.minimum(
        jnp.sum(pend_blk[None, :] <= jnp.arange(nb, dtype=I32)[:, None], axis=1),
        N_EXPERTS - 1).astype(I32)
    nused = pend_blk[-1:].astype(I32)
    first_row = jnp.sum(jnp.where(eidx_t[:TOP_K, :, None] == jnp.arange(N_EXPERTS), pstart, 0),
                        axis=-1)
    dest = (first_row + rank_t[:TOP_K]).astype(I32)
    pad_start = (pstart + counts).astype(I32)
    pad_len = (nblk * tm - counts).astype(I32)

    buf = _dispatch(dest, pad_start, pad_len, nused, h2_rows, nb * tm, tm)
    y_rows = _ffn(blk_exp, nused, buf, w_gu, b_gu.reshape(N_EXPERTS, 1, -1), w_down,
                  b_down.reshape(N_EXPERTS, 1, -1), tm, T)
    return _combine(gate4, x2, row2(out_g), y_rows)


def kernel(x, norm1_g, w_in, b_in, q_norm_g, k_norm_g, conv_w, conv_b, lru_wa, lru_ba, lru_wi,
           lru_bi, lru_lambda, w_attn_o, w_lru_o, w_out, norm2_g, w_router, b_router, w_gu, b_gu,
           w_down, b_down, final_g):
    B, S, D = x.shape
    depth = norm1_g.shape[0]
    assert depth == 1, "the fused final RMSNorm assumes a single layer"
    assert S % (TIME_CHUNKS * SUBLANES) == 0 and S % GRID_W == 0
    assert D == 2 * PACK_LINES * LANES, "a token's bf16 row must fill PACK_LINES word lines"
    out = _layer(x.reshape(B * S, D), B, S, norm1_g[0], w_in[0], b_in[0], q_norm_g[0],
                 k_norm_g[0], conv_w[0], conv_b[0], lru_wa[0], lru_ba[0], lru_wi[0], lru_bi[0],
                 lru_lambda[0], w_attn_o[0], w_lru_o[0], w_out[0], norm2_g[0], w_router[0],
                 b_router[0], w_gu[0], b_gu[0], w_down[0], b_down[0], final_g)
    return out.reshape(B, S, D)
```

```python
import functools
import math

import jax
import jax.numpy as jnp
from jax import lax
from jax.experimental import pallas as pl
from jax.experimental.pallas import tpu as pltpu

F32 = jnp.float32
BF16 = jnp.bfloat16
I32 = jnp.int32
U32 = jnp.uint32

LANES = 128
SUBLANES = 8
PACK_LINES = 4
SMEM_1D_TILE = 1024
VMEM_LIMIT_BYTES = 56 * 1024 * 1024

HEAD_DIM = 64
N_Q_HEADS = 8
N_KV_HEADS = 2
Q_GROUP = N_Q_HEADS // N_KV_HEADS
Q_WIDTH = N_Q_HEADS * HEAD_DIM
KV_WIDTH = N_KV_HEADS * HEAD_DIM
AXIS_DIM = HEAD_DIM // 2
ROT_HALF = AXIS_DIM // 2
ROPE_THETA = 10000.0
GRID_W = 64
LRU_BLOCKS = 16
RG_C = 8.0
N_EXPERTS = 32
TOP_K = 4
SWIGLU_LIMIT = 7.0
SWIGLU_ALPHA = 1.702
NORM_EPS = 1e-6
LOG2E = 1.4426950408889634
NEG_BIG = -1e30
TINY = 1e-30

ROW_TILE = 512
ATT_TQ = 512
ATT_TK = 512
LRU_CT = 256
LRU_ROWS = 256
TIME_CHUNKS = SUBLANES
FFN_TM = 512
FFN_SLICES = 4
DMA_TILE = 256


def _cparams(*sem):
    return pltpu.CompilerParams(dimension_semantics=sem, vmem_limit_bytes=VMEM_LIMIT_BYTES)


def _rms(x, g):
    return x * lax.rsqrt(jnp.mean(x * x, axis=-1, keepdims=True) + NORM_EPS) * g


def _qkv_kernel(x_ref, g1_ref, w_ref, b_ref, cos_ref, sin_ref, qg_ref, kg_ref, hsum_ref,
                q_ref, k_ref, v_ref):
    h = _rms(x_ref[...], g1_ref[...]).astype(BF16)
    p = jnp.dot(h, w_ref[...], preferred_element_type=F32) + b_ref[...]
    cos = cos_ref[...]
    sin = sin_ref[...]
    hsum = hsum_ref[...]
    lane = lax.broadcasted_iota(I32, cos.shape, 1)
    first_half = (lane % AXIS_DIM) < ROT_HALF

    def norm_rope(c, gain):
        sq = c * c
        hi = sq.astype(BF16)
        lo = (sq - hi.astype(F32)).astype(BF16)
        ms = (jnp.dot(hi, hsum, preferred_element_type=F32)
              + jnp.dot(lo, hsum, preferred_element_type=F32)) * (1.0 / HEAD_DIM)
        y = c * lax.rsqrt(ms + NORM_EPS) * gain
        partner = jnp.where(first_half, pltpu.roll(y, LANES - ROT_HALF, 1),
                            pltpu.roll(y, ROT_HALF, 1))
        return y * cos + partner * sin

    qg = qg_ref[...]
    for c in range(Q_WIDTH // LANES):
        y = norm_rope(p[:, c * LANES:(c + 1) * LANES], qg) * (HEAD_DIM ** -0.5 * LOG2E)
        yt = y.T.astype(BF16)
        q_ref[0, 2 * c] = yt[:HEAD_DIM]
        q_ref[0, 2 * c + 1] = yt[HEAD_DIM:]
    yk = norm_rope(p[:, Q_WIDTH:Q_WIDTH + KV_WIDTH], kg_ref[...])
    k_ref[0, 0] = yk[:, :HEAD_DIM].astype(BF16)
    k_ref[0, 1] = yk[:, HEAD_DIM:].astype(BF16)
    vt = p[:, Q_WIDTH + KV_WIDTH:].T
    tm = vt.shape[1]
    ones_row = (lax.broadcasted_iota(I32, (HEAD_DIM, tm), 0) == 0).astype(BF16)
    v_ref[0, 0] = jnp.concatenate([vt[:HEAD_DIM].astype(BF16), ones_row], axis=0)
    v_ref[0, 1] = jnp.concatenate([vt[HEAD_DIM:].astype(BF16), ones_row], axis=0)


def _qkv(x2d, g1, w_qkv, b_qkv, cos_t, sin_t, qg, kg, hsum, B, S):
    T, D = x2d.shape
    tm = min(ROW_TILE, S)
    ns = S // tm
    n = w_qkv.shape[1]
    full = lambda shape: pl.BlockSpec(shape, lambda i: (0,) * len(shape))
    return pl.pallas_call(
        _qkv_kernel,
        grid=(T // tm,),
        in_specs=[
            pl.BlockSpec((tm, D), lambda i: (i, 0)),
            full((1, D)), full((D, n)), full((1, n)),
            pl.BlockSpec((tm, LANES), lambda i: (i % ns, 0)),
            pl.BlockSpec((tm, LANES), lambda i: (i % ns, 0)),
            full((1, LANES)), full((1, LANES)), full((LANES, LANES)),
        ],
        out_specs=[
            pl.BlockSpec((1, N_Q_HEADS, HEAD_DIM, tm), lambda i: (i // ns, 0, 0, i % ns)),
            pl.BlockSpec((1, N_KV_HEADS, tm, HEAD_DIM), lambda i: (i // ns, 0, i % ns, 0)),
            pl.BlockSpec((1, N_KV_HEADS, LANES, tm), lambda i: (i // ns, 0, 0, i % ns)),
        ],
        out_shape=[
            jax.ShapeDtypeStruct((B, N_Q_HEADS, HEAD_DIM, S), BF16),
            jax.ShapeDtypeStruct((B, N_KV_HEADS, S, HEAD_DIM), BF16),
            jax.ShapeDtypeStruct((B, N_KV_HEADS, LANES, S), BF16),
        ],
        compiler_params=_cparams("parallel"),
        name="qkv",
    )(x2d, g1, w_qkv, b_qkv, cos_t, sin_t, qg, kg, hsum)


def _proj_kernel(x_ref, g_ref, w_ref, b_ref, o_ref):
    h = _rms(x_ref[...], g_ref[...]).astype(BF16)
    o_ref[...] = jnp.dot(h, w_ref[...], preferred_element_type=F32) + b_ref[...]


def _proj(x2d, g, w, b):
    T, D = x2d.shape
    n = w.shape[1]
    tm = min(ROW_TILE, T)
    return pl.pallas_call(
        _proj_kernel,
        grid=(T // tm,),
        in_specs=[
            pl.BlockSpec((tm, D), lambda i: (i, 0)),
            pl.BlockSpec((1, D), lambda i: (0, 0)),
            pl.BlockSpec((D, n), lambda i: (0, 0)),
            pl.BlockSpec((1, n), lambda i: (0, 0)),
        ],
        out_specs=pl.BlockSpec((tm, n), lambda i: (i, 0)),
        out_shape=jax.ShapeDtypeStruct((T, n), F32),
        compiler_params=_cparams("parallel"),
        name="proj",
    )(x2d, g, w, b)


def _attn_kernel(q_ref, k_ref, v_ref, o_ref, s0_ref, s1_ref, *, tk):
    tq = q_ref.shape[3]
    S = k_ref.shape[2]
    M = Q_GROUP * tq
    n = S // tk
    qT = jnp.concatenate([q_ref[0, h] for h in range(Q_GROUP)], axis=1)

    def scores(j, s_ref):
        off = pl.multiple_of(j * tk, tk)
        s_ref[...] = jnp.dot(k_ref[0, 0, pl.ds(off, tk), :], qT, preferred_element_type=F32)

    def absorb(j, s_ref, carry):
        m, acc = carry
        off = pl.multiple_of(j * tk, tk)
        vc = v_ref[0, 0, :, pl.ds(off, tk)]
        s = s_ref[...]
        m_new = jnp.maximum(m, jnp.max(s, axis=0, keepdims=True))
        alpha = jnp.exp2(m - m_new)
        p = jnp.exp2(s - m_new).astype(BF16)
        return m_new, alpha * acc + jnp.dot(vc, p, preferred_element_type=F32)

    def pair(i, carry):
        scores(2 * i + 1, s1_ref)
        carry = absorb(2 * i, s0_ref, carry)
        scores(2 * i + 2, s0_ref)
        return absorb(2 * i + 1, s1_ref, carry)

    scores(0, s0_ref)
    carry = (jnp.full((1, M), NEG_BIG, F32), jnp.zeros((LANES, M), F32))
    carry = lax.fori_loop(0, n // 2 - 1, pair, carry)
    scores(n - 1, s1_ref)
    carry = absorb(n - 2, s0_ref, carry)
    _, acc = absorb(n - 1, s1_ref, carry)
    outT = acc[:HEAD_DIM] / acc[HEAD_DIM:HEAD_DIM + 1]
    stacked = jnp.concatenate(
        [outT[:, g * tq:(g + 1) * tq] for g in range(Q_GROUP)], axis=0)
    o_ref[...] = stacked.T.astype(BF16)


def _attn(qT, k, vT):
    B, _, _, S = qT.shape
    tq = min(ATT_TQ, S)
    tk = min(ATT_TK, S)
    nq = S // tq
    assert (S // tk) % 2 == 0, "key chunks are processed in pairs"
    score_buf = pltpu.VMEM((tk, Q_GROUP * tq), F32)
    return pl.pallas_call(
        functools.partial(_attn_kernel, tk=tk),
        grid=(B, N_KV_HEADS, nq),
        in_specs=[
            pl.BlockSpec((1, Q_GROUP, HEAD_DIM, tq), lambda b, g, i: (b, g, 0, i)),
            pl.BlockSpec((1, 1, S, HEAD_DIM), lambda b, g, i: (b, g, 0, 0)),
            pl.BlockSpec((1, 1, LANES, S), lambda b, g, i: (b, g, 0, 0)),
        ],
        out_specs=pl.BlockSpec((tq, Q_GROUP * HEAD_DIM), lambda b, g, i: (b * nq + i, g)),
        out_shape=jax.ShapeDtypeStruct((B * S, Q_WIDTH), BF16),
        scratch_shapes=[score_buf, score_buf],
        compiler_params=_cparams("parallel", "parallel", "parallel"),
        name="attn",
    )(qT, k, vT)


def _shift_chunks(v, down):
    row = lax.broadcasted_iota(I32, v.shape, 0)
    if down:
        return jnp.where(row == 0, 0.0, pltpu.roll(v, 1, 0))
    return jnp.where(row == SUBLANES - 1, 0.0, pltpu.roll(v, SUBLANES - 1, 0))


def _lru_kernel(xr_ref, yr_ref, cw_ref, cb_ref, wg_ref, bg_ref, lam_ref, o_ref,
                xe_ref, af_ref, uf_ref, ab_ref, ub_ref, *, rows):
    S, ct = xr_ref.shape
    lc = S // TIME_CHUNKS
    halo = SUBLANES

    ng = ct // LANES

    def xe_rows(r0, n):
        return jnp.concatenate([xe_ref[g, pl.ds(r0, n), :] for g in range(ng)], axis=1)

    def xe_store(r0, n, val):
        for g in range(ng):
            xe_ref[g, pl.ds(r0, n), :] = val[:, g * LANES:(g + 1) * LANES]

    for s in range(TIME_CHUNKS):
        for g in range(ng):
            xe_ref[g, pl.ds(halo + s, lc, stride=SUBLANES), :] = (
                xr_ref[pl.ds(s * lc, lc), g * LANES:(g + 1) * LANES])
    xe_store(0, halo, _shift_chunks(xe_rows(S, halo), True))
    first = xe_rows(halo, halo)
    second = xe_rows(2 * halo, halo)
    xe_store(S + halo, halo, _shift_chunks(first, False))
    xe_store(S + 2 * halo, halo, _shift_chunks(second, False))

    cw = cw_ref[...]
    cb = cb_ref[...]
    lam = lam_ref[...]
    log_sig = jnp.minimum(lam, 0.0) - jnp.log(1.0 + jnp.exp(-jnp.abs(lam)))
    c_half = (0.5 * RG_C * LOG2E) * log_sig
    wg = wg_ref[0]
    bg = bg_ref[0]

    def gate_chunk(i, _):
        r0 = pl.multiple_of(i * rows, rows)
        xc = cb
        for j in range(4):
            xc = xc + cw[j:j + 1, :] * xe_rows(r0 + j * halo, rows)
        t = jnp.tanh(jnp.dot(xc.astype(BF16), wg, preferred_element_type=F32) + bg)
        x_half = 0.5 * xc
        for d, (a_ref, u_ref) in enumerate(((af_ref, uf_ref), (ab_ref, ub_ref))):
            t_r = t[:, (2 * d) * ct:(2 * d + 1) * ct]
            t_i = t[:, (2 * d + 1) * ct:(2 * d + 2) * ct]
            ch = c_half[d:d + 1, :]
            a = jnp.exp2(t_r * ch + ch)
            a_ref[pl.ds(r0, rows), :] = a
            v = 1.0 - a * a
            root = jnp.maximum(v, 0.0) * lax.rsqrt(jnp.maximum(v, TINY))
            u_ref[pl.ds(r0, rows), :] = root * ((t_i + 1.0) * x_half)
        return 0

    lax.fori_loop(0, S // rows, gate_chunk, 0)

    unroll = 8

    def scan_body(i, carry):
        hf, pf, hb, pb = carry
        for k in range(unroll):
            tf = i * unroll + k
            rf = pl.multiple_of(tf * SUBLANES, SUBLANES)
            rb = pl.multiple_of((lc - 1 - tf) * SUBLANES, SUBLANES)
            a = af_ref[pl.ds(rf, SUBLANES), :]
            hf = a * hf + uf_ref[pl.ds(rf, SUBLANES), :]
            pf = a * pf
            uf_ref[pl.ds(rf, SUBLANES), :] = hf
            af_ref[pl.ds(rf, SUBLANES), :] = pf
            a = ab_ref[pl.ds(rb, SUBLANES), :]
            hb = a * hb + ub_ref[pl.ds(rb, SUBLANES), :]
            pb = a * pb
            ub_ref[pl.ds(rb, SUBLANES), :] = hb
            ab_ref[pl.ds(rb, SUBLANES), :] = pb
        return hf, pf, hb, pb

    zero = jnp.zeros((SUBLANES, ct), F32)
    one = jnp.ones((SUBLANES, ct), F32)
    hf_end, pf_end, hb_end, pb_end = lax.fori_loop(0, lc // unroll, scan_body,
                                                   (zero, one, zero, one))

    def chunk_carry(h_end, p_end, down):
        f = h_end
        for _ in range(TIME_CHUNKS - 1):
            f = h_end + p_end * _shift_chunks(f, down)
        return _shift_chunks(f, down)

    cin_f = chunk_carry(hf_end, pf_end, True)
    cin_b = chunk_carry(hb_end, pb_end, False)

    def fix_chunk(i, _):
        r0 = pl.multiple_of(i * rows, rows)
        reps = rows // SUBLANES
        hf = uf_ref[pl.ds(r0, rows), :] + af_ref[pl.ds(r0, rows), :] * jnp.tile(cin_f, (reps, 1))
        hb = ub_ref[pl.ds(r0, rows), :] + ab_ref[pl.ds(r0, rows), :] * jnp.tile(cin_b, (reps, 1))
        xe_store(r0, rows, hf + hb)
        return 0

    lax.fori_loop(0, S // rows, fix_chunk, 0)

    for s in range(TIME_CHUNKS):
        hsum = jnp.concatenate(
            [xe_ref[g, pl.ds(s, lc, stride=SUBLANES), :] for g in range(ng)], axis=1)
        y = yr_ref[pl.ds(s * lc, lc), :]
        o_ref[pl.ds(s * lc, lc), :] = (hsum * jax.nn.gelu(y, approximate=True)).astype(BF16)


def _lru(proj, conv_w, conv_b, wg, bg, lam, B, S, C):
    ct = LRU_CT
    nct = C // ct
    rows = min(LRU_ROWS, S)
    return pl.pallas_call(
        functools.partial(_lru_kernel, rows=rows),
        grid=(B, nct),
        in_specs=[
            pl.BlockSpec((S, ct), lambda b, c: (b, c)),
            pl.BlockSpec((S, ct), lambda b, c: (b, nct + c)),
            pl.BlockSpec((4, ct), lambda b, c: (0, c)),
            pl.BlockSpec((1, ct), lambda b, c: (0, c)),
            pl.BlockSpec((1, ct, 4 * ct), lambda b, c: (c, 0, 0)),
            pl.BlockSpec((1, 1, 4 * ct), lambda b, c: (c, 0, 0)),
            pl.BlockSpec((2, ct), lambda b, c: (0, c)),
        ],
        out_specs=pl.BlockSpec((S, ct), lambda b, c: (b, c)),
        out_shape=jax.ShapeDtypeStruct((B * S, C), BF16),
        scratch_shapes=[
            pltpu.VMEM((ct // LANES, S + 3 * SUBLANES, LANES), F32),
            pltpu.VMEM((S, ct), F32), pltpu.VMEM((S, ct), F32),
            pltpu.VMEM((S, ct), F32), pltpu.VMEM((S, ct), F32),
        ],
        compiler_params=_cparams("parallel", "parallel"),
        name="lru",
    )(proj, proj, conv_w, conv_b, wg, bg, lam)


def _merge_kernel(x_ref, attn_ref, rec_ref, gl_ref, wa_ref, wr_ref, wo_ref, g2_ref,
                  rw_hi_ref, rw_lo_ref, rb_ref, x2_ref, h2_ref, lg_ref):
    D = x_ref.shape[1]
    gl = gl_ref[...]
    ma = jnp.dot(attn_ref[...], wa_ref[...], preferred_element_type=F32)
    mr = jnp.dot(rec_ref[...], wr_ref[...], preferred_element_type=F32)
    merged = jax.nn.sigmoid(gl[:, :D]) * ma + jax.nn.sigmoid(gl[:, D:]) * mr
    x2 = x_ref[...] + jnp.dot(merged.astype(BF16), wo_ref[...], preferred_element_type=F32)
    x2_ref[...] = x2
    h2 = _rms(x2, g2_ref[...])
    hi = h2.astype(BF16)
    lo = (h2 - hi.astype(F32)).astype(BF16)
    lg_ref[...] = (jnp.dot(hi, rw_hi_ref[...], preferred_element_type=F32)
                   + jnp.dot(lo, rw_hi_ref[...], preferred_element_type=F32)
                   + jnp.dot(hi, rw_lo_ref[...], preferred_element_type=F32)) + rb_ref[...]
    tm = h2.shape[0]
    bits = lax.bitcast_convert_type(hi.astype(F32), U32)
    words = bits[:, :D // 2] | (bits[:, D // 2:] >> 16)
    for j in range(PACK_LINES):
        h2_ref[pl.ds(j, tm, stride=SUBLANES), :] = words[:, j * LANES:(j + 1) * LANES]
    for j in range(PACK_LINES, SUBLANES):
        h2_ref[pl.ds(j, tm, stride=SUBLANES), :] = jnp.zeros((tm, LANES), U32)


def _merge(x2d, attn, rec, proj, wa, wr, wo, g2, rw_hi, rw_lo, rb):
    T, D = x2d.shape
    tm = min(ROW_TILE, T)
    nj = D // LANES
    full = lambda a: pl.BlockSpec(a.shape, lambda i: (0,) * a.ndim)
    return pl.pallas_call(
        _merge_kernel,
        grid=(T // tm,),
        in_specs=[
            pl.BlockSpec((tm, D), lambda i: (i, 0)),
            pl.BlockSpec((tm, Q_WIDTH), lambda i: (i, 0)),
            pl.BlockSpec((tm, D), lambda i: (i, 0)),
            pl.BlockSpec((tm, 2 * D), lambda i: (i, 1)),
            full(wa), full(wr), full(wo), full(g2), full(rw_hi), full(rw_lo), full(rb),
        ],
        out_specs=[
            pl.BlockSpec((tm, D), lambda i: (i, 0)),
            pl.BlockSpec((tm * nj, LANES), lambda i: (i, 0)),
            pl.BlockSpec((tm, LANES), lambda i: (i, 0)),
        ],
        out_shape=[
            jax.ShapeDtypeStruct((T, D), F32),
            jax.ShapeDtypeStruct((T * nj, LANES), U32),
            jax.ShapeDtypeStruct((T, LANES), F32),
        ],
        compiler_params=_cparams("parallel"),
        name="merge",
    )(x2d, attn, rec, proj, wa, wr, wo, g2, rw_hi, rw_lo, rb)


def _route_kernel(lg_ref, gate_ref, gate_t_ref, eidx_t_ref, rank_t_ref, cnt_ref):
    i = pl.program_id(0)

    @pl.when(i == 0)
    def _():
        cnt_ref[...] = jnp.zeros_like(cnt_ref)

    tm = lg_ref.shape[0]
    work = lg_ref[...].T[:N_EXPERTS]
    expert = lax.broadcasted_iota(I32, work.shape, 0)
    sels, vals, idxs = [], [], []
    for _ in range(TOP_K):
        mx = jnp.max(work, axis=0, keepdims=True)
        idx = jnp.min(jnp.where(work == mx, expert, N_EXPERTS), axis=0, keepdims=True)
        sel = expert == idx
        sels.append(sel)
        vals.append(mx)
        idxs.append(idx)
        work = jnp.where(sel, NEG_BIG, work)
    ex = [jnp.exp(v - vals[0]) for v in vals]
    den = ex[0] + ex[1] + ex[2] + ex[3]
    member = (sels[0] | sels[1] | sels[2] | sels[3]).astype(BF16)
    r = lax.broadcasted_iota(I32, (tm, tm), 0)
    c = lax.broadcasted_iota(I32, (tm, tm), 1)
    tri = (r < c).astype(BF16)
    base = cnt_ref[:, 0:1]
    before = jnp.dot(member, tri, preferred_element_type=F32) + base
    ranks = [jnp.sum(jnp.where(sels[k], before, 0.0), axis=0, keepdims=True)
             for k in range(TOP_K)]
    gates = [ex[k] / den for k in range(TOP_K)]
    pad = SUBLANES - TOP_K
    gate_t = jnp.concatenate(gates + [jnp.zeros((pad, tm), F32)], axis=0)
    gate_t_ref[...] = gate_t
    eidx_t_ref[...] = jnp.concatenate(idxs + [jnp.zeros((pad, tm), I32)], axis=0)
    rank_t_ref[...] = jnp.concatenate(ranks + [jnp.zeros((pad, tm), F32)], axis=0).astype(I32)
    gate_ref[...] = jnp.concatenate(
        [gate_t, jnp.zeros((LANES - SUBLANES, tm), F32)], axis=0).T
    cnt_ref[...] = jnp.broadcast_to(
        base + jnp.sum(member.astype(F32), axis=1, keepdims=True), cnt_ref.shape)


def _route(logits):
    T = logits.shape[0]
    tm = min(ROW_TILE, T)
    spec = pl.BlockSpec((tm, LANES), lambda i: (i, 0))
    spec_t = pl.BlockSpec((SUBLANES, tm), lambda i: (0, i))
    return pl.pallas_call(
        _route_kernel,
        grid=(T // tm,),
        in_specs=[spec],
        out_specs=[spec, spec_t, spec_t, spec_t,
                   pl.BlockSpec((N_EXPERTS, LANES), lambda i: (0, 0))],
        out_shape=[
            jax.ShapeDtypeStruct((T, LANES), F32),
            jax.ShapeDtypeStruct((SUBLANES, T), F32),
            jax.ShapeDtypeStruct((SUBLANES, T), I32),
            jax.ShapeDtypeStruct((SUBLANES, T), I32),
            jax.ShapeDtypeStruct((N_EXPERTS, LANES), F32),
        ],
        compiler_params=_cparams("arbitrary"),
        name="route",
    )(logits)


def _tile(ref, row):
    return ref.at[pl.ds(pl.multiple_of(row * SUBLANES, SUBLANES), SUBLANES)]


def _dispatch_kernel(dest_ref, last_blk_ref, nused_ref, h2_ref, buf_ref,
                     src_ref, pad_ref, sem, zsem, *, n_tokens):
    i = pl.program_id(0)
    tm = h2_ref.shape[0] // SUBLANES
    blk = pad_ref.shape[0] // (2 * SUBLANES)
    nblocks = buf_ref.shape[0] // (blk * SUBLANES)

    @pl.when(i == 0)
    def _():
        li = lax.broadcasted_iota(I32, pad_ref.shape, 0)
        trash = (TOP_K * n_tokens + (li >> 3)).astype(U32)
        pad_ref[...] = jnp.where((li & (SUBLANES - 1)) >= PACK_LINES, trash, jnp.uint32(0))
        rows = blk * SUBLANES

        def pad_block(b):
            half = pl.multiple_of(lax.rem(b, 2) * rows, rows)
            return pltpu.make_async_copy(
                pad_ref.at[pl.ds(half, rows)],
                buf_ref.at[pl.ds(pl.multiple_of(b * rows, rows), rows)], zsem)

        def expert_put(e, _):
            pl.when(last_blk_ref[e] >= 0)(lambda: pad_block(last_blk_ref[e]).start())
            return 0

        def expert_done(e, _):
            pl.when(last_blk_ref[e] >= 0)(lambda: pad_block(last_blk_ref[e]).wait())
            return 0

        def tail_put(b, _):
            pad_block(b).start()
            return 0

        def tail_done(b, _):
            pad_block(b).wait()
            return 0

        lax.fori_loop(0, N_EXPERTS, expert_put, 0)
        lax.fori_loop(nused_ref[0], nblocks, tail_put, 0)
        lax.fori_loop(0, N_EXPERTS, expert_done, 0)
        lax.fori_loop(nused_ref[0], nblocks, tail_done, 0)

    parity = lax.rem(i, 2)
    half = parity * TOP_K
    li = lax.broadcasted_iota(I32, h2_ref.shape, 0)
    tok = i * tm + (li >> 3)
    is_id = (li & (SUBLANES - 1)) >= PACK_LINES
    data = h2_ref[...]
    for k in range(TOP_K):
        src_ref[half + k] = jnp.where(is_id, (k * n_tokens + tok).astype(U32), data)

    def issue(r, _):
        for k in range(TOP_K):
            pltpu.make_async_copy(_tile(src_ref.at[half + k], r),
                                  _tile(buf_ref, dest_ref[k * (tm + 1) + r]),
                                  sem.at[parity]).start(priority=k % 2)
        return 0

    lax.fori_loop(0, tm, issue, 0)

    def wait_step(which):
        for k in range(TOP_K):
            pltpu.make_async_copy(src_ref.at[k], buf_ref.at[pl.ds(0, tm * SUBLANES)],
                                  sem.at[which]).wait()

    pl.when(i >= 1)(functools.partial(wait_step, 1 - parity))
    pl.when(i == pl.num_programs(0) - 1)(functools.partial(wait_step, parity))


def _dispatch(dest, last_blk, nused, h2_rows, n_rows, blk):
    T = h2_rows.shape[0] // SUBLANES
    tm = min(DMA_TILE, T)
    nt = T // tm
    words = pl.cdiv(TOP_K * (tm + 1), SMEM_1D_TILE) * SMEM_1D_TILE
    d = jnp.pad(dest.reshape(TOP_K, nt, tm).transpose(1, 0, 2), ((0, 0), (0, 0), (0, 1)))
    d = jnp.pad(d.reshape(nt, TOP_K * (tm + 1)), ((0, 0), (0, words - TOP_K * (tm + 1))))
    dest_flat = d.reshape(nt * words)
    smem = pl.BlockSpec(memory_space=pltpu.SMEM)
    return pl.pallas_call(
        functools.partial(_dispatch_kernel, n_tokens=T),
        grid=(nt,),
        in_specs=[
            pl.BlockSpec((words,), lambda i: (i,), memory_space=pltpu.SMEM),
            smem, smem,
            pl.BlockSpec((tm * SUBLANES, LANES), lambda i: (i, 0)),
        ],
        out_specs=pl.BlockSpec(memory_space=pl.ANY),
        out_shape=jax.ShapeDtypeStruct((n_rows * SUBLANES, LANES), U32),
        scratch_shapes=[pltpu.VMEM((2 * TOP_K, tm * SUBLANES, LANES), U32),
                        pltpu.VMEM((2 * blk * SUBLANES, LANES), U32),
                        pltpu.SemaphoreType.DMA((2,)), pltpu.SemaphoreType.DMA(())],
        compiler_params=_cparams("arbitrary"),
        name="dispatch",
    )(dest_flat, last_blk, nused, h2_rows)


def _ffn_kernel(blk_exp_ref, nused_ref, x_ref, wgu_ref, bgu_ref, wd_ref, bd_ref, y_ref,
                wgu_bf_ref, wd_bf_ref, out_ref, idv_ref, ids_ref, ssem, isem, *, n_tokens):
    b = pl.program_id(0)
    nb = pl.num_programs(0)
    tm = x_ref.shape[0] // SUBLANES
    groups = tm // LANES
    F = wd_ref.shape[1]
    nused = nused_ref[0]
    used = b < nused
    slot = lax.rem(b, 2)
    prev = 1 - slot
    new_expert = (b == 0) | (blk_exp_ref[b] != blk_exp_ref[jnp.maximum(b - 1, 0)])

    def send_row(s, r_hi, r_lo, row, priority):
        rid = ids_ref[s * SUBLANES + r_hi, r_lo]
        pltpu.make_async_copy(_tile(out_ref, s * tm + row), _tile(y_ref, rid),
                              ssem.at[s]).start(priority=priority)

    def send_rows_unrolled(s, lo, hi):
        for r in range(lo, hi):
            send_row(s, r // LANES, r % LANES, r, r % 2)

    def send_block_rolled(s):
        def body(g, _):
            def inner(l, _):
                send_row(s, g, l, g * LANES + l, 0)
                return 0
            return lax.fori_loop(0, LANES, inner, 0)
        lax.fori_loop(0, groups, body, 0)

    def wait_block(s):
        rows = tm * SUBLANES
        pltpu.make_async_copy(out_ref.at[pl.ds(0, rows)], y_ref.at[pl.ds(0, rows)],
                              ssem.at[s]).wait()

    def id_copy(s):
        return pltpu.make_async_copy(
            idv_ref, ids_ref.at[pl.ds(pl.multiple_of(s * SUBLANES, SUBLANES), SUBLANES)], isem)

    @pl.when(b == 0)
    def _():
        out_ref[...] = jnp.zeros_like(out_ref)
        rows = tm * SUBLANES
        pltpu.make_async_copy(out_ref.at[pl.ds(0, rows)],
                              y_ref.at[pl.ds(TOP_K * n_tokens * SUBLANES, rows)],
                              ssem.at[0]).start()
        pos = (lax.broadcasted_iota(I32, idv_ref.shape, 0) * LANES
               + lax.broadcasted_iota(I32, idv_ref.shape, 1))
        idv_ref[...] = TOP_K * n_tokens + tm + jnp.minimum(pos, tm - 1)
        id_copy(1).start()
        id_copy(1).wait()

    @pl.when((b >= 1) & (b <= nused))
    def _():
        id_copy(prev).wait()

    @pl.when(used)
    def _():
        idl = x_ref[pl.ds(PACK_LINES, tm, stride=SUBLANES), :].astype(I32)
        rr = lax.broadcasted_iota(I32, idl.shape, 0)
        ll = lax.broadcasted_iota(I32, idl.shape, 1)
        diag = jnp.where((rr & (LANES - 1)) == ll, idl, 0)
        idv_ref[pl.ds(0, groups), :] = jnp.sum(diag.reshape(groups, LANES, LANES), axis=1)
        id_copy(slot).start()

    @pl.when(used & new_expert)
    def _():
        wgu_bf_ref[...] = wgu_ref[0].astype(BF16)
        wd_bf_ref[...] = wd_ref[0].astype(BF16)

    def block(slot, prev):
        words = [x_ref[pl.ds(j, tm, stride=SUBLANES), :] for j in range(PACK_LINES)]
        hi = [lax.bitcast_convert_type(w & jnp.uint32(0xFFFF0000), F32) for w in words]
        lo = [lax.bitcast_convert_type(w << 16, F32) for w in words]
        x = jnp.concatenate(hi + lo, axis=1).astype(BF16)
        acts = []
        for c in range(FFN_SLICES):
            send_rows_unrolled(prev, c * tm // FFN_SLICES, (c + 1) * tm // FFN_SLICES)
            w = F // FFN_SLICES
            g = jnp.dot(x, wgu_bf_ref[:, c * w:(c + 1) * w], preferred_element_type=F32)
            u = jnp.dot(x, wgu_bf_ref[:, F + c * w:F + (c + 1) * w],
                        preferred_element_type=F32)
            g = jnp.minimum(g + bgu_ref[0, :, c * w:(c + 1) * w], SWIGLU_LIMIT)
            u = jnp.clip(u + bgu_ref[0, :, F + c * w:F + (c + 1) * w],
                         -SWIGLU_LIMIT, SWIGLU_LIMIT)
            acts.append((g * jax.nn.sigmoid(SWIGLU_ALPHA * g) * (u + 1.0)).astype(BF16))
        act = jnp.concatenate(acts, axis=1)
        y = jnp.dot(act, wd_bf_ref[...], preferred_element_type=F32) + bd_ref[0]
        wait_block(slot)
        base = slot * (tm * SUBLANES)
        for j in range(SUBLANES):
            out_ref[pl.ds(base + j, tm, stride=SUBLANES), :] = y[:, j * LANES:(j + 1) * LANES]

    for parity in range(2):
        pl.when(used & (slot == parity))(functools.partial(block, parity, 1 - parity))

    @pl.when(b == nused)
    def _():
        wait_block(slot)
        send_block_rolled(prev)
        wait_block(prev)

    @pl.when(used & (b == nb - 1))
    def _():
        wait_block(prev)
        id_copy(slot).wait()
        send_block_rolled(slot)
        wait_block(slot)


def _ffn(blk_exp, nused, buf, wgu, bgu, wd, bd, tm, n_tokens):
    nb = buf.shape[0] // (tm * SUBLANES)
    D = wgu.shape[1]
    F = wd.shape[1]

    def xmap(b, be, nu):
        return (jnp.minimum(b, nu[0] - 1), 0)

    def wmap(b, be, nu):
        return (be[jnp.minimum(b, nu[0] - 1)], 0, 0)

    grid_spec = pltpu.PrefetchScalarGridSpec(
        num_scalar_prefetch=2,
        grid=(nb,),
        in_specs=[
            pl.BlockSpec((tm * SUBLANES, LANES), xmap),
            pl.BlockSpec((1, D, 2 * F), wmap),
            pl.BlockSpec((1, 1, 2 * F), wmap),
            pl.BlockSpec((1, F, D), wmap),
            pl.BlockSpec((1, 1, D), wmap),
        ],
        out_specs=pl.BlockSpec(memory_space=pl.ANY),
        scratch_shapes=[
            pltpu.VMEM((D, 2 * F), BF16), pltpu.VMEM((F, D), BF16),
            pltpu.VMEM((2 * tm * SUBLANES, LANES), F32),
            pltpu.VMEM((SUBLANES, LANES), I32),
            pltpu.SMEM((2 * SUBLANES, LANES), I32),
            pltpu.SemaphoreType.DMA((2,)), pltpu.SemaphoreType.DMA(()),
        ],
    )
    return pl.pallas_call(
        functools.partial(_ffn_kernel, n_tokens=n_tokens),
        grid_spec=grid_spec,
        out_shape=jax.ShapeDtypeStruct(((TOP_K * n_tokens + 2 * tm) * SUBLANES, LANES), F32),
        compiler_params=_cparams("arbitrary"),
        name="ffn",
    )(blk_exp, nused, buf, wgu, bgu, wd, bd)


def _combine_kernel(gate_ref, x2_ref, fg_ref, *refs):
    y_refs, o_ref = refs[:TOP_K], refs[TOP_K]
    tm, D = x2_ref.shape
    nj = D // LANES
    gate = gate_ref[...]
    cols = []
    for j in range(nj):
        acc = None
        for k in range(TOP_K):
            part = gate[:, k:k + 1] * y_refs[k][pl.ds(j, tm, stride=nj), :]
            acc = part if acc is None else acc + part
        cols.append(acc)
    y = jnp.concatenate(cols, axis=1)
    o_ref[...] = _rms(x2_ref[...] + y, fg_ref[...])


def _combine(gate4, x2, fg, y_rows):
    T, D = x2.shape
    tm = min(ROW_TILE, T)
    nj = D // LANES
    nt = T // tm

    def slot_spec(k):
        return pl.BlockSpec((tm * nj, LANES), lambda i: (k * nt + i, 0))

    return pl.pallas_call(
        _combine_kernel,
        grid=(nt,),
        in_specs=[
            pl.BlockSpec((tm, LANES), lambda i: (i, 0)),
            pl.BlockSpec((tm, D), lambda i: (i, 0)),
            pl.BlockSpec((1, D), lambda i: (0, 0)),
        ] + [slot_spec(k) for k in range(TOP_K)],
        out_specs=pl.BlockSpec((tm, D), lambda i: (i, 0)),
        out_shape=jax.ShapeDtypeStruct((T, D), F32),
        compiler_params=_cparams("parallel"),
        name="combine",
    )(gate4, x2, fg, *([y_rows] * TOP_K))


def _rope_tables(S):
    rows = S // GRID_W
    row = jnp.repeat(jnp.arange(rows, dtype=I32), GRID_W).astype(F32)
    col = jnp.tile(jnp.arange(GRID_W, dtype=I32), rows).astype(F32)
    inv = ROPE_THETA ** (-jnp.arange(0, AXIS_DIM, 2, dtype=F32) / AXIS_DIM)
    ang_r = row[:, None] * inv[None, :]
    ang_c = col[:, None] * inv[None, :]
    cos = jnp.concatenate([jnp.cos(ang_r)] * 2 + [jnp.cos(ang_c)] * 2, axis=1)
    sin = jnp.concatenate([-jnp.sin(ang_r), jnp.sin(ang_r), -jnp.sin(ang_c), jnp.sin(ang_c)], axis=1)
    return jnp.tile(cos, (1, 2)), jnp.tile(sin, (1, 2))


def _block_diag_gates(wa, ba, wi, bi, ct):
    nb, bw = wa.shape[1], wa.shape[2]
    per = ct // bw
    nct = nb // per
    eye = jnp.eye(per, dtype=wa.dtype)

    def tiles(w):
        w = w.reshape(nct, per, bw, bw)
        return jnp.einsum('cpij,pq->cpiqj', w, eye).reshape(nct, ct, ct)

    wg = jnp.concatenate([tiles(wa[0]), tiles(wi[0]), tiles(wa[1]), tiles(wi[1])], axis=2)
    bias = lambda b: b.reshape(nct, 1, ct)
    bg = jnp.concatenate([bias(ba[0]), bias(bi[0]), bias(ba[1]), bias(bi[1])], axis=2)
    return (0.5 * wg).astype(BF16), 0.5 * bg


def _layer(x2d, B, S, norm1_g, w_in, b_in, q_norm_g, k_norm_g, conv_w, conv_b, lru_wa, lru_ba,
           lru_wi, lru_bi, lru_lambda, w_attn_o, w_lru_o, w_out, norm2_g, w_router, b_router,
           w_gu, b_gu, w_down, b_down, out_g):
    T, D = x2d.shape
    C = conv_w.shape[1]
    nqkv = Q_WIDTH + 2 * KV_WIDTH
    row2 = lambda v: v.reshape(1, -1)

    cos_t, sin_t = _rope_tables(S)
    head = jnp.arange(LANES) // HEAD_DIM
    hsum = (head[:, None] == head[None, :]).astype(BF16)
    q, k, v = _qkv(x2d, row2(norm1_g), w_in[:, :nqkv].astype(BF16), row2(b_in[:nqkv]),
                   cos_t, sin_t, row2(jnp.tile(q_norm_g, 2)), row2(jnp.tile(k_norm_g, 2)),
                   hsum, B, S)
    proj = _proj(x2d, row2(norm1_g), w_in[:, nqkv:].astype(BF16), row2(b_in[nqkv:]))
    attn = _attn(q, k, v)
    wg, bg = _block_diag_gates(lru_wa, lru_ba, lru_wi, lru_bi, LRU_CT)
    rec = _lru(proj, conv_w, row2(conv_b), wg, bg, lru_lambda, B, S, C)

    pad = LANES - N_EXPERTS
    rw = jnp.pad(w_router, ((0, 0), (0, pad)))
    rw_hi = rw.astype(BF16)
    rw_lo = (rw - rw_hi.astype(F32)).astype(BF16)
    rb = jnp.pad(b_router, (0, pad), constant_values=NEG_BIG).reshape(1, LANES)
    x2, h2_rows, logits = _merge(x2d, attn, rec, proj, w_attn_o.astype(BF16),
                                 w_lru_o.astype(BF16), w_out.astype(BF16), row2(norm2_g),
                                 rw_hi, rw_lo, rb)

    gate4, _, eidx_t, rank_t, cnt = _route(logits)
    counts = cnt[:, 0].astype(I32)
    tm = FFN_TM
    nblk = (counts + tm - 1) // tm
    pend_blk = jnp.cumsum(nblk)
    pstart = (pend_blk - nblk) * tm
    A = T * TOP_K
    nb = (A + N_EXPERTS * (tm - 1) + tm - 1) // tm
    blk_exp = jnp.minimum(
        jnp.sum(pend_blk[None, :] <= jnp.arange(nb, dtype=I32)[:, None], axis=1),
        N_EXPERTS - 1).astype(I32)
    nused = pend_blk[-1:].astype(I32)
    first_row = jnp.sum(jnp.where(eidx_t[:TOP_K, :, None] == jnp.arange(N_EXPERTS), pstart, 0),
                        axis=-1)
    dest = (first_row + rank_t[:TOP_K]).astype(I32)
    last_blk = jnp.where(nblk > 0, pend_blk - 1, -1).astype(I32)

    buf = _dispatch(dest, last_blk, nused, h2_rows, nb * tm, tm)
    y_rows = _ffn(blk_exp, nused, buf, w_gu, b_gu.reshape(N_EXPERTS, 1, -1), w_down,
                  b_down.reshape(N_EXPERTS, 1, -1), tm, T)
    return _combine(gate4, x2, row2(out_g), y_rows)


def kernel(x, norm1_g, w_in, b_in, q_norm_g, k_norm_g, conv_w, conv_b, lru_wa, lru_ba, lru_wi,
           lru_bi, lru_lambda, w_attn_o, w_lru_o, w_out, norm2_g, w_router, b_router, w_gu, b_gu,
           w_down, b_down, final_g):
    B, S, D = x.shape
    depth = norm1_g.shape[0]
    assert depth == 1, "the fused final RMSNorm assumes a single layer"
    assert S % (TIME_CHUNKS * SUBLANES) == 0 and S % GRID_W == 0
    assert D == 2 * PACK_LINES * LANES, "a token's bf16 row must fill PACK_LINES word lines"
    out = _layer(x.reshape(B * S, D), B, S, norm1_g[0], w_in[0], b_in[0], q_norm_g[0],
                 k_norm_g[0], conv_w[0], conv_b[0], lru_wa[0], lru_ba[0], lru_wi[0], lru_bi[0],
                 lru_lambda[0], w_attn_o[0], w_lru_o[0], w_out[0], norm2_g[0], w_router[0],
                 b_router[0], w_gu[0], b_gu[0], w_down[0], b_down[0], final_g)
    return out.reshape(B, S, D)
```

```python
import functools

import jax
import jax.numpy as jnp
from jax import lax
from jax.experimental import pallas as pl
from jax.experimental.pallas import tpu as pltpu

F32 = jnp.float32
BF16 = jnp.bfloat16
I32 = jnp.int32
U32 = jnp.uint32

LANES = 128
SUBLANES = 8
PACK_LINES = 4
BF16_BITS = 16
HIGH_HALF = 0xFFFF0000
LOG2_SUBLANES = 3
SMEM_1D_TILE = 1024
VMEM_LIMIT_BYTES = 56 * 1024 * 1024

HEAD_DIM = 64
N_Q_HEADS = 8
N_KV_HEADS = 2
Q_GROUP = N_Q_HEADS // N_KV_HEADS
Q_WIDTH = N_Q_HEADS * HEAD_DIM
KV_WIDTH = N_KV_HEADS * HEAD_DIM
AXIS_DIM = HEAD_DIM // 2
ROT_HALF = AXIS_DIM // 2
ROPE_THETA = 10000.0
GRID_W = 64
RG_C = 8.0
N_EXPERTS = 32
TOP_K = 4
SWIGLU_LIMIT = 7.0
SWIGLU_ALPHA = 1.702
NORM_EPS = 1e-6
LOG2E = 1.4426950408889634
NEG_BIG = -1e30
TINY = 1e-30

ROW_TILE = 512
ATT_TQ = 512
ATT_TK = 512
LRU_CT = 256
LRU_ROWS = 512
TIME_CHUNKS = SUBLANES
FFN_TM = 512
FFN_SLICES = 4
DMA_TILE = 256


def _cparams(*sem):
    return pltpu.CompilerParams(dimension_semantics=sem, vmem_limit_bytes=VMEM_LIMIT_BYTES)


def _rms(x, g):
    return x * lax.rsqrt(jnp.mean(x * x, axis=-1, keepdims=True) + NORM_EPS) * g


def _qkv_kernel(x_ref, g1_ref, w_ref, b_ref, cos_ref, sin_ref, qg_ref, kg_ref, hsum_ref,
                q_ref, k_ref, v_ref):
    h = _rms(x_ref[...], g1_ref[...]).astype(BF16)
    p = jnp.dot(h, w_ref[...], preferred_element_type=F32) + b_ref[...]
    cos = cos_ref[...]
    sin = sin_ref[...]
    hsum = hsum_ref[...]
    lane = lax.broadcasted_iota(I32, cos.shape, 1)
    first_half = (lane % AXIS_DIM) < ROT_HALF

    def norm_rope(c, gain):
        sq = c * c
        hi = sq.astype(BF16)
        lo = (sq - hi.astype(F32)).astype(BF16)
        ms = (jnp.dot(hi, hsum, preferred_element_type=F32)
              + jnp.dot(lo, hsum, preferred_element_type=F32)) * (1.0 / HEAD_DIM)
        y = c * lax.rsqrt(ms + NORM_EPS) * gain
        partner = jnp.where(first_half, pltpu.roll(y, LANES - ROT_HALF, 1),
                            pltpu.roll(y, ROT_HALF, 1))
        return y * cos + partner * sin

    qg = qg_ref[...]
    for c in range(Q_WIDTH // LANES):
        y = norm_rope(p[:, c * LANES:(c + 1) * LANES], qg) * (HEAD_DIM ** -0.5 * LOG2E)
        yt = y.T.astype(BF16)
        q_ref[0, 2 * c] = yt[:HEAD_DIM]
        q_ref[0, 2 * c + 1] = yt[HEAD_DIM:]
    yk = norm_rope(p[:, Q_WIDTH:Q_WIDTH + KV_WIDTH], kg_ref[...])
    k_ref[0, 0] = yk[:, :HEAD_DIM].astype(BF16)
    k_ref[0, 1] = yk[:, HEAD_DIM:].astype(BF16)
    vt = p[:, Q_WIDTH + KV_WIDTH:].T
    tm = vt.shape[1]
    ones_row = (lax.broadcasted_iota(I32, (HEAD_DIM, tm), 0) == 0).astype(BF16)
    v_ref[0, 0] = jnp.concatenate([vt[:HEAD_DIM].astype(BF16), ones_row], axis=0)
    v_ref[0, 1] = jnp.concatenate([vt[HEAD_DIM:].astype(BF16), ones_row], axis=0)


def _qkv(x2d, g1, w_qkv, b_qkv, cos_t, sin_t, qg, kg, hsum, B, S):
    T, D = x2d.shape
    tm = min(ROW_TILE, S)
    ns = S // tm
    n = w_qkv.shape[1]
    full = lambda shape: pl.BlockSpec(shape, lambda i: (0,) * len(shape))
    return pl.pallas_call(
        _qkv_kernel,
        grid=(T // tm,),
        in_specs=[
            pl.BlockSpec((tm, D), lambda i: (i, 0)),
            full((1, D)), full((D, n)), full((1, n)),
            pl.BlockSpec((tm, LANES), lambda i: (i % ns, 0)),
            pl.BlockSpec((tm, LANES), lambda i: (i % ns, 0)),
            full((1, LANES)), full((1, LANES)), full((LANES, LANES)),
        ],
        out_specs=[
            pl.BlockSpec((1, N_Q_HEADS, HEAD_DIM, tm), lambda i: (i // ns, 0, 0, i % ns)),
            pl.BlockSpec((1, N_KV_HEADS, tm, HEAD_DIM), lambda i: (i // ns, 0, i % ns, 0)),
            pl.BlockSpec((1, N_KV_HEADS, LANES, tm), lambda i: (i // ns, 0, 0, i % ns)),
        ],
        out_shape=[
            jax.ShapeDtypeStruct((B, N_Q_HEADS, HEAD_DIM, S), BF16),
            jax.ShapeDtypeStruct((B, N_KV_HEADS, S, HEAD_DIM), BF16),
            jax.ShapeDtypeStruct((B, N_KV_HEADS, LANES, S), BF16),
        ],
        compiler_params=_cparams("parallel"),
        name="qkv",
    )(x2d, g1, w_qkv, b_qkv, cos_t, sin_t, qg, kg, hsum)


def _proj_kernel(x_ref, g_ref, w_ref, b_ref, o_ref):
    h = _rms(x_ref[...], g_ref[...]).astype(BF16)
    o_ref[...] = jnp.dot(h, w_ref[...], preferred_element_type=F32) + b_ref[...]


def _proj(x2d, g, w, b):
    T, D = x2d.shape
    n = w.shape[1]
    tm = min(ROW_TILE, T)
    return pl.pallas_call(
        _proj_kernel,
        grid=(T // tm,),
        in_specs=[
            pl.BlockSpec((tm, D), lambda i: (i, 0)),
            pl.BlockSpec((1, D), lambda i: (0, 0)),
            pl.BlockSpec((D, n), lambda i: (0, 0)),
            pl.BlockSpec((1, n), lambda i: (0, 0)),
        ],
        out_specs=pl.BlockSpec((tm, n), lambda i: (i, 0)),
        out_shape=jax.ShapeDtypeStruct((T, n), F32),
        compiler_params=_cparams("parallel"),
        name="proj",
    )(x2d, g, w, b)


def _attn_kernel(q_ref, k_ref, v_ref, o_ref, s0_ref, s1_ref, *, tk):
    tq = q_ref.shape[3]
    S = k_ref.shape[2]
    M = Q_GROUP * tq
    n = S // tk
    qT = jnp.concatenate([q_ref[0, h] for h in range(Q_GROUP)], axis=1)

    def scores(j, s_ref):
        off = pl.multiple_of(j * tk, tk)
        s_ref[...] = jnp.dot(k_ref[0, 0, pl.ds(off, tk), :], qT, preferred_element_type=F32)

    def absorb(j, s_ref, carry):
        m, acc = carry
        off = pl.multiple_of(j * tk, tk)
        vc = v_ref[0, 0, :, pl.ds(off, tk)]
        s = s_ref[...]
        m_new = jnp.maximum(m, jnp.max(s, axis=0, keepdims=True))
        alpha = jnp.exp2(m - m_new)
        p = jnp.exp2(s - m_new).astype(BF16)
        return m_new, alpha * acc + jnp.dot(vc, p, preferred_element_type=F32)

    def pair(i, carry):
        scores(2 * i + 1, s1_ref)
        carry = absorb(2 * i, s0_ref, carry)
        scores(2 * i + 2, s0_ref)
        return absorb(2 * i + 1, s1_ref, carry)

    scores(0, s0_ref)
    carry = (jnp.full((1, M), NEG_BIG, F32), jnp.zeros((LANES, M), F32))
    carry = lax.fori_loop(0, n // 2 - 1, pair, carry)
    scores(n - 1, s1_ref)
    carry = absorb(n - 2, s0_ref, carry)
    _, acc = absorb(n - 1, s1_ref, carry)
    outT = acc[:HEAD_DIM] / acc[HEAD_DIM:HEAD_DIM + 1]
    stacked = jnp.concatenate(
        [outT[:, g * tq:(g + 1) * tq] for g in range(Q_GROUP)], axis=0)
    o_ref[...] = stacked.T.astype(BF16)


def _attn(qT, k, vT):
    B, _, _, S = qT.shape
    tq = min(ATT_TQ, S)
    tk = min(ATT_TK, S)
    nq = S // tq
    assert (S // tk) % 2 == 0, "key chunks are processed in pairs"
    score_buf = pltpu.VMEM((tk, Q_GROUP * tq), F32)
    return pl.pallas_call(
        functools.partial(_attn_kernel, tk=tk),
        grid=(B, N_KV_HEADS, nq),
        in_specs=[
            pl.BlockSpec((1, Q_GROUP, HEAD_DIM, tq), lambda b, g, i: (b, g, 0, i)),
            pl.BlockSpec((1, 1, S, HEAD_DIM), lambda b, g, i: (b, g, 0, 0)),
            pl.BlockSpec((1, 1, LANES, S), lambda b, g, i: (b, g, 0, 0)),
        ],
        out_specs=pl.BlockSpec((tq, Q_GROUP * HEAD_DIM), lambda b, g, i: (b * nq + i, g)),
        out_shape=jax.ShapeDtypeStruct((B * S, Q_WIDTH), BF16),
        scratch_shapes=[score_buf, score_buf],
        compiler_params=_cparams("parallel", "parallel", "parallel"),
        name="attn",
    )(qT, k, vT)


def _shift_chunks(v, down):
    row = lax.broadcasted_iota(I32, v.shape, 0)
    if down:
        return jnp.where(row == 0, 0.0, pltpu.roll(v, 1, 0))
    return jnp.where(row == SUBLANES - 1, 0.0, pltpu.roll(v, SUBLANES - 1, 0))


def _lru_kernel(xr_ref, yr_ref, cw_ref, cb_ref, wg_ref, bg_ref, lam_ref, o_ref,
                xe_ref, af_ref, uf_ref, ab_ref, ub_ref, *, rows):
    S, ct = xr_ref.shape
    lc = S // TIME_CHUNKS
    halo = SUBLANES

    ng = ct // LANES

    def xe_rows(r0, n):
        return jnp.concatenate([xe_ref[g, pl.ds(r0, n), :] for g in range(ng)], axis=1)

    def xe_store(r0, n, val):
        for g in range(ng):
            xe_ref[g, pl.ds(r0, n), :] = val[:, g * LANES:(g + 1) * LANES]

    for s in range(TIME_CHUNKS):
        for g in range(ng):
            xe_ref[g, pl.ds(halo + s, lc, stride=SUBLANES), :] = (
                xr_ref[pl.ds(s * lc, lc), g * LANES:(g + 1) * LANES])
    xe_store(0, halo, _shift_chunks(xe_rows(S, halo), True))
    first = xe_rows(halo, halo)
    second = xe_rows(2 * halo, halo)
    xe_store(S + halo, halo, _shift_chunks(first, False))
    xe_store(S + 2 * halo, halo, _shift_chunks(second, False))

    cw = cw_ref[...]
    cb = cb_ref[...]
    lam = lam_ref[...]
    log_sig = jnp.minimum(lam, 0.0) - jnp.log(1.0 + jnp.exp(-jnp.abs(lam)))
    c_half = (0.5 * RG_C * LOG2E) * log_sig
    wg = wg_ref[0]
    bg = bg_ref[0]

    def gate_chunk(i, _):
        r0 = pl.multiple_of(i * rows, rows)
        xc = cb
        for j in range(4):
            xc = xc + cw[j:j + 1, :] * xe_rows(r0 + j * halo, rows)
        t = jnp.tanh(jnp.dot(xc.astype(BF16), wg, preferred_element_type=F32) + bg)
        x_half = 0.5 * xc
        for d, (a_ref, u_ref) in enumerate(((af_ref, uf_ref), (ab_ref, ub_ref))):
            t_r = t[:, (2 * d) * ct:(2 * d + 1) * ct]
            t_i = t[:, (2 * d + 1) * ct:(2 * d + 2) * ct]
            ch = c_half[d:d + 1, :]
            a = jnp.exp2(t_r * ch + ch)
            a_ref[pl.ds(r0, rows), :] = a
            v = 1.0 - a * a
            root = jnp.maximum(v, 0.0) * lax.rsqrt(jnp.maximum(v, TINY))
            u_ref[pl.ds(r0, rows), :] = root * ((t_i + 1.0) * x_half)
        return 0

    lax.fori_loop(0, S // rows, gate_chunk, 0)

    unroll = 8

    def scan_body(i, carry):
        hf, pf, hb, pb = carry
        for k in range(unroll):
            tf = i * unroll + k
            rf = pl.multiple_of(tf * SUBLANES, SUBLANES)
            rb = pl.multiple_of((lc - 1 - tf) * SUBLANES, SUBLANES)
            a = af_ref[pl.ds(rf, SUBLANES), :]
            hf = a * hf + uf_ref[pl.ds(rf, SUBLANES), :]
            pf = a * pf
            uf_ref[pl.ds(rf, SUBLANES), :] = hf
            af_ref[pl.ds(rf, SUBLANES), :] = pf
            a = ab_ref[pl.ds(rb, SUBLANES), :]
            hb = a * hb + ub_ref[pl.ds(rb, SUBLANES), :]
            pb = a * pb
            ub_ref[pl.ds(rb, SUBLANES), :] = hb
            ab_ref[pl.ds(rb, SUBLANES), :] = pb
        return hf, pf, hb, pb

    zero = jnp.zeros((SUBLANES, ct), F32)
    one = jnp.ones((SUBLANES, ct), F32)
    hf_end, pf_end, hb_end, pb_end = lax.fori_loop(0, lc // unroll, scan_body,
                                                   (zero, one, zero, one))

    def chunk_carry(h_end, p_end, down):
        f = h_end
        for _ in range(TIME_CHUNKS - 1):
            f = h_end + p_end * _shift_chunks(f, down)
        return _shift_chunks(f, down)

    cin_f = chunk_carry(hf_end, pf_end, True)
    cin_b = chunk_carry(hb_end, pb_end, False)

    def fix_chunk(i, _):
        r0 = pl.multiple_of(i * rows, rows)
        reps = rows // SUBLANES
        hf = uf_ref[pl.ds(r0, rows), :] + af_ref[pl.ds(r0, rows), :] * jnp.tile(cin_f, (reps, 1))
        hb = ub_ref[pl.ds(r0, rows), :] + ab_ref[pl.ds(r0, rows), :] * jnp.tile(cin_b, (reps, 1))
        xe_store(r0, rows, hf + hb)
        return 0

    lax.fori_loop(0, S // rows, fix_chunk, 0)

    for s in range(TIME_CHUNKS):
        hsum = jnp.concatenate(
            [xe_ref[g, pl.ds(s, lc, stride=SUBLANES), :] for g in range(ng)], axis=1)
        y = yr_ref[pl.ds(s * lc, lc), :]
        o_ref[pl.ds(s * lc, lc), :] = (hsum * jax.nn.gelu(y, approximate=True)).astype(BF16)


def _lru(proj, conv_w, conv_b, wg, bg, lam, B, S, C):
    ct = LRU_CT
    nct = C // ct
    rows = min(LRU_ROWS, S)
    return pl.pallas_call(
        functools.partial(_lru_kernel, rows=rows),
        grid=(B, nct),
        in_specs=[
            pl.BlockSpec((S, ct), lambda b, c: (b, c)),
            pl.BlockSpec((S, ct), lambda b, c: (b, nct + c)),
            pl.BlockSpec((4, ct), lambda b, c: (0, c)),
            pl.BlockSpec((1, ct), lambda b, c: (0, c)),
            pl.BlockSpec((1, ct, 4 * ct), lambda b, c: (c, 0, 0)),
            pl.BlockSpec((1, 1, 4 * ct), lambda b, c: (c, 0, 0)),
            pl.BlockSpec((2, ct), lambda b, c: (0, c)),
        ],
        out_specs=pl.BlockSpec((S, ct), lambda b, c: (b, c)),
        out_shape=jax.ShapeDtypeStruct((B * S, C), BF16),
        scratch_shapes=[
            pltpu.VMEM((ct // LANES, S + 3 * SUBLANES, LANES), F32),
            pltpu.VMEM((S, ct), F32), pltpu.VMEM((S, ct), F32),
            pltpu.VMEM((S, ct), F32), pltpu.VMEM((S, ct), F32),
        ],
        compiler_params=_cparams("parallel", "parallel"),
        name="lru",
    )(proj, proj, conv_w, conv_b, wg, bg, lam)


def _merge_kernel(x_ref, attn_ref, rec_ref, gl_ref, wa_ref, wr_ref, wo_ref, g2_ref,
                  rw_hi_ref, rw_lo_ref, rb_ref, x2_ref, h2_ref, lg_ref):
    D = x_ref.shape[1]
    gl = gl_ref[...]
    ma = jnp.dot(attn_ref[...], wa_ref[...], preferred_element_type=F32)
    mr = jnp.dot(rec_ref[...], wr_ref[...], preferred_element_type=F32)
    merged = jax.nn.sigmoid(gl[:, :D]) * ma + jax.nn.sigmoid(gl[:, D:]) * mr
    x2 = x_ref[...] + jnp.dot(merged.astype(BF16), wo_ref[...], preferred_element_type=F32)
    x2_ref[...] = x2
    h2 = _rms(x2, g2_ref[...])
    hi = h2.astype(BF16)
    lo = (h2 - hi.astype(F32)).astype(BF16)
    lg_ref[...] = (jnp.dot(hi, rw_hi_ref[...], preferred_element_type=F32)
                   + jnp.dot(lo, rw_hi_ref[...], preferred_element_type=F32)
                   + jnp.dot(hi, rw_lo_ref[...], preferred_element_type=F32)) + rb_ref[...]
    tm = h2.shape[0]
    bits = lax.bitcast_convert_type(hi.astype(F32), U32)
    words = bits[:, :D // 2] | (bits[:, D // 2:] >> BF16_BITS)
    for j in range(PACK_LINES):
        h2_ref[pl.ds(j, tm, stride=SUBLANES), :] = words[:, j * LANES:(j + 1) * LANES]
    for j in range(PACK_LINES, SUBLANES):
        h2_ref[pl.ds(j, tm, stride=SUBLANES), :] = jnp.zeros((tm, LANES), U32)


def _merge(x2d, attn, rec, proj, wa, wr, wo, g2, rw_hi, rw_lo, rb):
    T, D = x2d.shape
    tm = min(ROW_TILE, T)
    nj = D // LANES
    full = lambda a: pl.BlockSpec(a.shape, lambda i: (0,) * a.ndim)
    return pl.pallas_call(
        _merge_kernel,
        grid=(T // tm,),
        in_specs=[
            pl.BlockSpec((tm, D), lambda i: (i, 0)),
            pl.BlockSpec((tm, Q_WIDTH), lambda i: (i, 0)),
            pl.BlockSpec((tm, D), lambda i: (i, 0)),
            pl.BlockSpec((tm, 2 * D), lambda i: (i, 1)),
            full(wa), full(wr), full(wo), full(g2), full(rw_hi), full(rw_lo), full(rb),
        ],
        out_specs=[
            pl.BlockSpec((tm, D), lambda i: (i, 0)),
            pl.BlockSpec((tm * nj, LANES), lambda i: (i, 0)),
            pl.BlockSpec((tm, LANES), lambda i: (i, 0)),
        ],
        out_shape=[
            jax.ShapeDtypeStruct((T, D), F32),
            jax.ShapeDtypeStruct((T * nj, LANES), U32),
            jax.ShapeDtypeStruct((T, LANES), F32),
        ],
        compiler_params=_cparams("parallel"),
        name="merge",
    )(x2d, attn, rec, proj, wa, wr, wo, g2, rw_hi, rw_lo, rb)


def _route_kernel(lg_ref, gate_ref, gate_t_ref, eidx_t_ref, rank_t_ref, cnt_ref):
    i = pl.program_id(0)

    @pl.when(i == 0)
    def _():
        cnt_ref[...] = jnp.zeros_like(cnt_ref)

    tm = lg_ref.shape[0]
    work = lg_ref[...].T[:N_EXPERTS]
    expert = lax.broadcasted_iota(I32, work.shape, 0)
    sels, vals, idxs = [], [], []
    for _ in range(TOP_K):
        mx = jnp.max(work, axis=0, keepdims=True)
        idx = jnp.min(jnp.where(work == mx, expert, N_EXPERTS), axis=0, keepdims=True)
        sel = expert == idx
        sels.append(sel)
        vals.append(mx)
        idxs.append(idx)
        work = jnp.where(sel, NEG_BIG, work)
    ex = [jnp.exp(v - vals[0]) for v in vals]
    den = ex[0] + ex[1] + ex[2] + ex[3]
    member = (sels[0] | sels[1] | sels[2] | sels[3]).astype(BF16)
    r = lax.broadcasted_iota(I32, (tm, tm), 0)
    c = lax.broadcasted_iota(I32, (tm, tm), 1)
    tri = (r < c).astype(BF16)
    base = cnt_ref[:, 0:1]
    before = jnp.dot(member, tri, preferred_element_type=F32) + base
    ranks = [jnp.sum(jnp.where(sels[k], before, 0.0), axis=0, keepdims=True)
             for k in range(TOP_K)]
    gates = [ex[k] / den for k in range(TOP_K)]
    pad = SUBLANES - TOP_K
    gate_t = jnp.concatenate(gates + [jnp.zeros((pad, tm), F32)], axis=0)
    gate_t_ref[...] = gate_t
    eidx_t_ref[...] = jnp.concatenate(idxs + [jnp.zeros((pad, tm), I32)], axis=0)
    rank_t_ref[...] = jnp.concatenate(ranks + [jnp.zeros((pad, tm), F32)], axis=0).astype(I32)
    gate_ref[...] = jnp.concatenate(
        [gate_t, jnp.zeros((LANES - SUBLANES, tm), F32)], axis=0).T
    cnt_ref[...] = jnp.broadcast_to(
        base + jnp.sum(member.astype(F32), axis=1, keepdims=True), cnt_ref.shape)


def _route(logits):
    T = logits.shape[0]
    tm = min(ROW_TILE, T)
    spec = pl.BlockSpec((tm, LANES), lambda i: (i, 0))
    spec_t = pl.BlockSpec((SUBLANES, tm), lambda i: (0, i))
    return pl.pallas_call(
        _route_kernel,
        grid=(T // tm,),
        in_specs=[spec],
        out_specs=[spec, spec_t, spec_t, spec_t,
                   pl.BlockSpec((N_EXPERTS, LANES), lambda i: (0, 0))],
        out_shape=[
            jax.ShapeDtypeStruct((T, LANES), F32),
            jax.ShapeDtypeStruct((SUBLANES, T), F32),
            jax.ShapeDtypeStruct((SUBLANES, T), I32),
            jax.ShapeDtypeStruct((SUBLANES, T), I32),
            jax.ShapeDtypeStruct((N_EXPERTS, LANES), F32),
        ],
        compiler_params=_cparams("arbitrary"),
        name="route",
    )(logits)


def _tile(ref, row):
    return ref.at[pl.ds(pl.multiple_of(row * SUBLANES, SUBLANES), SUBLANES)]


def _dispatch_kernel(dest_ref, last_blk_ref, nused_ref, h2_ref, buf_ref,
                     src_ref, pad_ref, sem, zsem, *, n_tokens):
    i = pl.program_id(0)
    tm = h2_ref.shape[0] // SUBLANES
    blk = pad_ref.shape[0] // (2 * SUBLANES)
    nblocks = buf_ref.shape[0] // (blk * SUBLANES)

    @pl.when(i == 0)
    def _():
        li = lax.broadcasted_iota(I32, pad_ref.shape, 0)
        trash = (TOP_K * n_tokens + (li >> LOG2_SUBLANES)).astype(U32)
        pad_ref[...] = jnp.where((li & (SUBLANES - 1)) >= PACK_LINES, trash, jnp.uint32(0))
        rows = blk * SUBLANES

        def pad_block(b):
            half = pl.multiple_of(lax.rem(b, 2) * rows, rows)
            return pltpu.make_async_copy(
                pad_ref.at[pl.ds(half, rows)],
                buf_ref.at[pl.ds(pl.multiple_of(b * rows, rows), rows)], zsem)

        def expert_put(e, _):
            pl.when(last_blk_ref[e] >= 0)(lambda: pad_block(last_blk_ref[e]).start())
            return 0

        def expert_done(e, _):
            pl.when(last_blk_ref[e] >= 0)(lambda: pad_block(last_blk_ref[e]).wait())
            return 0

        def tail_put(b, _):
            pad_block(b).start()
            return 0

        def tail_done(b, _):
            pad_block(b).wait()
            return 0

        lax.fori_loop(0, N_EXPERTS, expert_put, 0)
        lax.fori_loop(nused_ref[0], nblocks, tail_put, 0)
        lax.fori_loop(0, N_EXPERTS, expert_done, 0)
        lax.fori_loop(nused_ref[0], nblocks, tail_done, 0)

    parity = lax.rem(i, 2)
    half = parity * TOP_K
    li = lax.broadcasted_iota(I32, h2_ref.shape, 0)
    tok = i * tm + (li >> LOG2_SUBLANES)
    is_id = (li & (SUBLANES - 1)) >= PACK_LINES
    data = h2_ref[...]
    for k in range(TOP_K):
        src_ref[half + k] = jnp.where(is_id, (k * n_tokens + tok).astype(U32), data)

    def issue(r, _):
        for k in range(TOP_K):
            pltpu.make_async_copy(_tile(src_ref.at[half + k], r),
                                  _tile(buf_ref, dest_ref[k * (tm + 1) + r]),
                                  sem.at[parity]).start(priority=k % 2)
        return 0

    lax.fori_loop(0, tm, issue, 0)

    def wait_step(which):
        for k in range(TOP_K):
            pltpu.make_async_copy(src_ref.at[k], buf_ref.at[pl.ds(0, tm * SUBLANES)],
                                  sem.at[which]).wait()

    pl.when(i >= 1)(functools.partial(wait_step, 1 - parity))
    pl.when(i == pl.num_programs(0) - 1)(functools.partial(wait_step, parity))


def _dispatch(dest, last_blk, nused, h2_rows, n_rows, blk):
    T = h2_rows.shape[0] // SUBLANES
    tm = min(DMA_TILE, T)
    nt = T // tm
    words = pl.cdiv(TOP_K * (tm + 1), SMEM_1D_TILE) * SMEM_1D_TILE
    d = jnp.pad(dest.reshape(TOP_K, nt, tm).transpose(1, 0, 2), ((0, 0), (0, 0), (0, 1)))
    d = jnp.pad(d.reshape(nt, TOP_K * (tm + 1)), ((0, 0), (0, words - TOP_K * (tm + 1))))
    dest_flat = d.reshape(nt * words)
    smem = pl.BlockSpec(memory_space=pltpu.SMEM)
    return pl.pallas_call(
        functools.partial(_dispatch_kernel, n_tokens=T),
        grid=(nt,),
        in_specs=[
            pl.BlockSpec((words,), lambda i: (i,), memory_space=pltpu.SMEM),
            smem, smem,
            pl.BlockSpec((tm * SUBLANES, LANES), lambda i: (i, 0)),
        ],
        out_specs=pl.BlockSpec(memory_space=pl.ANY),
        out_shape=jax.ShapeDtypeStruct((n_rows * SUBLANES, LANES), U32),
        scratch_shapes=[pltpu.VMEM((2 * TOP_K, tm * SUBLANES, LANES), U32),
                        pltpu.VMEM((2 * blk * SUBLANES, LANES), U32),
                        pltpu.SemaphoreType.DMA((2,)), pltpu.SemaphoreType.DMA(())],
        compiler_params=_cparams("arbitrary"),
        name="dispatch",
    )(dest_flat, last_blk, nused, h2_rows)


def _ffn_kernel(blk_exp_ref, nused_ref, x_ref, wgu_ref, bgu_ref, wd_ref, bd_ref, y_ref,
                wgu_bf_ref, wd_bf_ref, out_ref, idv_ref, ids_ref, ssem, isem, *, n_tokens):
    b = pl.program_id(0)
    nb = pl.num_programs(0)
    tm = x_ref.shape[0] // SUBLANES
    groups = tm // LANES
    F = wd_ref.shape[1]
    nused = nused_ref[0]
    used = b < nused
    slot = lax.rem(b, 2)
    prev = 1 - slot
    new_expert = (b == 0) | (blk_exp_ref[b] != blk_exp_ref[jnp.maximum(b - 1, 0)])

    def send_row(s, r_hi, r_lo, row, priority):
        rid = ids_ref[s * SUBLANES + r_hi, r_lo]
        pltpu.make_async_copy(_tile(out_ref, s * tm + row), _tile(y_ref, rid),
                              ssem.at[s]).start(priority=priority)

    def send_rows_unrolled(s, lo, hi):
        for r in range(lo, hi):
            send_row(s, r // LANES, r % LANES, r, r % 2)

    def send_block_rolled(s):
        def body(g, _):
            def inner(l, _):
                send_row(s, g, l, g * LANES + l, 0)
                return 0
            return lax.fori_loop(0, LANES, inner, 0)
        lax.fori_loop(0, groups, body, 0)

    def wait_block(s):
        rows = tm * SUBLANES
        pltpu.make_async_copy(out_ref.at[pl.ds(0, rows)], y_ref.at[pl.ds(0, rows)],
                              ssem.at[s]).wait()

    def id_copy(s):
        return pltpu.make_async_copy(
            idv_ref, ids_ref.at[pl.ds(pl.multiple_of(s * SUBLANES, SUBLANES), SUBLANES)], isem)

    @pl.when(b == 0)
    def _():
        out_ref[...] = jnp.zeros_like(out_ref)
        rows = tm * SUBLANES
        pltpu.make_async_copy(out_ref.at[pl.ds(0, rows)],
                              y_ref.at[pl.ds(TOP_K * n_tokens * SUBLANES, rows)],
                              ssem.at[0]).start()
        pos = (lax.broadcasted_iota(I32, idv_ref.shape, 0) * LANES
               + lax.broadcasted_iota(I32, idv_ref.shape, 1))
        idv_ref[...] = TOP_K * n_tokens + tm + jnp.minimum(pos, tm - 1)
        id_copy(1).start()
        id_copy(1).wait()

    @pl.when((b >= 1) & (b <= nused))
    def _():
        id_copy(prev).wait()

    @pl.when(used)
    def _():
        idl = x_ref[pl.ds(PACK_LINES, tm, stride=SUBLANES), :].astype(I32)
        rr = lax.broadcasted_iota(I32, idl.shape, 0)
        ll = lax.broadcasted_iota(I32, idl.shape, 1)
        diag = jnp.where((rr & (LANES - 1)) == ll, idl, 0)
        idv_ref[pl.ds(0, groups), :] = jnp.sum(diag.reshape(groups, LANES, LANES), axis=1)
        id_copy(slot).start()

    @pl.when(used & new_expert)
    def _():
        wgu_bf_ref[...] = wgu_ref[0].astype(BF16)
        wd_bf_ref[...] = wd_ref[0].astype(BF16)

    def block(slot, prev):
        words = [x_ref[pl.ds(j, tm, stride=SUBLANES), :] for j in range(PACK_LINES)]
        hi = [lax.bitcast_convert_type(w & jnp.uint32(HIGH_HALF), F32) for w in words]
        lo = [lax.bitcast_convert_type(w << BF16_BITS, F32) for w in words]
        x = jnp.concatenate(hi + lo, axis=1).astype(BF16)
        acts = []
        for c in range(FFN_SLICES):
            send_rows_unrolled(prev, c * tm // FFN_SLICES, (c + 1) * tm // FFN_SLICES)
            w = F // FFN_SLICES
            g = jnp.dot(x, wgu_bf_ref[:, c * w:(c + 1) * w], preferred_element_type=F32)
            u = jnp.dot(x, wgu_bf_ref[:, F + c * w:F + (c + 1) * w],
                        preferred_element_type=F32)
            g = jnp.minimum(g + bgu_ref[0, :, c * w:(c + 1) * w], SWIGLU_LIMIT)
            u = jnp.clip(u + bgu_ref[0, :, F + c * w:F + (c + 1) * w],
                         -SWIGLU_LIMIT, SWIGLU_LIMIT)
            acts.append((g * jax.nn.sigmoid(SWIGLU_ALPHA * g) * (u + 1.0)).astype(BF16))
        act = jnp.concatenate(acts, axis=1)
        y = jnp.dot(act, wd_bf_ref[...], preferred_element_type=F32) + bd_ref[0]
        wait_block(slot)
        base = slot * (tm * SUBLANES)
        for j in range(SUBLANES):
            out_ref[pl.ds(base + j, tm, stride=SUBLANES), :] = y[:, j * LANES:(j + 1) * LANES]

    for parity in range(2):
        pl.when(used & (slot == parity))(functools.partial(block, parity, 1 - parity))

    @pl.when(b == nused)
    def _():
        wait_block(slot)
        send_block_rolled(prev)
        wait_block(prev)

    @pl.when(used & (b == nb - 1))
    def _():
        wait_block(prev)
        id_copy(slot).wait()
        send_block_rolled(slot)
        wait_block(slot)


def _ffn(blk_exp, nused, buf, wgu, bgu, wd, bd, tm, n_tokens):
    nb = buf.shape[0] // (tm * SUBLANES)
    D = wgu.shape[1]
    F = wd.shape[1]

    def xmap(b, be, nu):
        return (jnp.minimum(b, nu[0] - 1), 0)

    def wmap(b, be, nu):
        return (be[jnp.minimum(b, nu[0] - 1)], 0, 0)

    grid_spec = pltpu.PrefetchScalarGridSpec(
        num_scalar_prefetch=2,
        grid=(nb,),
        in_specs=[
            pl.BlockSpec((tm * SUBLANES, LANES), xmap),
            pl.BlockSpec((1, D, 2 * F), wmap),
            pl.BlockSpec((1, 1, 2 * F), wmap),
            pl.BlockSpec((1, F, D), wmap),
            pl.BlockSpec((1, 1, D), wmap),
        ],
        out_specs=pl.BlockSpec(memory_space=pl.ANY),
        scratch_shapes=[
            pltpu.VMEM((D, 2 * F), BF16), pltpu.VMEM((F, D), BF16),
            pltpu.VMEM((2 * tm * SUBLANES, LANES), F32),
            pltpu.VMEM((SUBLANES, LANES), I32),
            pltpu.SMEM((2 * SUBLANES, LANES), I32),
            pltpu.SemaphoreType.DMA((2,)), pltpu.SemaphoreType.DMA(()),
        ],
    )
    return pl.pallas_call(
        functools.partial(_ffn_kernel, n_tokens=n_tokens),
        grid_spec=grid_spec,
        out_shape=jax.ShapeDtypeStruct(((TOP_K * n_tokens + 2 * tm) * SUBLANES, LANES), F32),
        compiler_params=_cparams("arbitrary"),
        name="ffn",
    )(blk_exp, nused, buf, wgu, bgu, wd, bd)


def _combine_kernel(gate_ref, x2_ref, fg_ref, *refs):
    y_refs, o_ref = refs[:TOP_K], refs[TOP_K]
    tm, D = x2_ref.shape
    nj = D // LANES
    gate = gate_ref[...]
    cols = []
    for j in range(nj):
        acc = None
        for k in range(TOP_K):
            part = gate[:, k:k + 1] * y_refs[k][pl.ds(j, tm, stride=nj), :]
            acc = part if acc is None else acc + part
        cols.append(acc)
    y = jnp.concatenate(cols, axis=1)
    o_ref[...] = _rms(x2_ref[...] + y, fg_ref[...])


def _combine(gate4, x2, fg, y_rows):
    T, D = x2.shape
    tm = min(ROW_TILE, T)
    nj = D // LANES
    nt = T // tm

    def slot_spec(k):
        return pl.BlockSpec((tm * nj, LANES), lambda i: (k * nt + i, 0))

    return pl.pallas_call(
        _combine_kernel,
        grid=(nt,),
        in_specs=[
            pl.BlockSpec((tm, LANES), lambda i: (i, 0)),
            pl.BlockSpec((tm, D), lambda i: (i, 0)),
            pl.BlockSpec((1, D), lambda i: (0, 0)),
        ] + [slot_spec(k) for k in range(TOP_K)],
        out_specs=pl.BlockSpec((tm, D), lambda i: (i, 0)),
        out_shape=jax.ShapeDtypeStruct((T, D), F32),
        compiler_params=_cparams("parallel"),
        name="combine",
    )(gate4, x2, fg, *([y_rows] * TOP_K))


def _rope_tables(S):
    rows = S // GRID_W
    row = jnp.repeat(jnp.arange(rows, dtype=I32), GRID_W).astype(F32)
    col = jnp.tile(jnp.arange(GRID_W, dtype=I32), rows).astype(F32)
    inv = ROPE_THETA ** (-jnp.arange(0, AXIS_DIM, 2, dtype=F32) / AXIS_DIM)
    ang_r = row[:, None] * inv[None, :]
    ang_c = col[:, None] * inv[None, :]
    cos = jnp.concatenate([jnp.cos(ang_r)] * 2 + [jnp.cos(ang_c)] * 2, axis=1)
    sin = jnp.concatenate([-jnp.sin(ang_r), jnp.sin(ang_r), -jnp.sin(ang_c), jnp.sin(ang_c)], axis=1)
    return jnp.tile(cos, (1, 2)), jnp.tile(sin, (1, 2))


def _block_diag_gates(wa, ba, wi, bi, ct):
    nb, bw = wa.shape[1], wa.shape[2]
    per = ct // bw
    nct = nb // per
    eye = jnp.eye(per, dtype=wa.dtype)

    def tiles(w):
        w = w.reshape(nct, per, bw, bw)
        return jnp.einsum('cpij,pq->cpiqj', w, eye).reshape(nct, ct, ct)

    wg = jnp.concatenate([tiles(wa[0]), tiles(wi[0]), tiles(wa[1]), tiles(wi[1])], axis=2)
    bias = lambda b: b.reshape(nct, 1, ct)
    bg = jnp.concatenate([bias(ba[0]), bias(bi[0]), bias(ba[1]), bias(bi[1])], axis=2)
    return (0.5 * wg).astype(BF16), 0.5 * bg


def _layer(x2d, B, S, norm1_g, w_in, b_in, q_norm_g, k_norm_g, conv_w, conv_b, lru_wa, lru_ba,
           lru_wi, lru_bi, lru_lambda, w_attn_o, w_lru_o, w_out, norm2_g, w_router, b_router,
           w_gu, b_gu, w_down, b_down, out_g):
    T, D = x2d.shape
    C = conv_w.shape[1]
    nqkv = Q_WIDTH + 2 * KV_WIDTH
    row2 = lambda v: v.reshape(1, -1)

    cos_t, sin_t = _rope_tables(S)
    head = jnp.arange(LANES) // HEAD_DIM
    hsum = (head[:, None] == head[None, :]).astype(BF16)
    q, k, v = _qkv(x2d, row2(norm1_g), w_in[:, :nqkv].astype(BF16), row2(b_in[:nqkv]),
                   cos_t, sin_t, row2(jnp.tile(q_norm_g, 2)), row2(jnp.tile(k_norm_g, 2)),
                   hsum, B, S)
    proj = _proj(x2d, row2(norm1_g), w_in[:, nqkv:].astype(BF16), row2(b_in[nqkv:]))
    attn = _attn(q, k, v)
    wg, bg = _block_diag_gates(lru_wa, lru_ba, lru_wi, lru_bi, LRU_CT)
    rec = _lru(proj, conv_w, row2(conv_b), wg, bg, lru_lambda, B, S, C)

    pad = LANES - N_EXPERTS
    rw = jnp.pad(w_router, ((0, 0), (0, pad)))
    rw_hi = rw.astype(BF16)
    rw_lo = (rw - rw_hi.astype(F32)).astype(BF16)
    rb = jnp.pad(b_router, (0, pad), constant_values=NEG_BIG).reshape(1, LANES)
    x2, h2_rows, logits = _merge(x2d, attn, rec, proj, w_attn_o.astype(BF16),
                                 w_lru_o.astype(BF16), w_out.astype(BF16), row2(norm2_g),
                                 rw_hi, rw_lo, rb)

    gate4, _, eidx_t, rank_t, cnt = _route(logits)
    counts = cnt[:, 0].astype(I32)
    tm = FFN_TM
    nblk = (counts + tm - 1) // tm
    pend_blk = jnp.cumsum(nblk)
    pstart = (pend_blk - nblk) * tm
    A = T * TOP_K
    nb = (A + N_EXPERTS * (tm - 1) + tm - 1) // tm
    blk_exp = jnp.minimum(
        jnp.sum(pend_blk[None, :] <= jnp.arange(nb, dtype=I32)[:, None], axis=1),
        N_EXPERTS - 1).astype(I32)
    nused = pend_blk[-1:].astype(I32)
    first_row = jnp.sum(jnp.where(eidx_t[:TOP_K, :, None] == jnp.arange(N_EXPERTS), pstart, 0),
                        axis=-1)
    dest = (first_row + rank_t[:TOP_K]).astype(I32)
    last_blk = jnp.where(nblk > 0, pend_blk - 1, -1).astype(I32)

    buf = _dispatch(dest, last_blk, nused, h2_rows, nb * tm, tm)
    y_rows = _ffn(blk_exp, nused, buf, w_gu, b_gu.reshape(N_EXPERTS, 1, -1), w_down,
                  b_down.reshape(N_EXPERTS, 1, -1), tm, T)
    return _combine(gate4, x2, row2(out_g), y_rows)


def kernel(x, norm1_g, w_in, b_in, q_norm_g, k_norm_g, conv_w, conv_b, lru_wa, lru_ba, lru_wi,
           lru_bi, lru_lambda, w_attn_o, w_lru_o, w_out, norm2_g, w_router, b_router, w_gu, b_gu,
           w_down, b_down, final_g):
    B, S, D = x.shape
    depth = norm1_g.shape[0]
    assert depth == 1, "the fused final RMSNorm assumes a single layer"
    assert S % (TIME_CHUNKS * SUBLANES) == 0 and S % GRID_W == 0
    assert D == 2 * PACK_LINES * LANES, "a token's bf16 row must fill PACK_LINES word lines"
    out = _layer(x.reshape(B * S, D), B, S, norm1_g[0], w_in[0], b_in[0], q_norm_g[0],
                 k_norm_g[0], conv_w[0], conv_b[0], lru_wa[0], lru_ba[0], lru_wi[0], lru_bi[0],
                 lru_lambda[0], w_attn_o[0], w_lru_o[0], w_out[0], norm2_g[0], w_router[0],
                 b_router[0], w_gu[0], b_gu[0], w_down[0], b_down[0], final_g)
    return out.reshape(B, S, D)
```

```python
import functools

import jax
import jax.numpy as jnp
from jax import lax
from jax.experimental import pallas as pl
from jax.experimental.pallas import tpu as pltpu

F32 = jnp.float32
BF16 = jnp.bfloat16
I32 = jnp.int32
U32 = jnp.uint32

LANES = 128
SUBLANES = 8
PACK_LINES = 4
BF16_BITS = 16
HIGH_HALF = 0xFFFF0000
LOG2_SUBLANES = 3
SMEM_1D_TILE = 1024
VMEM_LIMIT_BYTES = 56 * 1024 * 1024

HEAD_DIM = 64
N_Q_HEADS = 8
N_KV_HEADS = 2
Q_GROUP = N_Q_HEADS // N_KV_HEADS
Q_WIDTH = N_Q_HEADS * HEAD_DIM
KV_WIDTH = N_KV_HEADS * HEAD_DIM
AXIS_DIM = HEAD_DIM // 2
ROT_HALF = AXIS_DIM // 2
ROPE_THETA = 10000.0
GRID_W = 64
RG_C = 8.0
N_EXPERTS = 32
TOP_K = 4
SWIGLU_LIMIT = 7.0
SWIGLU_ALPHA = 1.702
NORM_EPS = 1e-6
LOG2E = 1.4426950408889634
NEG_BIG = -1e30
TINY = 1e-30

ROW_TILE = 512
ATT_TQ = 1024
ATT_TK = 512
LRU_CT = 256
LRU_ROWS = 512
TIME_CHUNKS = SUBLANES
FFN_TM = 512
FFN_SLICES = 4
DMA_TILE = 256


def _cparams(*sem):
    return pltpu.CompilerParams(dimension_semantics=sem, vmem_limit_bytes=VMEM_LIMIT_BYTES)


def _rms(x, g):
    return x * lax.rsqrt(jnp.mean(x * x, axis=-1, keepdims=True) + NORM_EPS) * g


def _qkv_kernel(x_ref, g1_ref, w_ref, b_ref, cos_ref, sin_ref, qg_ref, kg_ref, hsum_ref,
                q_ref, k_ref, v_ref):
    h = _rms(x_ref[...], g1_ref[...]).astype(BF16)
    p = jnp.dot(h, w_ref[...], preferred_element_type=F32) + b_ref[...]
    cos = cos_ref[...]
    sin = sin_ref[...]
    hsum = hsum_ref[...]
    lane = lax.broadcasted_iota(I32, cos.shape, 1)
    first_half = (lane % AXIS_DIM) < ROT_HALF

    def norm_rope(c, gain):
        sq = c * c
        hi = sq.astype(BF16)
        lo = (sq - hi.astype(F32)).astype(BF16)
        ms = (jnp.dot(hi, hsum, preferred_element_type=F32)
              + jnp.dot(lo, hsum, preferred_element_type=F32)) * (1.0 / HEAD_DIM)
        y = c * lax.rsqrt(ms + NORM_EPS) * gain
        partner = jnp.where(first_half, pltpu.roll(y, LANES - ROT_HALF, 1),
                            pltpu.roll(y, ROT_HALF, 1))
        return y * cos + partner * sin

    qg = qg_ref[...]
    for c in range(Q_WIDTH // LANES):
        y = norm_rope(p[:, c * LANES:(c + 1) * LANES], qg) * (HEAD_DIM ** -0.5 * LOG2E)
        yt = y.T.astype(BF16)
        q_ref[0, 2 * c] = yt[:HEAD_DIM]
        q_ref[0, 2 * c + 1] = yt[HEAD_DIM:]
    yk = norm_rope(p[:, Q_WIDTH:Q_WIDTH + KV_WIDTH], kg_ref[...])
    k_ref[0, 0] = yk[:, :HEAD_DIM].astype(BF16)
    k_ref[0, 1] = yk[:, HEAD_DIM:].astype(BF16)
    vt = p[:, Q_WIDTH + KV_WIDTH:].T
    tm = vt.shape[1]
    ones_row = (lax.broadcasted_iota(I32, (HEAD_DIM, tm), 0) == 0).astype(BF16)
    v_ref[0, 0] = jnp.concatenate([vt[:HEAD_DIM].astype(BF16), ones_row], axis=0)
    v_ref[0, 1] = jnp.concatenate([vt[HEAD_DIM:].astype(BF16), ones_row], axis=0)


def _qkv(x2d, g1, w_qkv, b_qkv, cos_t, sin_t, qg, kg, hsum, B, S):
    T, D = x2d.shape
    tm = min(ROW_TILE, S)
    ns = S // tm
    n = w_qkv.shape[1]
    full = lambda shape: pl.BlockSpec(shape, lambda i: (0,) * len(shape))
    return pl.pallas_call(
        _qkv_kernel,
        grid=(T // tm,),
        in_specs=[
            pl.BlockSpec((tm, D), lambda i: (i, 0)),
            full((1, D)), full((D, n)), full((1, n)),
            pl.BlockSpec((tm, LANES), lambda i: (i % ns, 0)),
            pl.BlockSpec((tm, LANES), lambda i: (i % ns, 0)),
            full((1, LANES)), full((1, LANES)), full((LANES, LANES)),
        ],
        out_specs=[
            pl.BlockSpec((1, N_Q_HEADS, HEAD_DIM, tm), lambda i: (i // ns, 0, 0, i % ns)),
            pl.BlockSpec((1, N_KV_HEADS, tm, HEAD_DIM), lambda i: (i // ns, 0, i % ns, 0)),
            pl.BlockSpec((1, N_KV_HEADS, LANES, tm), lambda i: (i // ns, 0, 0, i % ns)),
        ],
        out_shape=[
            jax.ShapeDtypeStruct((B, N_Q_HEADS, HEAD_DIM, S), BF16),
            jax.ShapeDtypeStruct((B, N_KV_HEADS, S, HEAD_DIM), BF16),
            jax.ShapeDtypeStruct((B, N_KV_HEADS, LANES, S), BF16),
        ],
        compiler_params=_cparams("parallel"),
        name="qkv",
    )(x2d, g1, w_qkv, b_qkv, cos_t, sin_t, qg, kg, hsum)


def _proj_kernel(x_ref, g_ref, w_ref, b_ref, o_ref):
    h = _rms(x_ref[...], g_ref[...]).astype(BF16)
    o_ref[...] = jnp.dot(h, w_ref[...], preferred_element_type=F32) + b_ref[...]


def _proj(x2d, g, w, b):
    T, D = x2d.shape
    n = w.shape[1]
    tm = min(ROW_TILE, T)
    return pl.pallas_call(
        _proj_kernel,
        grid=(T // tm,),
        in_specs=[
            pl.BlockSpec((tm, D), lambda i: (i, 0)),
            pl.BlockSpec((1, D), lambda i: (0, 0)),
            pl.BlockSpec((D, n), lambda i: (0, 0)),
            pl.BlockSpec((1, n), lambda i: (0, 0)),
        ],
        out_specs=pl.BlockSpec((tm, n), lambda i: (i, 0)),
        out_shape=jax.ShapeDtypeStruct((T, n), F32),
        compiler_params=_cparams("parallel"),
        name="proj",
    )(x2d, g, w, b)


def _attn_kernel(q_ref, k_ref, v_ref, o_ref, s0_ref, s1_ref, *, tk):
    tq = q_ref.shape[3]
    S = k_ref.shape[2]
    M = Q_GROUP * tq
    n = S // tk
    qT = jnp.concatenate([q_ref[0, h] for h in range(Q_GROUP)], axis=1)

    def scores(j, s_ref):
        off = pl.multiple_of(j * tk, tk)
        s_ref[...] = jnp.dot(k_ref[0, 0, pl.ds(off, tk), :], qT, preferred_element_type=F32)

    def absorb(j, s_ref, carry):
        m, acc = carry
        off = pl.multiple_of(j * tk, tk)
        vc = v_ref[0, 0, :, pl.ds(off, tk)]
        s = s_ref[...]
        m_new = jnp.maximum(m, jnp.max(s, axis=0, keepdims=True))
        alpha = jnp.exp2(m - m_new)
        p = jnp.exp2(s - m_new).astype(BF16)
        return m_new, alpha * acc + jnp.dot(vc, p, preferred_element_type=F32)

    def pair(i, carry):
        scores(2 * i + 1, s1_ref)
        carry = absorb(2 * i, s0_ref, carry)
        scores(2 * i + 2, s0_ref)
        return absorb(2 * i + 1, s1_ref, carry)

    scores(0, s0_ref)
    carry = (jnp.full((1, M), NEG_BIG, F32), jnp.zeros((LANES, M), F32))
    carry = lax.fori_loop(0, n // 2 - 1, pair, carry)
    scores(n - 1, s1_ref)
    carry = absorb(n - 2, s0_ref, carry)
    _, acc = absorb(n - 1, s1_ref, carry)
    outT = acc[:HEAD_DIM] / acc[HEAD_DIM:HEAD_DIM + 1]
    stacked = jnp.concatenate(
        [outT[:, g * tq:(g + 1) * tq] for g in range(Q_GROUP)], axis=0)
    o_ref[...] = stacked.T.astype(BF16)


def _attn(qT, k, vT):
    B, _, _, S = qT.shape
    tq = min(ATT_TQ, S)
    tk = min(ATT_TK, S)
    nq = S // tq
    assert (S // tk) % 2 == 0, "key chunks are processed in pairs"
    score_buf = pltpu.VMEM((tk, Q_GROUP * tq), F32)
    return pl.pallas_call(
        functools.partial(_attn_kernel, tk=tk),
        grid=(B, N_KV_HEADS, nq),
        in_specs=[
            pl.BlockSpec((1, Q_GROUP, HEAD_DIM, tq), lambda b, g, i: (b, g, 0, i)),
            pl.BlockSpec((1, 1, S, HEAD_DIM), lambda b, g, i: (b, g, 0, 0)),
            pl.BlockSpec((1, 1, LANES, S), lambda b, g, i: (b, g, 0, 0)),
        ],
        out_specs=pl.BlockSpec((tq, Q_GROUP * HEAD_DIM), lambda b, g, i: (b * nq + i, g)),
        out_shape=jax.ShapeDtypeStruct((B * S, Q_WIDTH), BF16),
        scratch_shapes=[score_buf, score_buf],
        compiler_params=_cparams("parallel", "parallel", "parallel"),
        name="attn",
    )(qT, k, vT)


def _shift_chunks(v, down):
    row = lax.broadcasted_iota(I32, v.shape, 0)
    if down:
        return jnp.where(row == 0, 0.0, pltpu.roll(v, 1, 0))
    return jnp.where(row == SUBLANES - 1, 0.0, pltpu.roll(v, SUBLANES - 1, 0))


def _lru_kernel(xr_ref, yr_ref, cw_ref, cb_ref, wg_ref, bg_ref, lam_ref, o_ref,
                xe_ref, af_ref, uf_ref, ab_ref, ub_ref, *, rows):
    S, ct = xr_ref.shape
    lc = S // TIME_CHUNKS
    halo = SUBLANES

    ng = ct // LANES

    def xe_rows(r0, n):
        return jnp.concatenate([xe_ref[g, pl.ds(r0, n), :] for g in range(ng)], axis=1)

    def xe_store(r0, n, val):
        for g in range(ng):
            xe_ref[g, pl.ds(r0, n), :] = val[:, g * LANES:(g + 1) * LANES]

    for s in range(TIME_CHUNKS):
        for g in range(ng):
            xe_ref[g, pl.ds(halo + s, lc, stride=SUBLANES), :] = (
                xr_ref[pl.ds(s * lc, lc), g * LANES:(g + 1) * LANES])
    xe_store(0, halo, _shift_chunks(xe_rows(S, halo), True))
    first = xe_rows(halo, halo)
    second = xe_rows(2 * halo, halo)
    xe_store(S + halo, halo, _shift_chunks(first, False))
    xe_store(S + 2 * halo, halo, _shift_chunks(second, False))

    cw = cw_ref[...]
    cb = cb_ref[...]
    lam = lam_ref[...]
    log_sig = jnp.minimum(lam, 0.0) - jnp.log(1.0 + jnp.exp(-jnp.abs(lam)))
    c_half = (0.5 * RG_C * LOG2E) * log_sig
    wg = wg_ref[0]
    bg = bg_ref[0]

    def gate_chunk(i, _):
        r0 = pl.multiple_of(i * rows, rows)
        xc = cb
        for j in range(4):
            xc = xc + cw[j:j + 1, :] * xe_rows(r0 + j * halo, rows)
        t = jnp.tanh(jnp.dot(xc.astype(BF16), wg, preferred_element_type=F32) + bg)
        x_half = 0.5 * xc
        for d, (a_ref, u_ref) in enumerate(((af_ref, uf_ref), (ab_ref, ub_ref))):
            t_r = t[:, (2 * d) * ct:(2 * d + 1) * ct]
            t_i = t[:, (2 * d + 1) * ct:(2 * d + 2) * ct]
            ch = c_half[d:d + 1, :]
            a = jnp.exp2(t_r * ch + ch)
            a_ref[pl.ds(r0, rows), :] = a
            v = 1.0 - a * a
            root = jnp.maximum(v, 0.0) * lax.rsqrt(jnp.maximum(v, TINY))
            u_ref[pl.ds(r0, rows), :] = root * ((t_i + 1.0) * x_half)
        return 0

    lax.fori_loop(0, S // rows, gate_chunk, 0)

    unroll = 8

    def scan_body(i, carry):
        hf, pf, hb, pb = carry
        for k in range(unroll):
            tf = i * unroll + k
            rf = pl.multiple_of(tf * SUBLANES, SUBLANES)
            rb = pl.multiple_of((lc - 1 - tf) * SUBLANES, SUBLANES)
            a = af_ref[pl.ds(rf, SUBLANES), :]
            hf = a * hf + uf_ref[pl.ds(rf, SUBLANES), :]
            pf = a * pf
            uf_ref[pl.ds(rf, SUBLANES), :] = hf
            af_ref[pl.ds(rf, SUBLANES), :] = pf
            a = ab_ref[pl.ds(rb, SUBLANES), :]
            hb = a * hb + ub_ref[pl.ds(rb, SUBLANES), :]
            pb = a * pb
            ub_ref[pl.ds(rb, SUBLANES), :] = hb
            ab_ref[pl.ds(rb, SUBLANES), :] = pb
        return hf, pf, hb, pb

    zero = jnp.zeros((SUBLANES, ct), F32)
    one = jnp.ones((SUBLANES, ct), F32)
    hf_end, pf_end, hb_end, pb_end = lax.fori_loop(0, lc // unroll, scan_body,
                                                   (zero, one, zero, one))

    def chunk_carry(h_end, p_end, down):
        f = h_end
        for _ in range(TIME_CHUNKS - 1):
            f = h_end + p_end * _shift_chunks(f, down)
        return _shift_chunks(f, down)

    cin_f = chunk_carry(hf_end, pf_end, True)
    cin_b = chunk_carry(hb_end, pb_end, False)

    def fix_chunk(i, _):
        r0 = pl.multiple_of(i * rows, rows)
        reps = rows // SUBLANES
        hf = uf_ref[pl.ds(r0, rows), :] + af_ref[pl.ds(r0, rows), :] * jnp.tile(cin_f, (reps, 1))
        hb = ub_ref[pl.ds(r0, rows), :] + ab_ref[pl.ds(r0, rows), :] * jnp.tile(cin_b, (reps, 1))
        xe_store(r0, rows, hf + hb)
        return 0

    lax.fori_loop(0, S // rows, fix_chunk, 0)

    for s in range(TIME_CHUNKS):
        hsum = jnp.concatenate(
            [xe_ref[g, pl.ds(s, lc, stride=SUBLANES), :] for g in range(ng)], axis=1)
        y = yr_ref[pl.ds(s * lc, lc), :]
        o_ref[pl.ds(s * lc, lc), :] = (hsum * jax.nn.gelu(y, approximate=True)).astype(BF16)


def _lru(proj, conv_w, conv_b, wg, bg, lam, B, S, C):
    ct = LRU_CT
    nct = C // ct
    rows = min(LRU_ROWS, S)
    return pl.pallas_call(
        functools.partial(_lru_kernel, rows=rows),
        grid=(B, nct),
        in_specs=[
            pl.BlockSpec((S, ct), lambda b, c: (b, c)),
            pl.BlockSpec((S, ct), lambda b, c: (b, nct + c)),
            pl.BlockSpec((4, ct), lambda b, c: (0, c)),
            pl.BlockSpec((1, ct), lambda b, c: (0, c)),
            pl.BlockSpec((1, ct, 4 * ct), lambda b, c: (c, 0, 0)),
            pl.BlockSpec((1, 1, 4 * ct), lambda b, c: (c, 0, 0)),
            pl.BlockSpec((2, ct), lambda b, c: (0, c)),
        ],
        out_specs=pl.BlockSpec((S, ct), lambda b, c: (b, c)),
        out_shape=jax.ShapeDtypeStruct((B * S, C), BF16),
        scratch_shapes=[
            pltpu.VMEM((ct // LANES, S + 3 * SUBLANES, LANES), F32),
            pltpu.VMEM((S, ct), F32), pltpu.VMEM((S, ct), F32),
            pltpu.VMEM((S, ct), F32), pltpu.VMEM((S, ct), F32),
        ],
        compiler_params=_cparams("parallel", "parallel"),
        name="lru",
    )(proj, proj, conv_w, conv_b, wg, bg, lam)


def _merge_kernel(x_ref, attn_ref, rec_ref, gl_ref, wa_ref, wr_ref, wo_ref, g2_ref,
                  rw_hi_ref, rw_lo_ref, rb_ref, x2_ref, h2_ref, lg_ref):
    D = x_ref.shape[1]
    gl = gl_ref[...]
    ma = jnp.dot(attn_ref[...], wa_ref[...], preferred_element_type=F32)
    mr = jnp.dot(rec_ref[...], wr_ref[...], preferred_element_type=F32)
    merged = jax.nn.sigmoid(gl[:, :D]) * ma + jax.nn.sigmoid(gl[:, D:]) * mr
    x2 = x_ref[...] + jnp.dot(merged.astype(BF16), wo_ref[...], preferred_element_type=F32)
    x2_ref[...] = x2
    h2 = _rms(x2, g2_ref[...])
    hi = h2.astype(BF16)
    lo = (h2 - hi.astype(F32)).astype(BF16)
    lg_ref[...] = (jnp.dot(hi, rw_hi_ref[...], preferred_element_type=F32)
                   + jnp.dot(lo, rw_hi_ref[...], preferred_element_type=F32)
                   + jnp.dot(hi, rw_lo_ref[...], preferred_element_type=F32)) + rb_ref[...]
    tm = h2.shape[0]
    bits = lax.bitcast_convert_type(hi.astype(F32), U32)
    words = bits[:, :D // 2] | (bits[:, D // 2:] >> BF16_BITS)
    for j in range(PACK_LINES):
        h2_ref[pl.ds(j, tm, stride=SUBLANES), :] = words[:, j * LANES:(j + 1) * LANES]
    for j in range(PACK_LINES, SUBLANES):
        h2_ref[pl.ds(j, tm, stride=SUBLANES), :] = jnp.zeros((tm, LANES), U32)


def _merge(x2d, attn, rec, proj, wa, wr, wo, g2, rw_hi, rw_lo, rb):
    T, D = x2d.shape
    tm = min(ROW_TILE, T)
    nj = D // LANES
    full = lambda a: pl.BlockSpec(a.shape, lambda i: (0,) * a.ndim)
    return pl.pallas_call(
        _merge_kernel,
        grid=(T // tm,),
        in_specs=[
            pl.BlockSpec((tm, D), lambda i: (i, 0)),
            pl.BlockSpec((tm, Q_WIDTH), lambda i: (i, 0)),
            pl.BlockSpec((tm, D), lambda i: (i, 0)),
            pl.BlockSpec((tm, 2 * D), lambda i: (i, 1)),
            full(wa), full(wr), full(wo), full(g2), full(rw_hi), full(rw_lo), full(rb),
        ],
        out_specs=[
            pl.BlockSpec((tm, D), lambda i: (i, 0)),
            pl.BlockSpec((tm * nj, LANES), lambda i: (i, 0)),
            pl.BlockSpec((tm, LANES), lambda i: (i, 0)),
        ],
        out_shape=[
            jax.ShapeDtypeStruct((T, D), F32),
            jax.ShapeDtypeStruct((T * nj, LANES), U32),
            jax.ShapeDtypeStruct((T, LANES), F32),
        ],
        compiler_params=_cparams("parallel"),
        name="merge",
    )(x2d, attn, rec, proj, wa, wr, wo, g2, rw_hi, rw_lo, rb)


def _route_kernel(lg_ref, gate_ref, gate_t_ref, eidx_t_ref, rank_t_ref, cnt_ref):
    i = pl.program_id(0)

    @pl.when(i == 0)
    def _():
        cnt_ref[...] = jnp.zeros_like(cnt_ref)

    tm = lg_ref.shape[0]
    work = lg_ref[...].T[:N_EXPERTS]
    expert = lax.broadcasted_iota(I32, work.shape, 0)
    sels, vals, idxs = [], [], []
    for _ in range(TOP_K):
        mx = jnp.max(work, axis=0, keepdims=True)
        idx = jnp.min(jnp.where(work == mx, expert, N_EXPERTS), axis=0, keepdims=True)
        sel = expert == idx
        sels.append(sel)
        vals.append(mx)
        idxs.append(idx)
        work = jnp.where(sel, NEG_BIG, work)
    ex = [jnp.exp(v - vals[0]) for v in vals]
    den = ex[0] + ex[1] + ex[2] + ex[3]
    member = (sels[0] | sels[1] | sels[2] | sels[3]).astype(BF16)
    r = lax.broadcasted_iota(I32, (tm, tm), 0)
    c = lax.broadcasted_iota(I32, (tm, tm), 1)
    tri = (r < c).astype(BF16)
    base = cnt_ref[:, 0:1]
    before = jnp.dot(member, tri, preferred_element_type=F32) + base
    ranks = [jnp.sum(jnp.where(sels[k], before, 0.0), axis=0, keepdims=True)
             for k in range(TOP_K)]
    gates = [ex[k] / den for k in range(TOP_K)]
    pad = SUBLANES - TOP_K
    gate_t = jnp.concatenate(gates + [jnp.zeros((pad, tm), F32)], axis=0)
    gate_t_ref[...] = gate_t
    eidx_t_ref[...] = jnp.concatenate(idxs + [jnp.zeros((pad, tm), I32)], axis=0)
    rank_t_ref[...] = jnp.concatenate(ranks + [jnp.zeros((pad, tm), F32)], axis=0).astype(I32)
    gate_ref[...] = jnp.concatenate(
        [gate_t, jnp.zeros((LANES - SUBLANES, tm), F32)], axis=0).T
    cnt_ref[...] = jnp.broadcast_to(
        base + jnp.sum(member.astype(F32), axis=1, keepdims=True), cnt_ref.shape)


def _route(logits):
    T = logits.shape[0]
    tm = min(ROW_TILE, T)
    spec = pl.BlockSpec((tm, LANES), lambda i: (i, 0))
    spec_t = pl.BlockSpec((SUBLANES, tm), lambda i: (0, i))
    return pl.pallas_call(
        _route_kernel,
        grid=(T // tm,),
        in_specs=[spec],
        out_specs=[spec, spec_t, spec_t, spec_t,
                   pl.BlockSpec((N_EXPERTS, LANES), lambda i: (0, 0))],
        out_shape=[
            jax.ShapeDtypeStruct((T, LANES), F32),
            jax.ShapeDtypeStruct((SUBLANES, T), F32),
            jax.ShapeDtypeStruct((SUBLANES, T), I32),
            jax.ShapeDtypeStruct((SUBLANES, T), I32),
            jax.ShapeDtypeStruct((N_EXPERTS, LANES), F32),
        ],
        compiler_params=_cparams("arbitrary"),
        name="route",
    )(logits)


def _tile(ref, row):
    return ref.at[pl.ds(pl.multiple_of(row * SUBLANES, SUBLANES), SUBLANES)]


def _dispatch_kernel(dest_ref, last_blk_ref, nused_ref, h2_ref, buf_ref,
                     src_ref, pad_ref, sem, zsem, *, n_tokens):
    i = pl.program_id(0)
    tm = h2_ref.shape[0] // SUBLANES
    blk = pad_ref.shape[0] // (2 * SUBLANES)
    nblocks = buf_ref.shape[0] // (blk * SUBLANES)

    @pl.when(i == 0)
    def _():
        li = lax.broadcasted_iota(I32, pad_ref.shape, 0)
        trash = (TOP_K * n_tokens + (li >> LOG2_SUBLANES)).astype(U32)
        pad_ref[...] = jnp.where((li & (SUBLANES - 1)) >= PACK_LINES, trash, jnp.uint32(0))
        rows = blk * SUBLANES

        def pad_block(b):
            half = pl.multiple_of(lax.rem(b, 2) * rows, rows)
            return pltpu.make_async_copy(
                pad_ref.at[pl.ds(half, rows)],
                buf_ref.at[pl.ds(pl.multiple_of(b * rows, rows), rows)], zsem)

        def expert_put(e, _):
            pl.when(last_blk_ref[e] >= 0)(lambda: pad_block(last_blk_ref[e]).start())
            return 0

        def expert_done(e, _):
            pl.when(last_blk_ref[e] >= 0)(lambda: pad_block(last_blk_ref[e]).wait())
            return 0

        def tail_put(b, _):
            pad_block(b).start()
            return 0

        def tail_done(b, _):
            pad_block(b).wait()
            return 0

        lax.fori_loop(0, N_EXPERTS, expert_put, 0)
        lax.fori_loop(nused_ref[0], nblocks, tail_put, 0)
        lax.fori_loop(0, N_EXPERTS, expert_done, 0)
        lax.fori_loop(nused_ref[0], nblocks, tail_done, 0)

    parity = lax.rem(i, 2)
    half = parity * TOP_K
    li = lax.broadcasted_iota(I32, h2_ref.shape, 0)
    tok = i * tm + (li >> LOG2_SUBLANES)
    is_id = (li & (SUBLANES - 1)) >= PACK_LINES
    data = h2_ref[...]
    for k in range(TOP_K):
        src_ref[half + k] = jnp.where(is_id, (k * n_tokens + tok).astype(U32), data)

    def issue(r, _):
        for k in range(TOP_K):
            pltpu.make_async_copy(_tile(src_ref.at[half + k], r),
                                  _tile(buf_ref, dest_ref[k * (tm + 1) + r]),
                                  sem.at[parity]).start(priority=k % 2)
        return 0

    lax.fori_loop(0, tm, issue, 0)

    def wait_step(which):
        for k in range(TOP_K):
            pltpu.make_async_copy(src_ref.at[k], buf_ref.at[pl.ds(0, tm * SUBLANES)],
                                  sem.at[which]).wait()

    pl.when(i >= 1)(functools.partial(wait_step, 1 - parity))
    pl.when(i == pl.num_programs(0) - 1)(functools.partial(wait_step, parity))


def _dispatch(dest, last_blk, nused, h2_rows, n_rows, blk):
    T = h2_rows.shape[0] // SUBLANES
    tm = min(DMA_TILE, T)
    nt = T // tm
    words = pl.cdiv(TOP_K * (tm + 1), SMEM_1D_TILE) * SMEM_1D_TILE
    d = jnp.pad(dest.reshape(TOP_K, nt, tm).transpose(1, 0, 2), ((0, 0), (0, 0), (0, 1)))
    d = jnp.pad(d.reshape(nt, TOP_K * (tm + 1)), ((0, 0), (0, words - TOP_K * (tm + 1))))
    dest_flat = d.reshape(nt * words)
    smem = pl.BlockSpec(memory_space=pltpu.SMEM)
    return pl.pallas_call(
        functools.partial(_dispatch_kernel, n_tokens=T),
        grid=(nt,),
        in_specs=[
            pl.BlockSpec((words,), lambda i: (i,), memory_space=pltpu.SMEM),
            smem, smem,
            pl.BlockSpec((tm * SUBLANES, LANES), lambda i: (i, 0)),
        ],
        out_specs=pl.BlockSpec(memory_space=pl.ANY),
        out_shape=jax.ShapeDtypeStruct((n_rows * SUBLANES, LANES), U32),
        scratch_shapes=[pltpu.VMEM((2 * TOP_K, tm * SUBLANES, LANES), U32),
                        pltpu.VMEM((2 * blk * SUBLANES, LANES), U32),
                        pltpu.SemaphoreType.DMA((2,)), pltpu.SemaphoreType.DMA(())],
        compiler_params=_cparams("arbitrary"),
        name="dispatch",
    )(dest_flat, last_blk, nused, h2_rows)


def _ffn_kernel(blk_exp_ref, nused_ref, x_ref, wgu_ref, bgu_ref, wd_ref, bd_ref, y_ref,
                wgu_bf_ref, wd_bf_ref, out_ref, idv_ref, ids_ref, ssem, isem, *, n_tokens):
    b = pl.program_id(0)
    nb = pl.num_programs(0)
    tm = x_ref.shape[0] // SUBLANES
    groups = tm // LANES
    F = wd_ref.shape[1]
    nused = nused_ref[0]
    used = b < nused
    slot = lax.rem(b, 2)
    prev = 1 - slot
    new_expert = (b == 0) | (blk_exp_ref[b] != blk_exp_ref[jnp.maximum(b - 1, 0)])

    def send_row(s, r_hi, r_lo, row, priority):
        rid = ids_ref[s * SUBLANES + r_hi, r_lo]
        pltpu.make_async_copy(_tile(out_ref, s * tm + row), _tile(y_ref, rid),
                              ssem.at[s]).start(priority=priority)

    def send_rows_unrolled(s, lo, hi):
        for r in range(lo, hi):
            send_row(s, r // LANES, r % LANES, r, r % 2)

    def send_block_rolled(s):
        def body(g, _):
            def inner(l, _):
                send_row(s, g, l, g * LANES + l, 0)
                return 0
            return lax.fori_loop(0, LANES, inner, 0)
        lax.fori_loop(0, groups, body, 0)

    def wait_block(s):
        rows = tm * SUBLANES
        pltpu.make_async_copy(out_ref.at[pl.ds(0, rows)], y_ref.at[pl.ds(0, rows)],
                              ssem.at[s]).wait()

    def id_copy(s):
        return pltpu.make_async_copy(
            idv_ref, ids_ref.at[pl.ds(pl.multiple_of(s * SUBLANES, SUBLANES), SUBLANES)], isem)

    @pl.when(b == 0)
    def _():
        out_ref[...] = jnp.zeros_like(out_ref)
        rows = tm * SUBLANES
        pltpu.make_async_copy(out_ref.at[pl.ds(0, rows)],
                              y_ref.at[pl.ds(TOP_K * n_tokens * SUBLANES, rows)],
                              ssem.at[0]).start()
        pos = (lax.broadcasted_iota(I32, idv_ref.shape, 0) * LANES
               + lax.broadcasted_iota(I32, idv_ref.shape, 1))
        idv_ref[...] = TOP_K * n_tokens + tm + jnp.minimum(pos, tm - 1)
        id_copy(1).start()
        id_copy(1).wait()

    @pl.when((b >= 1) & (b <= nused))
    def _():
        id_copy(prev).wait()

    @pl.when(used)
    def _():
        idl = x_ref[pl.ds(PACK_LINES, tm, stride=SUBLANES), :].astype(I32)
        rr = lax.broadcasted_iota(I32, idl.shape, 0)
        ll = lax.broadcasted_iota(I32, idl.shape, 1)
        diag = jnp.where((rr & (LANES - 1)) == ll, idl, 0)
        idv_ref[pl.ds(0, groups), :] = jnp.sum(diag.reshape(groups, LANES, LANES), axis=1)
        id_copy(slot).start()

    @pl.when(used & new_expert)
    def _():
        wgu_bf_ref[...] = wgu_ref[0].astype(BF16)
        wd_bf_ref[...] = wd_ref[0].astype(BF16)

    def block(slot, prev):
        words = [x_ref[pl.ds(j, tm, stride=SUBLANES), :] for j in range(PACK_LINES)]
        hi = [lax.bitcast_convert_type(w & jnp.uint32(HIGH_HALF), F32) for w in words]
        lo = [lax.bitcast_convert_type(w << BF16_BITS, F32) for w in words]
        x = jnp.concatenate(hi + lo, axis=1).astype(BF16)
        acts = []
        for c in range(FFN_SLICES):
            send_rows_unrolled(prev, c * tm // FFN_SLICES, (c + 1) * tm // FFN_SLICES)
            w = F // FFN_SLICES
            g = jnp.dot(x, wgu_bf_ref[:, c * w:(c + 1) * w], preferred_element_type=F32)
            u = jnp.dot(x, wgu_bf_ref[:, F + c * w:F + (c + 1) * w],
                        preferred_element_type=F32)
            g = jnp.minimum(g + bgu_ref[0, :, c * w:(c + 1) * w], SWIGLU_LIMIT)
            u = jnp.clip(u + bgu_ref[0, :, F + c * w:F + (c + 1) * w],
                         -SWIGLU_LIMIT, SWIGLU_LIMIT)
            acts.append((g * jax.nn.sigmoid(SWIGLU_ALPHA * g) * (u + 1.0)).astype(BF16))
        act = jnp.concatenate(acts, axis=1)
        y = jnp.dot(act, wd_bf_ref[...], preferred_element_type=F32) + bd_ref[0]
        wait_block(slot)
        base = slot * (tm * SUBLANES)
        for j in range(SUBLANES):
            out_ref[pl.ds(base + j, tm, stride=SUBLANES), :] = y[:, j * LANES:(j + 1) * LANES]

    for parity in range(2):
        pl.when(used & (slot == parity))(functools.partial(block, parity, 1 - parity))

    @pl.when(b == nused)
    def _():
        wait_block(slot)
        send_block_rolled(prev)
        wait_block(prev)

    @pl.when(used & (b == nb - 1))
    def _():
        wait_block(prev)
        id_copy(slot).wait()
        send_block_rolled(slot)
        wait_block(slot)


def _ffn(blk_exp, nused, buf, wgu, bgu, wd, bd, tm, n_tokens):
    nb = buf.shape[0] // (tm * SUBLANES)
    D = wgu.shape[1]
    F = wd.shape[1]

    def xmap(b, be, nu):
        return (jnp.minimum(b, nu[0] - 1), 0)

    def wmap(b, be, nu):
        return (be[jnp.minimum(b, nu[0] - 1)], 0, 0)

    grid_spec = pltpu.PrefetchScalarGridSpec(
        num_scalar_prefetch=2,
        grid=(nb,),
        in_specs=[
            pl.BlockSpec((tm * SUBLANES, LANES), xmap),
            pl.BlockSpec((1, D, 2 * F), wmap),
            pl.BlockSpec((1, 1, 2 * F), wmap),
            pl.BlockSpec((1, F, D), wmap),
            pl.BlockSpec((1, 1, D), wmap),
        ],
        out_specs=pl.BlockSpec(memory_space=pl.ANY),
        scratch_shapes=[
            pltpu.VMEM((D, 2 * F), BF16), pltpu.VMEM((F, D), BF16),
            pltpu.VMEM((2 * tm * SUBLANES, LANES), F32),
            pltpu.VMEM((SUBLANES, LANES), I32),
            pltpu.SMEM((2 * SUBLANES, LANES), I32),
            pltpu.SemaphoreType.DMA((2,)), pltpu.SemaphoreType.DMA(()),
        ],
    )
    return pl.pallas_call(
        functools.partial(_ffn_kernel, n_tokens=n_tokens),
        grid_spec=grid_spec,
        out_shape=jax.ShapeDtypeStruct(((TOP_K * n_tokens + 2 * tm) * SUBLANES, LANES), F32),
        compiler_params=_cparams("arbitrary"),
        name="ffn",
    )(blk_exp, nused, buf, wgu, bgu, wd, bd)


def _combine_kernel(gate_ref, x2_ref, fg_ref, *refs):
    y_refs, o_ref = refs[:TOP_K], refs[TOP_K]
    tm, D = x2_ref.shape
    nj = D // LANES
    gate = gate_ref[...]
    cols = []
    for j in range(nj):
        acc = None
        for k in range(TOP_K):
            part = gate[:, k:k + 1] * y_refs[k][pl.ds(j, tm, stride=nj), :]
            acc = part if acc is None else acc + part
        cols.append(acc)
    y = jnp.concatenate(cols, axis=1)
    o_ref[...] = _rms(x2_ref[...] + y, fg_ref[...])


def _combine(gate4, x2, fg, y_rows):
    T, D = x2.shape
    tm = min(ROW_TILE, T)
    nj = D // LANES
    nt = T // tm

    def slot_spec(k):
        return pl.BlockSpec((tm * nj, LANES), lambda i: (k * nt + i, 0))

    return pl.pallas_call(
        _combine_kernel,
        grid=(nt,),
        in_specs=[
            pl.BlockSpec((tm, LANES), lambda i: (i, 0)),
            pl.BlockSpec((tm, D), lambda i: (i, 0)),
            pl.BlockSpec((1, D), lambda i: (0, 0)),
        ] + [slot_spec(k) for k in range(TOP_K)],
        out_specs=pl.BlockSpec((tm, D), lambda i: (i, 0)),
        out_shape=jax.ShapeDtypeStruct((T, D), F32),
        compiler_params=_cparams("parallel"),
        name="combine",
    )(gate4, x2, fg, *([y_rows] * TOP_K))


def _rope_tables(S):
    rows = S // GRID_W
    row = jnp.repeat(jnp.arange(rows, dtype=I32), GRID_W).astype(F32)
    col = jnp.tile(jnp.arange(GRID_W, dtype=I32), rows).astype(F32)
    inv = ROPE_THETA ** (-jnp.arange(0, AXIS_DIM, 2, dtype=F32) / AXIS_DIM)
    ang_r = row[:, None] * inv[None, :]
    ang_c = col[:, None] * inv[None, :]
    cos = jnp.concatenate([jnp.cos(ang_r)] * 2 + [jnp.cos(ang_c)] * 2, axis=1)
    sin = jnp.concatenate([-jnp.sin(ang_r), jnp.sin(ang_r), -jnp.sin(ang_c), jnp.sin(ang_c)], axis=1)
    return jnp.tile(cos, (1, 2)), jnp.tile(sin, (1, 2))


def _block_diag_gates(wa, ba, wi, bi, ct):
    nb, bw = wa.shape[1], wa.shape[2]
    per = ct // bw
    nct = nb // per
    eye = jnp.eye(per, dtype=wa.dtype)

    def tiles(w):
        w = w.reshape(nct, per, bw, bw)
        return jnp.einsum('cpij,pq->cpiqj', w, eye).reshape(nct, ct, ct)

    wg = jnp.concatenate([tiles(wa[0]), tiles(wi[0]), tiles(wa[1]), tiles(wi[1])], axis=2)
    bias = lambda b: b.reshape(nct, 1, ct)
    bg = jnp.concatenate([bias(ba[0]), bias(bi[0]), bias(ba[1]), bias(bi[1])], axis=2)
    return (0.5 * wg).astype(BF16), 0.5 * bg


def _layer(x2d, B, S, norm1_g, w_in, b_in, q_norm_g, k_norm_g, conv_w, conv_b, lru_wa, lru_ba,
           lru_wi, lru_bi, lru_lambda, w_attn_o, w_lru_o, w_out, norm2_g, w_router, b_router,
           w_gu, b_gu, w_down, b_down, out_g):
    T, D = x2d.shape
    C = conv_w.shape[1]
    nqkv = Q_WIDTH + 2 * KV_WIDTH
    row2 = lambda v: v.reshape(1, -1)

    cos_t, sin_t = _rope_tables(S)
    head = jnp.arange(LANES) // HEAD_DIM
    hsum = (head[:, None] == head[None, :]).astype(BF16)
    q, k, v = _qkv(x2d, row2(norm1_g), w_in[:, :nqkv].astype(BF16), row2(b_in[:nqkv]),
                   cos_t, sin_t, row2(jnp.tile(q_norm_g, 2)), row2(jnp.tile(k_norm_g, 2)),
                   hsum, B, S)
    proj = _proj(x2d, row2(norm1_g), w_in[:, nqkv:].astype(BF16), row2(b_in[nqkv:]))
    attn = _attn(q, k, v)
    wg, bg = _block_diag_gates(lru_wa, lru_ba, lru_wi, lru_bi, LRU_CT)
    rec = _lru(proj, conv_w, row2(conv_b), wg, bg, lru_lambda, B, S, C)

    pad = LANES - N_EXPERTS
    rw = jnp.pad(w_router, ((0, 0), (0, pad)))
    rw_hi = rw.astype(BF16)
    rw_lo = (rw - rw_hi.astype(F32)).astype(BF16)
    rb = jnp.pad(b_router, (0, pad), constant_values=NEG_BIG).reshape(1, LANES)
    x2, h2_rows, logits = _merge(x2d, attn, rec, proj, w_attn_o.astype(BF16),
                                 w_lru_o.astype(BF16), w_out.astype(BF16), row2(norm2_g),
                                 rw_hi, rw_lo, rb)

    gate4, _, eidx_t, rank_t, cnt = _route(logits)
    counts = cnt[:, 0].astype(I32)
    tm = FFN_TM
    nblk = (counts + tm - 1) // tm
    pend_blk = jnp.cumsum(nblk)
    pstart = (pend_blk - nblk) * tm
    A = T * TOP_K
    nb = (A + N_EXPERTS * (tm - 1) + tm - 1) // tm
    blk_exp = jnp.minimum(
        jnp.sum(pend_blk[None, :] <= jnp.arange(nb, dtype=I32)[:, None], axis=1),
        N_EXPERTS - 1).astype(I32)
    nused = pend_blk[-1:].astype(I32)
    first_row = jnp.sum(jnp.where(eidx_t[:TOP_K, :, None] == jnp.arange(N_EXPERTS), pstart, 0),
                        axis=-1)
    dest = (first_row + rank_t[:TOP_K]).astype(I32)
    last_blk = jnp.where(nblk > 0, pend_blk - 1, -1).astype(I32)

    buf = _dispatch(dest, last_blk, nused, h2_rows, nb * tm, tm)
    y_rows = _ffn(blk_exp, nused, buf, w_gu, b_gu.reshape(N_EXPERTS, 1, -1), w_down,
                  b_down.reshape(N_EXPERTS, 1, -1), tm, T)
    return _combine(gate4, x2, row2(out_g), y_rows)


def kernel(x, norm1_g, w_in, b_in, q_norm_g, k_norm_g, conv_w, conv_b, lru_wa, lru_ba, lru_wi,
           lru_bi, lru_lambda, w_attn_o, w_lru_o, w_out, norm2_g, w_router, b_router, w_gu, b_gu,
           w_down, b_down, final_g):
    B, S, D = x.shape
    depth = norm1_g.shape[0]
    assert depth == 1, "the fused final RMSNorm assumes a single layer"
    assert S % (TIME_CHUNKS * SUBLANES) == 0 and S % GRID_W == 0
    assert D == 2 * PACK_LINES * LANES, "a token's bf16 row must fill PACK_LINES word lines"
    out = _layer(x.reshape(B * S, D), B, S, norm1_g[0], w_in[0], b_in[0], q_norm_g[0],
                 k_norm_g[0], conv_w[0], conv_b[0], lru_wa[0], lru_ba[0], lru_wi[0], lru_bi[0],
                 lru_lambda[0], w_attn_o[0], w_lru_o[0], w_out[0], norm2_g[0], w_router[0],
                 b_router[0], w_gu[0], b_gu[0], w_down[0], b_down[0], final_g)
    return out.reshape(B, S, D)
```

```python
import functools

import jax
import jax.numpy as jnp
from jax import lax
from jax.experimental import pallas as pl
from jax.experimental.pallas import tpu as pltpu

F32 = jnp.float32
BF16 = jnp.bfloat16
I32 = jnp.int32
U32 = jnp.uint32

LANES = 128
SUBLANES = 8
PACK_LINES = 4
BF16_BITS = 16
HIGH_HALF = 0xFFFF0000
LOG2_SUBLANES = 3
SMEM_1D_TILE = 1024
VMEM_LIMIT_BYTES = 56 * 1024 * 1024

HEAD_DIM = 64
N_Q_HEADS = 8
N_KV_HEADS = 2
Q_GROUP = N_Q_HEADS // N_KV_HEADS
Q_WIDTH = N_Q_HEADS * HEAD_DIM
KV_WIDTH = N_KV_HEADS * HEAD_DIM
AXIS_DIM = HEAD_DIM // 2
ROT_HALF = AXIS_DIM // 2
ROPE_THETA = 10000.0
GRID_W = 64
RG_C = 8.0
N_EXPERTS = 32
TOP_K = 4
SWIGLU_LIMIT = 7.0
SWIGLU_ALPHA = 1.702
NORM_EPS = 1e-6
LOG2E = 1.4426950408889634
NEG_BIG = -1e30
TINY = 1e-30

ROW_TILE = 512
ATT_TQ = 1024
ATT_TK = 512
LRU_CT = 256
LRU_ROWS = 512
TIME_CHUNKS = SUBLANES
FFN_TM = 512
FFN_SLICES = 4
DMA_TILE = 256
ISSUE_UNROLL = 8


def _cparams(*sem):
    return pltpu.CompilerParams(dimension_semantics=sem, vmem_limit_bytes=VMEM_LIMIT_BYTES)


def _rms(x, g):
    return x * lax.rsqrt(jnp.mean(x * x, axis=-1, keepdims=True) + NORM_EPS) * g


def _qkv_kernel(x_ref, g1_ref, w_ref, b_ref, cos_ref, sin_ref, qg_ref, kg_ref, hsum_ref,
                q_ref, k_ref, v_ref):
    h = _rms(x_ref[...], g1_ref[...]).astype(BF16)
    p = jnp.dot(h, w_ref[...], preferred_element_type=F32) + b_ref[...]
    cos = cos_ref[...]
    sin = sin_ref[...]
    hsum = hsum_ref[...]
    lane = lax.broadcasted_iota(I32, cos.shape, 1)
    first_half = (lane % AXIS_DIM) < ROT_HALF

    def norm_rope(c, gain):
        sq = c * c
        hi = sq.astype(BF16)
        lo = (sq - hi.astype(F32)).astype(BF16)
        ms = (jnp.dot(hi, hsum, preferred_element_type=F32)
              + jnp.dot(lo, hsum, preferred_element_type=F32)) * (1.0 / HEAD_DIM)
        y = c * lax.rsqrt(ms + NORM_EPS) * gain
        partner = jnp.where(first_half, pltpu.roll(y, LANES - ROT_HALF, 1),
                            pltpu.roll(y, ROT_HALF, 1))
        return y * cos + partner * sin

    qg = qg_ref[...]
    for c in range(Q_WIDTH // LANES):
        y = norm_rope(p[:, c * LANES:(c + 1) * LANES], qg) * (HEAD_DIM ** -0.5 * LOG2E)
        yt = y.T.astype(BF16)
        q_ref[0, 2 * c] = yt[:HEAD_DIM]
        q_ref[0, 2 * c + 1] = yt[HEAD_DIM:]
    yk = norm_rope(p[:, Q_WIDTH:Q_WIDTH + KV_WIDTH], kg_ref[...])
    k_ref[0, 0] = yk[:, :HEAD_DIM].astype(BF16)
    k_ref[0, 1] = yk[:, HEAD_DIM:].astype(BF16)
    vt = p[:, Q_WIDTH + KV_WIDTH:].T
    tm = vt.shape[1]
    ones_row = (lax.broadcasted_iota(I32, (HEAD_DIM, tm), 0) == 0).astype(BF16)
    v_ref[0, 0] = jnp.concatenate([vt[:HEAD_DIM].astype(BF16), ones_row], axis=0)
    v_ref[0, 1] = jnp.concatenate([vt[HEAD_DIM:].astype(BF16), ones_row], axis=0)


def _qkv(x2d, g1, w_qkv, b_qkv, cos_t, sin_t, qg, kg, hsum, B, S):
    T, D = x2d.shape
    tm = min(ROW_TILE, S)
    ns = S // tm
    n = w_qkv.shape[1]
    full = lambda shape: pl.BlockSpec(shape, lambda i: (0,) * len(shape))
    return pl.pallas_call(
        _qkv_kernel,
        grid=(T // tm,),
        in_specs=[
            pl.BlockSpec((tm, D), lambda i: (i, 0)),
            full((1, D)), full((D, n)), full((1, n)),
            pl.BlockSpec((tm, LANES), lambda i: (i % ns, 0)),
            pl.BlockSpec((tm, LANES), lambda i: (i % ns, 0)),
            full((1, LANES)), full((1, LANES)), full((LANES, LANES)),
        ],
        out_specs=[
            pl.BlockSpec((1, N_Q_HEADS, HEAD_DIM, tm), lambda i: (i // ns, 0, 0, i % ns)),
            pl.BlockSpec((1, N_KV_HEADS, tm, HEAD_DIM), lambda i: (i // ns, 0, i % ns, 0)),
            pl.BlockSpec((1, N_KV_HEADS, LANES, tm), lambda i: (i // ns, 0, 0, i % ns)),
        ],
        out_shape=[
            jax.ShapeDtypeStruct((B, N_Q_HEADS, HEAD_DIM, S), BF16),
            jax.ShapeDtypeStruct((B, N_KV_HEADS, S, HEAD_DIM), BF16),
            jax.ShapeDtypeStruct((B, N_KV_HEADS, LANES, S), BF16),
        ],
        compiler_params=_cparams("parallel"),
        name="qkv",
    )(x2d, g1, w_qkv, b_qkv, cos_t, sin_t, qg, kg, hsum)


def _proj_kernel(x_ref, g_ref, w_ref, b_ref, o_ref):
    h = _rms(x_ref[...], g_ref[...]).astype(BF16)
    o_ref[...] = jnp.dot(h, w_ref[...], preferred_element_type=F32) + b_ref[...]


def _proj(x2d, g, w, b):
    T, D = x2d.shape
    n = w.shape[1]
    tm = min(ROW_TILE, T)
    return pl.pallas_call(
        _proj_kernel,
        grid=(T // tm,),
        in_specs=[
            pl.BlockSpec((tm, D), lambda i: (i, 0)),
            pl.BlockSpec((1, D), lambda i: (0, 0)),
            pl.BlockSpec((D, n), lambda i: (0, 0)),
            pl.BlockSpec((1, n), lambda i: (0, 0)),
        ],
        out_specs=pl.BlockSpec((tm, n), lambda i: (i, 0)),
        out_shape=jax.ShapeDtypeStruct((T, n), F32),
        compiler_params=_cparams("parallel"),
        name="proj",
    )(x2d, g, w, b)


def _attn_kernel(q_ref, k_ref, v_ref, o_ref, s0_ref, s1_ref, *, tk):
    tq = q_ref.shape[3]
    S = k_ref.shape[2]
    M = Q_GROUP * tq
    n = S // tk
    qT = jnp.concatenate([q_ref[0, h] for h in range(Q_GROUP)], axis=1)

    def scores(j, s_ref):
        off = pl.multiple_of(j * tk, tk)
        s_ref[...] = jnp.dot(k_ref[0, 0, pl.ds(off, tk), :], qT, preferred_element_type=F32)

    def absorb(j, s_ref, carry):
        m, acc = carry
        off = pl.multiple_of(j * tk, tk)
        vc = v_ref[0, 0, :, pl.ds(off, tk)]
        s = s_ref[...]
        m_new = jnp.maximum(m, jnp.max(s, axis=0, keepdims=True))
        alpha = jnp.exp2(m - m_new)
        p = jnp.exp2(s - m_new).astype(BF16)
        return m_new, alpha * acc + jnp.dot(vc, p, preferred_element_type=F32)

    def pair(i, carry):
        scores(2 * i + 1, s1_ref)
        carry = absorb(2 * i, s0_ref, carry)
        scores(2 * i + 2, s0_ref)
        return absorb(2 * i + 1, s1_ref, carry)

    scores(0, s0_ref)
    carry = (jnp.full((1, M), NEG_BIG, F32), jnp.zeros((LANES, M), F32))
    carry = lax.fori_loop(0, n // 2 - 1, pair, carry)
    scores(n - 1, s1_ref)
    carry = absorb(n - 2, s0_ref, carry)
    _, acc = absorb(n - 1, s1_ref, carry)
    outT = acc[:HEAD_DIM] / acc[HEAD_DIM:HEAD_DIM + 1]
    stacked = jnp.concatenate(
        [outT[:, g * tq:(g + 1) * tq] for g in range(Q_GROUP)], axis=0)
    o_ref[...] = stacked.T.astype(BF16)


def _attn(qT, k, vT):
    B, _, _, S = qT.shape
    tq = min(ATT_TQ, S)
    tk = min(ATT_TK, S)
    nq = S // tq
    assert (S // tk) % 2 == 0, "key chunks are processed in pairs"
    score_buf = pltpu.VMEM((tk, Q_GROUP * tq), F32)
    return pl.pallas_call(
        functools.partial(_attn_kernel, tk=tk),
        grid=(B, N_KV_HEADS, nq),
        in_specs=[
            pl.BlockSpec((1, Q_GROUP, HEAD_DIM, tq), lambda b, g, i: (b, g, 0, i)),
            pl.BlockSpec((1, 1, S, HEAD_DIM), lambda b, g, i: (b, g, 0, 0)),
            pl.BlockSpec((1, 1, LANES, S), lambda b, g, i: (b, g, 0, 0)),
        ],
        out_specs=pl.BlockSpec((tq, Q_GROUP * HEAD_DIM), lambda b, g, i: (b * nq + i, g)),
        out_shape=jax.ShapeDtypeStruct((B * S, Q_WIDTH), BF16),
        scratch_shapes=[score_buf, score_buf],
        compiler_params=_cparams("parallel", "parallel", "parallel"),
        name="attn",
    )(qT, k, vT)


def _shift_chunks(v, down):
    row = lax.broadcasted_iota(I32, v.shape, 0)
    if down:
        return jnp.where(row == 0, 0.0, pltpu.roll(v, 1, 0))
    return jnp.where(row == SUBLANES - 1, 0.0, pltpu.roll(v, SUBLANES - 1, 0))


def _lru_kernel(xr_ref, yr_ref, cw_ref, cb_ref, wg_ref, bg_ref, lam_ref, o_ref,
                xe_ref, af_ref, uf_ref, ab_ref, ub_ref, *, rows):
    S, ct = xr_ref.shape
    lc = S // TIME_CHUNKS
    halo = SUBLANES

    ng = ct // LANES

    def xe_rows(r0, n):
        return jnp.concatenate([xe_ref[g, pl.ds(r0, n), :] for g in range(ng)], axis=1)

    def xe_store(r0, n, val):
        for g in range(ng):
            xe_ref[g, pl.ds(r0, n), :] = val[:, g * LANES:(g + 1) * LANES]

    for s in range(TIME_CHUNKS):
        for g in range(ng):
            xe_ref[g, pl.ds(halo + s, lc, stride=SUBLANES), :] = (
                xr_ref[pl.ds(s * lc, lc), g * LANES:(g + 1) * LANES])
    xe_store(0, halo, _shift_chunks(xe_rows(S, halo), True))
    first = xe_rows(halo, halo)
    second = xe_rows(2 * halo, halo)
    xe_store(S + halo, halo, _shift_chunks(first, False))
    xe_store(S + 2 * halo, halo, _shift_chunks(second, False))

    cw = cw_ref[...]
    cb = cb_ref[...]
    lam = lam_ref[...]
    log_sig = jnp.minimum(lam, 0.0) - jnp.log(1.0 + jnp.exp(-jnp.abs(lam)))
    c_half = (0.5 * RG_C * LOG2E) * log_sig
    wg = wg_ref[0]
    bg = bg_ref[0]

    def gate_chunk(i, _):
        r0 = pl.multiple_of(i * rows, rows)
        xc = cb
        for j in range(4):
            xc = xc + cw[j:j + 1, :] * xe_rows(r0 + j * halo, rows)
        t = jnp.tanh(jnp.dot(xc.astype(BF16), wg, preferred_element_type=F32) + bg)
        x_half = 0.5 * xc
        for d, (a_ref, u_ref) in enumerate(((af_ref, uf_ref), (ab_ref, ub_ref))):
            t_r = t[:, (2 * d) * ct:(2 * d + 1) * ct]
            t_i = t[:, (2 * d + 1) * ct:(2 * d + 2) * ct]
            ch = c_half[d:d + 1, :]
            a = jnp.exp2(t_r * ch + ch)
            a_ref[pl.ds(r0, rows), :] = a
            v = 1.0 - a * a
            root = jnp.maximum(v, 0.0) * lax.rsqrt(jnp.maximum(v, TINY))
            u_ref[pl.ds(r0, rows), :] = root * ((t_i + 1.0) * x_half)
        return 0

    lax.fori_loop(0, S // rows, gate_chunk, 0)

    unroll = 8

    def scan_body(i, carry):
        hf, pf, hb, pb = carry
        for k in range(unroll):
            tf = i * unroll + k
            rf = pl.multiple_of(tf * SUBLANES, SUBLANES)
            rb = pl.multiple_of((lc - 1 - tf) * SUBLANES, SUBLANES)
            a = af_ref[pl.ds(rf, SUBLANES), :]
            hf = a * hf + uf_ref[pl.ds(rf, SUBLANES), :]
            pf = a * pf
            uf_ref[pl.ds(rf, SUBLANES), :] = hf
            af_ref[pl.ds(rf, SUBLANES), :] = pf
            a = ab_ref[pl.ds(rb, SUBLANES), :]
            hb = a * hb + ub_ref[pl.ds(rb, SUBLANES), :]
            pb = a * pb
            ub_ref[pl.ds(rb, SUBLANES), :] = hb
            ab_ref[pl.ds(rb, SUBLANES), :] = pb
        return hf, pf, hb, pb

    zero = jnp.zeros((SUBLANES, ct), F32)
    one = jnp.ones((SUBLANES, ct), F32)
    hf_end, pf_end, hb_end, pb_end = lax.fori_loop(0, lc // unroll, scan_body,
                                                   (zero, one, zero, one))

    def chunk_carry(h_end, p_end, down):
        f = h_end
        for _ in range(TIME_CHUNKS - 1):
            f = h_end + p_end * _shift_chunks(f, down)
        return _shift_chunks(f, down)

    cin_f = chunk_carry(hf_end, pf_end, True)
    cin_b = chunk_carry(hb_end, pb_end, False)

    def fix_chunk(i, _):
        r0 = pl.multiple_of(i * rows, rows)
        reps = rows // SUBLANES
        hf = uf_ref[pl.ds(r0, rows), :] + af_ref[pl.ds(r0, rows), :] * jnp.tile(cin_f, (reps, 1))
        hb = ub_ref[pl.ds(r0, rows), :] + ab_ref[pl.ds(r0, rows), :] * jnp.tile(cin_b, (reps, 1))
        xe_store(r0, rows, hf + hb)
        return 0

    lax.fori_loop(0, S // rows, fix_chunk, 0)

    for s in range(TIME_CHUNKS):
        hsum = jnp.concatenate(
            [xe_ref[g, pl.ds(s, lc, stride=SUBLANES), :] for g in range(ng)], axis=1)
        y = yr_ref[pl.ds(s * lc, lc), :]
        o_ref[pl.ds(s * lc, lc), :] = (hsum * jax.nn.gelu(y, approximate=True)).astype(BF16)


def _lru(proj, conv_w, conv_b, wg, bg, lam, B, S, C):
    ct = LRU_CT
    nct = C // ct
    rows = min(LRU_ROWS, S)
    return pl.pallas_call(
        functools.partial(_lru_kernel, rows=rows),
        grid=(B, nct),
        in_specs=[
            pl.BlockSpec((S, ct), lambda b, c: (b, c)),
            pl.BlockSpec((S, ct), lambda b, c: (b, nct + c)),
            pl.BlockSpec((4, ct), lambda b, c: (0, c)),
            pl.BlockSpec((1, ct), lambda b, c: (0, c)),
            pl.BlockSpec((1, ct, 4 * ct), lambda b, c: (c, 0, 0)),
            pl.BlockSpec((1, 1, 4 * ct), lambda b, c: (c, 0, 0)),
            pl.BlockSpec((2, ct), lambda b, c: (0, c)),
        ],
        out_specs=pl.BlockSpec((S, ct), lambda b, c: (b, c)),
        out_shape=jax.ShapeDtypeStruct((B * S, C), BF16),
        scratch_shapes=[
            pltpu.VMEM((ct // LANES, S + 3 * SUBLANES, LANES), F32),
            pltpu.VMEM((S, ct), F32), pltpu.VMEM((S, ct), F32),
            pltpu.VMEM((S, ct), F32), pltpu.VMEM((S, ct), F32),
        ],
        compiler_params=_cparams("parallel", "parallel"),
        name="lru",
    )(proj, proj, conv_w, conv_b, wg, bg, lam)


def _merge_kernel(x_ref, attn_ref, rec_ref, gl_ref, wa_ref, wr_ref, wo_ref, g2_ref,
                  rw_hi_ref, rw_lo_ref, rb_ref, x2_ref, h2_ref, lg_ref):
    D = x_ref.shape[1]
    gl = gl_ref[...]
    ma = jnp.dot(attn_ref[...], wa_ref[...], preferred_element_type=F32)
    mr = jnp.dot(rec_ref[...], wr_ref[...], preferred_element_type=F32)
    merged = jax.nn.sigmoid(gl[:, :D]) * ma + jax.nn.sigmoid(gl[:, D:]) * mr
    x2 = x_ref[...] + jnp.dot(merged.astype(BF16), wo_ref[...], preferred_element_type=F32)
    x2_ref[...] = x2
    h2 = _rms(x2, g2_ref[...])
    hi = h2.astype(BF16)
    lo = (h2 - hi.astype(F32)).astype(BF16)
    lg_ref[...] = (jnp.dot(hi, rw_hi_ref[...], preferred_element_type=F32)
                   + jnp.dot(lo, rw_hi_ref[...], preferred_element_type=F32)
                   + jnp.dot(hi, rw_lo_ref[...], preferred_element_type=F32)) + rb_ref[...]
    tm = h2.shape[0]
    bits = lax.bitcast_convert_type(hi.astype(F32), U32)
    words = bits[:, :D // 2] | (bits[:, D // 2:] >> BF16_BITS)
    for j in range(PACK_LINES):
        h2_ref[pl.ds(j, tm, stride=SUBLANES), :] = words[:, j * LANES:(j + 1) * LANES]
    for j in range(PACK_LINES, SUBLANES):
        h2_ref[pl.ds(j, tm, stride=SUBLANES), :] = jnp.zeros((tm, LANES), U32)


def _merge(x2d, attn, rec, proj, wa, wr, wo, g2, rw_hi, rw_lo, rb):
    T, D = x2d.shape
    tm = min(ROW_TILE, T)
    nj = D // LANES
    full = lambda a: pl.BlockSpec(a.shape, lambda i: (0,) * a.ndim)
    return pl.pallas_call(
        _merge_kernel,
        grid=(T // tm,),
        in_specs=[
            pl.BlockSpec((tm, D), lambda i: (i, 0)),
            pl.BlockSpec((tm, Q_WIDTH), lambda i: (i, 0)),
            pl.BlockSpec((tm, D), lambda i: (i, 0)),
            pl.BlockSpec((tm, 2 * D), lambda i: (i, 1)),
            full(wa), full(wr), full(wo), full(g2), full(rw_hi), full(rw_lo), full(rb),
        ],
        out_specs=[
            pl.BlockSpec((tm, D), lambda i: (i, 0)),
            pl.BlockSpec((tm * nj, LANES), lambda i: (i, 0)),
            pl.BlockSpec((tm, LANES), lambda i: (i, 0)),
        ],
        out_shape=[
            jax.ShapeDtypeStruct((T, D), F32),
            jax.ShapeDtypeStruct((T * nj, LANES), U32),
            jax.ShapeDtypeStruct((T, LANES), F32),
        ],
        compiler_params=_cparams("parallel"),
        name="merge",
    )(x2d, attn, rec, proj, wa, wr, wo, g2, rw_hi, rw_lo, rb)


def _route_kernel(lg_ref, gate_ref, gate_t_ref, eidx_t_ref, rank_t_ref, cnt_ref):
    i = pl.program_id(0)

    @pl.when(i == 0)
    def _():
        cnt_ref[...] = jnp.zeros_like(cnt_ref)

    tm = lg_ref.shape[0]
    work = lg_ref[...].T[:N_EXPERTS]
    expert = lax.broadcasted_iota(I32, work.shape, 0)
    sels, vals, idxs = [], [], []
    for _ in range(TOP_K):
        mx = jnp.max(work, axis=0, keepdims=True)
        idx = jnp.min(jnp.where(work == mx, expert, N_EXPERTS), axis=0, keepdims=True)
        sel = expert == idx
        sels.append(sel)
        vals.append(mx)
        idxs.append(idx)
        work = jnp.where(sel, NEG_BIG, work)
    ex = [jnp.exp(v - vals[0]) for v in vals]
    den = ex[0] + ex[1] + ex[2] + ex[3]
    member = (sels[0] | sels[1] | sels[2] | sels[3]).astype(BF16)
    r = lax.broadcasted_iota(I32, (tm, tm), 0)
    c = lax.broadcasted_iota(I32, (tm, tm), 1)
    tri = (r < c).astype(BF16)
    base = cnt_ref[:, 0:1]
    before = jnp.dot(member, tri, preferred_element_type=F32) + base
    ranks = [jnp.sum(jnp.where(sels[k], before, 0.0), axis=0, keepdims=True)
             for k in range(TOP_K)]
    gates = [ex[k] / den for k in range(TOP_K)]
    pad = SUBLANES - TOP_K
    gate_t = jnp.concatenate(gates + [jnp.zeros((pad, tm), F32)], axis=0)
    gate_t_ref[...] = gate_t
    eidx_t_ref[...] = jnp.concatenate(idxs + [jnp.zeros((pad, tm), I32)], axis=0)
    rank_t_ref[...] = jnp.concatenate(ranks + [jnp.zeros((pad, tm), F32)], axis=0).astype(I32)
    gate_ref[...] = jnp.concatenate(
        [gate_t, jnp.zeros((LANES - SUBLANES, tm), F32)], axis=0).T
    cnt_ref[...] = jnp.broadcast_to(
        base + jnp.sum(member.astype(F32), axis=1, keepdims=True), cnt_ref.shape)


def _route(logits):
    T = logits.shape[0]
    tm = min(ROW_TILE, T)
    spec = pl.BlockSpec((tm, LANES), lambda i: (i, 0))
    spec_t = pl.BlockSpec((SUBLANES, tm), lambda i: (0, i))
    return pl.pallas_call(
        _route_kernel,
        grid=(T // tm,),
        in_specs=[spec],
        out_specs=[spec, spec_t, spec_t, spec_t,
                   pl.BlockSpec((N_EXPERTS, LANES), lambda i: (0, 0))],
        out_shape=[
            jax.ShapeDtypeStruct((T, LANES), F32),
            jax.ShapeDtypeStruct((SUBLANES, T), F32),
            jax.ShapeDtypeStruct((SUBLANES, T), I32),
            jax.ShapeDtypeStruct((SUBLANES, T), I32),
            jax.ShapeDtypeStruct((N_EXPERTS, LANES), F32),
        ],
        compiler_params=_cparams("arbitrary"),
        name="route",
    )(logits)


def _tile(ref, row):
    return ref.at[pl.ds(pl.multiple_of(row * SUBLANES, SUBLANES), SUBLANES)]


def _dispatch_kernel(dest_ref, last_blk_ref, nused_ref, h2_ref, buf_ref,
                     src_ref, pad_ref, sem, zsem, *, n_tokens):
    i = pl.program_id(0)
    tm = h2_ref.shape[0] // SUBLANES
    blk = pad_ref.shape[0] // (2 * SUBLANES)
    nblocks = buf_ref.shape[0] // (blk * SUBLANES)

    @pl.when(i == 0)
    def _():
        li = lax.broadcasted_iota(I32, pad_ref.shape, 0)
        trash = (TOP_K * n_tokens + (li >> LOG2_SUBLANES)).astype(U32)
        pad_ref[...] = jnp.where((li & (SUBLANES - 1)) >= PACK_LINES, trash, jnp.uint32(0))
        rows = blk * SUBLANES

        def pad_block(b):
            half = pl.multiple_of(lax.rem(b, 2) * rows, rows)
            return pltpu.make_async_copy(
                pad_ref.at[pl.ds(half, rows)],
                buf_ref.at[pl.ds(pl.multiple_of(b * rows, rows), rows)], zsem)

        def expert_put(e, _):
            pl.when(last_blk_ref[e] >= 0)(lambda: pad_block(last_blk_ref[e]).start())
            return 0

        def expert_done(e, _):
            pl.when(last_blk_ref[e] >= 0)(lambda: pad_block(last_blk_ref[e]).wait())
            return 0

        def tail_put(b, _):
            pad_block(b).start()
            return 0

        def tail_done(b, _):
            pad_block(b).wait()
            return 0

        lax.fori_loop(0, N_EXPERTS, expert_put, 0)
        lax.fori_loop(nused_ref[0], nblocks, tail_put, 0)
        lax.fori_loop(0, N_EXPERTS, expert_done, 0)
        lax.fori_loop(nused_ref[0], nblocks, tail_done, 0)

    parity = lax.rem(i, 2)
    half = parity * TOP_K
    li = lax.broadcasted_iota(I32, h2_ref.shape, 0)
    tok = i * tm + (li >> LOG2_SUBLANES)
    is_id = (li & (SUBLANES - 1)) >= PACK_LINES
    data = h2_ref[...]
    for k in range(TOP_K):
        src_ref[half + k] = jnp.where(is_id, (k * n_tokens + tok).astype(U32), data)

    def issue(group, _):
        for dr in range(ISSUE_UNROLL):
            r = group * ISSUE_UNROLL + dr
            for k in range(TOP_K):
                pltpu.make_async_copy(_tile(src_ref.at[half + k], r),
                                      _tile(buf_ref, dest_ref[k * (tm + 1) + r]),
                                      sem.at[parity]).start(priority=k % 2)
        return 0

    lax.fori_loop(0, tm // ISSUE_UNROLL, issue, 0)

    def wait_step(which):
        for k in range(TOP_K):
            pltpu.make_async_copy(src_ref.at[k], buf_ref.at[pl.ds(0, tm * SUBLANES)],
                                  sem.at[which]).wait()

    pl.when(i >= 1)(functools.partial(wait_step, 1 - parity))
    pl.when(i == pl.num_programs(0) - 1)(functools.partial(wait_step, parity))


def _dispatch(dest, last_blk, nused, h2_rows, n_rows, blk):
    T = h2_rows.shape[0] // SUBLANES
    tm = min(DMA_TILE, T)
    nt = T // tm
    words = pl.cdiv(TOP_K * (tm + 1), SMEM_1D_TILE) * SMEM_1D_TILE
    d = jnp.pad(dest.reshape(TOP_K, nt, tm).transpose(1, 0, 2), ((0, 0), (0, 0), (0, 1)))
    d = jnp.pad(d.reshape(nt, TOP_K * (tm + 1)), ((0, 0), (0, words - TOP_K * (tm + 1))))
    dest_flat = d.reshape(nt * words)
    smem = pl.BlockSpec(memory_space=pltpu.SMEM)
    return pl.pallas_call(
        functools.partial(_dispatch_kernel, n_tokens=T),
        grid=(nt,),
        in_specs=[
            pl.BlockSpec((words,), lambda i: (i,), memory_space=pltpu.SMEM),
            smem, smem,
            pl.BlockSpec((tm * SUBLANES, LANES), lambda i: (i, 0)),
        ],
        out_specs=pl.BlockSpec(memory_space=pl.ANY),
        out_shape=jax.ShapeDtypeStruct((n_rows * SUBLANES, LANES), U32),
        scratch_shapes=[pltpu.VMEM((2 * TOP_K, tm * SUBLANES, LANES), U32),
                        pltpu.VMEM((2 * blk * SUBLANES, LANES), U32),
                        pltpu.SemaphoreType.DMA((2,)), pltpu.SemaphoreType.DMA(())],
        compiler_params=_cparams("arbitrary"),
        name="dispatch",
    )(dest_flat, last_blk, nused, h2_rows)


def _ffn_kernel(blk_exp_ref, nused_ref, x_ref, wgu_ref, bgu_ref, wd_ref, bd_ref, y_ref,
                wgu_bf_ref, wd_bf_ref, out_ref, idv_ref, ids_ref, ssem, isem, *, n_tokens):
    b = pl.program_id(0)
    nb = pl.num_programs(0)
    tm = x_ref.shape[0] // SUBLANES
    groups = tm // LANES
    F = wd_ref.shape[1]
    nused = nused_ref[0]
    used = b < nused
    slot = lax.rem(b, 2)
    prev = 1 - slot
    new_expert = (b == 0) | (blk_exp_ref[b] != blk_exp_ref[jnp.maximum(b - 1, 0)])

    def send_row(s, r_hi, r_lo, row, priority):
        rid = ids_ref[s * SUBLANES + r_hi, r_lo]
        pltpu.make_async_copy(_tile(out_ref, s * tm + row), _tile(y_ref, rid),
                              ssem.at[s]).start(priority=priority)

    def send_rows_unrolled(s, lo, hi):
        for r in range(lo, hi):
            send_row(s, r // LANES, r % LANES, r, r % 2)

    def send_block_rolled(s):
        def body(g, _):
            def inner(l, _):
                send_row(s, g, l, g * LANES + l, 0)
                return 0
            return lax.fori_loop(0, LANES, inner, 0)
        lax.fori_loop(0, groups, body, 0)

    def wait_block(s):
        rows = tm * SUBLANES
        pltpu.make_async_copy(out_ref.at[pl.ds(0, rows)], y_ref.at[pl.ds(0, rows)],
                              ssem.at[s]).wait()

    def id_copy(s):
        return pltpu.make_async_copy(
            idv_ref, ids_ref.at[pl.ds(pl.multiple_of(s * SUBLANES, SUBLANES), SUBLANES)], isem)

    @pl.when(b == 0)
    def _():
        out_ref[...] = jnp.zeros_like(out_ref)
        rows = tm * SUBLANES
        pltpu.make_async_copy(out_ref.at[pl.ds(0, rows)],
                              y_ref.at[pl.ds(TOP_K * n_tokens * SUBLANES, rows)],
                              ssem.at[0]).start()
        pos = (lax.broadcasted_iota(I32, idv_ref.shape, 0) * LANES
               + lax.broadcasted_iota(I32, idv_ref.shape, 1))
        idv_ref[...] = TOP_K * n_tokens + tm + jnp.minimum(pos, tm - 1)
        id_copy(1).start()
        id_copy(1).wait()

    @pl.when((b >= 1) & (b <= nused))
    def _():
        id_copy(prev).wait()

    @pl.when(used)
    def _():
        idl = x_ref[pl.ds(PACK_LINES, tm, stride=SUBLANES), :].astype(I32)
        rr = lax.broadcasted_iota(I32, idl.shape, 0)
        ll = lax.broadcasted_iota(I32, idl.shape, 1)
        diag = jnp.where((rr & (LANES - 1)) == ll, idl, 0)
        idv_ref[pl.ds(0, groups), :] = jnp.sum(diag.reshape(groups, LANES, LANES), axis=1)
        id_copy(slot).start()

    @pl.when(used & new_expert)
    def _():
        wgu_bf_ref[...] = wgu_ref[0].astype(BF16)
        wd_bf_ref[...] = wd_ref[0].astype(BF16)

    def block(slot, prev):
        words = [x_ref[pl.ds(j, tm, stride=SUBLANES), :] for j in range(PACK_LINES)]
        hi = [lax.bitcast_convert_type(w & jnp.uint32(HIGH_HALF), F32) for w in words]
        lo = [lax.bitcast_convert_type(w << BF16_BITS, F32) for w in words]
        x = jnp.concatenate(hi + lo, axis=1).astype(BF16)
        acts = []
        for c in range(FFN_SLICES):
            send_rows_unrolled(prev, c * tm // FFN_SLICES, (c + 1) * tm // FFN_SLICES)
            w = F // FFN_SLICES
            g = jnp.dot(x, wgu_bf_ref[:, c * w:(c + 1) * w], preferred_element_type=F32)
            u = jnp.dot(x, wgu_bf_ref[:, F + c * w:F + (c + 1) * w],
                        preferred_element_type=F32)
            g = jnp.minimum(g + bgu_ref[0, :, c * w:(c + 1) * w], SWIGLU_LIMIT)
            u = jnp.clip(u + bgu_ref[0, :, F + c * w:F + (c + 1) * w],
                         -SWIGLU_LIMIT, SWIGLU_LIMIT)
            acts.append((g * jax.nn.sigmoid(SWIGLU_ALPHA * g) * (u + 1.0)).astype(BF16))
        act = jnp.concatenate(acts, axis=1)
        y = jnp.dot(act, wd_bf_ref[...], preferred_element_type=F32) + bd_ref[0]
        wait_block(slot)
        base = slot * (tm * SUBLANES)
        for j in range(SUBLANES):
            out_ref[pl.ds(base + j, tm, stride=SUBLANES), :] = y[:, j * LANES:(j + 1) * LANES]

    for parity in range(2):
        pl.when(used & (slot == parity))(functools.partial(block, parity, 1 - parity))

    @pl.when(b == nused)
    def _():
        wait_block(slot)
        send_block_rolled(prev)
        wait_block(prev)

    @pl.when(used & (b == nb - 1))
    def _():
        wait_block(prev)
        id_copy(slot).wait()
        send_block_rolled(slot)
        wait_block(slot)


def _ffn(blk_exp, nused, buf, wgu, bgu, wd, bd, tm, n_tokens):
    nb = buf.shape[0] // (tm * SUBLANES)
    D = wgu.shape[1]
    F = wd.shape[1]

    def xmap(b, be, nu):
        return (jnp.minimum(b, nu[0] - 1), 0)

    def wmap(b, be, nu):
        return (be[jnp.minimum(b, nu[0] - 1)], 0, 0)

    grid_spec = pltpu.PrefetchScalarGridSpec(
        num_scalar_prefetch=2,
        grid=(nb,),
        in_specs=[
            pl.BlockSpec((tm * SUBLANES, LANES), xmap),
            pl.BlockSpec((1, D, 2 * F), wmap),
            pl.BlockSpec((1, 1, 2 * F), wmap),
            pl.BlockSpec((1, F, D), wmap),
            pl.BlockSpec((1, 1, D), wmap),
        ],
        out_specs=pl.BlockSpec(memory_space=pl.ANY),
        scratch_shapes=[
            pltpu.VMEM((D, 2 * F), BF16), pltpu.VMEM((F, D), BF16),
            pltpu.VMEM((2 * tm * SUBLANES, LANES), F32),
            pltpu.VMEM((SUBLANES, LANES), I32),
            pltpu.SMEM((2 * SUBLANES, LANES), I32),
            pltpu.SemaphoreType.DMA((2,)), pltpu.SemaphoreType.DMA(()),
        ],
    )
    return pl.pallas_call(
        functools.partial(_ffn_kernel, n_tokens=n_tokens),
        grid_spec=grid_spec,
        out_shape=jax.ShapeDtypeStruct(((TOP_K * n_tokens + 2 * tm) * SUBLANES, LANES), F32),
        compiler_params=_cparams("arbitrary"),
        name="ffn",
    )(blk_exp, nused, buf, wgu, bgu, wd, bd)


def _combine_kernel(gate_ref, x2_ref, fg_ref, *refs):
    y_refs, o_ref = refs[:TOP_K], refs[TOP_K]
    tm, D = x2_ref.shape
    nj = D // LANES
    gate = gate_ref[...]
    cols = []
    for j in range(nj):
        acc = None
        for k in range(TOP_K):
            part = gate[:, k:k + 1] * y_refs[k][pl.ds(j, tm, stride=nj), :]
            acc = part if acc is None else acc + part
        cols.append(acc)
    y = jnp.concatenate(cols, axis=1)
    o_ref[...] = _rms(x2_ref[...] + y, fg_ref[...])


def _combine(gate4, x2, fg, y_rows):
    T, D = x2.shape
    tm = min(ROW_TILE, T)
    nj = D // LANES
    nt = T // tm

    def slot_spec(k):
        return pl.BlockSpec((tm * nj, LANES), lambda i: (k * nt + i, 0))

    return pl.pallas_call(
        _combine_kernel,
        grid=(nt,),
        in_specs=[
            pl.BlockSpec((tm, LANES), lambda i: (i, 0)),
            pl.BlockSpec((tm, D), lambda i: (i, 0)),
            pl.BlockSpec((1, D), lambda i: (0, 0)),
        ] + [slot_spec(k) for k in range(TOP_K)],
        out_specs=pl.BlockSpec((tm, D), lambda i: (i, 0)),
        out_shape=jax.ShapeDtypeStruct((T, D), F32),
        compiler_params=_cparams("parallel"),
        name="combine",
    )(gate4, x2, fg, *([y_rows] * TOP_K))


def _rope_tables(S):
    rows = S // GRID_W
    row = jnp.repeat(jnp.arange(rows, dtype=I32), GRID_W).astype(F32)
    col = jnp.tile(jnp.arange(GRID_W, dtype=I32), rows).astype(F32)
    inv = ROPE_THETA ** (-jnp.arange(0, AXIS_DIM, 2, dtype=F32) / AXIS_DIM)
    ang_r = row[:, None] * inv[None, :]
    ang_c = col[:, None] * inv[None, :]
    cos = jnp.concatenate([jnp.cos(ang_r)] * 2 + [jnp.cos(ang_c)] * 2, axis=1)
    sin = jnp.concatenate([-jnp.sin(ang_r), jnp.sin(ang_r), -jnp.sin(ang_c), jnp.sin(ang_c)], axis=1)
    return jnp.tile(cos, (1, 2)), jnp.tile(sin, (1, 2))


def _block_diag_gates(wa, ba, wi, bi, ct):
    nb, bw = wa.shape[1], wa.shape[2]
    per = ct // bw
    nct = nb // per
    eye = jnp.eye(per, dtype=wa.dtype)

    def tiles(w):
        w = w.reshape(nct, per, bw, bw)
        return jnp.einsum('cpij,pq->cpiqj', w, eye).reshape(nct, ct, ct)

    wg = jnp.concatenate([tiles(wa[0]), tiles(wi[0]), tiles(wa[1]), tiles(wi[1])], axis=2)
    bias = lambda b: b.reshape(nct, 1, ct)
    bg = jnp.concatenate([bias(ba[0]), bias(bi[0]), bias(ba[1]), bias(bi[1])], axis=2)
    return (0.5 * wg).astype(BF16), 0.5 * bg


def _layer(x2d, B, S, norm1_g, w_in, b_in, q_norm_g, k_norm_g, conv_w, conv_b, lru_wa, lru_ba,
           lru_wi, lru_bi, lru_lambda, w_attn_o, w_lru_o, w_out, norm2_g, w_router, b_router,
           w_gu, b_gu, w_down, b_down, out_g):
    T, D = x2d.shape
    C = conv_w.shape[1]
    nqkv = Q_WIDTH + 2 * KV_WIDTH
    row2 = lambda v: v.reshape(1, -1)

    cos_t, sin_t = _rope_tables(S)
    head = jnp.arange(LANES) // HEAD_DIM
    hsum = (head[:, None] == head[None, :]).astype(BF16)
    q, k, v = _qkv(x2d, row2(norm1_g), w_in[:, :nqkv].astype(BF16), row2(b_in[:nqkv]),
                   cos_t, sin_t, row2(jnp.tile(q_norm_g, 2)), row2(jnp.tile(k_norm_g, 2)),
                   hsum, B, S)
    proj = _proj(x2d, row2(norm1_g), w_in[:, nqkv:].astype(BF16), row2(b_in[nqkv:]))
    attn = _attn(q, k, v)
    wg, bg = _block_diag_gates(lru_wa, lru_ba, lru_wi, lru_bi, LRU_CT)
    rec = _lru(proj, conv_w, row2(conv_b), wg, bg, lru_lambda, B, S, C)

    pad = LANES - N_EXPERTS
    rw = jnp.pad(w_router, ((0, 0), (0, pad)))
    rw_hi = rw.astype(BF16)
    rw_lo = (rw - rw_hi.astype(F32)).astype(BF16)
    rb = jnp.pad(b_router, (0, pad), constant_values=NEG_BIG).reshape(1, LANES)
    x2, h2_rows, logits = _merge(x2d, attn, rec, proj, w_attn_o.astype(BF16),
                                 w_lru_o.astype(BF16), w_out.astype(BF16), row2(norm2_g),
                                 rw_hi, rw_lo, rb)

    gate4, _, eidx_t, rank_t, cnt = _route(logits)
    counts = cnt[:, 0].astype(I32)
    tm = FFN_TM
    nblk = (counts + tm - 1) // tm
    pend_blk = jnp.cumsum(nblk)
    pstart = (pend_blk - nblk) * tm
    A = T * TOP_K
    nb = (A + N_EXPERTS * (tm - 1) + tm - 1) // tm
    blk_exp = jnp.minimum(
        jnp.sum(pend_blk[None, :] <= jnp.arange(nb, dtype=I32)[:, None], axis=1),
        N_EXPERTS - 1).astype(I32)
    nused = pend_blk[-1:].astype(I32)
    first_row = jnp.sum(jnp.where(eidx_t[:TOP_K, :, None] == jnp.arange(N_EXPERTS), pstart, 0),
                        axis=-1)
    dest = (first_row + rank_t[:TOP_K]).astype(I32)
    last_blk = jnp.where(nblk > 0, pend_blk - 1, -1).astype(I32)

    buf = _dispatch(dest, last_blk, nused, h2_rows, nb * tm, tm)
    y_rows = _ffn(blk_exp, nused, buf, w_gu, b_gu.reshape(N_EXPERTS, 1, -1), w_down,
                  b_down.reshape(N_EXPERTS, 1, -1), tm, T)
    return _combine(gate4, x2, row2(out_g), y_rows)


def kernel(x, norm1_g, w_in, b_in, q_norm_g, k_norm_g, conv_w, conv_b, lru_wa, lru_ba, lru_wi,
           lru_bi, lru_lambda, w_attn_o, w_lru_o, w_out, norm2_g, w_router, b_router, w_gu, b_gu,
           w_down, b_down, final_g):
    B, S, D = x.shape
    depth = norm1_g.shape[0]
    assert depth == 1, "the fused final RMSNorm assumes a single layer"
    assert S % (TIME_CHUNKS * SUBLANES) == 0 and S % GRID_W == 0
    assert D == 2 * PACK_LINES * LANES, "a token's bf16 row must fill PACK_LINES word lines"
    out = _layer(x.reshape(B * S, D), B, S, norm1_g[0], w_in[0], b_in[0], q_norm_g[0],
                 k_norm_g[0], conv_w[0], conv_b[0], lru_wa[0], lru_ba[0], lru_wi[0], lru_bi[0],
                 lru_lambda[0], w_attn_o[0], w_lru_o[0], w_out[0], norm2_g[0], w_router[0],
                 b_router[0], w_gu[0], b_gu[0], w_down[0], b_down[0], final_g)
    return out.reshape(B, S, D)
```

```python
import functools

import jax
import jax.numpy as jnp
from jax import lax
from jax.experimental import pallas as pl
from jax.experimental.pallas import tpu as pltpu

F32 = jnp.float32
BF16 = jnp.bfloat16
I32 = jnp.int32
U32 = jnp.uint32

LANES = 128
SUBLANES = 8
PACK_LINES = 4
BF16_BITS = 16
HIGH_HALF = 0xFFFF0000
LOG2_SUBLANES = 3
SMEM_1D_TILE = 1024
VMEM_LIMIT_BYTES = 56 * 1024 * 1024

HEAD_DIM = 64
N_Q_HEADS = 8
N_KV_HEADS = 2
Q_GROUP = N_Q_HEADS // N_KV_HEADS
Q_WIDTH = N_Q_HEADS * HEAD_DIM
KV_WIDTH = N_KV_HEADS * HEAD_DIM
AXIS_DIM = HEAD_DIM // 2
ROT_HALF = AXIS_DIM // 2
ROPE_THETA = 10000.0
GRID_W = 64
RG_C = 8.0
N_EXPERTS = 32
TOP_K = 4
SWIGLU_LIMIT = 7.0
SWIGLU_ALPHA = 1.702
NORM_EPS = 1e-6
LOG2E = 1.4426950408889634
NEG_BIG = -1e30
TINY = 1e-30

ROW_TILE = 512
ATT_TQ = 1024
ATT_TK = 512
LRU_CT = 256
LRU_ROWS = 512
SCAN_UNROLL = 32
TIME_CHUNKS = SUBLANES
FFN_TM = 512
FFN_SLICES = 4
DMA_TILE = 256


def _cparams(*sem):
    return pltpu.CompilerParams(dimension_semantics=sem, vmem_limit_bytes=VMEM_LIMIT_BYTES)


def _rms(x, g):
    return x * lax.rsqrt(jnp.mean(x * x, axis=-1, keepdims=True) + NORM_EPS) * g


def _qkv_kernel(x_ref, g1_ref, w_ref, b_ref, cos_ref, sin_ref, qg_ref, kg_ref, hsum_ref,
                q_ref, k_ref, v_ref):
    h = _rms(x_ref[...], g1_ref[...]).astype(BF16)
    p = jnp.dot(h, w_ref[...], preferred_element_type=F32) + b_ref[...]
    cos = cos_ref[...]
    sin = sin_ref[...]
    hsum = hsum_ref[...]
    lane = lax.broadcasted_iota(I32, cos.shape, 1)
    first_half = (lane % AXIS_DIM) < ROT_HALF

    def norm_rope(c, gain):
        sq = c * c
        hi = sq.astype(BF16)
        lo = (sq - hi.astype(F32)).astype(BF16)
        ms = (jnp.dot(hi, hsum, preferred_element_type=F32)
              + jnp.dot(lo, hsum, preferred_element_type=F32)) * (1.0 / HEAD_DIM)
        y = c * lax.rsqrt(ms + NORM_EPS) * gain
        partner = jnp.where(first_half, pltpu.roll(y, LANES - ROT_HALF, 1),
                            pltpu.roll(y, ROT_HALF, 1))
        return y * cos + partner * sin

    qg = qg_ref[...]
    for c in range(Q_WIDTH // LANES):
        y = norm_rope(p[:, c * LANES:(c + 1) * LANES], qg) * (HEAD_DIM ** -0.5 * LOG2E)
        yt = y.T.astype(BF16)
        q_ref[0, 2 * c] = yt[:HEAD_DIM]
        q_ref[0, 2 * c + 1] = yt[HEAD_DIM:]
    yk = norm_rope(p[:, Q_WIDTH:Q_WIDTH + KV_WIDTH], kg_ref[...])
    k_ref[0, 0] = yk[:, :HEAD_DIM].astype(BF16)
    k_ref[0, 1] = yk[:, HEAD_DIM:].astype(BF16)
    vt = p[:, Q_WIDTH + KV_WIDTH:].T
    tm = vt.shape[1]
    ones_row = (lax.broadcasted_iota(I32, (HEAD_DIM, tm), 0) == 0).astype(BF16)
    v_ref[0, 0] = jnp.concatenate([vt[:HEAD_DIM].astype(BF16), ones_row], axis=0)
    v_ref[0, 1] = jnp.concatenate([vt[HEAD_DIM:].astype(BF16), ones_row], axis=0)


def _qkv(x2d, g1, w_qkv, b_qkv, cos_t, sin_t, qg, kg, hsum, B, S):
    T, D = x2d.shape
    tm = min(ROW_TILE, S)
    ns = S // tm
    n = w_qkv.shape[1]
    full = lambda shape: pl.BlockSpec(shape, lambda i: (0,) * len(shape))
    return pl.pallas_call(
        _qkv_kernel,
        grid=(T // tm,),
        in_specs=[
            pl.BlockSpec((tm, D), lambda i: (i, 0)),
            full((1, D)), full((D, n)), full((1, n)),
            pl.BlockSpec((tm, LANES), lambda i: (i % ns, 0)),
            pl.BlockSpec((tm, LANES), lambda i: (i % ns, 0)),
            full((1, LANES)), full((1, LANES)), full((LANES, LANES)),
        ],
        out_specs=[
            pl.BlockSpec((1, N_Q_HEADS, HEAD_DIM, tm), lambda i: (i // ns, 0, 0, i % ns)),
            pl.BlockSpec((1, N_KV_HEADS, tm, HEAD_DIM), lambda i: (i // ns, 0, i % ns, 0)),
            pl.BlockSpec((1, N_KV_HEADS, LANES, tm), lambda i: (i // ns, 0, 0, i % ns)),
        ],
        out_shape=[
            jax.ShapeDtypeStruct((B, N_Q_HEADS, HEAD_DIM, S), BF16),
            jax.ShapeDtypeStruct((B, N_KV_HEADS, S, HEAD_DIM), BF16),
            jax.ShapeDtypeStruct((B, N_KV_HEADS, LANES, S), BF16),
        ],
        compiler_params=_cparams("parallel"),
        name="qkv",
    )(x2d, g1, w_qkv, b_qkv, cos_t, sin_t, qg, kg, hsum)


def _proj_kernel(x_ref, g_ref, w_ref, b_ref, o_ref):
    h = _rms(x_ref[...], g_ref[...]).astype(BF16)
    o_ref[...] = jnp.dot(h, w_ref[...], preferred_element_type=F32) + b_ref[...]


def _proj(x2d, g, w, b):
    T, D = x2d.shape
    n = w.shape[1]
    tm = min(ROW_TILE, T)
    return pl.pallas_call(
        _proj_kernel,
        grid=(T // tm,),
        in_specs=[
            pl.BlockSpec((tm, D), lambda i: (i, 0)),
            pl.BlockSpec((1, D), lambda i: (0, 0)),
            pl.BlockSpec((D, n), lambda i: (0, 0)),
            pl.BlockSpec((1, n), lambda i: (0, 0)),
        ],
        out_specs=pl.BlockSpec((tm, n), lambda i: (i, 0)),
        out_shape=jax.ShapeDtypeStruct((T, n), F32),
        compiler_params=_cparams("parallel"),
        name="proj",
    )(x2d, g, w, b)


def _attn_kernel(q_ref, k_ref, v_ref, o_ref, s0_ref, s1_ref, *, tk):
    tq = q_ref.shape[3]
    S = k_ref.shape[2]
    M = Q_GROUP * tq
    n = S // tk
    qT = jnp.concatenate([q_ref[0, h] for h in range(Q_GROUP)], axis=1)

    def scores(j, s_ref):
        off = pl.multiple_of(j * tk, tk)
        s_ref[...] = jnp.dot(k_ref[0, 0, pl.ds(off, tk), :], qT, preferred_element_type=F32)

    def absorb(j, s_ref, carry):
        m, acc = carry
        off = pl.multiple_of(j * tk, tk)
        vc = v_ref[0, 0, :, pl.ds(off, tk)]
        s = s_ref[...]
        m_new = jnp.maximum(m, jnp.max(s, axis=0, keepdims=True))
        alpha = jnp.exp2(m - m_new)
        p = jnp.exp2(s - m_new).astype(BF16)
        return m_new, alpha * acc + jnp.dot(vc, p, preferred_element_type=F32)

    def pair(i, carry):
        scores(2 * i + 1, s1_ref)
        carry = absorb(2 * i, s0_ref, carry)
        scores(2 * i + 2, s0_ref)
        return absorb(2 * i + 1, s1_ref, carry)

    scores(0, s0_ref)
    carry = (jnp.full((1, M), NEG_BIG, F32), jnp.zeros((LANES, M), F32))
    carry = lax.fori_loop(0, n // 2 - 1, pair, carry)
    scores(n - 1, s1_ref)
    carry = absorb(n - 2, s0_ref, carry)
    _, acc = absorb(n - 1, s1_ref, carry)
    outT = acc[:HEAD_DIM] / acc[HEAD_DIM:HEAD_DIM + 1]
    stacked = jnp.concatenate(
        [outT[:, g * tq:(g + 1) * tq] for g in range(Q_GROUP)], axis=0)
    o_ref[...] = stacked.T.astype(BF16)


def _attn(qT, k, vT):
    B, _, _, S = qT.shape
    tq = min(ATT_TQ, S)
    tk = min(ATT_TK, S)
    nq = S // tq
    assert (S // tk) % 2 == 0, "key chunks are processed in pairs"
    score_buf = pltpu.VMEM((tk, Q_GROUP * tq), F32)
    return pl.pallas_call(
        functools.partial(_attn_kernel, tk=tk),
        grid=(B, N_KV_HEADS, nq),
        in_specs=[
            pl.BlockSpec((1, Q_GROUP, HEAD_DIM, tq), lambda b, g, i: (b, g, 0, i)),
            pl.BlockSpec((1, 1, S, HEAD_DIM), lambda b, g, i: (b, g, 0, 0)),
            pl.BlockSpec((1, 1, LANES, S), lambda b, g, i: (b, g, 0, 0)),
        ],
        out_specs=pl.BlockSpec((tq, Q_GROUP * HEAD_DIM), lambda b, g, i: (b * nq + i, g)),
        out_shape=jax.ShapeDtypeStruct((B * S, Q_WIDTH), BF16),
        scratch_shapes=[score_buf, score_buf],
        compiler_params=_cparams("parallel", "parallel", "parallel"),
        name="attn",
    )(qT, k, vT)


def _shift_chunks(v, down):
    row = lax.broadcasted_iota(I32, v.shape, 0)
    if down:
        return jnp.where(row == 0, 0.0, pltpu.roll(v, 1, 0))
    return jnp.where(row == SUBLANES - 1, 0.0, pltpu.roll(v, SUBLANES - 1, 0))


def _lru_kernel(xr_ref, yr_ref, cw_ref, cb_ref, wg_ref, bg_ref, lam_ref, o_ref,
                xe_ref, af_ref, uf_ref, ab_ref, ub_ref, *, rows):
    S, ct = xr_ref.shape
    lc = S // TIME_CHUNKS
    halo = SUBLANES

    ng = ct // LANES

    def xe_rows(r0, n):
        return jnp.concatenate([xe_ref[g, pl.ds(r0, n), :] for g in range(ng)], axis=1)

    def xe_store(r0, n, val):
        for g in range(ng):
            xe_ref[g, pl.ds(r0, n), :] = val[:, g * LANES:(g + 1) * LANES]

    for s in range(TIME_CHUNKS):
        for g in range(ng):
            xe_ref[g, pl.ds(halo + s, lc, stride=SUBLANES), :] = (
                xr_ref[pl.ds(s * lc, lc), g * LANES:(g + 1) * LANES])
    xe_store(0, halo, _shift_chunks(xe_rows(S, halo), True))
    first = xe_rows(halo, halo)
    second = xe_rows(2 * halo, halo)
    xe_store(S + halo, halo, _shift_chunks(first, False))
    xe_store(S + 2 * halo, halo, _shift_chunks(second, False))

    cw = cw_ref[...]
    cb = cb_ref[...]
    lam = lam_ref[...]
    log_sig = jnp.minimum(lam, 0.0) - jnp.log(1.0 + jnp.exp(-jnp.abs(lam)))
    c_half = (0.5 * RG_C * LOG2E) * log_sig
    wg = wg_ref[0]
    bg = bg_ref[0]

    def gate_chunk(i, _):
        r0 = pl.multiple_of(i * rows, rows)
        xc = cb
        for j in range(4):
            xc = xc + cw[j:j + 1, :] * xe_rows(r0 + j * halo, rows)
        t = jnp.tanh(jnp.dot(xc.astype(BF16), wg, preferred_element_type=F32) + bg)
        x_half = 0.5 * xc
        for d, (a_ref, u_ref) in enumerate(((af_ref, uf_ref), (ab_ref, ub_ref))):
            t_r = t[:, (2 * d) * ct:(2 * d + 1) * ct]
            t_i = t[:, (2 * d + 1) * ct:(2 * d + 2) * ct]
            ch = c_half[d:d + 1, :]
            a = jnp.exp2(t_r * ch + ch)
            a_ref[pl.ds(r0, rows), :] = a
            v = 1.0 - a * a
            root = jnp.maximum(v, 0.0) * lax.rsqrt(jnp.maximum(v, TINY))
            u_ref[pl.ds(r0, rows), :] = root * ((t_i + 1.0) * x_half)
        return 0

    lax.fori_loop(0, S // rows, gate_chunk, 0)

    unroll = SCAN_UNROLL

    def scan_body(i, carry):
        hf, pf, hb, pb = carry
        for k in range(unroll):
            tf = i * unroll + k
            rf = pl.multiple_of(tf * SUBLANES, SUBLANES)
            rb = pl.multiple_of((lc - 1 - tf) * SUBLANES, SUBLANES)
            a = af_ref[pl.ds(rf, SUBLANES), :]
            hf = a * hf + uf_ref[pl.ds(rf, SUBLANES), :]
            pf = a * pf
            uf_ref[pl.ds(rf, SUBLANES), :] = hf
            af_ref[pl.ds(rf, SUBLANES), :] = pf
            a = ab_ref[pl.ds(rb, SUBLANES), :]
            hb = a * hb + ub_ref[pl.ds(rb, SUBLANES), :]
            pb = a * pb
            ub_ref[pl.ds(rb, SUBLANES), :] = hb
            ab_ref[pl.ds(rb, SUBLANES), :] = pb
        return hf, pf, hb, pb

    zero = jnp.zeros((SUBLANES, ct), F32)
    one = jnp.ones((SUBLANES, ct), F32)
    hf_end, pf_end, hb_end, pb_end = lax.fori_loop(0, lc // unroll, scan_body,
                                                   (zero, one, zero, one))

    def chunk_carry(h_end, p_end, down):
        f = h_end
        for _ in range(TIME_CHUNKS - 1):
            f = h_end + p_end * _shift_chunks(f, down)
        return _shift_chunks(f, down)

    cin_f = chunk_carry(hf_end, pf_end, True)
    cin_b = chunk_carry(hb_end, pb_end, False)

    def fix_chunk(i, _):
        r0 = pl.multiple_of(i * rows, rows)
        reps = rows // SUBLANES
        hf = uf_ref[pl.ds(r0, rows), :] + af_ref[pl.ds(r0, rows), :] * jnp.tile(cin_f, (reps, 1))
        hb = ub_ref[pl.ds(r0, rows), :] + ab_ref[pl.ds(r0, rows), :] * jnp.tile(cin_b, (reps, 1))
        xe_store(r0, rows, hf + hb)
        return 0

    lax.fori_loop(0, S // rows, fix_chunk, 0)

    for s in range(TIME_CHUNKS):
        hsum = jnp.concatenate(
            [xe_ref[g, pl.ds(s, lc, stride=SUBLANES), :] for g in range(ng)], axis=1)
        y = yr_ref[pl.ds(s * lc, lc), :]
        o_ref[pl.ds(s * lc, lc), :] = (hsum * jax.nn.gelu(y, approximate=True)).astype(BF16)


def _lru(proj, conv_w, conv_b, wg, bg, lam, B, S, C):
    ct = LRU_CT
    nct = C // ct
    rows = min(LRU_ROWS, S)
    return pl.pallas_call(
        functools.partial(_lru_kernel, rows=rows),
        grid=(B, nct),
        in_specs=[
            pl.BlockSpec((S, ct), lambda b, c: (b, c)),
            pl.BlockSpec((S, ct), lambda b, c: (b, nct + c)),
            pl.BlockSpec((4, ct), lambda b, c: (0, c)),
            pl.BlockSpec((1, ct), lambda b, c: (0, c)),
            pl.BlockSpec((1, ct, 4 * ct), lambda b, c: (c, 0, 0)),
            pl.BlockSpec((1, 1, 4 * ct), lambda b, c: (c, 0, 0)),
            pl.BlockSpec((2, ct), lambda b, c: (0, c)),
        ],
        out_specs=pl.BlockSpec((S, ct), lambda b, c: (b, c)),
        out_shape=jax.ShapeDtypeStruct((B * S, C), BF16),
        scratch_shapes=[
            pltpu.VMEM((ct // LANES, S + 3 * SUBLANES, LANES), F32),
            pltpu.VMEM((S, ct), F32), pltpu.VMEM((S, ct), F32),
            pltpu.VMEM((S, ct), F32), pltpu.VMEM((S, ct), F32),
        ],
        compiler_params=_cparams("parallel", "parallel"),
        name="lru",
    )(proj, proj, conv_w, conv_b, wg, bg, lam)


def _merge_kernel(x_ref, attn_ref, rec_ref, gl_ref, wa_ref, wr_ref, wo_ref, g2_ref,
                  rw_hi_ref, rw_lo_ref, rb_ref, x2_ref, h2_ref, lg_ref):
    D = x_ref.shape[1]
    gl = gl_ref[...]
    ma = jnp.dot(attn_ref[...], wa_ref[...], preferred_element_type=F32)
    mr = jnp.dot(rec_ref[...], wr_ref[...], preferred_element_type=F32)
    merged = jax.nn.sigmoid(gl[:, :D]) * ma + jax.nn.sigmoid(gl[:, D:]) * mr
    x2 = x_ref[...] + jnp.dot(merged.astype(BF16), wo_ref[...], preferred_element_type=F32)
    x2_ref[...] = x2
    h2 = _rms(x2, g2_ref[...])
    hi = h2.astype(BF16)
    lo = (h2 - hi.astype(F32)).astype(BF16)
    lg_ref[...] = (jnp.dot(hi, rw_hi_ref[...], preferred_element_type=F32)
                   + jnp.dot(lo, rw_hi_ref[...], preferred_element_type=F32)
                   + jnp.dot(hi, rw_lo_ref[...], preferred_element_type=F32)) + rb_ref[...]
    tm = h2.shape[0]
    bits = lax.bitcast_convert_type(hi.astype(F32), U32)
    words = bits[:, :D // 2] | (bits[:, D // 2:] >> BF16_BITS)
    for j in range(PACK_LINES):
        h2_ref[pl.ds(j, tm, stride=SUBLANES), :] = words[:, j * LANES:(j + 1) * LANES]
    for j in range(PACK_LINES, SUBLANES):
        h2_ref[pl.ds(j, tm, stride=SUBLANES), :] = jnp.zeros((tm, LANES), U32)


def _merge(x2d, attn, rec, proj, wa, wr, wo, g2, rw_hi, rw_lo, rb):
    T, D = x2d.shape
    tm = min(ROW_TILE, T)
    nj = D // LANES
    full = lambda a: pl.BlockSpec(a.shape, lambda i: (0,) * a.ndim)
    return pl.pallas_call(
        _merge_kernel,
        grid=(T // tm,),
        in_specs=[
            pl.BlockSpec((tm, D), lambda i: (i, 0)),
            pl.BlockSpec((tm, Q_WIDTH), lambda i: (i, 0)),
            pl.BlockSpec((tm, D), lambda i: (i, 0)),
            pl.BlockSpec((tm, 2 * D), lambda i: (i, 1)),
            full(wa), full(wr), full(wo), full(g2), full(rw_hi), full(rw_lo), full(rb),
        ],
        out_specs=[
            pl.BlockSpec((tm, D), lambda i: (i, 0)),
            pl.BlockSpec((tm * nj, LANES), lambda i: (i, 0)),
            pl.BlockSpec((tm, LANES), lambda i: (i, 0)),
        ],
        out_shape=[
            jax.ShapeDtypeStruct((T, D), F32),
            jax.ShapeDtypeStruct((T * nj, LANES), U32),
            jax.ShapeDtypeStruct((T, LANES), F32),
        ],
        compiler_params=_cparams("parallel"),
        name="merge",
    )(x2d, attn, rec, proj, wa, wr, wo, g2, rw_hi, rw_lo, rb)


def _route_kernel(lg_ref, gate_ref, gate_t_ref, eidx_t_ref, rank_t_ref, cnt_ref):
    i = pl.program_id(0)

    @pl.when(i == 0)
    def _():
        cnt_ref[...] = jnp.zeros_like(cnt_ref)

    tm = lg_ref.shape[0]
    work = lg_ref[...].T[:N_EXPERTS]
    expert = lax.broadcasted_iota(I32, work.shape, 0)
    sels, vals, idxs = [], [], []
    for _ in range(TOP_K):
        mx = jnp.max(work, axis=0, keepdims=True)
        idx = jnp.min(jnp.where(work == mx, expert, N_EXPERTS), axis=0, keepdims=True)
        sel = expert == idx
        sels.append(sel)
        vals.append(mx)
        idxs.append(idx)
        work = jnp.where(sel, NEG_BIG, work)
    ex = [jnp.exp(v - vals[0]) for v in vals]
    den = ex[0] + ex[1] + ex[2] + ex[3]
    member = (sels[0] | sels[1] | sels[2] | sels[3]).astype(BF16)
    r = lax.broadcasted_iota(I32, (tm, tm), 0)
    c = lax.broadcasted_iota(I32, (tm, tm), 1)
    tri = (r < c).astype(BF16)
    base = cnt_ref[:, 0:1]
    before = jnp.dot(member, tri, preferred_element_type=F32) + base
    ranks = [jnp.sum(jnp.where(sels[k], before, 0.0), axis=0, keepdims=True)
             for k in range(TOP_K)]
    gates = [ex[k] / den for k in range(TOP_K)]
    pad = SUBLANES - TOP_K
    gate_t = jnp.concatenate(gates + [jnp.zeros((pad, tm), F32)], axis=0)
    gate_t_ref[...] = gate_t
    eidx_t_ref[...] = jnp.concatenate(idxs + [jnp.zeros((pad, tm), I32)], axis=0)
    rank_t_ref[...] = jnp.concatenate(ranks + [jnp.zeros((pad, tm), F32)], axis=0).astype(I32)
    gate_ref[...] = jnp.concatenate(
        [gate_t, jnp.zeros((LANES - SUBLANES, tm), F32)], axis=0).T
    cnt_ref[...] = jnp.broadcast_to(
        base + jnp.sum(member.astype(F32), axis=1, keepdims=True), cnt_ref.shape)


def _route(logits):
    T = logits.shape[0]
    tm = min(ROW_TILE, T)
    spec = pl.BlockSpec((tm, LANES), lambda i: (i, 0))
    spec_t = pl.BlockSpec((SUBLANES, tm), lambda i: (0, i))
    return pl.pallas_call(
        _route_kernel,
        grid=(T // tm,),
        in_specs=[spec],
        out_specs=[spec, spec_t, spec_t, spec_t,
                   pl.BlockSpec((N_EXPERTS, LANES), lambda i: (0, 0))],
        out_shape=[
            jax.ShapeDtypeStruct((T, LANES), F32),
            jax.ShapeDtypeStruct((SUBLANES, T), F32),
            jax.ShapeDtypeStruct((SUBLANES, T), I32),
            jax.ShapeDtypeStruct((SUBLANES, T), I32),
            jax.ShapeDtypeStruct((N_EXPERTS, LANES), F32),
        ],
        compiler_params=_cparams("arbitrary"),
        name="route",
    )(logits)


def _tile(ref, row):
    return ref.at[pl.ds(pl.multiple_of(row * SUBLANES, SUBLANES), SUBLANES)]


def _dispatch_kernel(dest_ref, last_blk_ref, nused_ref, h2_ref, buf_ref,
                     src_ref, pad_ref, sem, zsem, *, n_tokens):
    i = pl.program_id(0)
    tm = h2_ref.shape[0] // SUBLANES
    blk = pad_ref.shape[0] // (2 * SUBLANES)
    nblocks = buf_ref.shape[0] // (blk * SUBLANES)

    @pl.when(i == 0)
    def _():
        li = lax.broadcasted_iota(I32, pad_ref.shape, 0)
        trash = (TOP_K * n_tokens + (li >> LOG2_SUBLANES)).astype(U32)
        pad_ref[...] = jnp.where((li & (SUBLANES - 1)) >= PACK_LINES, trash, jnp.uint32(0))
        rows = blk * SUBLANES

        def pad_block(b):
            half = pl.multiple_of(lax.rem(b, 2) * rows, rows)
            return pltpu.make_async_copy(
                pad_ref.at[pl.ds(half, rows)],
                buf_ref.at[pl.ds(pl.multiple_of(b * rows, rows), rows)], zsem)

        def expert_put(e, _):
            pl.when(last_blk_ref[e] >= 0)(lambda: pad_block(last_blk_ref[e]).start())
            return 0

        def expert_done(e, _):
            pl.when(last_blk_ref[e] >= 0)(lambda: pad_block(last_blk_ref[e]).wait())
            return 0

        def tail_put(b, _):
            pad_block(b).start()
            return 0

        def tail_done(b, _):
            pad_block(b).wait()
            return 0

        lax.fori_loop(0, N_EXPERTS, expert_put, 0)
        lax.fori_loop(nused_ref[0], nblocks, tail_put, 0)
        lax.fori_loop(0, N_EXPERTS, expert_done, 0)
        lax.fori_loop(nused_ref[0], nblocks, tail_done, 0)

    parity = lax.rem(i, 2)
    half = parity * TOP_K
    li = lax.broadcasted_iota(I32, h2_ref.shape, 0)
    tok = i * tm + (li >> LOG2_SUBLANES)
    is_id = (li & (SUBLANES - 1)) >= PACK_LINES
    data = h2_ref[...]
    for k in range(TOP_K):
        src_ref[half + k] = jnp.where(is_id, (k * n_tokens + tok).astype(U32), data)

    def issue_all(par):
        for r in range(tm):
            for k in range(TOP_K):
                pltpu.make_async_copy(_tile(src_ref.at[par * TOP_K + k], r),
                                      _tile(buf_ref, dest_ref[k * (tm + 1) + r]),
                                      sem.at[par]).start(priority=k % 2)

    for par in range(2):
        pl.when(parity == par)(functools.partial(issue_all, par))

    def wait_step(which):
        for k in range(TOP_K):
            pltpu.make_async_copy(src_ref.at[k], buf_ref.at[pl.ds(0, tm * SUBLANES)],
                                  sem.at[which]).wait()

    pl.when(i >= 1)(functools.partial(wait_step, 1 - parity))
    pl.when(i == pl.num_programs(0) - 1)(functools.partial(wait_step, parity))


def _dispatch(dest, last_blk, nused, h2_rows, n_rows, blk):
    T = h2_rows.shape[0] // SUBLANES
    tm = min(DMA_TILE, T)
    nt = T // tm
    words = pl.cdiv(TOP_K * (tm + 1), SMEM_1D_TILE) * SMEM_1D_TILE
    d = jnp.pad(dest.reshape(TOP_K, nt, tm).transpose(1, 0, 2), ((0, 0), (0, 0), (0, 1)))
    d = jnp.pad(d.reshape(nt, TOP_K * (tm + 1)), ((0, 0), (0, words - TOP_K * (tm + 1))))
    dest_flat = d.reshape(nt * words)
    smem = pl.BlockSpec(memory_space=pltpu.SMEM)
    return pl.pallas_call(
        functools.partial(_dispatch_kernel, n_tokens=T),
        grid=(nt,),
        in_specs=[
            pl.BlockSpec((words,), lambda i: (i,), memory_space=pltpu.SMEM),
            smem, smem,
            pl.BlockSpec((tm * SUBLANES, LANES), lambda i: (i, 0)),
        ],
        out_specs=pl.BlockSpec(memory_space=pl.ANY),
        out_shape=jax.ShapeDtypeStruct((n_rows * SUBLANES, LANES), U32),
        scratch_shapes=[pltpu.VMEM((2 * TOP_K, tm * SUBLANES, LANES), U32),
                        pltpu.VMEM((2 * blk * SUBLANES, LANES), U32),
                        pltpu.SemaphoreType.DMA((2,)), pltpu.SemaphoreType.DMA(())],
        compiler_params=_cparams("arbitrary"),
        name="dispatch",
    )(dest_flat, last_blk, nused, h2_rows)


def _ffn_kernel(blk_exp_ref, nused_ref, x_ref, wgu_ref, bgu_ref, wd_ref, bd_ref, y_ref,
                wgu_bf_ref, wd_bf_ref, out_ref, idv_ref, ids_ref, ssem, isem, *, n_tokens):
    b = pl.program_id(0)
    nb = pl.num_programs(0)
    tm = x_ref.shape[0] // SUBLANES
    groups = tm // LANES
    F = wd_ref.shape[1]
    nused = nused_ref[0]
    used = b < nused
    slot = lax.rem(b, 2)
    prev = 1 - slot
    new_expert = (b == 0) | (blk_exp_ref[b] != blk_exp_ref[jnp.maximum(b - 1, 0)])

    def send_row(s, r_hi, r_lo, row, priority):
        rid = ids_ref[s * SUBLANES + r_hi, r_lo]
        pltpu.make_async_copy(_tile(out_ref, s * tm + row), _tile(y_ref, rid),
                              ssem.at[s]).start(priority=priority)

    def send_rows_unrolled(s, lo, hi):
        for r in range(lo, hi):
            send_row(s, r // LANES, r % LANES, r, r % 2)

    def send_block_rolled(s):
        def body(g, _):
            def inner(l, _):
                send_row(s, g, l, g * LANES + l, 0)
                return 0
            return lax.fori_loop(0, LANES, inner, 0)
        lax.fori_loop(0, groups, body, 0)

    def wait_block(s):
        rows = tm * SUBLANES
        pltpu.make_async_copy(out_ref.at[pl.ds(0, rows)], y_ref.at[pl.ds(0, rows)],
                              ssem.at[s]).wait()

    def id_copy(s):
        return pltpu.make_async_copy(
            idv_ref, ids_ref.at[pl.ds(pl.multiple_of(s * SUBLANES, SUBLANES), SUBLANES)], isem)

    @pl.when(b == 0)
    def _():
        out_ref[...] = jnp.zeros_like(out_ref)
        rows = tm * SUBLANES
        pltpu.make_async_copy(out_ref.at[pl.ds(0, rows)],
                              y_ref.at[pl.ds(TOP_K * n_tokens * SUBLANES, rows)],
                              ssem.at[0]).start()
        pos = (lax.broadcasted_iota(I32, idv_ref.shape, 0) * LANES
               + lax.broadcasted_iota(I32, idv_ref.shape, 1))
        idv_ref[...] = TOP_K * n_tokens + tm + jnp.minimum(pos, tm - 1)
        id_copy(1).start()
        id_copy(1).wait()

    @pl.when((b >= 1) & (b <= nused))
    def _():
        id_copy(prev).wait()

    @pl.when(used)
    def _():
        idl = x_ref[pl.ds(PACK_LINES, tm, stride=SUBLANES), :].astype(I32)
        rr = lax.broadcasted_iota(I32, idl.shape, 0)
        ll = lax.broadcasted_iota(I32, idl.shape, 1)
        diag = jnp.where((rr & (LANES - 1)) == ll, idl, 0)
        idv_ref[pl.ds(0, groups), :] = jnp.sum(diag.reshape(groups, LANES, LANES), axis=1)
        id_copy(slot).start()

    @pl.when(used & new_expert)
    def _():
        wgu_bf_ref[...] = wgu_ref[0].astype(BF16)
        wd_bf_ref[...] = wd_ref[0].astype(BF16)

    def block(slot, prev):
        words = [x_ref[pl.ds(j, tm, stride=SUBLANES), :] for j in range(PACK_LINES)]
        hi = [lax.bitcast_convert_type(w & jnp.uint32(HIGH_HALF), F32) for w in words]
        lo = [lax.bitcast_convert_type(w << BF16_BITS, F32) for w in words]
        x = jnp.concatenate(hi + lo, axis=1).astype(BF16)
        acts = []
        for c in range(FFN_SLICES):
            send_rows_unrolled(prev, c * tm // FFN_SLICES, (c + 1) * tm // FFN_SLICES)
            w = F // FFN_SLICES
            g = jnp.dot(x, wgu_bf_ref[:, c * w:(c + 1) * w], preferred_element_type=F32)
            u = jnp.dot(x, wgu_bf_ref[:, F + c * w:F + (c + 1) * w],
                        preferred_element_type=F32)
            g = jnp.minimum(g + bgu_ref[0, :, c * w:(c + 1) * w], SWIGLU_LIMIT)
            u = jnp.clip(u + bgu_ref[0, :, F + c * w:F + (c + 1) * w],
                         -SWIGLU_LIMIT, SWIGLU_LIMIT)
            acts.append((g * jax.nn.sigmoid(SWIGLU_ALPHA * g) * (u + 1.0)).astype(BF16))
        act = jnp.concatenate(acts, axis=1)
        y = jnp.dot(act, wd_bf_ref[...], preferred_element_type=F32) + bd_ref[0]
        wait_block(slot)
        base = slot * (tm * SUBLANES)
        for j in range(SUBLANES):
            out_ref[pl.ds(base + j, tm, stride=SUBLANES), :] = y[:, j * LANES:(j + 1) * LANES]

    for parity in range(2):
        pl.when(used & (slot == parity))(functools.partial(block, parity, 1 - parity))

    @pl.when(b == nused)
    def _():
        wait_block(slot)
        send_block_rolled(prev)
        wait_block(prev)

    @pl.when(used & (b == nb - 1))
    def _():
        wait_block(prev)
        id_copy(slot).wait()
        send_block_rolled(slot)
        wait_block(slot)


def _ffn(blk_exp, nused, buf, wgu, bgu, wd, bd, tm, n_tokens):
    nb = buf.shape[0] // (tm * SUBLANES)
    D = wgu.shape[1]
    F = wd.shape[1]

    def xmap(b, be, nu):
        return (jnp.minimum(b, nu[0] - 1), 0)

    def wmap(b, be, nu):
        return (be[jnp.minimum(b, nu[0] - 1)], 0, 0)

    grid_spec = pltpu.PrefetchScalarGridSpec(
        num_scalar_prefetch=2,
        grid=(nb,),
        in_specs=[
            pl.BlockSpec((tm * SUBLANES, LANES), xmap),
            pl.BlockSpec((1, D, 2 * F), wmap),
            pl.BlockSpec((1, 1, 2 * F), wmap),
            pl.BlockSpec((1, F, D), wmap),
            pl.BlockSpec((1, 1, D), wmap),
        ],
        out_specs=pl.BlockSpec(memory_space=pl.ANY),
        scratch_shapes=[
            pltpu.VMEM((D, 2 * F), BF16), pltpu.VMEM((F, D), BF16),
            pltpu.VMEM((2 * tm * SUBLANES, LANES), F32),
            pltpu.VMEM((SUBLANES, LANES), I32),
            pltpu.SMEM((2 * SUBLANES, LANES), I32),
            pltpu.SemaphoreType.DMA((2,)), pltpu.SemaphoreType.DMA(()),
        ],
    )
    return pl.pallas_call(
        functools.partial(_ffn_kernel, n_tokens=n_tokens),
        grid_spec=grid_spec,
        out_shape=jax.ShapeDtypeStruct(((TOP_K * n_tokens + 2 * tm) * SUBLANES, LANES), F32),
        compiler_params=_cparams("arbitrary"),
        name="ffn",
    )(blk_exp, nused, buf, wgu, bgu, wd, bd)


def _combine_kernel(gate_ref, x2_ref, fg_ref, *refs):
    y_refs, o_ref = refs[:TOP_K], refs[TOP_K]
    tm, D = x2_ref.shape
    nj = D // LANES
    gate = gate_ref[...]
    cols = []
    for j in range(nj):
        acc = None
        for k in range(TOP_K):
            part = gate[:, k:k + 1] * y_refs[k][pl.ds(j, tm, stride=nj), :]
            acc = part if acc is None else acc + part
        cols.append(acc)
    y = jnp.concatenate(cols, axis=1)
    o_ref[...] = _rms(x2_ref[...] + y, fg_ref[...])


def _combine(gate4, x2, fg, y_rows):
    T, D = x2.shape
    tm = min(ROW_TILE, T)
    nj = D // LANES
    nt = T // tm

    def slot_spec(k):
        return pl.BlockSpec((tm * nj, LANES), lambda i: (k * nt + i, 0))

    return pl.pallas_call(
        _combine_kernel,
        grid=(nt,),
        in_specs=[
            pl.BlockSpec((tm, LANES), lambda i: (i, 0)),
            pl.BlockSpec((tm, D), lambda i: (i, 0)),
            pl.BlockSpec((1, D), lambda i: (0, 0)),
        ] + [slot_spec(k) for k in range(TOP_K)],
        out_specs=pl.BlockSpec((tm, D), lambda i: (i, 0)),
        out_shape=jax.ShapeDtypeStruct((T, D), F32),
        compiler_params=_cparams("parallel"),
        name="combine",
    )(gate4, x2, fg, *([y_rows] * TOP_K))


def _rope_tables(S):
    rows = S // GRID_W
    row = jnp.repeat(jnp.arange(rows, dtype=I32), GRID_W).astype(F32)
    col = jnp.tile(jnp.arange(GRID_W, dtype=I32), rows).astype(F32)
    inv = ROPE_THETA ** (-jnp.arange(0, AXIS_DIM, 2, dtype=F32) / AXIS_DIM)
    ang_r = row[:, None] * inv[None, :]
    ang_c = col[:, None] * inv[None, :]
    cos = jnp.concatenate([jnp.cos(ang_r)] * 2 + [jnp.cos(ang_c)] * 2, axis=1)
    sin = jnp.concatenate([-jnp.sin(ang_r), jnp.sin(ang_r), -jnp.sin(ang_c), jnp.sin(ang_c)], axis=1)
    return jnp.tile(cos, (1, 2)), jnp.tile(sin, (1, 2))


def _block_diag_gates(wa, ba, wi, bi, ct):
    nb, bw = wa.shape[1], wa.shape[2]
    per = ct // bw
    nct = nb // per
    eye = jnp.eye(per, dtype=wa.dtype)

    def tiles(w):
        w = w.reshape(nct, per, bw, bw)
        return jnp.einsum('cpij,pq->cpiqj', w, eye).reshape(nct, ct, ct)

    wg = jnp.concatenate([tiles(wa[0]), tiles(wi[0]), tiles(wa[1]), tiles(wi[1])], axis=2)
    bias = lambda b: b.reshape(nct, 1, ct)
    bg = jnp.concatenate([bias(ba[0]), bias(bi[0]), bias(ba[1]), bias(bi[1])], axis=2)
    return (0.5 * wg).astype(BF16), 0.5 * bg


def _layer(x2d, B, S, norm1_g, w_in, b_in, q_norm_g, k_norm_g, conv_w, conv_b, lru_wa, lru_ba,
           lru_wi, lru_bi, lru_lambda, w_attn_o, w_lru_o, w_out, norm2_g, w_router, b_router,
           w_gu, b_gu, w_down, b_down, out_g):
    T, D = x2d.shape
    C = conv_w.shape[1]
    nqkv = Q_WIDTH + 2 * KV_WIDTH
    row2 = lambda v: v.reshape(1, -1)

    cos_t, sin_t = _rope_tables(S)
    head = jnp.arange(LANES) // HEAD_DIM
    hsum = (head[:, None] == head[None, :]).astype(BF16)
    q, k, v = _qkv(x2d, row2(norm1_g), w_in[:, :nqkv].astype(BF16), row2(b_in[:nqkv]),
                   cos_t, sin_t, row2(jnp.tile(q_norm_g, 2)), row2(jnp.tile(k_norm_g, 2)),
                   hsum, B, S)
    proj = _proj(x2d, row2(norm1_g), w_in[:, nqkv:].astype(BF16), row2(b_in[nqkv:]))
    attn = _attn(q, k, v)
    wg, bg = _block_diag_gates(lru_wa, lru_ba, lru_wi, lru_bi, LRU_CT)
    rec = _lru(proj, conv_w, row2(conv_b), wg, bg, lru_lambda, B, S, C)

    pad = LANES - N_EXPERTS
    rw = jnp.pad(w_router, ((0, 0), (0, pad)))
    rw_hi = rw.astype(BF16)
    rw_lo = (rw - rw_hi.astype(F32)).astype(BF16)
    rb = jnp.pad(b_router, (0, pad), constant_values=NEG_BIG).reshape(1, LANES)
    x2, h2_rows, logits = _merge(x2d, attn, rec, proj, w_attn_o.astype(BF16),
                                 w_lru_o.astype(BF16), w_out.astype(BF16), row2(norm2_g),
                                 rw_hi, rw_lo, rb)

    gate4, _, eidx_t, rank_t, cnt = _route(logits)
    counts = cnt[:, 0].astype(I32)
    tm = FFN_TM
    nblk = (counts + tm - 1) // tm
    pend_blk = jnp.cumsum(nblk)
    pstart = (pend_blk - nblk) * tm
    A = T * TOP_K
    nb = (A + N_EXPERTS * (tm - 1) + tm - 1) // tm
    blk_exp = jnp.minimum(
        jnp.sum(pend_blk[None, :] <= jnp.arange(nb, dtype=I32)[:, None], axis=1),
        N_EXPERTS - 1).astype(I32)
    nused = pend_blk[-1:].astype(I32)
    first_row = jnp.sum(jnp.where(eidx_t[:TOP_K, :, None] == jnp.arange(N_EXPERTS), pstart, 0),
                        axis=-1)
    dest = (first_row + rank_t[:TOP_K]).astype(I32)
    last_blk = jnp.where(nblk > 0, pend_blk - 1, -1).astype(I32)

    buf = _dispatch(dest, last_blk, nused, h2_rows, nb * tm, tm)
    y_rows = _ffn(blk_exp, nused, buf, w_gu, b_gu.reshape(N_EXPERTS, 1, -1), w_down,
                  b_down.reshape(N_EXPERTS, 1, -1), tm, T)
    return _combine(gate4, x2, row2(out_g), y_rows)


def kernel(x, norm1_g, w_in, b_in, q_norm_g, k_norm_g, conv_w, conv_b, lru_wa, lru_ba, lru_wi,
           lru_bi, lru_lambda, w_attn_o, w_lru_o, w_out, norm2_g, w_router, b_router, w_gu, b_gu,
           w_down, b_down, final_g):
    B, S, D = x.shape
    depth = norm1_g.shape[0]
    assert depth == 1, "the fused final RMSNorm assumes a single layer"
    assert S % (TIME_CHUNKS * SCAN_UNROLL) == 0 and S % GRID_W == 0
    assert D == 2 * PACK_LINES * LANES, "a token's bf16 row must fill PACK_LINES word lines"
    out = _layer(x.reshape(B * S, D), B, S, norm1_g[0], w_in[0], b_in[0], q_norm_g[0],
                 k_norm_g[0], conv_w[0], conv_b[0], lru_wa[0], lru_ba[0], lru_wi[0], lru_bi[0],
                 lru_lambda[0], w_attn_o[0], w_lru_o[0], w_out[0], norm2_g[0], w_router[0],
                 b_router[0], w_gu[0], b_gu[0], w_down[0], b_down[0], final_g)
    return out.reshape(B, S, D)
```

```python
import functools

import jax
import jax.numpy as jnp
from jax import lax
from jax.experimental import pallas as pl
from jax.experimental.pallas import tpu as pltpu

F32 = jnp.float32
BF16 = jnp.bfloat16
I32 = jnp.int32
U32 = jnp.uint32

LANES = 128
SUBLANES = 8
PACK_LINES = 4
BF16_BITS = 16
HIGH_HALF = 0xFFFF0000
LOG2_SUBLANES = 3
EXPERT_ID_BITS = 8
EXPERT_ID_MASK = 0xFF
NO_EXPERT = 0xFFFFFFFF
SMEM_1D_TILE = 1024
VMEM_LIMIT_BYTES = 56 * 1024 * 1024

HEAD_DIM = 64
N_Q_HEADS = 8
N_KV_HEADS = 2
Q_GROUP = N_Q_HEADS // N_KV_HEADS
Q_WIDTH = N_Q_HEADS * HEAD_DIM
KV_WIDTH = N_KV_HEADS * HEAD_DIM
AXIS_DIM = HEAD_DIM // 2
ROT_HALF = AXIS_DIM // 2
ROPE_THETA = 10000.0
GRID_W = 64
RG_C = 8.0
N_EXPERTS = 32
TOP_K = 4
SWIGLU_LIMIT = 7.0
SWIGLU_ALPHA = 1.702
NORM_EPS = 1e-6
LOG2E = 1.4426950408889634
NEG_BIG = -1e30
TINY = 1e-30

ROW_TILE = 512
ATT_TQ = 1024
ATT_TK = 512
LRU_CT = 256
LRU_ROWS = 512
SCAN_UNROLL = 32
TIME_CHUNKS = SUBLANES
FFN_TM = 512
FFN_SLICES = 4
DMA_TILE = 256


def _cparams(*sem):
    return pltpu.CompilerParams(dimension_semantics=sem, vmem_limit_bytes=VMEM_LIMIT_BYTES)


def _rms(x, g):
    return x * lax.rsqrt(jnp.mean(x * x, axis=-1, keepdims=True) + NORM_EPS) * g


def _qkv_kernel(x_ref, g1_ref, w_ref, b_ref, cos_ref, sin_ref, qg_ref, kg_ref, hsum_ref,
                q_ref, k_ref, v_ref):
    h = _rms(x_ref[...], g1_ref[...]).astype(BF16)
    p = jnp.dot(h, w_ref[...], preferred_element_type=F32) + b_ref[...]
    cos = cos_ref[...]
    sin = sin_ref[...]
    hsum = hsum_ref[...]
    lane = lax.broadcasted_iota(I32, cos.shape, 1)
    first_half = (lane % AXIS_DIM) < ROT_HALF

    def norm_rope(c, gain):
        sq = c * c
        hi = sq.astype(BF16)
        lo = (sq - hi.astype(F32)).astype(BF16)
        ms = (jnp.dot(hi, hsum, preferred_element_type=F32)
              + jnp.dot(lo, hsum, preferred_element_type=F32)) * (1.0 / HEAD_DIM)
        y = c * lax.rsqrt(ms + NORM_EPS) * gain
        partner = jnp.where(first_half, pltpu.roll(y, LANES - ROT_HALF, 1),
                            pltpu.roll(y, ROT_HALF, 1))
        return y * cos + partner * sin

    qg = qg_ref[...]
    for c in range(Q_WIDTH // LANES):
        y = norm_rope(p[:, c * LANES:(c + 1) * LANES], qg) * (HEAD_DIM ** -0.5 * LOG2E)
        yt = y.T.astype(BF16)
        q_ref[0, 2 * c] = yt[:HEAD_DIM]
        q_ref[0, 2 * c + 1] = yt[HEAD_DIM:]
    yk = norm_rope(p[:, Q_WIDTH:Q_WIDTH + KV_WIDTH], kg_ref[...])
    k_ref[0, 0] = yk[:, :HEAD_DIM].astype(BF16)
    k_ref[0, 1] = yk[:, HEAD_DIM:].astype(BF16)
    vt = p[:, Q_WIDTH + KV_WIDTH:].T
    tm = vt.shape[1]
    ones_row = (lax.broadcasted_iota(I32, (HEAD_DIM, tm), 0) == 0).astype(BF16)
    v_ref[0, 0] = jnp.concatenate([vt[:HEAD_DIM].astype(BF16), ones_row], axis=0)
    v_ref[0, 1] = jnp.concatenate([vt[HEAD_DIM:].astype(BF16), ones_row], axis=0)


def _qkv(x2d, g1, w_qkv, b_qkv, cos_t, sin_t, qg, kg, hsum, B, S):
    T, D = x2d.shape
    tm = min(ROW_TILE, S)
    ns = S // tm
    n = w_qkv.shape[1]
    full = lambda shape: pl.BlockSpec(shape, lambda i: (0,) * len(shape))
    return pl.pallas_call(
        _qkv_kernel,
        grid=(T // tm,),
        in_specs=[
            pl.BlockSpec((tm, D), lambda i: (i, 0)),
            full((1, D)), full((D, n)), full((1, n)),
            pl.BlockSpec((tm, LANES), lambda i: (i % ns, 0)),
            pl.BlockSpec((tm, LANES), lambda i: (i % ns, 0)),
            full((1, LANES)), full((1, LANES)), full((LANES, LANES)),
        ],
        out_specs=[
            pl.BlockSpec((1, N_Q_HEADS, HEAD_DIM, tm), lambda i: (i // ns, 0, 0, i % ns)),
            pl.BlockSpec((1, N_KV_HEADS, tm, HEAD_DIM), lambda i: (i // ns, 0, i % ns, 0)),
            pl.BlockSpec((1, N_KV_HEADS, LANES, tm), lambda i: (i // ns, 0, 0, i % ns)),
        ],
        out_shape=[
            jax.ShapeDtypeStruct((B, N_Q_HEADS, HEAD_DIM, S), BF16),
            jax.ShapeDtypeStruct((B, N_KV_HEADS, S, HEAD_DIM), BF16),
            jax.ShapeDtypeStruct((B, N_KV_HEADS, LANES, S), BF16),
        ],
        compiler_params=_cparams("parallel"),
        name="qkv",
    )(x2d, g1, w_qkv, b_qkv, cos_t, sin_t, qg, kg, hsum)


def _proj_kernel(x_ref, g_ref, w_ref, b_ref, o_ref):
    h = _rms(x_ref[...], g_ref[...]).astype(BF16)
    o_ref[...] = jnp.dot(h, w_ref[...], preferred_element_type=F32) + b_ref[...]


def _proj(x2d, g, w, b):
    T, D = x2d.shape
    n = w.shape[1]
    tm = min(ROW_TILE, T)
    return pl.pallas_call(
        _proj_kernel,
        grid=(T // tm,),
        in_specs=[
            pl.BlockSpec((tm, D), lambda i: (i, 0)),
            pl.BlockSpec((1, D), lambda i: (0, 0)),
            pl.BlockSpec((D, n), lambda i: (0, 0)),
            pl.BlockSpec((1, n), lambda i: (0, 0)),
        ],
        out_specs=pl.BlockSpec((tm, n), lambda i: (i, 0)),
        out_shape=jax.ShapeDtypeStruct((T, n), F32),
        compiler_params=_cparams("parallel"),
        name="proj",
    )(x2d, g, w, b)


def _attn_kernel(q_ref, k_ref, v_ref, o_ref, s0_ref, s1_ref, *, tk):
    tq = q_ref.shape[3]
    S = k_ref.shape[2]
    M = Q_GROUP * tq
    n = S // tk
    qT = jnp.concatenate([q_ref[0, h] for h in range(Q_GROUP)], axis=1)

    def scores(j, s_ref):
        off = pl.multiple_of(j * tk, tk)
        s_ref[...] = jnp.dot(k_ref[0, 0, pl.ds(off, tk), :], qT, preferred_element_type=F32)

    def absorb(j, s_ref, carry):
        m, acc = carry
        off = pl.multiple_of(j * tk, tk)
        vc = v_ref[0, 0, :, pl.ds(off, tk)]
        s = s_ref[...]
        m_new = jnp.maximum(m, jnp.max(s, axis=0, keepdims=True))
        alpha = jnp.exp2(m - m_new)
        p = jnp.exp2(s - m_new).astype(BF16)
        return m_new, alpha * acc + jnp.dot(vc, p, preferred_element_type=F32)

    def pair(i, carry):
        scores(2 * i + 1, s1_ref)
        carry = absorb(2 * i, s0_ref, carry)
        scores(2 * i + 2, s0_ref)
        return absorb(2 * i + 1, s1_ref, carry)

    scores(0, s0_ref)
    carry = (jnp.full((1, M), NEG_BIG, F32), jnp.zeros((LANES, M), F32))
    carry = lax.fori_loop(0, n // 2 - 1, pair, carry)
    scores(n - 1, s1_ref)
    carry = absorb(n - 2, s0_ref, carry)
    _, acc = absorb(n - 1, s1_ref, carry)
    outT = acc[:HEAD_DIM] / acc[HEAD_DIM:HEAD_DIM + 1]
    stacked = jnp.concatenate(
        [outT[:, g * tq:(g + 1) * tq] for g in range(Q_GROUP)], axis=0)
    o_ref[...] = stacked.T.astype(BF16)


def _attn(qT, k, vT):
    B, _, _, S = qT.shape
    tq = min(ATT_TQ, S)
    tk = min(ATT_TK, S)
    nq = S // tq
    assert (S // tk) % 2 == 0, "key chunks are processed in pairs"
    score_buf = pltpu.VMEM((tk, Q_GROUP * tq), F32)
    return pl.pallas_call(
        functools.partial(_attn_kernel, tk=tk),
        grid=(B, N_KV_HEADS, nq),
        in_specs=[
            pl.BlockSpec((1, Q_GROUP, HEAD_DIM, tq), lambda b, g, i: (b, g, 0, i)),
            pl.BlockSpec((1, 1, S, HEAD_DIM), lambda b, g, i: (b, g, 0, 0)),
            pl.BlockSpec((1, 1, LANES, S), lambda b, g, i: (b, g, 0, 0)),
        ],
        out_specs=pl.BlockSpec((tq, Q_GROUP * HEAD_DIM), lambda b, g, i: (b * nq + i, g)),
        out_shape=jax.ShapeDtypeStruct((B * S, Q_WIDTH), BF16),
        scratch_shapes=[score_buf, score_buf],
        compiler_params=_cparams("parallel", "parallel", "parallel"),
        name="attn",
    )(qT, k, vT)


def _shift_chunks(v, down):
    row = lax.broadcasted_iota(I32, v.shape, 0)
    if down:
        return jnp.where(row == 0, 0.0, pltpu.roll(v, 1, 0))
    return jnp.where(row == SUBLANES - 1, 0.0, pltpu.roll(v, SUBLANES - 1, 0))


def _lru_kernel(xr_ref, yr_ref, cw_ref, cb_ref, wg_ref, bg_ref, lam_ref, o_ref,
                xe_ref, af_ref, uf_ref, ab_ref, ub_ref, *, rows):
    S, ct = xr_ref.shape
    lc = S // TIME_CHUNKS
    halo = SUBLANES

    ng = ct // LANES

    def xe_rows(r0, n):
        return jnp.concatenate([xe_ref[g, pl.ds(r0, n), :] for g in range(ng)], axis=1)

    def xe_store(r0, n, val):
        for g in range(ng):
            xe_ref[g, pl.ds(r0, n), :] = val[:, g * LANES:(g + 1) * LANES]

    for s in range(TIME_CHUNKS):
        for g in range(ng):
            xe_ref[g, pl.ds(halo + s, lc, stride=SUBLANES), :] = (
                xr_ref[pl.ds(s * lc, lc), g * LANES:(g + 1) * LANES])
    xe_store(0, halo, _shift_chunks(xe_rows(S, halo), True))
    first = xe_rows(halo, halo)
    second = xe_rows(2 * halo, halo)
    xe_store(S + halo, halo, _shift_chunks(first, False))
    xe_store(S + 2 * halo, halo, _shift_chunks(second, False))

    cw = cw_ref[...]
    cb = cb_ref[...]
    lam = lam_ref[...]
    log_sig = jnp.minimum(lam, 0.0) - jnp.log(1.0 + jnp.exp(-jnp.abs(lam)))
    c_half = (0.5 * RG_C * LOG2E) * log_sig
    wg = wg_ref[0]
    bg = bg_ref[0]

    def gate_chunk(i, _):
        r0 = pl.multiple_of(i * rows, rows)
        xc = cb
        for j in range(4):
            xc = xc + cw[j:j + 1, :] * xe_rows(r0 + j * halo, rows)
        t = jnp.tanh(jnp.dot(xc.astype(BF16), wg, preferred_element_type=F32) + bg)
        x_half = 0.5 * xc
        for d, (a_ref, u_ref) in enumerate(((af_ref, uf_ref), (ab_ref, ub_ref))):
            t_r = t[:, (2 * d) * ct:(2 * d + 1) * ct]
            t_i = t[:, (2 * d + 1) * ct:(2 * d + 2) * ct]
            ch = c_half[d:d + 1, :]
            a = jnp.exp2(t_r * ch + ch)
            a_ref[pl.ds(r0, rows), :] = a
            v = 1.0 - a * a
            root = jnp.maximum(v, 0.0) * lax.rsqrt(jnp.maximum(v, TINY))
            u_ref[pl.ds(r0, rows), :] = root * ((t_i + 1.0) * x_half)
        return 0

    lax.fori_loop(0, S // rows, gate_chunk, 0)

    unroll = SCAN_UNROLL

    def scan_body(i, carry):
        hf, pf, hb, pb = carry
        for k in range(unroll):
            tf = i * unroll + k
            rf = pl.multiple_of(tf * SUBLANES, SUBLANES)
            rb = pl.multiple_of((lc - 1 - tf) * SUBLANES, SUBLANES)
            a = af_ref[pl.ds(rf, SUBLANES), :]
            hf = a * hf + uf_ref[pl.ds(rf, SUBLANES), :]
            pf = a * pf
            uf_ref[pl.ds(rf, SUBLANES), :] = hf
            af_ref[pl.ds(rf, SUBLANES), :] = pf
            a = ab_ref[pl.ds(rb, SUBLANES), :]
            hb = a * hb + ub_ref[pl.ds(rb, SUBLANES), :]
            pb = a * pb
            ub_ref[pl.ds(rb, SUBLANES), :] = hb
            ab_ref[pl.ds(rb, SUBLANES), :] = pb
        return hf, pf, hb, pb

    zero = jnp.zeros((SUBLANES, ct), F32)
    one = jnp.ones((SUBLANES, ct), F32)
    hf_end, pf_end, hb_end, pb_end = lax.fori_loop(0, lc // unroll, scan_body,
                                                   (zero, one, zero, one))

    def chunk_carry(h_end, p_end, down):
        f = h_end
        for _ in range(TIME_CHUNKS - 1):
            f = h_end + p_end * _shift_chunks(f, down)
        return _shift_chunks(f, down)

    cin_f = chunk_carry(hf_end, pf_end, True)
    cin_b = chunk_carry(hb_end, pb_end, False)

    def fix_chunk(i, _):
        r0 = pl.multiple_of(i * rows, rows)
        reps = rows // SUBLANES
        hf = uf_ref[pl.ds(r0, rows), :] + af_ref[pl.ds(r0, rows), :] * jnp.tile(cin_f, (reps, 1))
        hb = ub_ref[pl.ds(r0, rows), :] + ab_ref[pl.ds(r0, rows), :] * jnp.tile(cin_b, (reps, 1))
        xe_store(r0, rows, hf + hb)
        return 0

    lax.fori_loop(0, S // rows, fix_chunk, 0)

    for s in range(TIME_CHUNKS):
        hsum = jnp.concatenate(
            [xe_ref[g, pl.ds(s, lc, stride=SUBLANES), :] for g in range(ng)], axis=1)
        y = yr_ref[pl.ds(s * lc, lc), :]
        o_ref[pl.ds(s * lc, lc), :] = (hsum * jax.nn.gelu(y, approximate=True)).astype(BF16)


def _lru(proj, conv_w, conv_b, wg, bg, lam, B, S, C):
    ct = LRU_CT
    nct = C // ct
    rows = min(LRU_ROWS, S)
    return pl.pallas_call(
        functools.partial(_lru_kernel, rows=rows),
        grid=(B, nct),
        in_specs=[
            pl.BlockSpec((S, ct), lambda b, c: (b, c)),
            pl.BlockSpec((S, ct), lambda b, c: (b, nct + c)),
            pl.BlockSpec((4, ct), lambda b, c: (0, c)),
            pl.BlockSpec((1, ct), lambda b, c: (0, c)),
            pl.BlockSpec((1, ct, 4 * ct), lambda b, c: (c, 0, 0)),
            pl.BlockSpec((1, 1, 4 * ct), lambda b, c: (c, 0, 0)),
            pl.BlockSpec((2, ct), lambda b, c: (0, c)),
        ],
        out_specs=pl.BlockSpec((S, ct), lambda b, c: (b, c)),
        out_shape=jax.ShapeDtypeStruct((B * S, C), BF16),
        scratch_shapes=[
            pltpu.VMEM((ct // LANES, S + 3 * SUBLANES, LANES), F32),
            pltpu.VMEM((S, ct), F32), pltpu.VMEM((S, ct), F32),
            pltpu.VMEM((S, ct), F32), pltpu.VMEM((S, ct), F32),
        ],
        compiler_params=_cparams("parallel", "parallel"),
        name="lru",
    )(proj, proj, conv_w, conv_b, wg, bg, lam)


def _merge_kernel(x_ref, attn_ref, rec_ref, gl_ref, wa_ref, wr_ref, wo_ref, g2_ref,
                  rw_hi_ref, rw_lo_ref, rb_ref, x2_ref, h2_ref, lg_ref):
    D = x_ref.shape[1]
    gl = gl_ref[...]
    ma = jnp.dot(attn_ref[...], wa_ref[...], preferred_element_type=F32)
    mr = jnp.dot(rec_ref[...], wr_ref[...], preferred_element_type=F32)
    merged = jax.nn.sigmoid(gl[:, :D]) * ma + jax.nn.sigmoid(gl[:, D:]) * mr
    x2 = x_ref[...] + jnp.dot(merged.astype(BF16), wo_ref[...], preferred_element_type=F32)
    x2_ref[...] = x2
    h2 = _rms(x2, g2_ref[...])
    hi = h2.astype(BF16)
    lo = (h2 - hi.astype(F32)).astype(BF16)
    lg_ref[...] = (jnp.dot(hi, rw_hi_ref[...], preferred_element_type=F32)
                   + jnp.dot(lo, rw_hi_ref[...], preferred_element_type=F32)
                   + jnp.dot(hi, rw_lo_ref[...], preferred_element_type=F32)) + rb_ref[...]
    tm = h2.shape[0]
    bits = lax.bitcast_convert_type(hi.astype(F32), U32)
    words = bits[:, :D // 2] | (bits[:, D // 2:] >> BF16_BITS)
    for j in range(PACK_LINES):
        h2_ref[pl.ds(j, tm, stride=SUBLANES), :] = words[:, j * LANES:(j + 1) * LANES]
    for j in range(PACK_LINES, SUBLANES):
        h2_ref[pl.ds(j, tm, stride=SUBLANES), :] = jnp.zeros((tm, LANES), U32)


def _merge(x2d, attn, rec, proj, wa, wr, wo, g2, rw_hi, rw_lo, rb):
    T, D = x2d.shape
    tm = min(ROW_TILE, T)
    nj = D // LANES
    full = lambda a: pl.BlockSpec(a.shape, lambda i: (0,) * a.ndim)
    return pl.pallas_call(
        _merge_kernel,
        grid=(T // tm,),
        in_specs=[
            pl.BlockSpec((tm, D), lambda i: (i, 0)),
            pl.BlockSpec((tm, Q_WIDTH), lambda i: (i, 0)),
            pl.BlockSpec((tm, D), lambda i: (i, 0)),
            pl.BlockSpec((tm, 2 * D), lambda i: (i, 1)),
            full(wa), full(wr), full(wo), full(g2), full(rw_hi), full(rw_lo), full(rb),
        ],
        out_specs=[
            pl.BlockSpec((tm, D), lambda i: (i, 0)),
            pl.BlockSpec((tm * nj, LANES), lambda i: (i, 0)),
            pl.BlockSpec((tm, LANES), lambda i: (i, 0)),
        ],
        out_shape=[
            jax.ShapeDtypeStruct((T, D), F32),
            jax.ShapeDtypeStruct((T * nj, LANES), U32),
            jax.ShapeDtypeStruct((T, LANES), F32),
        ],
        compiler_params=_cparams("parallel"),
        name="merge",
    )(x2d, attn, rec, proj, wa, wr, wo, g2, rw_hi, rw_lo, rb)


def _route_kernel(lg_ref, gate_ref, epack_ref, gate_t_ref, eidx_t_ref, rank_t_ref, cnt_ref):
    i = pl.program_id(0)

    @pl.when(i == 0)
    def _():
        cnt_ref[...] = jnp.zeros_like(cnt_ref)

    tm = lg_ref.shape[0]
    work = lg_ref[...].T[:N_EXPERTS]
    expert = lax.broadcasted_iota(I32, work.shape, 0)
    sels, vals, idxs = [], [], []
    for _ in range(TOP_K):
        mx = jnp.max(work, axis=0, keepdims=True)
        idx = jnp.min(jnp.where(work == mx, expert, N_EXPERTS), axis=0, keepdims=True)
        sel = expert == idx
        sels.append(sel)
        vals.append(mx)
        idxs.append(idx)
        work = jnp.where(sel, NEG_BIG, work)
    ex = [jnp.exp(v - vals[0]) for v in vals]
    den = ex[0] + ex[1] + ex[2] + ex[3]
    member = (sels[0] | sels[1] | sels[2] | sels[3]).astype(BF16)
    r = lax.broadcasted_iota(I32, (tm, tm), 0)
    c = lax.broadcasted_iota(I32, (tm, tm), 1)
    tri = (r < c).astype(BF16)
    base = cnt_ref[:, 0:1]
    before = jnp.dot(member, tri, preferred_element_type=F32) + base
    ranks = [jnp.sum(jnp.where(sels[k], before, 0.0), axis=0, keepdims=True)
             for k in range(TOP_K)]
    gates = [ex[k] / den for k in range(TOP_K)]
    pad = SUBLANES - TOP_K
    gate_t = jnp.concatenate(gates + [jnp.zeros((pad, tm), F32)], axis=0)
    gate_t_ref[...] = gate_t
    eidx_t_ref[...] = jnp.concatenate(idxs + [jnp.zeros((pad, tm), I32)], axis=0)
    rank_t_ref[...] = jnp.concatenate(ranks + [jnp.zeros((pad, tm), F32)], axis=0).astype(I32)
    gate_ref[...] = jnp.concatenate(
        [gate_t, jnp.zeros((LANES - SUBLANES, tm), F32)], axis=0).T
    packed = idxs[0]
    for k in range(1, TOP_K):
        packed = packed | (idxs[k] << (k * EXPERT_ID_BITS))
    epack_ref[...] = jnp.broadcast_to(packed, (LANES, tm)).T
    cnt_ref[...] = jnp.broadcast_to(
        base + jnp.sum(member.astype(F32), axis=1, keepdims=True), cnt_ref.shape)


def _route(logits):
    T = logits.shape[0]
    tm = min(ROW_TILE, T)
    spec = pl.BlockSpec((tm, LANES), lambda i: (i, 0))
    spec_t = pl.BlockSpec((SUBLANES, tm), lambda i: (0, i))
    return pl.pallas_call(
        _route_kernel,
        grid=(T // tm,),
        in_specs=[spec],
        out_specs=[spec, spec, spec_t, spec_t, spec_t,
                   pl.BlockSpec((N_EXPERTS, LANES), lambda i: (0, 0))],
        out_shape=[
            jax.ShapeDtypeStruct((T, LANES), F32),
            jax.ShapeDtypeStruct((T, LANES), I32),
            jax.ShapeDtypeStruct((SUBLANES, T), F32),
            jax.ShapeDtypeStruct((SUBLANES, T), I32),
            jax.ShapeDtypeStruct((SUBLANES, T), I32),
            jax.ShapeDtypeStruct((N_EXPERTS, LANES), F32),
        ],
        compiler_params=_cparams("arbitrary"),
        name="route",
    )(logits)


def _tile(ref, row):
    return ref.at[pl.ds(pl.multiple_of(row * SUBLANES, SUBLANES), SUBLANES)]


def _dispatch_kernel(dest_ref, last_blk_ref, nused_ref, h2_ref, epack_ref, buf_ref,
                     src_ref, pad_ref, sem, zsem, *, n_tokens):
    i = pl.program_id(0)
    tm = h2_ref.shape[0] // SUBLANES
    blk = pad_ref.shape[0] // (2 * SUBLANES)
    nblocks = buf_ref.shape[0] // (blk * SUBLANES)

    @pl.when(i == 0)
    def _():
        li = lax.broadcasted_iota(I32, pad_ref.shape, 0)
        line = li & (SUBLANES - 1)
        trash = (TOP_K * n_tokens + (li >> LOG2_SUBLANES)).astype(U32)
        tag = jnp.where((line & 1) == 0, trash, jnp.uint32(NO_EXPERT))
        pad_ref[...] = jnp.where(line >= PACK_LINES, tag, jnp.uint32(0))
        rows = blk * SUBLANES

        def pad_block(b):
            half = pl.multiple_of(lax.rem(b, 2) * rows, rows)
            return pltpu.make_async_copy(
                pad_ref.at[pl.ds(half, rows)],
                buf_ref.at[pl.ds(pl.multiple_of(b * rows, rows), rows)], zsem)

        def expert_put(e, _):
            pl.when(last_blk_ref[e] >= 0)(lambda: pad_block(last_blk_ref[e]).start())
            return 0

        def expert_done(e, _):
            pl.when(last_blk_ref[e] >= 0)(lambda: pad_block(last_blk_ref[e]).wait())
            return 0

        def tail_put(b, _):
            pad_block(b).start()
            return 0

        def tail_done(b, _):
            pad_block(b).wait()
            return 0

        lax.fori_loop(0, N_EXPERTS, expert_put, 0)
        lax.fori_loop(nused_ref[0], nblocks, tail_put, 0)
        lax.fori_loop(0, N_EXPERTS, expert_done, 0)
        lax.fori_loop(nused_ref[0], nblocks, tail_done, 0)

    parity = lax.rem(i, 2)
    li = lax.broadcasted_iota(I32, h2_ref.shape, 0)
    line = li & (SUBLANES - 1)
    tok = i * tm + (li >> LOG2_SUBLANES)
    experts = jnp.broadcast_to(epack_ref[...][:, None, :], (tm, SUBLANES, LANES)).reshape(
        tm * SUBLANES, LANES)
    tag = jnp.where((line & 1) == 0, tok, experts).astype(U32)
    src_ref[parity] = jnp.where(line >= PACK_LINES, tag, h2_ref[...])

    def issue_all(par):
        for r in range(tm):
            for k in range(TOP_K):
                pltpu.make_async_copy(_tile(src_ref.at[par], r),
                                      _tile(buf_ref, dest_ref[k * (tm + 1) + r]),
                                      sem.at[par]).start(priority=k % 2)

    for par in range(2):
        pl.when(parity == par)(functools.partial(issue_all, par))

    def wait_step(which):
        for k in range(TOP_K):
            pltpu.make_async_copy(src_ref.at[0], buf_ref.at[pl.ds(0, tm * SUBLANES)],
                                  sem.at[which]).wait()

    pl.when(i >= 1)(functools.partial(wait_step, 1 - parity))
    pl.when(i == pl.num_programs(0) - 1)(functools.partial(wait_step, parity))


def _dispatch(dest, last_blk, nused, h2_rows, epack, n_rows, blk):
    T = h2_rows.shape[0] // SUBLANES
    tm = min(DMA_TILE, T)
    nt = T // tm
    words = pl.cdiv(TOP_K * (tm + 1), SMEM_1D_TILE) * SMEM_1D_TILE
    d = jnp.pad(dest.reshape(TOP_K, nt, tm).transpose(1, 0, 2), ((0, 0), (0, 0), (0, 1)))
    d = jnp.pad(d.reshape(nt, TOP_K * (tm + 1)), ((0, 0), (0, words - TOP_K * (tm + 1))))
    dest_flat = d.reshape(nt * words)
    smem = pl.BlockSpec(memory_space=pltpu.SMEM)
    return pl.pallas_call(
        functools.partial(_dispatch_kernel, n_tokens=T),
        grid=(nt,),
        in_specs=[
            pl.BlockSpec((words,), lambda i: (i,), memory_space=pltpu.SMEM),
            smem, smem,
            pl.BlockSpec((tm * SUBLANES, LANES), lambda i: (i, 0)),
            pl.BlockSpec((tm, LANES), lambda i: (i, 0)),
        ],
        out_specs=pl.BlockSpec(memory_space=pl.ANY),
        out_shape=jax.ShapeDtypeStruct((n_rows * SUBLANES, LANES), U32),
        scratch_shapes=[pltpu.VMEM((2, tm * SUBLANES, LANES), U32),
                        pltpu.VMEM((2 * blk * SUBLANES, LANES), U32),
                        pltpu.SemaphoreType.DMA((2,)), pltpu.SemaphoreType.DMA(())],
        compiler_params=_cparams("arbitrary"),
        name="dispatch",
    )(dest_flat, last_blk, nused, h2_rows, epack)


def _ffn_kernel(blk_exp_ref, nused_ref, x_ref, wgu_ref, bgu_ref, wd_ref, bd_ref, y_ref,
                wgu_bf_ref, wd_bf_ref, out_ref, idv_ref, ids_ref, ssem, isem, *, n_tokens):
    b = pl.program_id(0)
    nb = pl.num_programs(0)
    tm = x_ref.shape[0] // SUBLANES
    groups = tm // LANES
    F = wd_ref.shape[1]
    nused = nused_ref[0]
    used = b < nused
    slot = lax.rem(b, 2)
    prev = 1 - slot
    new_expert = (b == 0) | (blk_exp_ref[b] != blk_exp_ref[jnp.maximum(b - 1, 0)])

    def send_row(s, r_hi, r_lo, row, priority):
        rid = ids_ref[s * SUBLANES + r_hi, r_lo]
        pltpu.make_async_copy(_tile(out_ref, s * tm + row), _tile(y_ref, rid),
                              ssem.at[s]).start(priority=priority)

    def send_rows_unrolled(s, lo, hi):
        for r in range(lo, hi):
            send_row(s, r // LANES, r % LANES, r, r % 2)

    def send_block_rolled(s):
        def body(g, _):
            def inner(l, _):
                send_row(s, g, l, g * LANES + l, 0)
                return 0
            return lax.fori_loop(0, LANES, inner, 0)
        lax.fori_loop(0, groups, body, 0)

    def wait_block(s):
        rows = tm * SUBLANES
        pltpu.make_async_copy(out_ref.at[pl.ds(0, rows)], y_ref.at[pl.ds(0, rows)],
                              ssem.at[s]).wait()

    def id_copy(s):
        return pltpu.make_async_copy(
            idv_ref, ids_ref.at[pl.ds(pl.multiple_of(s * SUBLANES, SUBLANES), SUBLANES)], isem)

    @pl.when(b == 0)
    def _():
        out_ref[...] = jnp.zeros_like(out_ref)
        rows = tm * SUBLANES
        pltpu.make_async_copy(out_ref.at[pl.ds(0, rows)],
                              y_ref.at[pl.ds(TOP_K * n_tokens * SUBLANES, rows)],
                              ssem.at[0]).start()
        pos = (lax.broadcasted_iota(I32, idv_ref.shape, 0) * LANES
               + lax.broadcasted_iota(I32, idv_ref.shape, 1))
        idv_ref[...] = TOP_K * n_tokens + tm + jnp.minimum(pos, tm - 1)
        id_copy(1).start()
        id_copy(1).wait()

    @pl.when((b >= 1) & (b <= nused))
    def _():
        id_copy(prev).wait()

    @pl.when(used)
    def _():
        rr = lax.broadcasted_iota(I32, (tm, LANES), 0)
        ll = lax.broadcasted_iota(I32, (tm, LANES), 1)
        on_diag = (rr & (LANES - 1)) == ll

        def per_row(tag_line):
            v = x_ref[pl.ds(tag_line, tm, stride=SUBLANES), :].astype(I32)
            return jnp.sum(jnp.where(on_diag, v, 0).reshape(groups, LANES, LANES), axis=1)

        tok = per_row(PACK_LINES)
        experts = per_row(PACK_LINES + 1)
        e = blk_exp_ref[b]
        slot_of = jnp.zeros_like(tok)
        for k in range(1, TOP_K):
            hit = ((experts >> (k * EXPERT_ID_BITS)) & EXPERT_ID_MASK) == e
            slot_of = jnp.where(hit, k, slot_of)
        idv_ref[pl.ds(0, groups), :] = slot_of * n_tokens + tok
        id_copy(slot).start()

    @pl.when(used & new_expert)
    def _():
        wgu_bf_ref[...] = wgu_ref[0].astype(BF16)
        wd_bf_ref[...] = wd_ref[0].astype(BF16)

    def block(slot, prev):
        words = [x_ref[pl.ds(j, tm, stride=SUBLANES), :] for j in range(PACK_LINES)]
        hi = [lax.bitcast_convert_type(w & jnp.uint32(HIGH_HALF), F32) for w in words]
        lo = [lax.bitcast_convert_type(w << BF16_BITS, F32) for w in words]
        x = jnp.concatenate(hi + lo, axis=1).astype(BF16)
        acts = []
        for c in range(FFN_SLICES):
            send_rows_unrolled(prev, c * tm // FFN_SLICES, (c + 1) * tm // FFN_SLICES)
            w = F // FFN_SLICES
            g = jnp.dot(x, wgu_bf_ref[:, c * w:(c + 1) * w], preferred_element_type=F32)
            u = jnp.dot(x, wgu_bf_ref[:, F + c * w:F + (c + 1) * w],
                        preferred_element_type=F32)
            g = jnp.minimum(g + bgu_ref[0, :, c * w:(c + 1) * w], SWIGLU_LIMIT)
            u = jnp.clip(u + bgu_ref[0, :, F + c * w:F + (c + 1) * w],
                         -SWIGLU_LIMIT, SWIGLU_LIMIT)
            acts.append((g * jax.nn.sigmoid(SWIGLU_ALPHA * g) * (u + 1.0)).astype(BF16))
        act = jnp.concatenate(acts, axis=1)
        y = jnp.dot(act, wd_bf_ref[...], preferred_element_type=F32) + bd_ref[0]
        wait_block(slot)
        base = slot * (tm * SUBLANES)
        for j in range(SUBLANES):
            out_ref[pl.ds(base + j, tm, stride=SUBLANES), :] = y[:, j * LANES:(j + 1) * LANES]

    for parity in range(2):
        pl.when(used & (slot == parity))(functools.partial(block, parity, 1 - parity))

    @pl.when(b == nused)
    def _():
        wait_block(slot)
        send_block_rolled(prev)
        wait_block(prev)

    @pl.when(used & (b == nb - 1))
    def _():
        wait_block(prev)
        id_copy(slot).wait()
        send_block_rolled(slot)
        wait_block(slot)


def _ffn(blk_exp, nused, buf, wgu, bgu, wd, bd, tm, n_tokens):
    nb = buf.shape[0] // (tm * SUBLANES)
    D = wgu.shape[1]
    F = wd.shape[1]

    def xmap(b, be, nu):
        return (jnp.minimum(b, nu[0] - 1), 0)

    def wmap(b, be, nu):
        return (be[jnp.minimum(b, nu[0] - 1)], 0, 0)

    grid_spec = pltpu.PrefetchScalarGridSpec(
        num_scalar_prefetch=2,
        grid=(nb,),
        in_specs=[
            pl.BlockSpec((tm * SUBLANES, LANES), xmap),
            pl.BlockSpec((1, D, 2 * F), wmap),
            pl.BlockSpec((1, 1, 2 * F), wmap),
            pl.BlockSpec((1, F, D), wmap),
            pl.BlockSpec((1, 1, D), wmap),
        ],
        out_specs=pl.BlockSpec(memory_space=pl.ANY),
        scratch_shapes=[
            pltpu.VMEM((D, 2 * F), BF16), pltpu.VMEM((F, D), BF16),
            pltpu.VMEM((2 * tm * SUBLANES, LANES), F32),
            pltpu.VMEM((SUBLANES, LANES), I32),
            pltpu.SMEM((2 * SUBLANES, LANES), I32),
            pltpu.SemaphoreType.DMA((2,)), pltpu.SemaphoreType.DMA(()),
        ],
    )
    return pl.pallas_call(
        functools.partial(_ffn_kernel, n_tokens=n_tokens),
        grid_spec=grid_spec,
        out_shape=jax.ShapeDtypeStruct(((TOP_K * n_tokens + 2 * tm) * SUBLANES, LANES), F32),
        compiler_params=_cparams("arbitrary"),
        name="ffn",
    )(blk_exp, nused, buf, wgu, bgu, wd, bd)


def _combine_kernel(gate_ref, x2_ref, fg_ref, *refs):
    y_refs, o_ref = refs[:TOP_K], refs[TOP_K]
    tm, D = x2_ref.shape
    nj = D // LANES
    gate = gate_ref[...]
    cols = []
    for j in range(nj):
        acc = None
        for k in range(TOP_K):
            part = gate[:, k:k + 1] * y_refs[k][pl.ds(j, tm, stride=nj), :]
            acc = part if acc is None else acc + part
        cols.append(acc)
    y = jnp.concatenate(cols, axis=1)
    o_ref[...] = _rms(x2_ref[...] + y, fg_ref[...])


def _combine(gate4, x2, fg, y_rows):
    T, D = x2.shape
    tm = min(ROW_TILE, T)
    nj = D // LANES
    nt = T // tm

    def slot_spec(k):
        return pl.BlockSpec((tm * nj, LANES), lambda i: (k * nt + i, 0))

    return pl.pallas_call(
        _combine_kernel,
        grid=(nt,),
        in_specs=[
            pl.BlockSpec((tm, LANES), lambda i: (i, 0)),
            pl.BlockSpec((tm, D), lambda i: (i, 0)),
            pl.BlockSpec((1, D), lambda i: (0, 0)),
        ] + [slot_spec(k) for k in range(TOP_K)],
        out_specs=pl.BlockSpec((tm, D), lambda i: (i, 0)),
        out_shape=jax.ShapeDtypeStruct((T, D), F32),
        compiler_params=_cparams("parallel"),
        name="combine",
    )(gate4, x2, fg, *([y_rows] * TOP_K))


def _rope_tables(S):
    rows = S // GRID_W
    row = jnp.repeat(jnp.arange(rows, dtype=I32), GRID_W).astype(F32)
    col = jnp.tile(jnp.arange(GRID_W, dtype=I32), rows).astype(F32)
    inv = ROPE_THETA ** (-jnp.arange(0, AXIS_DIM, 2, dtype=F32) / AXIS_DIM)
    ang_r = row[:, None] * inv[None, :]
    ang_c = col[:, None] * inv[None, :]
    cos = jnp.concatenate([jnp.cos(ang_r)] * 2 + [jnp.cos(ang_c)] * 2, axis=1)
    sin = jnp.concatenate([-jnp.sin(ang_r), jnp.sin(ang_r), -jnp.sin(ang_c), jnp.sin(ang_c)], axis=1)
    return jnp.tile(cos, (1, 2)), jnp.tile(sin, (1, 2))


def _block_diag_gates(wa, ba, wi, bi, ct):
    nb, bw = wa.shape[1], wa.shape[2]
    per = ct // bw
    nct = nb // per
    eye = jnp.eye(per, dtype=wa.dtype)

    def tiles(w):
        w = w.reshape(nct, per, bw, bw)
        return jnp.einsum('cpij,pq->cpiqj', w, eye).reshape(nct, ct, ct)

    wg = jnp.concatenate([tiles(wa[0]), tiles(wi[0]), tiles(wa[1]), tiles(wi[1])], axis=2)
    bias = lambda b: b.reshape(nct, 1, ct)
    bg = jnp.concatenate([bias(ba[0]), bias(bi[0]), bias(ba[1]), bias(bi[1])], axis=2)
    return (0.5 * wg).astype(BF16), 0.5 * bg


def _layer(x2d, B, S, norm1_g, w_in, b_in, q_norm_g, k_norm_g, conv_w, conv_b, lru_wa, lru_ba,
           lru_wi, lru_bi, lru_lambda, w_attn_o, w_lru_o, w_out, norm2_g, w_router, b_router,
           w_gu, b_gu, w_down, b_down, out_g):
    T, D = x2d.shape
    C = conv_w.shape[1]
    nqkv = Q_WIDTH + 2 * KV_WIDTH
    row2 = lambda v: v.reshape(1, -1)

    cos_t, sin_t = _rope_tables(S)
    head = jnp.arange(LANES) // HEAD_DIM
    hsum = (head[:, None] == head[None, :]).astype(BF16)
    q, k, v = _qkv(x2d, row2(norm1_g), w_in[:, :nqkv].astype(BF16), row2(b_in[:nqkv]),
                   cos_t, sin_t, row2(jnp.tile(q_norm_g, 2)), row2(jnp.tile(k_norm_g, 2)),
                   hsum, B, S)
    proj = _proj(x2d, row2(norm1_g), w_in[:, nqkv:].astype(BF16), row2(b_in[nqkv:]))
    attn = _attn(q, k, v)
    wg, bg = _block_diag_gates(lru_wa, lru_ba, lru_wi, lru_bi, LRU_CT)
    rec = _lru(proj, conv_w, row2(conv_b), wg, bg, lru_lambda, B, S, C)

    pad = LANES - N_EXPERTS
    rw = jnp.pad(w_router, ((0, 0), (0, pad)))
    rw_hi = rw.astype(BF16)
    rw_lo = (rw - rw_hi.astype(F32)).astype(BF16)
    rb = jnp.pad(b_router, (0, pad), constant_values=NEG_BIG).reshape(1, LANES)
    x2, h2_rows, logits = _merge(x2d, attn, rec, proj, w_attn_o.astype(BF16),
                                 w_lru_o.astype(BF16), w_out.astype(BF16), row2(norm2_g),
                                 rw_hi, rw_lo, rb)

    gate4, epack, _, eidx_t, rank_t, cnt = _route(logits)
    counts = cnt[:, 0].astype(I32)
    tm = FFN_TM
    nblk = (counts + tm - 1) // tm
    pend_blk = jnp.cumsum(nblk)
    pstart = (pend_blk - nblk) * tm
    A = T * TOP_K
    nb = (A + N_EXPERTS * (tm - 1) + tm - 1) // tm
    blk_exp = jnp.minimum(
        jnp.sum(pend_blk[None, :] <= jnp.arange(nb, dtype=I32)[:, None], axis=1),
        N_EXPERTS - 1).astype(I32)
    nused = pend_blk[-1:].astype(I32)
    first_row = jnp.sum(jnp.where(eidx_t[:TOP_K, :, None] == jnp.arange(N_EXPERTS), pstart, 0),
                        axis=-1)
    dest = (first_row + rank_t[:TOP_K]).astype(I32)
    last_blk = jnp.where(nblk > 0, pend_blk - 1, -1).astype(I32)

    buf = _dispatch(dest, last_blk, nused, h2_rows, epack, nb * tm, tm)
    y_rows = _ffn(blk_exp, nused, buf, w_gu, b_gu.reshape(N_EXPERTS, 1, -1), w_down,
                  b_down.reshape(N_EXPERTS, 1, -1), tm, T)
    return _combine(gate4, x2, row2(out_g), y_rows)


def kernel(x, norm1_g, w_in, b_in, q_norm_g, k_norm_g, conv_w, conv_b, lru_wa, lru_ba, lru_wi,
           lru_bi, lru_lambda, w_attn_o, w_lru_o, w_out, norm2_g, w_router, b_router, w_gu, b_gu,
           w_down, b_down, final_g):
    B, S, D = x.shape
    depth = norm1_g.shape[0]
    assert depth == 1, "the fused final RMSNorm assumes a single layer"
    assert S % (TIME_CHUNKS * SCAN_UNROLL) == 0 and S % GRID_W == 0
    assert D == 2 * PACK_LINES * LANES, "a token's bf16 row must fill PACK_LINES word lines"
    out = _layer(x.reshape(B * S, D), B, S, norm1_g[0], w_in[0], b_in[0], q_norm_g[0],
                 k_norm_g[0], conv_w[0], conv_b[0], lru_wa[0], lru_ba[0], lru_wi[0], lru_bi[0],
                 lru_lambda[0], w_attn_o[0], w_lru_o[0], w_out[0], norm2_g[0], w_router[0],
                 b_router[0], w_gu[0], b_gu[0], w_down[0], b_down[0], final_g)
    return out.reshape(B, S, D)
```

```python
import functools

import jax
import jax.numpy as jnp
from jax import lax
from jax.experimental import pallas as pl
from jax.experimental.pallas import tpu as pltpu

F32 = jnp.float32
BF16 = jnp.bfloat16
I32 = jnp.int32
U32 = jnp.uint32

LANES = 128
SUBLANES = 8
PACK_LINES = 4
BF16_BITS = 16
HIGH_HALF = 0xFFFF0000
LOG2_SUBLANES = 3
EXPERT_ID_BITS = 8
EXPERT_ID_MASK = 0xFF
NO_EXPERT = 0xFFFFFFFF
SMEM_1D_TILE = 1024
VMEM_LIMIT_BYTES = 56 * 1024 * 1024

HEAD_DIM = 64
N_Q_HEADS = 8
N_KV_HEADS = 2
Q_GROUP = N_Q_HEADS // N_KV_HEADS
Q_WIDTH = N_Q_HEADS * HEAD_DIM
KV_WIDTH = N_KV_HEADS * HEAD_DIM
AXIS_DIM = HEAD_DIM // 2
ROT_HALF = AXIS_DIM // 2
ROPE_THETA = 10000.0
GRID_W = 64
RG_C = 8.0
N_EXPERTS = 32
TOP_K = 4
SWIGLU_LIMIT = 7.0
SWIGLU_ALPHA = 1.702
NORM_EPS = 1e-6
LOG2E = 1.4426950408889634
NEG_BIG = -1e30
TINY = 1e-30

ROW_TILE = 512
ATT_TQ = 1024
ATT_TK = 512
LRU_CT = 256
LRU_ROWS = 512
SCAN_UNROLL = 32
TIME_CHUNKS = SUBLANES
FFN_TM = 512
FFN_SLICES = 4
DMA_TILE = 256


def _cparams(*sem):
    return pltpu.CompilerParams(dimension_semantics=sem, vmem_limit_bytes=VMEM_LIMIT_BYTES)


def _rms(x, g):
    return x * lax.rsqrt(jnp.mean(x * x, axis=-1, keepdims=True) + NORM_EPS) * g


def _qkv_kernel(x_ref, g1_ref, w_ref, b_ref, cos_ref, sin_ref, qg_ref, kg_ref, hsum_ref,
                q_ref, k_ref, v_ref):
    h = _rms(x_ref[...], g1_ref[...]).astype(BF16)
    p = jnp.dot(h, w_ref[...], preferred_element_type=F32) + b_ref[...]
    cos = cos_ref[...]
    sin = sin_ref[...]
    hsum = hsum_ref[...]
    lane = lax.broadcasted_iota(I32, cos.shape, 1)
    first_half = (lane % AXIS_DIM) < ROT_HALF

    def norm_rope(c, gain):
        sq = c * c
        hi = sq.astype(BF16)
        lo = (sq - hi.astype(F32)).astype(BF16)
        ms = (jnp.dot(hi, hsum, preferred_element_type=F32)
              + jnp.dot(lo, hsum, preferred_element_type=F32)) * (1.0 / HEAD_DIM)
        y = c * lax.rsqrt(ms + NORM_EPS) * gain
        partner = jnp.where(first_half, pltpu.roll(y, LANES - ROT_HALF, 1),
                            pltpu.roll(y, ROT_HALF, 1))
        return y * cos + partner * sin

    qg = qg_ref[...]
    for c in range(Q_WIDTH // LANES):
        y = norm_rope(p[:, c * LANES:(c + 1) * LANES], qg) * (HEAD_DIM ** -0.5 * LOG2E)
        yt = y.T.astype(BF16)
        q_ref[0, 2 * c] = yt[:HEAD_DIM]
        q_ref[0, 2 * c + 1] = yt[HEAD_DIM:]
    yk = norm_rope(p[:, Q_WIDTH:Q_WIDTH + KV_WIDTH], kg_ref[...])
    k_ref[0, 0] = yk[:, :HEAD_DIM].astype(BF16)
    k_ref[0, 1] = yk[:, HEAD_DIM:].astype(BF16)
    vt = p[:, Q_WIDTH + KV_WIDTH:].T
    tm = vt.shape[1]
    ones_row = (lax.broadcasted_iota(I32, (HEAD_DIM, tm), 0) == 0).astype(BF16)
    v_ref[0, 0] = jnp.concatenate([vt[:HEAD_DIM].astype(BF16), ones_row], axis=0)
    v_ref[0, 1] = jnp.concatenate([vt[HEAD_DIM:].astype(BF16), ones_row], axis=0)


def _qkv(x2d, g1, w_qkv, b_qkv, cos_t, sin_t, qg, kg, hsum, B, S):
    T, D = x2d.shape
    tm = min(ROW_TILE, S)
    ns = S // tm
    n = w_qkv.shape[1]
    full = lambda shape: pl.BlockSpec(shape, lambda i: (0,) * len(shape))
    return pl.pallas_call(
        _qkv_kernel,
        grid=(T // tm,),
        in_specs=[
            pl.BlockSpec((tm, D), lambda i: (i, 0)),
            full((1, D)), full((D, n)), full((1, n)),
            pl.BlockSpec((tm, LANES), lambda i: (i % ns, 0)),
            pl.BlockSpec((tm, LANES), lambda i: (i % ns, 0)),
            full((1, LANES)), full((1, LANES)), full((LANES, LANES)),
        ],
        out_specs=[
            pl.BlockSpec((1, N_Q_HEADS, HEAD_DIM, tm), lambda i: (i // ns, 0, 0, i % ns)),
            pl.BlockSpec((1, N_KV_HEADS, tm, HEAD_DIM), lambda i: (i // ns, 0, i % ns, 0)),
            pl.BlockSpec((1, N_KV_HEADS, LANES, tm), lambda i: (i // ns, 0, 0, i % ns)),
        ],
        out_shape=[
            jax.ShapeDtypeStruct((B, N_Q_HEADS, HEAD_DIM, S), BF16),
            jax.ShapeDtypeStruct((B, N_KV_HEADS, S, HEAD_DIM), BF16),
            jax.ShapeDtypeStruct((B, N_KV_HEADS, LANES, S), BF16),
        ],
        compiler_params=_cparams("parallel"),
        name="qkv",
    )(x2d, g1, w_qkv, b_qkv, cos_t, sin_t, qg, kg, hsum)


def _proj_kernel(x_ref, g_ref, w_ref, b_ref, o_ref):
    h = _rms(x_ref[...], g_ref[...]).astype(BF16)
    o_ref[...] = jnp.dot(h, w_ref[...], preferred_element_type=F32) + b_ref[...]


def _proj(x2d, g, w, b):
    T, D = x2d.shape
    n = w.shape[1]
    tm = min(ROW_TILE, T)
    return pl.pallas_call(
        _proj_kernel,
        grid=(T // tm,),
        in_specs=[
            pl.BlockSpec((tm, D), lambda i: (i, 0)),
            pl.BlockSpec((1, D), lambda i: (0, 0)),
            pl.BlockSpec((D, n), lambda i: (0, 0)),
            pl.BlockSpec((1, n), lambda i: (0, 0)),
        ],
        out_specs=pl.BlockSpec((tm, n), lambda i: (i, 0)),
        out_shape=jax.ShapeDtypeStruct((T, n), F32),
        compiler_params=_cparams("parallel"),
        name="proj",
    )(x2d, g, w, b)


def _attn_kernel(q_ref, k_ref, v_ref, o_ref, s0_ref, s1_ref, *, tk):
    tq = q_ref.shape[3]
    S = k_ref.shape[2]
    M = Q_GROUP * tq
    n = S // tk
    qT = jnp.concatenate([q_ref[0, h] for h in range(Q_GROUP)], axis=1)

    def scores(j, s_ref):
        off = pl.multiple_of(j * tk, tk)
        s_ref[...] = jnp.dot(k_ref[0, 0, pl.ds(off, tk), :], qT, preferred_element_type=F32)

    def absorb(j, s_ref, carry):
        m, acc = carry
        off = pl.multiple_of(j * tk, tk)
        vc = v_ref[0, 0, :, pl.ds(off, tk)]
        s = s_ref[...]
        m_new = jnp.maximum(m, jnp.max(s, axis=0, keepdims=True))
        alpha = jnp.exp2(m - m_new)
        p = jnp.exp2(s - m_new).astype(BF16)
        return m_new, alpha * acc + jnp.dot(vc, p, preferred_element_type=F32)

    def pair(i, carry):
        scores(2 * i + 1, s1_ref)
        carry = absorb(2 * i, s0_ref, carry)
        scores(2 * i + 2, s0_ref)
        return absorb(2 * i + 1, s1_ref, carry)

    scores(0, s0_ref)
    carry = (jnp.full((1, M), NEG_BIG, F32), jnp.zeros((LANES, M), F32))
    for i in range(n // 2 - 1):
        carry = pair(i, carry)
    scores(n - 1, s1_ref)
    carry = absorb(n - 2, s0_ref, carry)
    _, acc = absorb(n - 1, s1_ref, carry)
    outT = acc[:HEAD_DIM] / acc[HEAD_DIM:HEAD_DIM + 1]
    stacked = jnp.concatenate(
        [outT[:, g * tq:(g + 1) * tq] for g in range(Q_GROUP)], axis=0)
    o_ref[...] = stacked.T.astype(BF16)


def _attn(qT, k, vT):
    B, _, _, S = qT.shape
    tq = min(ATT_TQ, S)
    tk = min(ATT_TK, S)
    nq = S // tq
    assert (S // tk) % 2 == 0, "key chunks are processed in pairs"
    score_buf = pltpu.VMEM((tk, Q_GROUP * tq), F32)
    return pl.pallas_call(
        functools.partial(_attn_kernel, tk=tk),
        grid=(B, N_KV_HEADS, nq),
        in_specs=[
            pl.BlockSpec((1, Q_GROUP, HEAD_DIM, tq), lambda b, g, i: (b, g, 0, i)),
            pl.BlockSpec((1, 1, S, HEAD_DIM), lambda b, g, i: (b, g, 0, 0)),
            pl.BlockSpec((1, 1, LANES, S), lambda b, g, i: (b, g, 0, 0)),
        ],
        out_specs=pl.BlockSpec((tq, Q_GROUP * HEAD_DIM), lambda b, g, i: (b * nq + i, g)),
        out_shape=jax.ShapeDtypeStruct((B * S, Q_WIDTH), BF16),
        scratch_shapes=[score_buf, score_buf],
        compiler_params=_cparams("parallel", "parallel", "parallel"),
        name="attn",
    )(qT, k, vT)


def _shift_chunks(v, down):
    row = lax.broadcasted_iota(I32, v.shape, 0)
    if down:
        return jnp.where(row == 0, 0.0, pltpu.roll(v, 1, 0))
    return jnp.where(row == SUBLANES - 1, 0.0, pltpu.roll(v, SUBLANES - 1, 0))


def _lru_kernel(xr_ref, yr_ref, cw_ref, cb_ref, wg_ref, bg_ref, lam_ref, o_ref,
                xe_ref, af_ref, uf_ref, ab_ref, ub_ref, *, rows):
    S, ct = xr_ref.shape
    lc = S // TIME_CHUNKS
    halo = SUBLANES

    ng = ct // LANES

    def xe_rows(r0, n):
        return jnp.concatenate([xe_ref[g, pl.ds(r0, n), :] for g in range(ng)], axis=1)

    def xe_store(r0, n, val):
        for g in range(ng):
            xe_ref[g, pl.ds(r0, n), :] = val[:, g * LANES:(g + 1) * LANES]

    for s in range(TIME_CHUNKS):
        for g in range(ng):
            xe_ref[g, pl.ds(halo + s, lc, stride=SUBLANES), :] = (
                xr_ref[pl.ds(s * lc, lc), g * LANES:(g + 1) * LANES])
    xe_store(0, halo, _shift_chunks(xe_rows(S, halo), True))
    first = xe_rows(halo, halo)
    second = xe_rows(2 * halo, halo)
    xe_store(S + halo, halo, _shift_chunks(first, False))
    xe_store(S + 2 * halo, halo, _shift_chunks(second, False))

    cw = cw_ref[...]
    cb = cb_ref[...]
    lam = lam_ref[...]
    log_sig = jnp.minimum(lam, 0.0) - jnp.log(1.0 + jnp.exp(-jnp.abs(lam)))
    c_half = (0.5 * RG_C * LOG2E) * log_sig
    wg = wg_ref[0]
    bg = bg_ref[0]

    def gate_chunk(i, _):
        r0 = pl.multiple_of(i * rows, rows)
        xc = cb
        for j in range(4):
            xc = xc + cw[j:j + 1, :] * xe_rows(r0 + j * halo, rows)
        t = jnp.tanh(jnp.dot(xc.astype(BF16), wg, preferred_element_type=F32) + bg)
        x_half = 0.5 * xc
        for d, (a_ref, u_ref) in enumerate(((af_ref, uf_ref), (ab_ref, ub_ref))):
            t_r = t[:, (2 * d) * ct:(2 * d + 1) * ct]
            t_i = t[:, (2 * d + 1) * ct:(2 * d + 2) * ct]
            ch = c_half[d:d + 1, :]
            a = jnp.exp2(t_r * ch + ch)
            a_ref[pl.ds(r0, rows), :] = a
            v = 1.0 - a * a
            root = jnp.maximum(v, 0.0) * lax.rsqrt(jnp.maximum(v, TINY))
            u_ref[pl.ds(r0, rows), :] = root * ((t_i + 1.0) * x_half)
        return 0

    lax.fori_loop(0, S // rows, gate_chunk, 0)

    unroll = SCAN_UNROLL

    def scan_body(i, carry):
        hf, pf, hb, pb = carry
        for k in range(unroll):
            tf = i * unroll + k
            rf = pl.multiple_of(tf * SUBLANES, SUBLANES)
            rb = pl.multiple_of((lc - 1 - tf) * SUBLANES, SUBLANES)
            a = af_ref[pl.ds(rf, SUBLANES), :]
            hf = a * hf + uf_ref[pl.ds(rf, SUBLANES), :]
            pf = a * pf
            uf_ref[pl.ds(rf, SUBLANES), :] = hf
            af_ref[pl.ds(rf, SUBLANES), :] = pf
            a = ab_ref[pl.ds(rb, SUBLANES), :]
            hb = a * hb + ub_ref[pl.ds(rb, SUBLANES), :]
            pb = a * pb
            ub_ref[pl.ds(rb, SUBLANES), :] = hb
            ab_ref[pl.ds(rb, SUBLANES), :] = pb
        return hf, pf, hb, pb

    zero = jnp.zeros((SUBLANES, ct), F32)
    one = jnp.ones((SUBLANES, ct), F32)
    hf_end, pf_end, hb_end, pb_end = lax.fori_loop(0, lc // unroll, scan_body,
                                                   (zero, one, zero, one))

    def chunk_carry(h_end, p_end, down):
        f = h_end
        for _ in range(TIME_CHUNKS - 1):
            f = h_end + p_end * _shift_chunks(f, down)
        return _shift_chunks(f, down)

    cin_f = chunk_carry(hf_end, pf_end, True)
    cin_b = chunk_carry(hb_end, pb_end, False)

    def fix_chunk(i, _):
        r0 = pl.multiple_of(i * rows, rows)
        reps = rows // SUBLANES
        hf = uf_ref[pl.ds(r0, rows), :] + af_ref[pl.ds(r0, rows), :] * jnp.tile(cin_f, (reps, 1))
        hb = ub_ref[pl.ds(r0, rows), :] + ab_ref[pl.ds(r0, rows), :] * jnp.tile(cin_b, (reps, 1))
        xe_store(r0, rows, hf + hb)
        return 0

    lax.fori_loop(0, S // rows, fix_chunk, 0)

    for s in range(TIME_CHUNKS):
        hsum = jnp.concatenate(
            [xe_ref[g, pl.ds(s, lc, stride=SUBLANES), :] for g in range(ng)], axis=1)
        y = yr_ref[pl.ds(s * lc, lc), :]
        o_ref[pl.ds(s * lc, lc), :] = (hsum * jax.nn.gelu(y, approximate=True)).astype(BF16)


def _lru(proj, conv_w, conv_b, wg, bg, lam, B, S, C):
    ct = LRU_CT
    nct = C // ct
    rows = min(LRU_ROWS, S)
    return pl.pallas_call(
        functools.partial(_lru_kernel, rows=rows),
        grid=(B, nct),
        in_specs=[
            pl.BlockSpec((S, ct), lambda b, c: (b, c)),
            pl.BlockSpec((S, ct), lambda b, c: (b, nct + c)),
            pl.BlockSpec((4, ct), lambda b, c: (0, c)),
            pl.BlockSpec((1, ct), lambda b, c: (0, c)),
            pl.BlockSpec((1, ct, 4 * ct), lambda b, c: (c, 0, 0)),
            pl.BlockSpec((1, 1, 4 * ct), lambda b, c: (c, 0, 0)),
            pl.BlockSpec((2, ct), lambda b, c: (0, c)),
        ],
        out_specs=pl.BlockSpec((S, ct), lambda b, c: (b, c)),
        out_shape=jax.ShapeDtypeStruct((B * S, C), BF16),
        scratch_shapes=[
            pltpu.VMEM((ct // LANES, S + 3 * SUBLANES, LANES), F32),
            pltpu.VMEM((S, ct), F32), pltpu.VMEM((S, ct), F32),
            pltpu.VMEM((S, ct), F32), pltpu.VMEM((S, ct), F32),
        ],
        compiler_params=_cparams("parallel", "parallel"),
        name="lru",
    )(proj, proj, conv_w, conv_b, wg, bg, lam)


def _merge_kernel(x_ref, attn_ref, rec_ref, gl_ref, wa_ref, wr_ref, wo_ref, g2_ref,
                  rw_hi_ref, rw_lo_ref, rb_ref, x2_ref, h2_ref, lg_ref):
    D = x_ref.shape[1]
    gl = gl_ref[...]
    ma = jnp.dot(attn_ref[...], wa_ref[...], preferred_element_type=F32)
    mr = jnp.dot(rec_ref[...], wr_ref[...], preferred_element_type=F32)
    merged = jax.nn.sigmoid(gl[:, :D]) * ma + jax.nn.sigmoid(gl[:, D:]) * mr
    x2 = x_ref[...] + jnp.dot(merged.astype(BF16), wo_ref[...], preferred_element_type=F32)
    x2_ref[...] = x2
    h2 = _rms(x2, g2_ref[...])
    hi = h2.astype(BF16)
    lo = (h2 - hi.astype(F32)).astype(BF16)
    lg_ref[...] = (jnp.dot(hi, rw_hi_ref[...], preferred_element_type=F32)
                   + jnp.dot(lo, rw_hi_ref[...], preferred_element_type=F32)
                   + jnp.dot(hi, rw_lo_ref[...], preferred_element_type=F32)) + rb_ref[...]
    tm = h2.shape[0]
    bits = lax.bitcast_convert_type(hi.astype(F32), U32)
    words = bits[:, :D // 2] | (bits[:, D // 2:] >> BF16_BITS)
    for j in range(PACK_LINES):
        h2_ref[pl.ds(j, tm, stride=SUBLANES), :] = words[:, j * LANES:(j + 1) * LANES]
    for j in range(PACK_LINES, SUBLANES):
        h2_ref[pl.ds(j, tm, stride=SUBLANES), :] = jnp.zeros((tm, LANES), U32)


def _merge(x2d, attn, rec, proj, wa, wr, wo, g2, rw_hi, rw_lo, rb):
    T, D = x2d.shape
    tm = min(ROW_TILE, T)
    nj = D // LANES
    full = lambda a: pl.BlockSpec(a.shape, lambda i: (0,) * a.ndim)
    return pl.pallas_call(
        _merge_kernel,
        grid=(T // tm,),
        in_specs=[
            pl.BlockSpec((tm, D), lambda i: (i, 0)),
            pl.BlockSpec((tm, Q_WIDTH), lambda i: (i, 0)),
            pl.BlockSpec((tm, D), lambda i: (i, 0)),
            pl.BlockSpec((tm, 2 * D), lambda i: (i, 1)),
            full(wa), full(wr), full(wo), full(g2), full(rw_hi), full(rw_lo), full(rb),
        ],
        out_specs=[
            pl.BlockSpec((tm, D), lambda i: (i, 0)),
            pl.BlockSpec((tm * nj, LANES), lambda i: (i, 0)),
            pl.BlockSpec((tm, LANES), lambda i: (i, 0)),
        ],
        out_shape=[
            jax.ShapeDtypeStruct((T, D), F32),
            jax.ShapeDtypeStruct((T * nj, LANES), U32),
            jax.ShapeDtypeStruct((T, LANES), F32),
        ],
        compiler_params=_cparams("parallel"),
        name="merge",
    )(x2d, attn, rec, proj, wa, wr, wo, g2, rw_hi, rw_lo, rb)


def _route_kernel(lg_ref, gate_ref, epack_ref, gate_t_ref, eidx_t_ref, rank_t_ref, cnt_ref):
    i = pl.program_id(0)

    @pl.when(i == 0)
    def _():
        cnt_ref[...] = jnp.zeros_like(cnt_ref)

    tm = lg_ref.shape[0]
    work = lg_ref[...].T[:N_EXPERTS]
    expert = lax.broadcasted_iota(I32, work.shape, 0)
    sels, vals, idxs = [], [], []
    for _ in range(TOP_K):
        mx = jnp.max(work, axis=0, keepdims=True)
        idx = jnp.min(jnp.where(work == mx, expert, N_EXPERTS), axis=0, keepdims=True)
        sel = expert == idx
        sels.append(sel)
        vals.append(mx)
        idxs.append(idx)
        work = jnp.where(sel, NEG_BIG, work)
    ex = [jnp.exp(v - vals[0]) for v in vals]
    den = ex[0] + ex[1] + ex[2] + ex[3]
    member = (sels[0] | sels[1] | sels[2] | sels[3]).astype(BF16)
    r = lax.broadcasted_iota(I32, (tm, tm), 0)
    c = lax.broadcasted_iota(I32, (tm, tm), 1)
    tri = (r < c).astype(BF16)
    base = cnt_ref[:, 0:1]
    before = jnp.dot(member, tri, preferred_element_type=F32) + base
    ranks = [jnp.sum(jnp.where(sels[k], before, 0.0), axis=0, keepdims=True)
             for k in range(TOP_K)]
    gates = [ex[k] / den for k in range(TOP_K)]
    pad = SUBLANES - TOP_K
    gate_t = jnp.concatenate(gates + [jnp.zeros((pad, tm), F32)], axis=0)
    gate_t_ref[...] = gate_t
    eidx_t_ref[...] = jnp.concatenate(idxs + [jnp.zeros((pad, tm), I32)], axis=0)
    rank_t_ref[...] = jnp.concatenate(ranks + [jnp.zeros((pad, tm), F32)], axis=0).astype(I32)
    gate_ref[...] = jnp.concatenate(
        [gate_t, jnp.zeros((LANES - SUBLANES, tm), F32)], axis=0).T
    packed = idxs[0]
    for k in range(1, TOP_K):
        packed = packed | (idxs[k] << (k * EXPERT_ID_BITS))
    epack_ref[...] = jnp.broadcast_to(packed, (LANES, tm)).T
    cnt_ref[...] = jnp.broadcast_to(
        base + jnp.sum(member.astype(F32), axis=1, keepdims=True), cnt_ref.shape)


def _route(logits):
    T = logits.shape[0]
    tm = min(ROW_TILE, T)
    spec = pl.BlockSpec((tm, LANES), lambda i: (i, 0))
    spec_t = pl.BlockSpec((SUBLANES, tm), lambda i: (0, i))
    return pl.pallas_call(
        _route_kernel,
        grid=(T // tm,),
        in_specs=[spec],
        out_specs=[spec, spec, spec_t, spec_t, spec_t,
                   pl.BlockSpec((N_EXPERTS, LANES), lambda i: (0, 0))],
        out_shape=[
            jax.ShapeDtypeStruct((T, LANES), F32),
            jax.ShapeDtypeStruct((T, LANES), I32),
            jax.ShapeDtypeStruct((SUBLANES, T), F32),
            jax.ShapeDtypeStruct((SUBLANES, T), I32),
            jax.ShapeDtypeStruct((SUBLANES, T), I32),
            jax.ShapeDtypeStruct((N_EXPERTS, LANES), F32),
        ],
        compiler_params=_cparams("arbitrary"),
        name="route",
    )(logits)


def _tile(ref, row):
    return ref.at[pl.ds(pl.multiple_of(row * SUBLANES, SUBLANES), SUBLANES)]


def _dispatch_kernel(dest_ref, last_blk_ref, nused_ref, h2_ref, epack_ref, buf_ref,
                     src_ref, pad_ref, sem, zsem, *, n_tokens):
    i = pl.program_id(0)
    tm = h2_ref.shape[0] // SUBLANES
    blk = pad_ref.shape[0] // (2 * SUBLANES)
    nblocks = buf_ref.shape[0] // (blk * SUBLANES)

    @pl.when(i == 0)
    def _():
        li = lax.broadcasted_iota(I32, pad_ref.shape, 0)
        line = li & (SUBLANES - 1)
        trash = (TOP_K * n_tokens + (li >> LOG2_SUBLANES)).astype(U32)
        tag = jnp.where((line & 1) == 0, trash, jnp.uint32(NO_EXPERT))
        pad_ref[...] = jnp.where(line >= PACK_LINES, tag, jnp.uint32(0))
        rows = blk * SUBLANES

        def pad_block(b):
            half = pl.multiple_of(lax.rem(b, 2) * rows, rows)
            return pltpu.make_async_copy(
                pad_ref.at[pl.ds(half, rows)],
                buf_ref.at[pl.ds(pl.multiple_of(b * rows, rows), rows)], zsem)

        def expert_put(e, _):
            pl.when(last_blk_ref[e] >= 0)(lambda: pad_block(last_blk_ref[e]).start())
            return 0

        def expert_done(e, _):
            pl.when(last_blk_ref[e] >= 0)(lambda: pad_block(last_blk_ref[e]).wait())
            return 0

        def tail_put(b, _):
            pad_block(b).start()
            return 0

        def tail_done(b, _):
            pad_block(b).wait()
            return 0

        lax.fori_loop(0, N_EXPERTS, expert_put, 0)
        lax.fori_loop(nused_ref[0], nblocks, tail_put, 0)
        lax.fori_loop(0, N_EXPERTS, expert_done, 0)
        lax.fori_loop(nused_ref[0], nblocks, tail_done, 0)

    parity = lax.rem(i, 2)
    li = lax.broadcasted_iota(I32, h2_ref.shape, 0)
    line = li & (SUBLANES - 1)
    tok = i * tm + (li >> LOG2_SUBLANES)
    experts = jnp.broadcast_to(epack_ref[...][:, None, :], (tm, SUBLANES, LANES)).reshape(
        tm * SUBLANES, LANES)
    tag = jnp.where((line & 1) == 0, tok, experts).astype(U32)
    src_ref[parity] = jnp.where(line >= PACK_LINES, tag, h2_ref[...])

    def issue_all(par):
        for r in range(tm):
            for k in range(TOP_K):
                pltpu.make_async_copy(_tile(src_ref.at[par], r),
                                      _tile(buf_ref, dest_ref[k * (tm + 1) + r]),
                                      sem.at[par]).start(priority=k % 2)

    for par in range(2):
        pl.when(parity == par)(functools.partial(issue_all, par))

    def wait_step(which):
        for k in range(TOP_K):
            pltpu.make_async_copy(src_ref.at[0], buf_ref.at[pl.ds(0, tm * SUBLANES)],
                                  sem.at[which]).wait()

    pl.when(i >= 1)(functools.partial(wait_step, 1 - parity))
    pl.when(i == pl.num_programs(0) - 1)(functools.partial(wait_step, parity))


def _dispatch(dest, last_blk, nused, h2_rows, epack, n_rows, blk):
    T = h2_rows.shape[0] // SUBLANES
    tm = min(DMA_TILE, T)
    nt = T // tm
    words = pl.cdiv(TOP_K * (tm + 1), SMEM_1D_TILE) * SMEM_1D_TILE
    d = jnp.pad(dest.reshape(TOP_K, nt, tm).transpose(1, 0, 2), ((0, 0), (0, 0), (0, 1)))
    d = jnp.pad(d.reshape(nt, TOP_K * (tm + 1)), ((0, 0), (0, words - TOP_K * (tm + 1))))
    dest_flat = d.reshape(nt * words)
    smem = pl.BlockSpec(memory_space=pltpu.SMEM)
    return pl.pallas_call(
        functools.partial(_dispatch_kernel, n_tokens=T),
        grid=(nt,),
        in_specs=[
            pl.BlockSpec((words,), lambda i: (i,), memory_space=pltpu.SMEM),
            smem, smem,
            pl.BlockSpec((tm * SUBLANES, LANES), lambda i: (i, 0)),
            pl.BlockSpec((tm, LANES), lambda i: (i, 0)),
        ],
        out_specs=pl.BlockSpec(memory_space=pl.ANY),
        out_shape=jax.ShapeDtypeStruct((n_rows * SUBLANES, LANES), U32),
        scratch_shapes=[pltpu.VMEM((2, tm * SUBLANES, LANES), U32),
                        pltpu.VMEM((2 * blk * SUBLANES, LANES), U32),
                        pltpu.SemaphoreType.DMA((2,)), pltpu.SemaphoreType.DMA(())],
        compiler_params=_cparams("arbitrary"),
        name="dispatch",
    )(dest_flat, last_blk, nused, h2_rows, epack)


def _ffn_kernel(blk_exp_ref, nused_ref, x_ref, wgu_ref, bgu_ref, wd_ref, bd_ref, y_ref,
                wgu_bf_ref, wd_bf_ref, out_ref, idv_ref, ids_ref, ssem, isem, *, n_tokens):
    b = pl.program_id(0)
    nb = pl.num_programs(0)
    tm = x_ref.shape[0] // SUBLANES
    groups = tm // LANES
    F = wd_ref.shape[1]
    nused = nused_ref[0]
    used = b < nused
    slot = lax.rem(b, 2)
    prev = 1 - slot
    new_expert = (b == 0) | (blk_exp_ref[b] != blk_exp_ref[jnp.maximum(b - 1, 0)])

    def send_row(s, r_hi, r_lo, row, priority):
        rid = ids_ref[s * SUBLANES + r_hi, r_lo]
        pltpu.make_async_copy(_tile(out_ref, s * tm + row), _tile(y_ref, rid),
                              ssem.at[s]).start(priority=priority)

    def send_rows_unrolled(s, lo, hi):
        for r in range(lo, hi):
            send_row(s, r // LANES, r % LANES, r, r % 2)

    def send_block_rolled(s):
        def body(g, _):
            def inner(l, _):
                send_row(s, g, l, g * LANES + l, 0)
                return 0
            return lax.fori_loop(0, LANES, inner, 0)
        lax.fori_loop(0, groups, body, 0)

    def wait_block(s):
        rows = tm * SUBLANES
        pltpu.make_async_copy(out_ref.at[pl.ds(0, rows)], y_ref.at[pl.ds(0, rows)],
                              ssem.at[s]).wait()

    def id_copy(s):
        return pltpu.make_async_copy(
            idv_ref, ids_ref.at[pl.ds(pl.multiple_of(s * SUBLANES, SUBLANES), SUBLANES)], isem)

    @pl.when(b == 0)
    def _():
        out_ref[...] = jnp.zeros_like(out_ref)
        rows = tm * SUBLANES
        pltpu.make_async_copy(out_ref.at[pl.ds(0, rows)],
                              y_ref.at[pl.ds(TOP_K * n_tokens * SUBLANES, rows)],
                              ssem.at[0]).start()
        pos = (lax.broadcasted_iota(I32, idv_ref.shape, 0) * LANES
               + lax.broadcasted_iota(I32, idv_ref.shape, 1))
        idv_ref[...] = TOP_K * n_tokens + tm + jnp.minimum(pos, tm - 1)
        id_copy(1).start()
        id_copy(1).wait()

    @pl.when((b >= 1) & (b <= nused))
    def _():
        id_copy(prev).wait()

    @pl.when(used)
    def _():
        rr = lax.broadcasted_iota(I32, (tm, LANES), 0)
        ll = lax.broadcasted_iota(I32, (tm, LANES), 1)
        on_diag = (rr & (LANES - 1)) == ll

        def per_row(tag_line):
            v = x_ref[pl.ds(tag_line, tm, stride=SUBLANES), :].astype(I32)
            return jnp.sum(jnp.where(on_diag, v, 0).reshape(groups, LANES, LANES), axis=1)

        tok = per_row(PACK_LINES)
        experts = per_row(PACK_LINES + 1)
        e = blk_exp_ref[b]
        slot_of = jnp.zeros_like(tok)
        for k in range(1, TOP_K):
            hit = ((experts >> (k * EXPERT_ID_BITS)) & EXPERT_ID_MASK) == e
            slot_of = jnp.where(hit, k, slot_of)
        idv_ref[pl.ds(0, groups), :] = slot_of * n_tokens + tok
        id_copy(slot).start()

    @pl.when(used & new_expert)
    def _():
        wgu_bf_ref[...] = wgu_ref[0].astype(BF16)
        wd_bf_ref[...] = wd_ref[0].astype(BF16)

    def block(slot, prev):
        words = [x_ref[pl.ds(j, tm, stride=SUBLANES), :] for j in range(PACK_LINES)]
        hi = [lax.bitcast_convert_type(w & jnp.uint32(HIGH_HALF), F32) for w in words]
        lo = [lax.bitcast_convert_type(w << BF16_BITS, F32) for w in words]
        x = jnp.concatenate(hi + lo, axis=1).astype(BF16)
        acts = []
        for c in range(FFN_SLICES):
            send_rows_unrolled(prev, c * tm // FFN_SLICES, (c + 1) * tm // FFN_SLICES)
            w = F // FFN_SLICES
            g = jnp.dot(x, wgu_bf_ref[:, c * w:(c + 1) * w], preferred_element_type=F32)
            u = jnp.dot(x, wgu_bf_ref[:, F + c * w:F + (c + 1) * w],
                        preferred_element_type=F32)
            g = jnp.minimum(g + bgu_ref[0, :, c * w:(c + 1) * w], SWIGLU_LIMIT)
            u = jnp.clip(u + bgu_ref[0, :, F + c * w:F + (c + 1) * w],
                         -SWIGLU_LIMIT, SWIGLU_LIMIT)
            acts.append((g * jax.nn.sigmoid(SWIGLU_ALPHA * g) * (u + 1.0)).astype(BF16))
        act = jnp.concatenate(acts, axis=1)
        y = jnp.dot(act, wd_bf_ref[...], preferred_element_type=F32) + bd_ref[0]
        wait_block(slot)
        base = slot * (tm * SUBLANES)
        for j in range(SUBLANES):
            out_ref[pl.ds(base + j, tm, stride=SUBLANES), :] = y[:, j * LANES:(j + 1) * LANES]

    for parity in range(2):
        pl.when(used & (slot == parity))(functools.partial(block, parity, 1 - parity))

    @pl.when(b == nused)
    def _():
        wait_block(slot)
        send_block_rolled(prev)
        wait_block(prev)

    @pl.when(used & (b == nb - 1))
    def _():
        wait_block(prev)
        id_copy(slot).wait()
        send_block_rolled(slot)
        wait_block(slot)


def _ffn(blk_exp, nused, buf, wgu, bgu, wd, bd, tm, n_tokens):
    nb = buf.shape[0] // (tm * SUBLANES)
    D = wgu.shape[1]
    F = wd.shape[1]

    def xmap(b, be, nu):
        return (jnp.minimum(b, nu[0] - 1), 0)

    def wmap(b, be, nu):
        return (be[jnp.minimum(b, nu[0] - 1)], 0, 0)

    grid_spec = pltpu.PrefetchScalarGridSpec(
        num_scalar_prefetch=2,
        grid=(nb,),
        in_specs=[
            pl.BlockSpec((tm * SUBLANES, LANES), xmap),
            pl.BlockSpec((1, D, 2 * F), wmap),
            pl.BlockSpec((1, 1, 2 * F), wmap),
            pl.BlockSpec((1, F, D), wmap),
            pl.BlockSpec((1, 1, D), wmap),
        ],
        out_specs=pl.BlockSpec(memory_space=pl.ANY),
        scratch_shapes=[
            pltpu.VMEM((D, 2 * F), BF16), pltpu.VMEM((F, D), BF16),
            pltpu.VMEM((2 * tm * SUBLANES, LANES), F32),
            pltpu.VMEM((SUBLANES, LANES), I32),
            pltpu.SMEM((2 * SUBLANES, LANES), I32),
            pltpu.SemaphoreType.DMA((2,)), pltpu.SemaphoreType.DMA(()),
        ],
    )
    return pl.pallas_call(
        functools.partial(_ffn_kernel, n_tokens=n_tokens),
        grid_spec=grid_spec,
        out_shape=jax.ShapeDtypeStruct(((TOP_K * n_tokens + 2 * tm) * SUBLANES, LANES), F32),
        compiler_params=_cparams("arbitrary"),
        name="ffn",
    )(blk_exp, nused, buf, wgu, bgu, wd, bd)


def _combine_kernel(gate_ref, x2_ref, fg_ref, *refs):
    y_refs, o_ref = refs[:TOP_K], refs[TOP_K]
    tm, D = x2_ref.shape
    nj = D // LANES
    gate = gate_ref[...]
    cols = []
    for j in range(nj):
        acc = None
        for k in range(TOP_K):
            part = gate[:, k:k + 1] * y_refs[k][pl.ds(j, tm, stride=nj), :]
            acc = part if acc is None else acc + part
        cols.append(acc)
    y = jnp.concatenate(cols, axis=1)
    o_ref[...] = _rms(x2_ref[...] + y, fg_ref[...])


def _combine(gate4, x2, fg, y_rows):
    T, D = x2.shape
    tm = min(ROW_TILE, T)
    nj = D // LANES
    nt = T // tm

    def slot_spec(k):
        return pl.BlockSpec((tm * nj, LANES), lambda i: (k * nt + i, 0))

    return pl.pallas_call(
        _combine_kernel,
        grid=(nt,),
        in_specs=[
            pl.BlockSpec((tm, LANES), lambda i: (i, 0)),
            pl.BlockSpec((tm, D), lambda i: (i, 0)),
            pl.BlockSpec((1, D), lambda i: (0, 0)),
        ] + [slot_spec(k) for k in range(TOP_K)],
        out_specs=pl.BlockSpec((tm, D), lambda i: (i, 0)),
        out_shape=jax.ShapeDtypeStruct((T, D), F32),
        compiler_params=_cparams("parallel"),
        name="combine",
    )(gate4, x2, fg, *([y_rows] * TOP_K))


def _rope_tables(S):
    rows = S // GRID_W
    row = jnp.repeat(jnp.arange(rows, dtype=I32), GRID_W).astype(F32)
    col = jnp.tile(jnp.arange(GRID_W, dtype=I32), rows).astype(F32)
    inv = ROPE_THETA ** (-jnp.arange(0, AXIS_DIM, 2, dtype=F32) / AXIS_DIM)
    ang_r = row[:, None] * inv[None, :]
    ang_c = col[:, None] * inv[None, :]
    cos = jnp.concatenate([jnp.cos(ang_r)] * 2 + [jnp.cos(ang_c)] * 2, axis=1)
    sin = jnp.concatenate([-jnp.sin(ang_r), jnp.sin(ang_r), -jnp.sin(ang_c), jnp.sin(ang_c)], axis=1)
    return jnp.tile(cos, (1, 2)), jnp.tile(sin, (1, 2))


def _block_diag_gates(wa, ba, wi, bi, ct):
    nb, bw = wa.shape[1], wa.shape[2]
    per = ct // bw
    nct = nb // per
    eye = jnp.eye(per, dtype=wa.dtype)

    def tiles(w):
        w = w.reshape(nct, per, bw, bw)
        return jnp.einsum('cpij,pq->cpiqj', w, eye).reshape(nct, ct, ct)

    wg = jnp.concatenate([tiles(wa[0]), tiles(wi[0]), tiles(wa[1]), tiles(wi[1])], axis=2)
    bias = lambda b: b.reshape(nct, 1, ct)
    bg = jnp.concatenate([bias(ba[0]), bias(bi[0]), bias(ba[1]), bias(bi[1])], axis=2)
    return (0.5 * wg).astype(BF16), 0.5 * bg


def _layer(x2d, B, S, norm1_g, w_in, b_in, q_norm_g, k_norm_g, conv_w, conv_b, lru_wa, lru_ba,
           lru_wi, lru_bi, lru_lambda, w_attn_o, w_lru_o, w_out, norm2_g, w_router, b_router,
           w_gu, b_gu, w_down, b_down, out_g):
    T, D = x2d.shape
    C = conv_w.shape[1]
    nqkv = Q_WIDTH + 2 * KV_WIDTH
    row2 = lambda v: v.reshape(1, -1)

    cos_t, sin_t = _rope_tables(S)
    head = jnp.arange(LANES) // HEAD_DIM
    hsum = (head[:, None] == head[None, :]).astype(BF16)
    q, k, v = _qkv(x2d, row2(norm1_g), w_in[:, :nqkv].astype(BF16), row2(b_in[:nqkv]),
                   cos_t, sin_t, row2(jnp.tile(q_norm_g, 2)), row2(jnp.tile(k_norm_g, 2)),
                   hsum, B, S)
    proj = _proj(x2d, row2(norm1_g), w_in[:, nqkv:].astype(BF16), row2(b_in[nqkv:]))
    attn = _attn(q, k, v)
    wg, bg = _block_diag_gates(lru_wa, lru_ba, lru_wi, lru_bi, LRU_CT)
    rec = _lru(proj, conv_w, row2(conv_b), wg, bg, lru_lambda, B, S, C)

    pad = LANES - N_EXPERTS
    rw = jnp.pad(w_router, ((0, 0), (0, pad)))
    rw_hi = rw.astype(BF16)
    rw_lo = (rw - rw_hi.astype(F32)).astype(BF16)
    rb = jnp.pad(b_router, (0, pad), constant_values=NEG_BIG).reshape(1, LANES)
    x2, h2_rows, logits = _merge(x2d, attn, rec, proj, w_attn_o.astype(BF16),
                                 w_lru_o.astype(BF16), w_out.astype(BF16), row2(norm2_g),
                                 rw_hi, rw_lo, rb)

    gate4, epack, _, eidx_t, rank_t, cnt = _route(logits)
    counts = cnt[:, 0].astype(I32)
    tm = FFN_TM
    nblk = (counts + tm - 1) // tm
    pend_blk = jnp.cumsum(nblk)
    pstart = (pend_blk - nblk) * tm
    A = T * TOP_K
    nb = (A + N_EXPERTS * (tm - 1) + tm - 1) // tm
    blk_exp = jnp.minimum(
        jnp.sum(pend_blk[None, :] <= jnp.arange(nb, dtype=I32)[:, None], axis=1),
        N_EXPERTS - 1).astype(I32)
    nused = pend_blk[-1:].astype(I32)
    first_row = jnp.sum(jnp.where(eidx_t[:TOP_K, :, None] == jnp.arange(N_EXPERTS), pstart, 0),
                        axis=-1)
    dest = (first_row + rank_t[:TOP_K]).astype(I32)
    last_blk = jnp.where(nblk > 0, pend_blk - 1, -1).astype(I32)

    buf = _dispatch(dest, last_blk, nused, h2_rows, epack, nb * tm, tm)
    y_rows = _ffn(blk_exp, nused, buf, w_gu, b_gu.reshape(N_EXPERTS, 1, -1), w_down,
                  b_down.reshape(N_EXPERTS, 1, -1), tm, T)
    return _combine(gate4, x2, row2(out_g), y_rows)


def kernel(x, norm1_g, w_in, b_in, q_norm_g, k_norm_g, conv_w, conv_b, lru_wa, lru_ba, lru_wi,
           lru_bi, lru_lambda, w_attn_o, w_lru_o, w_out, norm2_g, w_router, b_router, w_gu, b_gu,
           w_down, b_down, final_g):
    B, S, D = x.shape
    depth = norm1_g.shape[0]
    assert depth == 1, "the fused final RMSNorm assumes a single layer"
    assert S % (TIME_CHUNKS * SCAN_UNROLL) == 0 and S % GRID_W == 0
    assert D == 2 * PACK_LINES * LANES, "a token's bf16 row must fill PACK_LINES word lines"
    out = _layer(x.reshape(B * S, D), B, S, norm1_g[0], w_in[0], b_in[0], q_norm_g[0],
                 k_norm_g[0], conv_w[0], conv_b[0], lru_wa[0], lru_ba[0], lru_wi[0], lru_bi[0],
                 lru_lambda[0], w_attn_o[0], w_lru_o[0], w_out[0], norm2_g[0], w_router[0],
                 b_router[0], w_gu[0], b_gu[0], w_down[0], b_down[0], final_g)
    return out.reshape(B, S, D)
```

```python
import functools

import jax
import jax.numpy as jnp
from jax import lax
from jax.experimental import pallas as pl
from jax.experimental.pallas import tpu as pltpu

F32 = jnp.float32
BF16 = jnp.bfloat16
I32 = jnp.int32
U32 = jnp.uint32

LANES = 128
SUBLANES = 8
PACK_LINES = 4
BF16_BITS = 16
HIGH_HALF = 0xFFFF0000
LOG2_SUBLANES = 3
EXPERT_ID_BITS = 8
EXPERT_ID_MASK = 0xFF
NO_EXPERT = 0xFFFFFFFF
SMEM_1D_TILE = 1024
VMEM_LIMIT_BYTES = 56 * 1024 * 1024

HEAD_DIM = 64
N_Q_HEADS = 8
N_KV_HEADS = 2
Q_GROUP = N_Q_HEADS // N_KV_HEADS
Q_WIDTH = N_Q_HEADS * HEAD_DIM
KV_WIDTH = N_KV_HEADS * HEAD_DIM
AXIS_DIM = HEAD_DIM // 2
ROT_HALF = AXIS_DIM // 2
ROPE_THETA = 10000.0
GRID_W = 64
RG_C = 8.0
N_EXPERTS = 32
TOP_K = 4
SWIGLU_LIMIT = 7.0
SWIGLU_ALPHA = 1.702
NORM_EPS = 1e-6
LOG2E = 1.4426950408889634
NEG_BIG = -1e30
TINY = 1e-30

ROW_TILE = 512
ATT_TQ = 1024
ATT_TK = 512
LRU_CT = 256
LRU_ROWS = 512
SCAN_UNROLL = 32
TIME_CHUNKS = SUBLANES
FFN_TM = 512
FFN_SLICES = 4
DMA_TILE = 256


def _cparams(*sem):
    return pltpu.CompilerParams(dimension_semantics=sem, vmem_limit_bytes=VMEM_LIMIT_BYTES)


def _rms(x, g):
    return x * lax.rsqrt(jnp.mean(x * x, axis=-1, keepdims=True) + NORM_EPS) * g


def _qkv_kernel(x_ref, g1_ref, w_ref, b_ref, cos_ref, sin_ref, qg_ref, kg_ref, hsum_ref,
                q_ref, k_ref, v_ref):
    h = _rms(x_ref[...], g1_ref[...]).astype(BF16)
    p = jnp.dot(h, w_ref[...], preferred_element_type=F32) + b_ref[...]
    cos = cos_ref[...]
    sin = sin_ref[...]
    hsum = hsum_ref[...]
    lane = lax.broadcasted_iota(I32, cos.shape, 1)
    first_half = (lane % AXIS_DIM) < ROT_HALF

    def norm_rope(c, gain):
        sq = c * c
        hi = sq.astype(BF16)
        lo = (sq - hi.astype(F32)).astype(BF16)
        ms = (jnp.dot(hi, hsum, preferred_element_type=F32)
              + jnp.dot(lo, hsum, preferred_element_type=F32)) * (1.0 / HEAD_DIM)
        y = c * lax.rsqrt(ms + NORM_EPS) * gain
        partner = jnp.where(first_half, pltpu.roll(y, LANES - ROT_HALF, 1),
                            pltpu.roll(y, ROT_HALF, 1))
        return y * cos + partner * sin

    qg = qg_ref[...]
    for c in range(Q_WIDTH // LANES):
        y = norm_rope(p[:, c * LANES:(c + 1) * LANES], qg) * (HEAD_DIM ** -0.5 * LOG2E)
        yt = y.T.astype(BF16)
        q_ref[0, 2 * c] = yt[:HEAD_DIM]
        q_ref[0, 2 * c + 1] = yt[HEAD_DIM:]
    yk = norm_rope(p[:, Q_WIDTH:Q_WIDTH + KV_WIDTH], kg_ref[...])
    k_ref[0, 0] = yk[:, :HEAD_DIM].astype(BF16)
    k_ref[0, 1] = yk[:, HEAD_DIM:].astype(BF16)
    vt = p[:, Q_WIDTH + KV_WIDTH:].T
    tm = vt.shape[1]
    ones_row = (lax.broadcasted_iota(I32, (HEAD_DIM, tm), 0) == 0).astype(BF16)
    v_ref[0, 0] = jnp.concatenate([vt[:HEAD_DIM].astype(BF16), ones_row], axis=0)
    v_ref[0, 1] = jnp.concatenate([vt[HEAD_DIM:].astype(BF16), ones_row], axis=0)


def _qkv(x2d, g1, w_qkv, b_qkv, cos_t, sin_t, qg, kg, hsum, B, S):
    T, D = x2d.shape
    tm = min(ROW_TILE, S)
    ns = S // tm
    n = w_qkv.shape[1]
    full = lambda shape: pl.BlockSpec(shape, lambda i: (0,) * len(shape))
    return pl.pallas_call(
        _qkv_kernel,
        grid=(T // tm,),
        in_specs=[
            pl.BlockSpec((tm, D), lambda i: (i, 0)),
            full((1, D)), full((D, n)), full((1, n)),
            pl.BlockSpec((tm, LANES), lambda i: (i % ns, 0)),
            pl.BlockSpec((tm, LANES), lambda i: (i % ns, 0)),
            full((1, LANES)), full((1, LANES)), full((LANES, LANES)),
        ],
        out_specs=[
            pl.BlockSpec((1, N_Q_HEADS, HEAD_DIM, tm), lambda i: (i // ns, 0, 0, i % ns)),
            pl.BlockSpec((1, N_KV_HEADS, tm, HEAD_DIM), lambda i: (i // ns, 0, i % ns, 0)),
            pl.BlockSpec((1, N_KV_HEADS, LANES, tm), lambda i: (i // ns, 0, 0, i % ns)),
        ],
        out_shape=[
            jax.ShapeDtypeStruct((B, N_Q_HEADS, HEAD_DIM, S), BF16),
            jax.ShapeDtypeStruct((B, N_KV_HEADS, S, HEAD_DIM), BF16),
            jax.ShapeDtypeStruct((B, N_KV_HEADS, LANES, S), BF16),
        ],
        compiler_params=_cparams("parallel"),
        name="qkv",
    )(x2d, g1, w_qkv, b_qkv, cos_t, sin_t, qg, kg, hsum)


def _proj_kernel(x_ref, g_ref, w_ref, b_ref, o_ref):
    h = _rms(x_ref[...], g_ref[...]).astype(BF16)
    o_ref[...] = jnp.dot(h, w_ref[...], preferred_element_type=F32) + b_ref[...]


def _proj(x2d, g, w, b):
    T, D = x2d.shape
    n = w.shape[1]
    tm = min(ROW_TILE, T)
    return pl.pallas_call(
        _proj_kernel,
        grid=(T // tm,),
        in_specs=[
            pl.BlockSpec((tm, D), lambda i: (i, 0)),
            pl.BlockSpec((1, D), lambda i: (0, 0)),
            pl.BlockSpec((D, n), lambda i: (0, 0)),
            pl.BlockSpec((1, n), lambda i: (0, 0)),
        ],
        out_specs=pl.BlockSpec((tm, n), lambda i: (i, 0)),
        out_shape=jax.ShapeDtypeStruct((T, n), F32),
        compiler_params=_cparams("parallel"),
        name="proj",
    )(x2d, g, w, b)


def _attn_kernel(q_ref, k_ref, v_ref, o_ref, s0_ref, s1_ref, *, tk):
    tq = q_ref.shape[3]
    S = k_ref.shape[2]
    M = Q_GROUP * tq
    n = S // tk
    qT = jnp.concatenate([q_ref[0, h] for h in range(Q_GROUP)], axis=1)

    def scores(j, s_ref):
        off = pl.multiple_of(j * tk, tk)
        s_ref[...] = jnp.dot(k_ref[0, 0, pl.ds(off, tk), :], qT, preferred_element_type=F32)

    def absorb(j, s_ref, carry):
        m, acc = carry
        off = pl.multiple_of(j * tk, tk)
        vc = v_ref[0, 0, :, pl.ds(off, tk)]
        s = s_ref[...]
        m_new = jnp.maximum(m, jnp.max(s, axis=0, keepdims=True))
        alpha = jnp.exp2(m - m_new)
        p = jnp.exp2(s - m_new).astype(BF16)
        return m_new, alpha * acc + jnp.dot(vc, p, preferred_element_type=F32)

    def pair(i, carry):
        scores(2 * i + 1, s1_ref)
        carry = absorb(2 * i, s0_ref, carry)
        scores(2 * i + 2, s0_ref)
        return absorb(2 * i + 1, s1_ref, carry)

    scores(0, s0_ref)
    carry = (jnp.full((1, M), NEG_BIG, F32), jnp.zeros((LANES, M), F32))
    for i in range(n // 2 - 1):
        carry = pair(i, carry)
    scores(n - 1, s1_ref)
    carry = absorb(n - 2, s0_ref, carry)
    _, acc = absorb(n - 1, s1_ref, carry)
    outT = acc[:HEAD_DIM] / acc[HEAD_DIM:HEAD_DIM + 1]
    stacked = jnp.concatenate(
        [outT[:, g * tq:(g + 1) * tq] for g in range(Q_GROUP)], axis=0)
    o_ref[...] = stacked.T.astype(BF16)


def _attn(qT, k, vT):
    B, _, _, S = qT.shape
    tq = min(ATT_TQ, S)
    tk = min(ATT_TK, S)
    nq = S // tq
    assert (S // tk) % 2 == 0, "key chunks are processed in pairs"
    score_buf = pltpu.VMEM((tk, Q_GROUP * tq), F32)
    return pl.pallas_call(
        functools.partial(_attn_kernel, tk=tk),
        grid=(B, N_KV_HEADS, nq),
        in_specs=[
            pl.BlockSpec((1, Q_GROUP, HEAD_DIM, tq), lambda b, g, i: (b, g, 0, i)),
            pl.BlockSpec((1, 1, S, HEAD_DIM), lambda b, g, i: (b, g, 0, 0)),
            pl.BlockSpec((1, 1, LANES, S), lambda b, g, i: (b, g, 0, 0)),
        ],
        out_specs=pl.BlockSpec((tq, Q_GROUP * HEAD_DIM), lambda b, g, i: (b * nq + i, g)),
        out_shape=jax.ShapeDtypeStruct((B * S, Q_WIDTH), BF16),
        scratch_shapes=[score_buf, score_buf],
        compiler_params=_cparams("parallel", "parallel", "parallel"),
        name="attn",
    )(qT, k, vT)


def _shift_chunks(v, down):
    row = lax.broadcasted_iota(I32, v.shape, 0)
    if down:
        return jnp.where(row == 0, 0.0, pltpu.roll(v, 1, 0))
    return jnp.where(row == SUBLANES - 1, 0.0, pltpu.roll(v, SUBLANES - 1, 0))


def _lru_kernel(xr_ref, yr_ref, cw_ref, cb_ref, wg_ref, bg_ref, lam_ref, o_ref,
                xe_ref, af_ref, uf_ref, ab_ref, ub_ref, *, rows):
    S, ct = xr_ref.shape
    lc = S // TIME_CHUNKS
    halo = SUBLANES

    ng = ct // LANES

    def xe_rows(r0, n):
        return jnp.concatenate([xe_ref[g, pl.ds(r0, n), :] for g in range(ng)], axis=1)

    def xe_store(r0, n, val):
        for g in range(ng):
            xe_ref[g, pl.ds(r0, n), :] = val[:, g * LANES:(g + 1) * LANES]

    for s in range(TIME_CHUNKS):
        for g in range(ng):
            xe_ref[g, pl.ds(halo + s, lc, stride=SUBLANES), :] = (
                xr_ref[pl.ds(s * lc, lc), g * LANES:(g + 1) * LANES])
    xe_store(0, halo, _shift_chunks(xe_rows(S, halo), True))
    first = xe_rows(halo, halo)
    second = xe_rows(2 * halo, halo)
    xe_store(S + halo, halo, _shift_chunks(first, False))
    xe_store(S + 2 * halo, halo, _shift_chunks(second, False))

    cw = cw_ref[...]
    cb = cb_ref[...]
    lam = lam_ref[...]
    log_sig = jnp.minimum(lam, 0.0) - jnp.log(1.0 + jnp.exp(-jnp.abs(lam)))
    c_half = (0.5 * RG_C * LOG2E) * log_sig
    wg = wg_ref[0]
    bg = bg_ref[0]

    def gate_chunk(i, _):
        r0 = pl.multiple_of(i * rows, rows)
        xc = cb
        for j in range(4):
            xc = xc + cw[j:j + 1, :] * xe_rows(r0 + j * halo, rows)
        t = jnp.tanh(jnp.dot(xc.astype(BF16), wg, preferred_element_type=F32) + bg)
        x_half = 0.5 * xc
        for d, (a_ref, u_ref) in enumerate(((af_ref, uf_ref), (ab_ref, ub_ref))):
            t_r = t[:, (2 * d) * ct:(2 * d + 1) * ct]
            t_i = t[:, (2 * d + 1) * ct:(2 * d + 2) * ct]
            ch = c_half[d:d + 1, :]
            a = jnp.exp2(t_r * ch + ch)
            a_ref[pl.ds(r0, rows), :] = a
            v = 1.0 - a * a
            root = jnp.maximum(v, 0.0) * lax.rsqrt(jnp.maximum(v, TINY))
            u_ref[pl.ds(r0, rows), :] = root * ((t_i + 1.0) * x_half)
        return 0

    for i in range(S // rows):
        gate_chunk(i, 0)

    unroll = SCAN_UNROLL

    def scan_body(i, carry):
        hf, pf, hb, pb = carry
        for k in range(unroll):
            tf = i * unroll + k
            rf = pl.multiple_of(tf * SUBLANES, SUBLANES)
            rb = pl.multiple_of((lc - 1 - tf) * SUBLANES, SUBLANES)
            a = af_ref[pl.ds(rf, SUBLANES), :]
            hf = a * hf + uf_ref[pl.ds(rf, SUBLANES), :]
            pf = a * pf
            uf_ref[pl.ds(rf, SUBLANES), :] = hf
            af_ref[pl.ds(rf, SUBLANES), :] = pf
            a = ab_ref[pl.ds(rb, SUBLANES), :]
            hb = a * hb + ub_ref[pl.ds(rb, SUBLANES), :]
            pb = a * pb
            ub_ref[pl.ds(rb, SUBLANES), :] = hb
            ab_ref[pl.ds(rb, SUBLANES), :] = pb
        return hf, pf, hb, pb

    zero = jnp.zeros((SUBLANES, ct), F32)
    one = jnp.ones((SUBLANES, ct), F32)
    hf_end, pf_end, hb_end, pb_end = lax.fori_loop(0, lc // unroll, scan_body,
                                                   (zero, one, zero, one))

    def chunk_carry(h_end, p_end, down):
        f = h_end
        for _ in range(TIME_CHUNKS - 1):
            f = h_end + p_end * _shift_chunks(f, down)
        return _shift_chunks(f, down)

    cin_f = chunk_carry(hf_end, pf_end, True)
    cin_b = chunk_carry(hb_end, pb_end, False)

    def fix_chunk(i, _):
        r0 = pl.multiple_of(i * rows, rows)
        reps = rows // SUBLANES
        hf = uf_ref[pl.ds(r0, rows), :] + af_ref[pl.ds(r0, rows), :] * jnp.tile(cin_f, (reps, 1))
        hb = ub_ref[pl.ds(r0, rows), :] + ab_ref[pl.ds(r0, rows), :] * jnp.tile(cin_b, (reps, 1))
        xe_store(r0, rows, hf + hb)
        return 0

    for i in range(S // rows):
        fix_chunk(i, 0)

    for s in range(TIME_CHUNKS):
        hsum = jnp.concatenate(
            [xe_ref[g, pl.ds(s, lc, stride=SUBLANES), :] for g in range(ng)], axis=1)
        y = yr_ref[pl.ds(s * lc, lc), :]
        o_ref[pl.ds(s * lc, lc), :] = (hsum * jax.nn.gelu(y, approximate=True)).astype(BF16)


def _lru(proj, conv_w, conv_b, wg, bg, lam, B, S, C):
    ct = LRU_CT
    nct = C // ct
    rows = min(LRU_ROWS, S)
    return pl.pallas_call(
        functools.partial(_lru_kernel, rows=rows),
        grid=(B, nct),
        in_specs=[
            pl.BlockSpec((S, ct), lambda b, c: (b, c)),
            pl.BlockSpec((S, ct), lambda b, c: (b, nct + c)),
            pl.BlockSpec((4, ct), lambda b, c: (0, c)),
            pl.BlockSpec((1, ct), lambda b, c: (0, c)),
            pl.BlockSpec((1, ct, 4 * ct), lambda b, c: (c, 0, 0)),
            pl.BlockSpec((1, 1, 4 * ct), lambda b, c: (c, 0, 0)),
            pl.BlockSpec((2, ct), lambda b, c: (0, c)),
        ],
        out_specs=pl.BlockSpec((S, ct), lambda b, c: (b, c)),
        out_shape=jax.ShapeDtypeStruct((B * S, C), BF16),
        scratch_shapes=[
            pltpu.VMEM((ct // LANES, S + 3 * SUBLANES, LANES), F32),
            pltpu.VMEM((S, ct), F32), pltpu.VMEM((S, ct), F32),
            pltpu.VMEM((S, ct), F32), pltpu.VMEM((S, ct), F32),
        ],
        compiler_params=_cparams("parallel", "parallel"),
        name="lru",
    )(proj, proj, conv_w, conv_b, wg, bg, lam)


def _merge_kernel(x_ref, attn_ref, rec_ref, gl_ref, wa_ref, wr_ref, wo_ref, g2_ref,
                  rw_hi_ref, rw_lo_ref, rb_ref, x2_ref, h2_ref, lg_ref):
    D = x_ref.shape[1]
    gl = gl_ref[...]
    ma = jnp.dot(attn_ref[...], wa_ref[...], preferred_element_type=F32)
    mr = jnp.dot(rec_ref[...], wr_ref[...], preferred_element_type=F32)
    merged = jax.nn.sigmoid(gl[:, :D]) * ma + jax.nn.sigmoid(gl[:, D:]) * mr
    x2 = x_ref[...] + jnp.dot(merged.astype(BF16), wo_ref[...], preferred_element_type=F32)
    x2_ref[...] = x2
    h2 = _rms(x2, g2_ref[...])
    hi = h2.astype(BF16)
    lo = (h2 - hi.astype(F32)).astype(BF16)
    lg_ref[...] = (jnp.dot(hi, rw_hi_ref[...], preferred_element_type=F32)
                   + jnp.dot(lo, rw_hi_ref[...], preferred_element_type=F32)
                   + jnp.dot(hi, rw_lo_ref[...], preferred_element_type=F32)) + rb_ref[...]
    tm = h2.shape[0]
    bits = lax.bitcast_convert_type(hi.astype(F32), U32)
    words = bits[:, :D // 2] | (bits[:, D // 2:] >> BF16_BITS)
    for j in range(PACK_LINES):
        h2_ref[pl.ds(j, tm, stride=SUBLANES), :] = words[:, j * LANES:(j + 1) * LANES]
    for j in range(PACK_LINES, SUBLANES):
        h2_ref[pl.ds(j, tm, stride=SUBLANES), :] = jnp.zeros((tm, LANES), U32)


def _merge(x2d, attn, rec, proj, wa, wr, wo, g2, rw_hi, rw_lo, rb):
    T, D = x2d.shape
    tm = min(ROW_TILE, T)
    nj = D // LANES
    full = lambda a: pl.BlockSpec(a.shape, lambda i: (0,) * a.ndim)
    return pl.pallas_call(
        _merge_kernel,
        grid=(T // tm,),
        in_specs=[
            pl.BlockSpec((tm, D), lambda i: (i, 0)),
            pl.BlockSpec((tm, Q_WIDTH), lambda i: (i, 0)),
            pl.BlockSpec((tm, D), lambda i: (i, 0)),
            pl.BlockSpec((tm, 2 * D), lambda i: (i, 1)),
            full(wa), full(wr), full(wo), full(g2), full(rw_hi), full(rw_lo), full(rb),
        ],
        out_specs=[
            pl.BlockSpec((tm, D), lambda i: (i, 0)),
            pl.BlockSpec((tm * nj, LANES), lambda i: (i, 0)),
            pl.BlockSpec((tm, LANES), lambda i: (i, 0)),
        ],
        out_shape=[
            jax.ShapeDtypeStruct((T, D), F32),
            jax.ShapeDtypeStruct((T * nj, LANES), U32),
            jax.ShapeDtypeStruct((T, LANES), F32),
        ],
        compiler_params=_cparams("parallel"),
        name="merge",
    )(x2d, attn, rec, proj, wa, wr, wo, g2, rw_hi, rw_lo, rb)


def _route_kernel(lg_ref, gate_ref, epack_ref, gate_t_ref, eidx_t_ref, rank_t_ref, cnt_ref):
    i = pl.program_id(0)

    @pl.when(i == 0)
    def _():
        cnt_ref[...] = jnp.zeros_like(cnt_ref)

    tm = lg_ref.shape[0]
    work = lg_ref[...].T[:N_EXPERTS]
    expert = lax.broadcasted_iota(I32, work.shape, 0)
    sels, vals, idxs = [], [], []
    for _ in range(TOP_K):
        mx = jnp.max(work, axis=0, keepdims=True)
        idx = jnp.min(jnp.where(work == mx, expert, N_EXPERTS), axis=0, keepdims=True)
        sel = expert == idx
        sels.append(sel)
        vals.append(mx)
        idxs.append(idx)
        work = jnp.where(sel, NEG_BIG, work)
    ex = [jnp.exp(v - vals[0]) for v in vals]
    den = ex[0] + ex[1] + ex[2] + ex[3]
    member = (sels[0] | sels[1] | sels[2] | sels[3]).astype(BF16)
    r = lax.broadcasted_iota(I32, (tm, tm), 0)
    c = lax.broadcasted_iota(I32, (tm, tm), 1)
    tri = (r < c).astype(BF16)
    base = cnt_ref[:, 0:1]
    before = jnp.dot(member, tri, preferred_element_type=F32) + base
    ranks = [jnp.sum(jnp.where(sels[k], before, 0.0), axis=0, keepdims=True)
             for k in range(TOP_K)]
    gates = [ex[k] / den for k in range(TOP_K)]
    pad = SUBLANES - TOP_K
    gate_t = jnp.concatenate(gates + [jnp.zeros((pad, tm), F32)], axis=0)
    gate_t_ref[...] = gate_t
    eidx_t_ref[...] = jnp.concatenate(idxs + [jnp.zeros((pad, tm), I32)], axis=0)
    rank_t_ref[...] = jnp.concatenate(ranks + [jnp.zeros((pad, tm), F32)], axis=0).astype(I32)
    gate_ref[...] = jnp.concatenate(
        [gate_t, jnp.zeros((LANES - SUBLANES, tm), F32)], axis=0).T
    packed = idxs[0]
    for k in range(1, TOP_K):
        packed = packed | (idxs[k] << (k * EXPERT_ID_BITS))
    epack_ref[...] = jnp.broadcast_to(packed, (LANES, tm)).T
    cnt_ref[...] = jnp.broadcast_to(
        base + jnp.sum(member.astype(F32), axis=1, keepdims=True), cnt_ref.shape)


def _route(logits):
    T = logits.shape[0]
    tm = min(ROW_TILE, T)
    spec = pl.BlockSpec((tm, LANES), lambda i: (i, 0))
    spec_t = pl.BlockSpec((SUBLANES, tm), lambda i: (0, i))
    return pl.pallas_call(
        _route_kernel,
        grid=(T // tm,),
        in_specs=[spec],
        out_specs=[spec, spec, spec_t, spec_t, spec_t,
                   pl.BlockSpec((N_EXPERTS, LANES), lambda i: (0, 0))],
        out_shape=[
            jax.ShapeDtypeStruct((T, LANES), F32),
            jax.ShapeDtypeStruct((T, LANES), I32),
            jax.ShapeDtypeStruct((SUBLANES, T), F32),
            jax.ShapeDtypeStruct((SUBLANES, T), I32),
            jax.ShapeDtypeStruct((SUBLANES, T), I32),
            jax.ShapeDtypeStruct((N_EXPERTS, LANES), F32),
        ],
        compiler_params=_cparams("arbitrary"),
        name="route",
    )(logits)


def _tile(ref, row):
    return ref.at[pl.ds(pl.multiple_of(row * SUBLANES, SUBLANES), SUBLANES)]


def _dispatch_kernel(dest_ref, last_blk_ref, nused_ref, h2_ref, epack_ref, buf_ref,
                     src_ref, pad_ref, sem, zsem, *, n_tokens):
    i = pl.program_id(0)
    tm = h2_ref.shape[0] // SUBLANES
    blk = pad_ref.shape[0] // (2 * SUBLANES)
    nblocks = buf_ref.shape[0] // (blk * SUBLANES)

    @pl.when(i == 0)
    def _():
        li = lax.broadcasted_iota(I32, pad_ref.shape, 0)
        line = li & (SUBLANES - 1)
        trash = (TOP_K * n_tokens + (li >> LOG2_SUBLANES)).astype(U32)
        tag = jnp.where((line & 1) == 0, trash, jnp.uint32(NO_EXPERT))
        pad_ref[...] = jnp.where(line >= PACK_LINES, tag, jnp.uint32(0))
        rows = blk * SUBLANES

        def pad_block(b):
            half = pl.multiple_of(lax.rem(b, 2) * rows, rows)
            return pltpu.make_async_copy(
                pad_ref.at[pl.ds(half, rows)],
                buf_ref.at[pl.ds(pl.multiple_of(b * rows, rows), rows)], zsem)

        def expert_put(e, _):
            pl.when(last_blk_ref[e] >= 0)(lambda: pad_block(last_blk_ref[e]).start())
            return 0

        def expert_done(e, _):
            pl.when(last_blk_ref[e] >= 0)(lambda: pad_block(last_blk_ref[e]).wait())
            return 0

        def tail_put(b, _):
            pad_block(b).start()
            return 0

        def tail_done(b, _):
            pad_block(b).wait()
            return 0

        lax.fori_loop(0, N_EXPERTS, expert_put, 0)
        lax.fori_loop(nused_ref[0], nblocks, tail_put, 0)
        lax.fori_loop(0, N_EXPERTS, expert_done, 0)
        lax.fori_loop(nused_ref[0], nblocks, tail_done, 0)

    parity = lax.rem(i, 2)
    li = lax.broadcasted_iota(I32, h2_ref.shape, 0)
    line = li & (SUBLANES - 1)
    tok = i * tm + (li >> LOG2_SUBLANES)
    experts = jnp.broadcast_to(epack_ref[...][:, None, :], (tm, SUBLANES, LANES)).reshape(
        tm * SUBLANES, LANES)
    tag = jnp.where((line & 1) == 0, tok, experts).astype(U32)
    src_ref[parity] = jnp.where(line >= PACK_LINES, tag, h2_ref[...])

    def issue_all(par):
        for r in range(tm):
            for k in range(TOP_K):
                pltpu.make_async_copy(_tile(src_ref.at[par], r),
                                      _tile(buf_ref, dest_ref[k * (tm + 1) + r]),
                                      sem.at[par]).start(priority=k % 2)

    for par in range(2):
        pl.when(parity == par)(functools.partial(issue_all, par))

    def wait_step(which):
        for k in range(TOP_K):
            pltpu.make_async_copy(src_ref.at[0], buf_ref.at[pl.ds(0, tm * SUBLANES)],
                                  sem.at[which]).wait()

    pl.when(i >= 1)(functools.partial(wait_step, 1 - parity))
    pl.when(i == pl.num_programs(0) - 1)(functools.partial(wait_step, parity))


def _dispatch(dest, last_blk, nused, h2_rows, epack, n_rows, blk):
    T = h2_rows.shape[0] // SUBLANES
    tm = min(DMA_TILE, T)
    nt = T // tm
    words = pl.cdiv(TOP_K * (tm + 1), SMEM_1D_TILE) * SMEM_1D_TILE
    d = jnp.pad(dest.reshape(TOP_K, nt, tm).transpose(1, 0, 2), ((0, 0), (0, 0), (0, 1)))
    d = jnp.pad(d.reshape(nt, TOP_K * (tm + 1)), ((0, 0), (0, words - TOP_K * (tm + 1))))
    dest_flat = d.reshape(nt * words)
    smem = pl.BlockSpec(memory_space=pltpu.SMEM)
    return pl.pallas_call(
        functools.partial(_dispatch_kernel, n_tokens=T),
        grid=(nt,),
        in_specs=[
            pl.BlockSpec((words,), lambda i: (i,), memory_space=pltpu.SMEM),
            smem, smem,
            pl.BlockSpec((tm * SUBLANES, LANES), lambda i: (i, 0)),
            pl.BlockSpec((tm, LANES), lambda i: (i, 0)),
        ],
        out_specs=pl.BlockSpec(memory_space=pl.ANY),
        out_shape=jax.ShapeDtypeStruct((n_rows * SUBLANES, LANES), U32),
        scratch_shapes=[pltpu.VMEM((2, tm * SUBLANES, LANES), U32),
                        pltpu.VMEM((2 * blk * SUBLANES, LANES), U32),
                        pltpu.SemaphoreType.DMA((2,)), pltpu.SemaphoreType.DMA(())],
        compiler_params=_cparams("arbitrary"),
        name="dispatch",
    )(dest_flat, last_blk, nused, h2_rows, epack)


def _ffn_kernel(blk_exp_ref, nused_ref, x_ref, wgu_ref, bgu_ref, wd_ref, bd_ref, y_ref,
                wgu_bf_ref, wd_bf_ref, out_ref, idv_ref, ids_ref, ssem, isem, *, n_tokens):
    b = pl.program_id(0)
    nb = pl.num_programs(0)
    tm = x_ref.shape[0] // SUBLANES
    groups = tm // LANES
    F = wd_ref.shape[1]
    nused = nused_ref[0]
    used = b < nused
    slot = lax.rem(b, 2)
    prev = 1 - slot
    new_expert = (b == 0) | (blk_exp_ref[b] != blk_exp_ref[jnp.maximum(b - 1, 0)])

    def send_row(s, r_hi, r_lo, row, priority):
        rid = ids_ref[s * SUBLANES + r_hi, r_lo]
        pltpu.make_async_copy(_tile(out_ref, s * tm + row), _tile(y_ref, rid),
                              ssem.at[s]).start(priority=priority)

    def send_rows_unrolled(s, lo, hi):
        for r in range(lo, hi):
            send_row(s, r // LANES, r % LANES, r, r % 2)

    def send_block_rolled(s):
        def body(g, _):
            def inner(l, _):
                send_row(s, g, l, g * LANES + l, 0)
                return 0
            return lax.fori_loop(0, LANES, inner, 0)
        lax.fori_loop(0, groups, body, 0)

    def wait_block(s):
        rows = tm * SUBLANES
        pltpu.make_async_copy(out_ref.at[pl.ds(0, rows)], y_ref.at[pl.ds(0, rows)],
                              ssem.at[s]).wait()

    def id_copy(s):
        return pltpu.make_async_copy(
            idv_ref, ids_ref.at[pl.ds(pl.multiple_of(s * SUBLANES, SUBLANES), SUBLANES)], isem)

    @pl.when(b == 0)
    def _():
        out_ref[...] = jnp.zeros_like(out_ref)
        rows = tm * SUBLANES
        pltpu.make_async_copy(out_ref.at[pl.ds(0, rows)],
                              y_ref.at[pl.ds(TOP_K * n_tokens * SUBLANES, rows)],
                              ssem.at[0]).start()
        pos = (lax.broadcasted_iota(I32, idv_ref.shape, 0) * LANES
               + lax.broadcasted_iota(I32, idv_ref.shape, 1))
        idv_ref[...] = TOP_K * n_tokens + tm + jnp.minimum(pos, tm - 1)
        id_copy(1).start()
        id_copy(1).wait()

    @pl.when((b >= 1) & (b <= nused))
    def _():
        id_copy(prev).wait()

    @pl.when(used)
    def _():
        rr = lax.broadcasted_iota(I32, (tm, LANES), 0)
        ll = lax.broadcasted_iota(I32, (tm, LANES), 1)
        on_diag = (rr & (LANES - 1)) == ll

        def per_row(tag_line):
            v = x_ref[pl.ds(tag_line, tm, stride=SUBLANES), :].astype(I32)
            return jnp.sum(jnp.where(on_diag, v, 0).reshape(groups, LANES, LANES), axis=1)

        tok = per_row(PACK_LINES)
        experts = per_row(PACK_LINES + 1)
        e = blk_exp_ref[b]
        slot_of = jnp.zeros_like(tok)
        for k in range(1, TOP_K):
            hit = ((experts >> (k * EXPERT_ID_BITS)) & EXPERT_ID_MASK) == e
            slot_of = jnp.where(hit, k, slot_of)
        idv_ref[pl.ds(0, groups), :] = slot_of * n_tokens + tok
        id_copy(slot).start()

    @pl.when(used & new_expert)
    def _():
        wgu_bf_ref[...] = wgu_ref[0].astype(BF16)
        wd_bf_ref[...] = wd_ref[0].astype(BF16)

    def block(slot, prev):
        words = [x_ref[pl.ds(j, tm, stride=SUBLANES), :] for j in range(PACK_LINES)]
        hi = [lax.bitcast_convert_type(w & jnp.uint32(HIGH_HALF), F32) for w in words]
        lo = [lax.bitcast_convert_type(w << BF16_BITS, F32) for w in words]
        x = jnp.concatenate(hi + lo, axis=1).astype(BF16)
        acts = []
        for c in range(FFN_SLICES):
            send_rows_unrolled(prev, c * tm // FFN_SLICES, (c + 1) * tm // FFN_SLICES)
            w = F // FFN_SLICES
            g = jnp.dot(x, wgu_bf_ref[:, c * w:(c + 1) * w], preferred_element_type=F32)
            u = jnp.dot(x, wgu_bf_ref[:, F + c * w:F + (c + 1) * w],
                        preferred_element_type=F32)
            g = jnp.minimum(g + bgu_ref[0, :, c * w:(c + 1) * w], SWIGLU_LIMIT)
            u = jnp.clip(u + bgu_ref[0, :, F + c * w:F + (c + 1) * w],
                         -SWIGLU_LIMIT, SWIGLU_LIMIT)
            acts.append((g * jax.nn.sigmoid(SWIGLU_ALPHA * g) * (u + 1.0)).astype(BF16))
        act = jnp.concatenate(acts, axis=1)
        y = jnp.dot(act, wd_bf_ref[...], preferred_element_type=F32) + bd_ref[0]
        wait_block(slot)
        base = slot * (tm * SUBLANES)
        for j in range(SUBLANES):
            out_ref[pl.ds(base + j, tm, stride=SUBLANES), :] = y[:, j * LANES:(j + 1) * LANES]

    for parity in range(2):
        pl.when(used & (slot == parity))(functools.partial(block, parity, 1 - parity))

    @pl.when(b == nused)
    def _():
        wait_block(slot)
        send_block_rolled(prev)
        wait_block(prev)

    @pl.when(used & (b == nb - 1))
    def _():
        wait_block(prev)
        id_copy(slot).wait()
        send_block_rolled(slot)
        wait_block(slot)


def _ffn(blk_exp, nused, buf, wgu, bgu, wd, bd, tm, n_tokens):
    nb = buf.shape[0] // (tm * SUBLANES)
    D = wgu.shape[1]
    F = wd.shape[1]

    def xmap(b, be, nu):
        return (jnp.minimum(b, nu[0] - 1), 0)

    def wmap(b, be, nu):
        return (be[jnp.minimum(b, nu[0] - 1)], 0, 0)

    grid_spec = pltpu.PrefetchScalarGridSpec(
        num_scalar_prefetch=2,
        grid=(nb,),
        in_specs=[
            pl.BlockSpec((tm * SUBLANES, LANES), xmap),
            pl.BlockSpec((1, D, 2 * F), wmap),
            pl.BlockSpec((1, 1, 2 * F), wmap),
            pl.BlockSpec((1, F, D), wmap),
            pl.BlockSpec((1, 1, D), wmap),
        ],
        out_specs=pl.BlockSpec(memory_space=pl.ANY),
        scratch_shapes=[
            pltpu.VMEM((D, 2 * F), BF16), pltpu.VMEM((F, D), BF16),
            pltpu.VMEM((2 * tm * SUBLANES, LANES), F32),
            pltpu.VMEM((SUBLANES, LANES), I32),
            pltpu.SMEM((2 * SUBLANES, LANES), I32),
            pltpu.SemaphoreType.DMA((2,)), pltpu.SemaphoreType.DMA(()),
        ],
    )
    return pl.pallas_call(
        functools.partial(_ffn_kernel, n_tokens=n_tokens),
        grid_spec=grid_spec,
        out_shape=jax.ShapeDtypeStruct(((TOP_K * n_tokens + 2 * tm) * SUBLANES, LANES), F32),
        compiler_params=_cparams("arbitrary"),
        name="ffn",
    )(blk_exp, nused, buf, wgu, bgu, wd, bd)


def _combine_kernel(gate_ref, x2_ref, fg_ref, *refs):
    y_refs, o_ref = refs[:TOP_K], refs[TOP_K]
    tm, D = x2_ref.shape
    nj = D // LANES
    gate = gate_ref[...]
    cols = []
    for j in range(nj):
        acc = None
        for k in range(TOP_K):
            part = gate[:, k:k + 1] * y_refs[k][pl.ds(j, tm, stride=nj), :]
            acc = part if acc is None else acc + part
        cols.append(acc)
    y = jnp.concatenate(cols, axis=1)
    o_ref[...] = _rms(x2_ref[...] + y, fg_ref[...])


def _combine(gate4, x2, fg, y_rows):
    T, D = x2.shape
    tm = min(ROW_TILE, T)
    nj = D // LANES
    nt = T // tm

    def slot_spec(k):
        return pl.BlockSpec((tm * nj, LANES), lambda i: (k * nt + i, 0))

    return pl.pallas_call(
        _combine_kernel,
        grid=(nt,),
        in_specs=[
            pl.BlockSpec((tm, LANES), lambda i: (i, 0)),
            pl.BlockSpec((tm, D), lambda i: (i, 0)),
            pl.BlockSpec((1, D), lambda i: (0, 0)),
        ] + [slot_spec(k) for k in range(TOP_K)],
        out_specs=pl.BlockSpec((tm, D), lambda i: (i, 0)),
        out_shape=jax.ShapeDtypeStruct((T, D), F32),
        compiler_params=_cparams("parallel"),
        name="combine",
    )(gate4, x2, fg, *([y_rows] * TOP_K))


def _rope_tables(S):
    rows = S // GRID_W
    row = jnp.repeat(jnp.arange(rows, dtype=I32), GRID_W).astype(F32)
    col = jnp.tile(jnp.arange(GRID_W, dtype=I32), rows).astype(F32)
    inv = ROPE_THETA ** (-jnp.arange(0, AXIS_DIM, 2, dtype=F32) / AXIS_DIM)
    ang_r = row[:, None] * inv[None, :]
    ang_c = col[:, None] * inv[None, :]
    cos = jnp.concatenate([jnp.cos(ang_r)] * 2 + [jnp.cos(ang_c)] * 2, axis=1)
    sin = jnp.concatenate([-jnp.sin(ang_r), jnp.sin(ang_r), -jnp.sin(ang_c), jnp.sin(ang_c)], axis=1)
    return jnp.tile(cos, (1, 2)), jnp.tile(sin, (1, 2))


def _block_diag_gates(wa, ba, wi, bi, ct):
    nb, bw = wa.shape[1], wa.shape[2]
    per = ct // bw
    nct = nb // per
    eye = jnp.eye(per, dtype=wa.dtype)

    def tiles(w):
        w = w.reshape(nct, per, bw, bw)
        return jnp.einsum('cpij,pq->cpiqj', w, eye).reshape(nct, ct, ct)

    wg = jnp.concatenate([tiles(wa[0]), tiles(wi[0]), tiles(wa[1]), tiles(wi[1])], axis=2)
    bias = lambda b: b.reshape(nct, 1, ct)
    bg = jnp.concatenate([bias(ba[0]), bias(bi[0]), bias(ba[1]), bias(bi[1])], axis=2)
    return (0.5 * wg).astype(BF16), 0.5 * bg


def _layer(x2d, B, S, norm1_g, w_in, b_in, q_norm_g, k_norm_g, conv_w, conv_b, lru_wa, lru_ba,
           lru_wi, lru_bi, lru_lambda, w_attn_o, w_lru_o, w_out, norm2_g, w_router, b_router,
           w_gu, b_gu, w_down, b_down, out_g):
    T, D = x2d.shape
    C = conv_w.shape[1]
    nqkv = Q_WIDTH + 2 * KV_WIDTH
    row2 = lambda v: v.reshape(1, -1)

    cos_t, sin_t = _rope_tables(S)
    head = jnp.arange(LANES) // HEAD_DIM
    hsum = (head[:, None] == head[None, :]).astype(BF16)
    q, k, v = _qkv(x2d, row2(norm1_g), w_in[:, :nqkv].astype(BF16), row2(b_in[:nqkv]),
                   cos_t, sin_t, row2(jnp.tile(q_norm_g, 2)), row2(jnp.tile(k_norm_g, 2)),
                   hsum, B, S)
    proj = _proj(x2d, row2(norm1_g), w_in[:, nqkv:].astype(BF16), row2(b_in[nqkv:]))
    attn = _attn(q, k, v)
    wg, bg = _block_diag_gates(lru_wa, lru_ba, lru_wi, lru_bi, LRU_CT)
    rec = _lru(proj, conv_w, row2(conv_b), wg, bg, lru_lambda, B, S, C)

    pad = LANES - N_EXPERTS
    rw = jnp.pad(w_router, ((0, 0), (0, pad)))
    rw_hi = rw.astype(BF16)
    rw_lo = (rw - rw_hi.astype(F32)).astype(BF16)
    rb = jnp.pad(b_router, (0, pad), constant_values=NEG_BIG).reshape(1, LANES)
    x2, h2_rows, logits = _merge(x2d, attn, rec, proj, w_attn_o.astype(BF16),
                                 w_lru_o.astype(BF16), w_out.astype(BF16), row2(norm2_g),
                                 rw_hi, rw_lo, rb)

    gate4, epack, _, eidx_t, rank_t, cnt = _route(logits)
    counts = cnt[:, 0].astype(I32)
    tm = FFN_TM
    nblk = (counts + tm - 1) // tm
    pend_blk = jnp.cumsum(nblk)
    pstart = (pend_blk - nblk) * tm
    A = T * TOP_K
    nb = (A + N_EXPERTS * (tm - 1) + tm - 1) // tm
    blk_exp = jnp.minimum(
        jnp.sum(pend_blk[None, :] <= jnp.arange(nb, dtype=I32)[:, None], axis=1),
        N_EXPERTS - 1).astype(I32)
    nused = pend_blk[-1:].astype(I32)
    first_row = jnp.sum(jnp.where(eidx_t[:TOP_K, :, None] == jnp.arange(N_EXPERTS), pstart, 0),
                        axis=-1)
    dest = (first_row + rank_t[:TOP_K]).astype(I32)
    last_blk = jnp.where(nblk > 0, pend_blk - 1, -1).astype(I32)

    buf = _dispatch(dest, last_blk, nused, h2_rows, epack, nb * tm, tm)
    y_rows = _ffn(blk_exp, nused, buf, w_gu, b_gu.reshape(N_EXPERTS, 1, -1), w_down,
                  b_down.reshape(N_EXPERTS, 1, -1), tm, T)
    return _combine(gate4, x2, row2(out_g), y_rows)


def kernel(x, norm1_g, w_in, b_in, q_norm_g, k_norm_g, conv_w, conv_b, lru_wa, lru_ba, lru_wi,
           lru_bi, lru_lambda, w_attn_o, w_lru_o, w_out, norm2_g, w_router, b_router, w_gu, b_gu,
           w_down, b_down, final_g):
    B, S, D = x.shape
    depth = norm1_g.shape[0]
    assert depth == 1, "the fused final RMSNorm assumes a single layer"
    assert S % (TIME_CHUNKS * SCAN_UNROLL) == 0 and S % GRID_W == 0
    assert D == 2 * PACK_LINES * LANES, "a token's bf16 row must fill PACK_LINES word lines"
    out = _layer(x.reshape(B * S, D), B, S, norm1_g[0], w_in[0], b_in[0], q_norm_g[0],
                 k_norm_g[0], conv_w[0], conv_b[0], lru_wa[0], lru_ba[0], lru_wi[0], lru_bi[0],
                 lru_lambda[0], w_attn_o[0], w_lru_o[0], w_out[0], norm2_g[0], w_router[0],
                 b_router[0], w_gu[0], b_gu[0], w_down[0], b_down[0], final_g)
    return out.reshape(B, S, D)
```
